```python
import jax, jax.numpy as jnp
from jax import lax
import numpy as np

D_MODEL = 2048
BATCH = 8
SEQ = 8192
DEPTH = 4

MEM_LEN = 256
HEAD_DIM = 128
N_MIX_HEADS = 12
N_MEM_HEADS = 4
MIX_WIDTH = N_MIX_HEADS * HEAD_DIM
MEM_WIDTH = N_MEM_HEADS * HEAD_DIM
CAT_WIDTH = MIX_WIDTH + MEM_WIDTH

Q_LORA = 512
KV_LORA = 512
QK_NOPE = 128
QK_ROPE = 64
V_HEAD = HEAD_DIM
ROPE_THETA = 10000.0
Q_BLOCK = 128
MLA_MIX_IN = Q_LORA + KV_LORA + QK_ROPE
MLA_IN = MLA_MIX_IN + MEM_WIDTH

CONV_WIDTH = 4
CHUNK = 64
GDN_MIX_IN = 4 * MIX_WIDTH + 2 * N_MIX_HEADS
GDN_IN = GDN_MIX_IN + MEM_WIDTH

D_FF = 4 * D_MODEL
ALPHA = (2 * DEPTH) ** 0.25
BETA_INIT = (8 * DEPTH) ** -0.25
N_MLA = (DEPTH + 1) // 2
N_GDN = DEPTH // 2
LN_EPS = 1e-5
RMS_EPS = 1e-6

kernel_name = "hybrid_mla_gdn_deepnorm_memory"


def layer_norm(x, g, b):
    xf = x.astype(jnp.float32)
    mu = jnp.mean(xf, -1, keepdims=True)
    var = jnp.mean(jnp.square(xf - mu), -1, keepdims=True)
    return ((xf - mu) * lax.rsqrt(var + LN_EPS) * g + b).astype(x.dtype)


def rms_norm(x, g):
    xf = x.astype(jnp.float32)
    return (xf * lax.rsqrt(jnp.mean(xf * xf, -1, keepdims=True) + RMS_EPS) * g).astype(x.dtype)


def l2_normalize(x):
    return x * lax.rsqrt(jnp.sum(x * x, -1, keepdims=True) + 1e-6)


def rope(x, cos, sin):
    half = x.shape[-1] // 2
    x1, x2 = x[..., :half], x[..., half:]
    return jnp.concatenate([x1 * cos - x2 * sin, x2 * cos + x1 * sin], -1)


def causal_block_attention(q, k, v):
    B_, S_, H, Dq = q.shape
    nb = S_ // Q_BLOCK
    qb = jnp.moveaxis(q.reshape(B_, nb, Q_BLOCK, H, Dq), 1, 0)
    kpos = jnp.arange(S_)
    scale = Dq ** -0.5

    def one_block(args):
        qi, bi = args
        s = jnp.einsum('bqhd,bkhd->bhqk', qi, k).astype(jnp.float32) * scale
        qpos = bi * Q_BLOCK + jnp.arange(Q_BLOCK)
        s = jnp.where(kpos[None, :] <= qpos[:, None], s, -jnp.inf)
        p = jax.nn.softmax(s, axis=-1).astype(v.dtype)
        return jnp.einsum('bhqk,bkhd->bqhd', p, v)

    o = lax.map(one_block, (qb, jnp.arange(nb)))
    return jnp.moveaxis(o, 0, 1).reshape(B_, S_, H, v.shape[-1])


def memory_attention(q, mem_kv):
    B_, M_, _ = mem_kv.shape
    k = mem_kv[..., :MEM_WIDTH].reshape(B_, M_, N_MEM_HEADS, HEAD_DIM)
    v = mem_kv[..., MEM_WIDTH:].reshape(B_, M_, N_MEM_HEADS, HEAD_DIM)
    s = jnp.einsum('bshd,bmhd->bhsm', q, k).astype(jnp.float32) * HEAD_DIM ** -0.5
    p = jax.nn.softmax(s, axis=-1).astype(v.dtype)
    o = jnp.einsum('bhsm,bmhd->bshd', p, v)
    return o.reshape(q.shape[0], q.shape[1], MEM_WIDTH)


def mla_mixer(h, cos, sin, q_norm, w_uq, kv_norm, w_ukv):
    B_, S_, _ = h.shape
    c_q = h[..., :Q_LORA]
    c_kv = h[..., Q_LORA:Q_LORA + KV_LORA]
    k_r = h[..., Q_LORA + KV_LORA:]
    q = (rms_norm(c_q, q_norm) @ w_uq).reshape(B_, S_, N_MIX_HEADS, QK_NOPE + QK_ROPE)
    q = jnp.concatenate([q[..., :QK_NOPE], rope(q[..., QK_NOPE:], cos, sin)], -1)
    kv = (rms_norm(c_kv, kv_norm) @ w_ukv).reshape(B_, S_, N_MIX_HEADS, QK_NOPE + V_HEAD)
    k_r = rope(k_r[:, :, None, :], cos, sin)
    k = jnp.concatenate([kv[..., :QK_NOPE], jnp.broadcast_to(k_r, (B_, S_, N_MIX_HEADS, QK_ROPE))], -1)
    v = kv[..., QK_NOPE:]
    return causal_block_attention(q, k, v).reshape(B_, S_, MIX_WIDTH)


def causal_depthwise_conv(x, w):
    C = x.shape[-1]
    return lax.conv_general_dilated(
        x, w[:, None, :].astype(x.dtype), window_strides=(1,),
        padding=[(CONV_WIDTH - 1, 0)], dimension_numbers=('NWC', 'WIO', 'NWC'),
        feature_group_count=C)


def gated_delta_chunked(q, k, v, g, beta):
    B_, S_, H, Dk = q.shape
    Dv = v.shape[-1]
    N = S_ // CHUNK

    def to_chunks(t):
        return jnp.moveaxis(t.reshape((B_, N, CHUNK) + t.shape[2:]), 2, 3)

    q = to_chunks(q * Dk ** -0.5)
    k = to_chunks(k)
    v = to_chunks(v)
    beta = to_chunks(beta)
    gc = jnp.cumsum(to_chunks(g), axis=-1)

    tril = jnp.tril(jnp.ones((CHUNK, CHUNK), bool))
    strict = jnp.tril(jnp.ones((CHUNK, CHUNK), bool), k=-1)
    diff = gc[..., :, None] - gc[..., None, :]
    decay = jnp.where(tril, jnp.exp(jnp.where(tril, diff, 0.0)), 0.0)

    kb = k * beta[..., None]
    L = jnp.where(strict, jnp.einsum('bnhcd,bnhed->bnhce', kb, k) * decay, 0.0)
    a = L + jnp.eye(CHUNK, dtype=jnp.float32)
    rhs = jnp.concatenate([v * beta[..., None], kb * jnp.exp(gc)[..., None]], -1)
    sol = lax.linalg.triangular_solve(a, rhs, left_side=True, lower=True, unit_diagonal=True)
    u, w = sol[..., :Dv], sol[..., Dv:]

    a_qk = jnp.where(tril, jnp.einsum('bnhcd,bnhed->bnhce', q, k) * decay, 0.0)
    q_dec = q * jnp.exp(gc)[..., None]
    k_dec = k * jnp.exp(gc[..., -1:] - gc)[..., None]
    g_last = jnp.exp(gc[..., -1])

    def step(state, inp):
        qd, kd, uu, ww, aqk, gl = inp
        v_new = uu - jnp.einsum('bhcd,bhde->bhce', ww, state)
        o = jnp.einsum('bhcd,bhde->bhce', qd, state) + jnp.einsum('bhce,bhef->bhcf', aqk, v_new)
        state = state * gl[..., None, None] + jnp.einsum('bhcd,bhce->bhde', kd, v_new)
        return state, o

    xs = tuple(jnp.moveaxis(t, 1, 0) for t in (q_dec, k_dec, u, w, a_qk, g_last))
    state0 = jnp.zeros((B_, H, Dk, Dv), jnp.float32)
    _, o = lax.scan(step, state0, xs)
    return jnp.transpose(o, (1, 0, 3, 2, 4)).reshape(B_, S_, H, Dv)


def gdn_mixer(h, conv_w, a_log, dt_bias, o_norm):
    B_, S_, _ = h.shape
    qkv = jax.nn.silu(causal_depthwise_conv(h[..., :3 * MIX_WIDTH], conv_w)).astype(jnp.float32)
    z = h[..., 3 * MIX_WIDTH:4 * MIX_WIDTH].astype(jnp.float32)
    a_in = h[..., 4 * MIX_WIDTH:4 * MIX_WIDTH + N_MIX_HEADS].astype(jnp.float32)
    b_in = h[..., 4 * MIX_WIDTH + N_MIX_HEADS:].astype(jnp.float32)
    shp = (B_, S_, N_MIX_HEADS, HEAD_DIM)
    q = l2_normalize(qkv[..., :MIX_WIDTH].reshape(shp))
    k = l2_normalize(qkv[..., MIX_WIDTH:2 * MIX_WIDTH].reshape(shp))
    v = qkv[..., 2 * MIX_WIDTH:].reshape(shp)
    g = -jnp.exp(a_log.astype(jnp.float32)) * jax.nn.softplus(a_in + dt_bias.astype(jnp.float32))
    beta = jax.nn.sigmoid(b_in)
    o = gated_delta_chunked(q, k, v, g, beta)
    o = rms_norm(o, o_norm.astype(jnp.float32)) * jax.nn.silu(z.reshape(shp))
    return o.reshape(B_, S_, MIX_WIDTH).astype(h.dtype)


def _fwd_setup_inputs(seed: int = 0) -> dict:
    key = jax.random.key(seed)
    ks = jax.random.split(key, 24)
    f32 = jnp.float32
    nrm = lambda k, shape, scale: jax.random.normal(k, shape, f32) * scale
    gain = lambda k, shape: 1.0 + 0.02 * jax.random.normal(k, shape, f32)
    offset = jax.random.randint(ks[2], (BATCH, 1), 0, 1024, dtype=jnp.int32)
    positions = (offset + jnp.arange(SEQ, dtype=jnp.int32)[None, :]).astype(jnp.int32)
    dt = jnp.exp(jax.random.uniform(ks[11], (N_GDN, N_MIX_HEADS), f32, np.log(1e-3), np.log(1e-1)))
    return {
        "x": jax.random.normal(ks[0], (BATCH, SEQ, D_MODEL), f32),
        "mem": jax.random.normal(ks[1], (BATCH, MEM_LEN, D_MODEL), f32),
        "positions": positions,
        "mla_w_in": nrm(ks[3], (N_MLA, D_MODEL, MLA_IN), D_MODEL ** -0.5),
        "mla_q_norm": gain(ks[4], (N_MLA, Q_LORA)),
        "mla_w_uq": nrm(ks[5], (N_MLA, Q_LORA, N_MIX_HEADS * (QK_NOPE + QK_ROPE)), Q_LORA ** -0.5),
        "mla_kv_norm": gain(ks[6], (N_MLA, KV_LORA)),
        "mla_w_ukv": nrm(ks[7], (N_MLA, KV_LORA, N_MIX_HEADS * (QK_NOPE + V_HEAD)), KV_LORA ** -0.5),
        "gdn_w_in": nrm(ks[8], (N_GDN, D_MODEL, GDN_IN), D_MODEL ** -0.5),
        "gdn_conv": nrm(ks[9], (N_GDN, CONV_WIDTH, 3 * MIX_WIDTH), CONV_WIDTH ** -0.5),
        "gdn_a_log": jnp.log(jax.random.uniform(ks[10], (N_GDN, N_MIX_HEADS), f32, 1.0, 16.0)),
        "gdn_dt_bias": dt + jnp.log(-jnp.expm1(-dt)),
        "gdn_o_norm": gain(ks[12], (N_GDN, HEAD_DIM)),
        "mem_w_kv": nrm(ks[13], (DEPTH, D_MODEL, 2 * MEM_WIDTH), D_MODEL ** -0.5),
        "w_out": nrm(ks[14], (DEPTH, CAT_WIDTH, D_MODEL), CAT_WIDTH ** -0.5 * BETA_INIT),
        "ln1_g": gain(ks[15], (DEPTH, D_MODEL)),
        "ln1_b": nrm(ks[16], (DEPTH, D_MODEL), 0.02),
        "mlp_w1": nrm(ks[17], (DEPTH, D_MODEL, D_FF), D_MODEL ** -0.5),
        "mlp_w2": nrm(ks[18], (DEPTH, D_FF, D_MODEL), D_FF ** -0.5 * BETA_INIT),
        "ln2_g": gain(ks[19], (DEPTH, D_MODEL)),
        "ln2_b": nrm(ks[20], (DEPTH, D_MODEL), 0.02),
    }


def _fwd_reference(x, mem, positions, mla_w_in, mla_q_norm, mla_w_uq, mla_kv_norm, mla_w_ukv,
              gdn_w_in, gdn_conv, gdn_a_log, gdn_dt_bias, gdn_o_norm, mem_w_kv, w_out,
              ln1_g, ln1_b, mlp_w1, mlp_w2, ln2_g, ln2_b):
    B_, S_, _ = x.shape
    inv_freq = 1.0 / (ROPE_THETA ** (jnp.arange(0, QK_ROPE, 2, dtype=jnp.float32) / QK_ROPE))
    ang = positions.astype(jnp.float32)[:, :, None, None] * inv_freq
    cos = jnp.cos(ang).astype(x.dtype)
    sin = jnp.sin(ang).astype(x.dtype)
    for i in range(DEPTH):
        j = i // 2
        mem_kv = mem @ mem_w_kv[i]
        if i % 2 == 0:
            h = x @ mla_w_in[j]
            mix = mla_mixer(h[..., :MLA_MIX_IN], cos, sin, mla_q_norm[j], mla_w_uq[j],
                            mla_kv_norm[j], mla_w_ukv[j])
        else:
            h = x @ gdn_w_in[j]
            mix = gdn_mixer(h[..., :GDN_MIX_IN], gdn_conv[j], gdn_a_log[j], gdn_dt_bias[j], gdn_o_norm[j])
        q_mem = h[..., -MEM_WIDTH:].reshape(B_, S_, N_MEM_HEADS, HEAD_DIM)
        mem_o = memory_attention(q_mem, mem_kv)
        y = jnp.concatenate([mix, mem_o], -1) @ w_out[i]
        x = layer_norm(ALPHA * x + y, ln1_g[i], ln1_b[i])
        ff = jnp.square(jax.nn.relu(x @ mlp_w1[i])) @ mlp_w2[i]
        x = layer_norm(ALPHA * x + ff, ln2_g[i], ln2_b[i])
    return x


import jax as _jax
import jax.numpy as _jnp

TWIN_FORMAT = 'train_step'
FWD_PARAMS = ['x', 'mem', 'positions', 'mla_w_in', 'mla_q_norm', 'mla_w_uq', 'mla_kv_norm', 'mla_w_ukv', 'gdn_w_in', 'gdn_conv', 'gdn_a_log', 'gdn_dt_bias', 'gdn_o_norm', 'mem_w_kv', 'w_out', 'ln1_g', 'ln1_b', 'mlp_w1', 'mlp_w2', 'ln2_g', 'ln2_b']
TWIN_WEIGHTS = ['mla_w_in', 'mla_q_norm', 'mla_w_uq', 'mla_kv_norm', 'mla_w_ukv', 'gdn_w_in', 'gdn_conv', 'gdn_a_log', 'gdn_dt_bias', 'gdn_o_norm', 'mem_w_kv', 'w_out', 'ln1_g', 'ln1_b', 'mlp_w1', 'mlp_w2', 'ln2_g', 'ln2_b']
TWIN_DIFF_INPUT = 'x'
TWIN_INPUTS = ['x', 'mem', 'positions', 'mla_w_in', 'mla_q_norm', 'mla_w_uq', 'mla_kv_norm', 'mla_w_ukv', 'gdn_w_in', 'gdn_conv', 'gdn_a_log', 'gdn_dt_bias', 'gdn_o_norm', 'mem_w_kv', 'w_out', 'ln1_g', 'ln1_b', 'mlp_w1', 'mlp_w2', 'ln2_g', 'ln2_b', 'loss_target', 'm_mla_w_in', 'm_mla_q_norm', 'm_mla_w_uq', 'm_mla_kv_norm', 'm_mla_w_ukv', 'm_gdn_w_in', 'm_gdn_conv', 'm_gdn_a_log', 'm_gdn_dt_bias', 'm_gdn_o_norm', 'm_mem_w_kv', 'm_w_out', 'm_ln1_g', 'm_ln1_b', 'm_mlp_w1', 'm_mlp_w2', 'm_ln2_g', 'm_ln2_b', 'v_mla_w_in', 'v_mla_q_norm', 'v_mla_w_uq', 'v_mla_kv_norm', 'v_mla_w_ukv', 'v_gdn_w_in', 'v_gdn_conv', 'v_gdn_a_log', 'v_gdn_dt_bias', 'v_gdn_o_norm', 'v_mem_w_kv', 'v_w_out', 'v_ln1_g', 'v_ln1_b', 'v_mlp_w1', 'v_mlp_w2', 'v_ln2_g', 'v_ln2_b']
TWIN_OUTPUTS = ['loss', 'grad_x', 'grad_mla_w_in', 'grad_mla_q_norm', 'grad_mla_w_uq', 'grad_mla_kv_norm', 'grad_mla_w_ukv', 'grad_gdn_w_in', 'grad_gdn_conv', 'grad_gdn_a_log', 'grad_gdn_dt_bias', 'grad_gdn_o_norm', 'grad_mem_w_kv', 'grad_w_out', 'grad_ln1_g', 'grad_ln1_b', 'grad_mlp_w1', 'grad_mlp_w2', 'grad_ln2_g', 'grad_ln2_b', 'delta_mla_w_in', 'delta_mla_q_norm', 'delta_mla_w_uq', 'delta_mla_kv_norm', 'delta_mla_w_ukv', 'delta_gdn_w_in', 'delta_gdn_conv', 'delta_gdn_a_log', 'delta_gdn_dt_bias', 'delta_gdn_o_norm', 'delta_mem_w_kv', 'delta_w_out', 'delta_ln1_g', 'delta_ln1_b', 'delta_mlp_w1', 'delta_mlp_w2', 'delta_ln2_g', 'delta_ln2_b', 'new_m_mla_w_in', 'new_m_mla_q_norm', 'new_m_mla_w_uq', 'new_m_mla_kv_norm', 'new_m_mla_w_ukv', 'new_m_gdn_w_in', 'new_m_gdn_conv', 'new_m_gdn_a_log', 'new_m_gdn_dt_bias', 'new_m_gdn_o_norm', 'new_m_mem_w_kv', 'new_m_w_out', 'new_m_ln1_g', 'new_m_ln1_b', 'new_m_mlp_w1', 'new_m_mlp_w2', 'new_m_ln2_g', 'new_m_ln2_b', 'new_v_mla_w_in', 'new_v_mla_q_norm', 'new_v_mla_w_uq', 'new_v_mla_kv_norm', 'new_v_mla_w_ukv', 'new_v_gdn_w_in', 'new_v_gdn_conv', 'new_v_gdn_a_log', 'new_v_gdn_dt_bias', 'new_v_gdn_o_norm', 'new_v_mem_w_kv', 'new_v_w_out', 'new_v_ln1_g', 'new_v_ln1_b', 'new_v_mlp_w1', 'new_v_mlp_w2', 'new_v_ln2_g', 'new_v_ln2_b']
TWIN_LEAF_KINDS = {'loss': 'loss', 'grad_x': 'grad_x', 'grad_mla_w_in': 'grad_w', 'grad_mla_q_norm': 'grad_w', 'grad_mla_w_uq': 'grad_w', 'grad_mla_kv_norm': 'grad_w', 'grad_mla_w_ukv': 'grad_w', 'grad_gdn_w_in': 'grad_w', 'grad_gdn_conv': 'grad_w', 'grad_gdn_a_log': 'grad_w', 'grad_gdn_dt_bias': 'grad_w', 'grad_gdn_o_norm': 'grad_w', 'grad_mem_w_kv': 'grad_w', 'grad_w_out': 'grad_w', 'grad_ln1_g': 'grad_w', 'grad_ln1_b': 'grad_w', 'grad_mlp_w1': 'grad_w', 'grad_mlp_w2': 'grad_w', 'grad_ln2_g': 'grad_w', 'grad_ln2_b': 'grad_w', 'delta_mla_w_in': 'delta_w', 'delta_mla_q_norm': 'delta_w', 'delta_mla_w_uq': 'delta_w', 'delta_mla_kv_norm': 'delta_w', 'delta_mla_w_ukv': 'delta_w', 'delta_gdn_w_in': 'delta_w', 'delta_gdn_conv': 'delta_w', 'delta_gdn_a_log': 'delta_w', 'delta_gdn_dt_bias': 'delta_w', 'delta_gdn_o_norm': 'delta_w', 'delta_mem_w_kv': 'delta_w', 'delta_w_out': 'delta_w', 'delta_ln1_g': 'delta_w', 'delta_ln1_b': 'delta_w', 'delta_mlp_w1': 'delta_w', 'delta_mlp_w2': 'delta_w', 'delta_ln2_g': 'delta_w', 'delta_ln2_b': 'delta_w', 'new_m_mla_w_in': 'new_m', 'new_m_mla_q_norm': 'new_m', 'new_m_mla_w_uq': 'new_m', 'new_m_mla_kv_norm': 'new_m', 'new_m_mla_w_ukv': 'new_m', 'new_m_gdn_w_in': 'new_m', 'new_m_gdn_conv': 'new_m', 'new_m_gdn_a_log': 'new_m', 'new_m_gdn_dt_bias': 'new_m', 'new_m_gdn_o_norm': 'new_m', 'new_m_mem_w_kv': 'new_m', 'new_m_w_out': 'new_m', 'new_m_ln1_g': 'new_m', 'new_m_ln1_b': 'new_m', 'new_m_mlp_w1': 'new_m', 'new_m_mlp_w2': 'new_m', 'new_m_ln2_g': 'new_m', 'new_m_ln2_b': 'new_m', 'new_v_mla_w_in': 'new_v', 'new_v_mla_q_norm': 'new_v', 'new_v_mla_w_uq': 'new_v', 'new_v_mla_kv_norm': 'new_v', 'new_v_mla_w_ukv': 'new_v', 'new_v_gdn_w_in': 'new_v', 'new_v_gdn_conv': 'new_v', 'new_v_gdn_a_log': 'new_v', 'new_v_gdn_dt_bias': 'new_v', 'new_v_gdn_o_norm': 'new_v', 'new_v_mem_w_kv': 'new_v', 'new_v_w_out': 'new_v', 'new_v_ln1_g': 'new_v', 'new_v_ln1_b': 'new_v', 'new_v_mlp_w1': 'new_v', 'new_v_mlp_w2': 'new_v', 'new_v_ln2_g': 'new_v', 'new_v_ln2_b': 'new_v'}


def _forward(args):
    return _fwd_reference(*[args[k] for k in FWD_PARAMS])


def _output_shape():
    def fwd():
        inp = _fwd_setup_inputs(0)
        return _fwd_reference(*[inp[k] for k in FWD_PARAMS])
    out = _jax.eval_shape(fwd)
    return out.shape, out.dtype

N_MICROBATCH = 1
ADAM_LR = 0.001
ADAM_B1 = 0.9
ADAM_B2 = 0.999
ADAM_EPS = 1e-08
ADAM_WD = 0.01
ADAM_STEP = 10
PER_EXAMPLE_BATCH_AXIS = {'x': 0, 'mem': 0, 'positions': 0, 'loss_target': 0}
SHARED_INPUTS = []
_WEIGHT_DTYPES = {'mla_w_in': _jnp.float32, 'mla_q_norm': _jnp.float32, 'mla_w_uq': _jnp.float32, 'mla_kv_norm': _jnp.float32, 'mla_w_ukv': _jnp.float32, 'gdn_w_in': _jnp.float32, 'gdn_conv': _jnp.float32, 'gdn_a_log': _jnp.float32, 'gdn_dt_bias': _jnp.float32, 'gdn_o_norm': _jnp.float32, 'mem_w_kv': _jnp.float32, 'w_out': _jnp.float32, 'ln1_g': _jnp.float32, 'ln1_b': _jnp.float32, 'mlp_w1': _jnp.float32, 'mlp_w2': _jnp.float32, 'ln2_g': _jnp.float32, 'ln2_b': _jnp.float32}
MOMENT_SCALE = {'mla_w_in': 1.221126e-02, 'mla_q_norm': 1.005951e-02, 'mla_w_uq': 4.791708e-03, 'mla_kv_norm': 1.725422e-02, 'mla_w_ukv': 7.295876e-03, 'gdn_w_in': 1.582525e-02, 'gdn_conv': 1.623698e-02, 'gdn_a_log': 7.413662e-02, 'gdn_dt_bias': 7.276536e-02, 'gdn_o_norm': 6.861966e-02, 'mem_w_kv': 3.709719e-03, 'w_out': 3.651874e-02, 'ln1_g': 8.771254e-01, 'ln1_b': 5.322863e-01, 'mlp_w1': 2.263899e-02, 'mlp_w2': 1.517112e-01, 'ln2_g': 1.609435e+01, 'ln2_b': 3.866475e+00}


def _to_microbatches(a, axis):
    t = _jnp.moveaxis(a, axis, 0)
    t = t.reshape((N_MICROBATCH, t.shape[0] // N_MICROBATCH) + t.shape[1:])
    return _jnp.moveaxis(t, 1, axis + 1)


def setup_inputs(seed: int = 0) -> dict:
    inp = _fwd_setup_inputs(seed)
    key = _jax.random.fold_in(_jax.random.key(seed), 7919)
    shape, _ = _output_shape()
    out = dict(inp)
    out["loss_target"] = _jax.random.normal(_jax.random.fold_in(key, 0), shape, _jnp.float32)
    for i, name in enumerate(TWIN_WEIGHTS):
        w = inp[name].astype(_jnp.float32)
        if MOMENT_SCALE is None:
            s = _jnp.sqrt(_jnp.mean(_jnp.square(w)) + 1e-30)
        else:
            s = MOMENT_SCALE[name]
        km, kv = _jax.random.split(_jax.random.fold_in(key, i + 1))
        out[name] = w
        out["m_" + name] = s * _jax.random.normal(km, w.shape, _jnp.float32)
        out["v_" + name] = (s * s) * _jax.random.uniform(kv, w.shape, _jnp.float32, 0.5, 1.5)
    if N_MICROBATCH > 1:
        for name, axis in PER_EXAMPLE_BATCH_AXIS.items():
            out[name] = _to_microbatches(out[name], axis)
    return {'x': out['x'], 'mem': out['mem'], 'positions': out['positions'], 'mla_w_in': out['mla_w_in'], 'mla_q_norm': out['mla_q_norm'], 'mla_w_uq': out['mla_w_uq'], 'mla_kv_norm': out['mla_kv_norm'], 'mla_w_ukv': out['mla_w_ukv'], 'gdn_w_in': out['gdn_w_in'], 'gdn_conv': out['gdn_conv'], 'gdn_a_log': out['gdn_a_log'], 'gdn_dt_bias': out['gdn_dt_bias'], 'gdn_o_norm': out['gdn_o_norm'], 'mem_w_kv': out['mem_w_kv'], 'w_out': out['w_out'], 'ln1_g': out['ln1_g'], 'ln1_b': out['ln1_b'], 'mlp_w1': out['mlp_w1'], 'mlp_w2': out['mlp_w2'], 'ln2_g': out['ln2_g'], 'ln2_b': out['ln2_b'], 'loss_target': out['loss_target'], 'm_mla_w_in': out['m_mla_w_in'], 'm_mla_q_norm': out['m_mla_q_norm'], 'm_mla_w_uq': out['m_mla_w_uq'], 'm_mla_kv_norm': out['m_mla_kv_norm'], 'm_mla_w_ukv': out['m_mla_w_ukv'], 'm_gdn_w_in': out['m_gdn_w_in'], 'm_gdn_conv': out['m_gdn_conv'], 'm_gdn_a_log': out['m_gdn_a_log'], 'm_gdn_dt_bias': out['m_gdn_dt_bias'], 'm_gdn_o_norm': out['m_gdn_o_norm'], 'm_mem_w_kv': out['m_mem_w_kv'], 'm_w_out': out['m_w_out'], 'm_ln1_g': out['m_ln1_g'], 'm_ln1_b': out['m_ln1_b'], 'm_mlp_w1': out['m_mlp_w1'], 'm_mlp_w2': out['m_mlp_w2'], 'm_ln2_g': out['m_ln2_g'], 'm_ln2_b': out['m_ln2_b'], 'v_mla_w_in': out['v_mla_w_in'], 'v_mla_q_norm': out['v_mla_q_norm'], 'v_mla_w_uq': out['v_mla_w_uq'], 'v_mla_kv_norm': out['v_mla_kv_norm'], 'v_mla_w_ukv': out['v_mla_w_ukv'], 'v_gdn_w_in': out['v_gdn_w_in'], 'v_gdn_conv': out['v_gdn_conv'], 'v_gdn_a_log': out['v_gdn_a_log'], 'v_gdn_dt_bias': out['v_gdn_dt_bias'], 'v_gdn_o_norm': out['v_gdn_o_norm'], 'v_mem_w_kv': out['v_mem_w_kv'], 'v_w_out': out['v_w_out'], 'v_ln1_g': out['v_ln1_g'], 'v_ln1_b': out['v_ln1_b'], 'v_mlp_w1': out['v_mlp_w1'], 'v_mlp_w2': out['v_mlp_w2'], 'v_ln2_g': out['v_ln2_g'], 'v_ln2_b': out['v_ln2_b']}


def _loss(weights, diff, rest, loss_target):
    with _jax.named_scope("forward"):
        args = {**rest, TWIN_DIFF_INPUT: diff, **{k: w.astype(_WEIGHT_DTYPES[k]) for k, w in weights.items()}}
        y = _forward(args)
    with _jax.named_scope("loss_head"):
        err = _jnp.square(y.astype(_jnp.float32) - loss_target)
        return 0.5 * _jnp.sum(_jnp.mean(err, axis=-1)) if err.ndim else 0.5 * err


def _adamw(w, g, m, v):
    m = ADAM_B1 * m + (1.0 - ADAM_B1) * g
    v = ADAM_B2 * v + (1.0 - ADAM_B2) * _jnp.square(g)
    m_hat = m / (1.0 - ADAM_B1 ** ADAM_STEP)
    v_hat = v / (1.0 - ADAM_B2 ** ADAM_STEP)
    delta = -ADAM_LR * (m_hat / (_jnp.sqrt(v_hat) + ADAM_EPS) + ADAM_WD * w)
    return delta, m, v


def reference(x, mem, positions, mla_w_in, mla_q_norm, mla_w_uq, mla_kv_norm, mla_w_ukv, gdn_w_in, gdn_conv, gdn_a_log, gdn_dt_bias, gdn_o_norm, mem_w_kv, w_out, ln1_g, ln1_b, mlp_w1, mlp_w2, ln2_g, ln2_b, loss_target, m_mla_w_in, m_mla_q_norm, m_mla_w_uq, m_mla_kv_norm, m_mla_w_ukv, m_gdn_w_in, m_gdn_conv, m_gdn_a_log, m_gdn_dt_bias, m_gdn_o_norm, m_mem_w_kv, m_w_out, m_ln1_g, m_ln1_b, m_mlp_w1, m_mlp_w2, m_ln2_g, m_ln2_b, v_mla_w_in, v_mla_q_norm, v_mla_w_uq, v_mla_kv_norm, v_mla_w_ukv, v_gdn_w_in, v_gdn_conv, v_gdn_a_log, v_gdn_dt_bias, v_gdn_o_norm, v_mem_w_kv, v_w_out, v_ln1_g, v_ln1_b, v_mlp_w1, v_mlp_w2, v_ln2_g, v_ln2_b):
    given = dict(x=x, mem=mem, positions=positions, mla_w_in=mla_w_in, mla_q_norm=mla_q_norm, mla_w_uq=mla_w_uq, mla_kv_norm=mla_kv_norm, mla_w_ukv=mla_w_ukv, gdn_w_in=gdn_w_in, gdn_conv=gdn_conv, gdn_a_log=gdn_a_log, gdn_dt_bias=gdn_dt_bias, gdn_o_norm=gdn_o_norm, mem_w_kv=mem_w_kv, w_out=w_out, ln1_g=ln1_g, ln1_b=ln1_b, mlp_w1=mlp_w1, mlp_w2=mlp_w2, ln2_g=ln2_g, ln2_b=ln2_b, loss_target=loss_target, m_mla_w_in=m_mla_w_in, m_mla_q_norm=m_mla_q_norm, m_mla_w_uq=m_mla_w_uq, m_mla_kv_norm=m_mla_kv_norm, m_mla_w_ukv=m_mla_w_ukv, m_gdn_w_in=m_gdn_w_in, m_gdn_conv=m_gdn_conv, m_gdn_a_log=m_gdn_a_log, m_gdn_dt_bias=m_gdn_dt_bias, m_gdn_o_norm=m_gdn_o_norm, m_mem_w_kv=m_mem_w_kv, m_w_out=m_w_out, m_ln1_g=m_ln1_g, m_ln1_b=m_ln1_b, m_mlp_w1=m_mlp_w1, m_mlp_w2=m_mlp_w2, m_ln2_g=m_ln2_g, m_ln2_b=m_ln2_b, v_mla_w_in=v_mla_w_in, v_mla_q_norm=v_mla_q_norm, v_mla_w_uq=v_mla_w_uq, v_mla_kv_norm=v_mla_kv_norm, v_mla_w_ukv=v_mla_w_ukv, v_gdn_w_in=v_gdn_w_in, v_gdn_conv=v_gdn_conv, v_gdn_a_log=v_gdn_a_log, v_gdn_dt_bias=v_gdn_dt_bias, v_gdn_o_norm=v_gdn_o_norm, v_mem_w_kv=v_mem_w_kv, v_w_out=v_w_out, v_ln1_g=v_ln1_g, v_ln1_b=v_ln1_b, v_mlp_w1=v_mlp_w1, v_mlp_w2=v_mlp_w2, v_ln2_g=v_ln2_g, v_ln2_b=v_ln2_b)
    weights = {n: given[n] for n in TWIN_WEIGHTS}
    shared = {n: given[n] for n in SHARED_INPUTS}
    per_example = {n: given[n] for n in ['x', 'mem', 'positions']}
    grad_fn = _jax.value_and_grad(_loss, argnums=(0, 1))

    def one_microbatch(ex, loss_target):
        ex = dict(ex)
        diff = ex.pop(TWIN_DIFF_INPUT)
        return grad_fn(weights, diff, {**shared, **ex}, loss_target)

    if N_MICROBATCH == 1:
        loss, (grad_w, grad_x) = one_microbatch(per_example, given["loss_target"])
    else:
        def body(carry, xs):
            loss_sum, grad_sum = carry
            l_k, (gw_k, gx_k) = one_microbatch(xs[0], xs[1])
            with _jax.named_scope("update"):
                return (loss_sum + l_k, _jax.tree.map(_jnp.add, grad_sum, gw_k)), gx_k

        init = (_jnp.zeros((), _jnp.float32), _jax.tree.map(_jnp.zeros_like, weights))
        (loss, grad_w), grad_x = _jax.lax.scan(body, init, (per_example, given["loss_target"]))
    with _jax.named_scope("update"):
        delta_w, new_m, new_v = {}, {}, {}
        for n in TWIN_WEIGHTS:
            delta_w[n], new_m[n], new_v[n] = _adamw(weights[n], grad_w[n], given["m_" + n], given["v_" + n])
    return (loss, grad_x, *[grad_w[n] for n in TWIN_WEIGHTS], *[delta_w[n] for n in TWIN_WEIGHTS],
            *[new_m[n] for n in TWIN_WEIGHTS], *[new_v[n] for n in TWIN_WEIGHTS])
```

```python
import functools

import jax
import jax.numpy as jnp
from jax import lax
from jax.experimental import pallas as pl
from jax.experimental.pallas import tpu as pltpu

F32 = jnp.float32
MXU_DTYPE = jnp.bfloat16
COMM_DTYPE = jnp.bfloat16

N_DEV = 8
D_MODEL = 2048
DEPTH = 4
HEAD_DIM = 128
N_MIX_HEADS = 12
N_MEM_HEADS = 4
MIX_WIDTH = N_MIX_HEADS * HEAD_DIM
MEM_WIDTH = N_MEM_HEADS * HEAD_DIM
Q_LORA = 512
KV_LORA = 512
QK_NOPE = 128
QK_ROPE = 64
ROPE_THETA = 10000.0
CONV_WIDTH = 4
CHUNK = 64
D_FF = 4 * D_MODEL
ALPHA = (2 * DEPTH) ** 0.25
LN_EPS = 1e-5
RMS_EPS = 1e-6
MLA_IN = Q_LORA + KV_LORA + QK_ROPE + MEM_WIDTH
GDN_IN = 4 * MIX_WIDTH + 2 * N_MIX_HEADS + MEM_WIDTH
MLA_IN_P = 1792
GDN_IN_P = 6912
Q_HEAD_P = 256
ATT_SCALE = (QK_NOPE + QK_ROPE) ** -0.5
ADAM_LR, ADAM_B1, ADAM_B2, ADAM_EPS, ADAM_WD, ADAM_STEP = 0.001, 0.9, 0.999, 1e-08, 0.01, 10
LANES = 128
VMEM_LIMIT = 56 * 1024 * 1024

NN = (((1,), (0,)), ((), ()))
NT = (((1,), (1,)), ((), ()))
TN = (((0,), (0,)), ((), ()))
HI = lax.Precision.HIGHEST


def _dot(a, b, dims, hi=False):
    if hi:
        return lax.dot_general(a, b, dims, precision=HI, preferred_element_type=F32)
    return lax.dot_general(a.astype(MXU_DTYPE), b.astype(MXU_DTYPE), dims, preferred_element_type=F32)


@functools.partial(jax.custom_vjp, nondiff_argnums=(2,))
def _nn_d(a, b, hi):
    return _dot(a, b, NN, hi)


@functools.partial(jax.custom_vjp, nondiff_argnums=(2,))
def _nt_d(a, b, hi):
    return _dot(a, b, NT, hi)


@functools.partial(jax.custom_vjp, nondiff_argnums=(2,))
def _tn_d(a, b, hi):
    return _dot(a, b, TN, hi)


_nn_d.defvjp(lambda a, b, hi: (_dot(a, b, NN, hi), (a, b)),
             lambda hi, r, g: (_nt_d(g, r[1], hi), _tn_d(r[0], g, hi)))
_nt_d.defvjp(lambda a, b, hi: (_dot(a, b, NT, hi), (a, b)),
             lambda hi, r, g: (_nn_d(g, r[1], hi), _tn_d(g, r[0], hi)))
_tn_d.defvjp(lambda a, b, hi: (_dot(a, b, TN, hi), (a, b)),
             lambda hi, r, g: (_nt_d(r[1], g, hi), _nn_d(r[0], g, hi)))


class _RawOps:
    nn = staticmethod(lambda a, b, hi=False: _dot(a, b, NN, hi))
    nt = staticmethod(lambda a, b, hi=False: _dot(a, b, NT, hi))
    tn = staticmethod(lambda a, b, hi=False: _dot(a, b, TN, hi))


class _DiffOps:
    nn = staticmethod(lambda a, b, hi=False: _nn_d(a, b, hi))
    nt = staticmethod(lambda a, b, hi=False: _nt_d(a, b, hi))
    tn = staticmethod(lambda a, b, hi=False: _tn_d(a, b, hi))


def _pick(dim, cands=(512, 384, 256, 128)):
    for c in cands:
        if dim % c == 0:
            return c
    return dim


def _pc(body, *, grid, in_specs, out_specs, out_shape, name, scratch=()):
    return pl.pallas_call(
        body, grid=grid, in_specs=in_specs, out_specs=out_specs, out_shape=out_shape,
        scratch_shapes=list(scratch), name=name,
        compiler_params=pltpu.CompilerParams(dimension_semantics=("arbitrary",) * len(grid),
                                             vmem_limit_bytes=VMEM_LIMIT))


def _rows(ts, w, cb=0):
    return pl.BlockSpec((ts, w), lambda i, *_: (i, cb))


def _whole(shape):
    return pl.BlockSpec(shape, lambda *_: (0,) * len(shape))


def _sds(shape, dtype=F32):
    return jax.ShapeDtypeStruct(shape, dtype)


def _mm(a, b, *, name, ta=False, tb=False, out_dtype=F32, a_fn=None, epi=None, e=None):
    m, k = (a.shape[1], a.shape[0]) if ta else a.shape
    n = b.shape[0] if tb else b.shape[1]
    assert k == (b.shape[1] if tb else b.shape[0]), (a.shape, b.shape, ta, tb)
    tm, tn, tk = _pick(m, (1024, 512, 384, 256, 128)), _pick(n, (1024, 512, 384, 256, 128)), _pick(k)
    nk = k // tk
    dims = (((0 if ta else 1,), (1 if tb else 0,)), ((), ()))

    def body(*refs):
        if e is None:
            a_ref, b_ref, o_ref, acc = refs
        else:
            a_ref, b_ref, e_ref, o_ref, acc = refs
        kk = pl.program_id(2)

        @pl.when(kk == 0)
        def _():
            acc[...] = jnp.zeros_like(acc)

        av = a_ref[...]
        if a_fn is not None:
            av = a_fn(av.astype(F32))
        acc[...] += lax.dot_general(av.astype(MXU_DTYPE), b_ref[...].astype(MXU_DTYPE), dims,
                                    preferred_element_type=F32)

        @pl.when(kk == nk - 1)
        def _():
            r = acc[...]
            if epi is not None:
                r = epi(r, e_ref[...].astype(F32))
            o_ref[...] = r.astype(out_dtype)

    a_spec = pl.BlockSpec((tk, tm), lambda i, j, kk: (kk, i)) if ta else pl.BlockSpec((tm, tk), lambda i, j, kk: (i, kk))
    b_spec = pl.BlockSpec((tn, tk), lambda i, j, kk: (j, kk)) if tb else pl.BlockSpec((tk, tn), lambda i, j, kk: (kk, j))
    o_spec = pl.BlockSpec((tm, tn), lambda i, j, kk: (i, j))
    ins, specs = [a, b], [a_spec, b_spec]
    if e is not None:
        assert e.shape == (m, n)
        ins.append(e)
        specs.append(o_spec)
    return _pc(body, grid=(m // tm, n // tn, nk), in_specs=specs, out_specs=o_spec,
               out_shape=_sds((m, n), out_dtype), name=name, scratch=[pltpu.VMEM((tm, tn), F32)])(*ins)


def _relu2(v):
    r = jnp.maximum(v, 0.0)
    return r * r


def _relu2_bwd(acc, h1):
    return acc * (2.0 * jnp.maximum(h1, 0.0))


def _add_alpha(acc, dr):
    return acc + ALPHA * dr


def _ln(r, g, b):
    mu = jnp.mean(r, -1, keepdims=True)
    var = jnp.mean(jnp.square(r - mu), -1, keepdims=True)
    return (r - mu) * lax.rsqrt(var + LN_EPS) * g + b


def _ln_fwd(x, y, g, b, name):
    s, d = x.shape
    ts = _pick(s, (256, 128))

    def body(x_ref, y_ref, g_ref, b_ref, o_ref):
        o_ref[...] = _ln(ALPHA * x_ref[...] + y_ref[...], g_ref[...], b_ref[...])

    return _pc(body, grid=(s // ts,), in_specs=[_rows(ts, d), _rows(ts, d), _whole((1, d)), _whole((1, d))],
               out_specs=_rows(ts, d), out_shape=_sds((s, d)), name=name)(x, y, g, b)


def _ln_bwd(x, y, g, b, dout, name):
    s, d = x.shape
    ts = _pick(s, (256, 128))

    def body(x_ref, y_ref, g_ref, b_ref, do_ref, dr_ref, dg_ref, db_ref):
        @pl.when(pl.program_id(0) == 0)
        def _():
            dg_ref[...] = jnp.zeros_like(dg_ref)
            db_ref[...] = jnp.zeros_like(db_ref)

        r = ALPHA * x_ref[...] + y_ref[...]
        _, vjp = jax.vjp(_ln, r, g_ref[...], b_ref[...])
        dr, dg, db = vjp(do_ref[...])
        dr_ref[...] = dr
        dg_ref[...] += dg
        db_ref[...] += db

    return _pc(body, grid=(s // ts,),
               in_specs=[_rows(ts, d), _rows(ts, d), _whole((1, d)), _whole((1, d)), _rows(ts, d)],
               out_specs=[_rows(ts, d), _whole((1, d)), _whole((1, d))],
               out_shape=[_sds((s, d)), _sds((1, d)), _sds((1, d))], name=name)(x, y, g, b, dout)


def _loss_and_grad(y, target, name):
    s, d = y.shape
    ts = _pick(s, (256, 128))

    def body(y_ref, t_ref, l_ref, dy_ref):
        @pl.when(pl.program_id(0) == 0)
        def _():
            l_ref[...] = jnp.zeros_like(l_ref)

        diff = y_ref[...] - t_ref[...]
        per_tok = jnp.mean(jnp.square(diff), -1, keepdims=True)
        l_ref[...] += 0.5 * jnp.sum(per_tok, 0, keepdims=True) * jnp.ones((1, LANES), F32)
        dy_ref[...] = diff * (1.0 / d)

    return _pc(body, grid=(s // ts,), in_specs=[_rows(ts, d), _rows(ts, d)],
               out_specs=[_whole((1, LANES)), _rows(ts, d)],
               out_shape=[_sds((1, LANES)), _sds((s, d))], name=name)(y, target)


def _rope(blk, cs, sn):
    return blk * cs + pltpu.roll(blk, 64, 1) * sn


def _rope_t(dblk, cs, sn):
    return dblk * cs + pltpu.roll(dblk * sn, 64, 1)


def _rms(v, g):
    return v * lax.rsqrt(jnp.mean(v * v, -1, keepdims=True) + RMS_EPS) * g


def _mla_prep_fwd(h, qn, kvn, cs, sn, name):
    s = h.shape[0]
    ts = _pick(s, (512, 256, 128))

    def body(cq_ref, ckv_ref, kr_ref, qn_ref, kvn_ref, cs_ref, sn_ref, cqn_ref, ckvn_ref, krr_ref):
        cqn_ref[...] = _rms(cq_ref[...], qn_ref[...])
        ckvn_ref[...] = _rms(ckv_ref[...], kvn_ref[...])
        krr_ref[...] = _rope(kr_ref[...], cs_ref[...], sn_ref[...])

    return _pc(body, grid=(s // ts,),
               in_specs=[_rows(ts, 512, 0), _rows(ts, 512, 1), _rows(ts, LANES, 12), _whole((1, 512)), _whole((1, 512)),
                         _rows(ts, LANES), _rows(ts, LANES)],
               out_specs=[_rows(ts, 512), _rows(ts, 512), _rows(ts, LANES)],
               out_shape=[_sds((s, 512)), _sds((s, 512)), _sds((s, LANES))], name=name)(h, h, h, qn, kvn, cs, sn)


def _mla_prep_bwd(h, qn, kvn, cs, sn, dcqn, dckvn, dkrr_heads, name):
    s = h.shape[0]
    ts = _pick(s, (512, 256, 128))

    def body(cq_ref, ckv_ref, qn_ref, kvn_ref, cs_ref, sn_ref, dcqn_ref, dckvn_ref, dkrr_ref,
             dcq_ref, dckv_ref, dkr_ref, dqn_ref, dkvn_ref):
        @pl.when(pl.program_id(0) == 0)
        def _():
            dqn_ref[...] = jnp.zeros_like(dqn_ref)
            dkvn_ref[...] = jnp.zeros_like(dkvn_ref)

        _, vjp = jax.vjp(_rms, cq_ref[...], qn_ref[...])
        dcq, dqn = vjp(dcqn_ref[...])
        dcq_ref[...] = dcq
        dqn_ref[...] += dqn
        _, vjp = jax.vjp(_rms, ckv_ref[...], kvn_ref[...])
        dckv, dkvn = vjp(dckvn_ref[...])
        dckv_ref[...] = dckv
        dkvn_ref[...] += dkvn
        dkrr = dkrr_ref[0]
        for hh in range(1, N_MIX_HEADS):
            dkrr = dkrr + dkrr_ref[hh]
        dkr_ref[...] = _rope_t(dkrr, cs_ref[...], sn_ref[...])

    heads3 = pl.BlockSpec((N_MIX_HEADS, ts, LANES), lambda i: (0, i, 0))
    return _pc(body, grid=(s // ts,),
               in_specs=[_rows(ts, 512, 0), _rows(ts, 512, 1), _whole((1, 512)), _whole((1, 512)),
                         _rows(ts, LANES), _rows(ts, LANES), _rows(ts, 512), _rows(ts, 512), heads3],
               out_specs=[_rows(ts, 512), _rows(ts, 512), _rows(ts, LANES), _whole((1, 512)), _whole((1, 512))],
               out_shape=[_sds((s, 512)), _sds((s, 512)), _sds((s, LANES)), _sds((1, 512)), _sds((1, 512))],
               name=name)(h, h, qn, kvn, cs, sn, dcqn, dckvn, dkrr_heads)


def _att_tiles(s):
    t = _pick(s, (512, 256, 128))
    return t, s // t


def _causal_mask(i, j, t):
    rows = i * t + lax.broadcasted_iota(jnp.int32, (t, t), 0)
    cols = j * t + lax.broadcasted_iota(jnp.int32, (t, t), 1)
    return cols <= rows


def _flash_fwd(q, kv, krr, cs, sn, name):
    s = q.shape[0]
    t, nb = _att_tiles(s)

    def body(qn_ref, qr_ref, cs_ref, sn_ref, kn_ref, v_ref, kr_ref, o_ref, lse_ref, qn_s, qr_s, m_s, l_s, acc_s):
        i, j = pl.program_id(1), pl.program_id(2)

        @pl.when(j == 0)
        def _():
            qn_s[...] = qn_ref[...].astype(MXU_DTYPE)
            qr_s[...] = _rope(qr_ref[...], cs_ref[...], sn_ref[...]).astype(MXU_DTYPE)
            m_s[...] = jnp.full_like(m_s, -jnp.inf)
            l_s[...] = jnp.zeros_like(l_s)
            acc_s[...] = jnp.zeros_like(acc_s)

        @pl.when(j <= i)
        def _():
            sc = (_dot(qn_s[...], kn_ref[...], NT) + _dot(qr_s[...], kr_ref[...], NT)) * ATT_SCALE
            sc = jnp.where(_causal_mask(i, j, t), sc, -jnp.inf)
            m_old = m_s[...]
            m_new = jnp.maximum(m_old, jnp.max(sc, -1, keepdims=True))
            p = jnp.exp(sc - m_new[:, :1])
            corr = jnp.exp(m_old - m_new)
            l_s[...] = corr * l_s[...] + jnp.sum(p, -1, keepdims=True)
            acc_s[...] = corr * acc_s[...] + _dot(p, v_ref[...], NN)
            m_s[...] = m_new

        @pl.when(j == i)
        def _():
            o_ref[...] = acc_s[...] / l_s[...]
            lse_ref[...] = m_s[...] + jnp.log(l_s[...])

    qb = lambda c: pl.BlockSpec((t, LANES), lambda h, i, j: (i, 2 * h + c))
    kb = lambda c: pl.BlockSpec((t, LANES), lambda h, i, j: (jnp.minimum(j, i), 2 * h + c))
    tab = pl.BlockSpec((t, LANES), lambda h, i, j: (i, 0))
    ob = pl.BlockSpec((t, LANES), lambda h, i, j: (i, h))
    return _pc(body, grid=(N_MIX_HEADS, nb, nb),
               in_specs=[qb(0), qb(1), tab, tab, kb(0), kb(1), pl.BlockSpec((t, LANES), lambda h, i, j: (jnp.minimum(j, i), 0))],
               out_specs=[ob, ob], out_shape=[_sds((s, MIX_WIDTH)), _sds((s, MIX_WIDTH))], name=name,
               scratch=[pltpu.VMEM((t, LANES), MXU_DTYPE), pltpu.VMEM((t, LANES), MXU_DTYPE),
                        pltpu.VMEM((t, LANES), F32), pltpu.VMEM((t, LANES), F32), pltpu.VMEM((t, LANES), F32)],
               )(q, q, cs, sn, kv, kv, krr)


def _flash_bwd_q(q, kv, krr, cs, sn, o, lse, dcat, name):
    s = q.shape[0]
    t, nb = _att_tiles(s)

    def body(qn_ref, qr_ref, cs_ref, sn_ref, o_ref, lse_ref, do_ref, kn_ref, v_ref, kr_ref, dq_ref,
             qn_s, qr_s, dl_s, an_s, ar_s):
        i, j = pl.program_id(1), pl.program_id(2)

        @pl.when(j == 0)
        def _():
            qn_s[...] = qn_ref[...].astype(MXU_DTYPE)
            qr_s[...] = _rope(qr_ref[...], cs_ref[...], sn_ref[...]).astype(MXU_DTYPE)
            dl_s[...] = jnp.sum(o_ref[...] * do_ref[...], -1, keepdims=True) * jnp.ones((1, LANES), F32)
            an_s[...] = jnp.zeros_like(an_s)
            ar_s[...] = jnp.zeros_like(ar_s)

        @pl.when(j <= i)
        def _():
            sc = (_dot(qn_s[...], kn_ref[...], NT) + _dot(qr_s[...], kr_ref[...], NT)) * ATT_SCALE
            p = jnp.where(_causal_mask(i, j, t), jnp.exp(sc - lse_ref[...][:, :1]), 0.0)
            dp = _dot(do_ref[...], v_ref[...], NT)
            ds = p * (dp - dl_s[...][:, :1]) * ATT_SCALE
            an_s[...] += _dot(ds, kn_ref[...], NN)
            ar_s[...] += _dot(ds, kr_ref[...], NN)

        @pl.when(j == i)
        def _():
            dq_ref[:, :LANES] = an_s[...]
            dq_ref[:, LANES:] = _rope_t(ar_s[...], cs_ref[...], sn_ref[...])

    qb = lambda c: pl.BlockSpec((t, LANES), lambda h, i, j: (i, 2 * h + c))
    kb = lambda c: pl.BlockSpec((t, LANES), lambda h, i, j: (jnp.minimum(j, i), 2 * h + c))
    tab = pl.BlockSpec((t, LANES), lambda h, i, j: (i, 0))
    ob = pl.BlockSpec((t, LANES), lambda h, i, j: (i, h))
    return _pc(body, grid=(N_MIX_HEADS, nb, nb),
               in_specs=[qb(0), qb(1), tab, tab, ob, ob, ob, kb(0), kb(1),
                         pl.BlockSpec((t, LANES), lambda h, i, j: (jnp.minimum(j, i), 0))],
               out_specs=pl.BlockSpec((t, Q_HEAD_P), lambda h, i, j: (i, h)),
               out_shape=_sds((s, N_MIX_HEADS * Q_HEAD_P)), name=name,
               scratch=[pltpu.VMEM((t, LANES), MXU_DTYPE), pltpu.VMEM((t, LANES), MXU_DTYPE),
                        pltpu.VMEM((t, LANES), F32), pltpu.VMEM((t, LANES), F32), pltpu.VMEM((t, LANES), F32)],
               )(q, q, cs, sn, o, lse, dcat, kv, kv, krr)


def _flash_bwd_kv(q, kv, krr, cs, sn, o, lse, dcat, name):
    s = q.shape[0]
    t, nb = _att_tiles(s)

    def body(kn_ref, v_ref, kr_ref, qn_ref, qr_ref, cs_ref, sn_ref, o_ref, lse_ref, do_ref, dkv_ref, dkr_ref,
             akn_s, av_s, akr_s):
        j, i = pl.program_id(1), pl.program_id(2)

        @pl.when(i == 0)
        def _():
            akn_s[...] = jnp.zeros_like(akn_s)
            av_s[...] = jnp.zeros_like(av_s)
            akr_s[...] = jnp.zeros_like(akr_s)

        @pl.when(i >= j)
        def _():
            qn = qn_ref[...]
            qr = _rope(qr_ref[...], cs_ref[...], sn_ref[...])
            do = do_ref[...]
            sc = (_dot(qn, kn_ref[...], NT) + _dot(qr, kr_ref[...], NT)) * ATT_SCALE
            p = jnp.where(_causal_mask(i, j, t), jnp.exp(sc - lse_ref[...][:, :1]), 0.0)
            dp = _dot(do, v_ref[...], NT)
            dl = jnp.sum(o_ref[...] * do, -1, keepdims=True)
            ds = p * (dp - dl) * ATT_SCALE
            av_s[...] += _dot(p, do, TN)
            akn_s[...] += _dot(ds, qn, TN)
            akr_s[...] += _dot(ds, qr, TN)

        @pl.when(i == nb - 1)
        def _():
            dkv_ref[:, :LANES] = akn_s[...]
            dkv_ref[:, LANES:] = av_s[...]
            dkr_ref[0] = akr_s[...]

    kb = lambda c: pl.BlockSpec((t, LANES), lambda h, j, i: (j, 2 * h + c))
    qb = lambda c: pl.BlockSpec((t, LANES), lambda h, j, i: (jnp.maximum(i, j), 2 * h + c))
    tab = pl.BlockSpec((t, LANES), lambda h, j, i: (jnp.maximum(i, j), 0))
    ob = pl.BlockSpec((t, LANES), lambda h, j, i: (jnp.maximum(i, j), h))
    return _pc(body, grid=(N_MIX_HEADS, nb, nb),
               in_specs=[kb(0), kb(1), pl.BlockSpec((t, LANES), lambda h, j, i: (j, 0)), qb(0), qb(1), tab, tab, ob, ob, ob],
               out_specs=[pl.BlockSpec((t, Q_HEAD_P), lambda h, j, i: (j, h)),
                          pl.BlockSpec((1, t, LANES), lambda h, j, i: (h, j, 0))],
               out_shape=[_sds((s, N_MIX_HEADS * Q_HEAD_P)), _sds((N_MIX_HEADS, s, LANES))], name=name,
               scratch=[pltpu.VMEM((t, LANES), F32)] * 3,
               )(kv, kv, krr, q, q, cs, sn, o, lse, dcat)


def _mem_head(ops, qh, kh, vh):
    sc = ops.nt(qh, kh) * HEAD_DIM ** -0.5
    e = jnp.exp(sc - lax.stop_gradient(jnp.max(sc, -1, keepdims=True)))
    p = e / jnp.sum(e, -1, keepdims=True)
    return ops.nn(p, vh)


def _mem_fwd(h, qcol, mem_kv, name):
    s = h.shape[0]
    m = mem_kv.shape[0]
    ts = _pick(s, (512, 256, 128))

    def body(q_ref, kv_ref, o_ref):
        for hh in range(N_MEM_HEADS):
            sl = slice(hh * HEAD_DIM, (hh + 1) * HEAD_DIM)
            vsl = slice(MEM_WIDTH + hh * HEAD_DIM, MEM_WIDTH + (hh + 1) * HEAD_DIM)
            o_ref[:, sl] = _mem_head(_RawOps, q_ref[:, sl], kv_ref[:, sl], kv_ref[:, vsl])

    return _pc(body, grid=(s // ts,), in_specs=[_rows(ts, MEM_WIDTH, qcol), _whole((m, 2 * MEM_WIDTH))],
               out_specs=_rows(ts, MEM_WIDTH), out_shape=_sds((s, MEM_WIDTH)), name=name)(h, mem_kv)


def _mem_bwd(h, qcol, mem_kv, dcat, name):
    s = h.shape[0]
    m = mem_kv.shape[0]
    ts = _pick(s, (512, 256, 128))

    def body(q_ref, kv_ref, do_ref, dq_ref, dkv_ref):
        @pl.when(pl.program_id(0) == 0)
        def _():
            dkv_ref[...] = jnp.zeros_like(dkv_ref)

        for hh in range(N_MEM_HEADS):
            sl = slice(hh * HEAD_DIM, (hh + 1) * HEAD_DIM)
            vsl = slice(MEM_WIDTH + hh * HEAD_DIM, MEM_WIDTH + (hh + 1) * HEAD_DIM)
            _, vjp = jax.vjp(functools.partial(_mem_head, _DiffOps), q_ref[:, sl], kv_ref[:, sl], kv_ref[:, vsl])
            dq, dk, dv = vjp(do_ref[:, sl])
            dq_ref[:, sl] = dq
            dkv_ref[:, sl] += dk
            dkv_ref[:, vsl] += dv

    return _pc(body, grid=(s // ts,),
               in_specs=[_rows(ts, MEM_WIDTH, qcol), _whole((m, 2 * MEM_WIDTH)), _rows(ts, MEM_WIDTH, 3)],
               out_specs=[_rows(ts, MEM_WIDTH), _whole((m, 2 * MEM_WIDTH))],
               out_shape=[_sds((s, MEM_WIDTH)), _sds((m, 2 * MEM_WIDTH))], name=name)(h, mem_kv, dcat)


CONV_COLS = 3 * MIX_WIDTH
HALO = 8


def _conv_fwd(h, w, name):
    s = h.shape[0]
    ts = _pick(s, (512, 256, 128))
    wc = 512

    def body(x_ref, halo_ref, w_ref, o_ref, ext):
        i = pl.program_id(0)
        ext[pl.ds(0, HALO), :] = jnp.where(i > 0, halo_ref[...], 0.0)
        ext[pl.ds(HALO, ts), :] = x_ref[...]
        acc = w_ref[0:1, :] * ext[pl.ds(HALO - 3, ts), :]
        for j in range(1, CONV_WIDTH):
            acc = acc + w_ref[j:j + 1, :] * ext[pl.ds(HALO - 3 + j, ts), :]
        o_ref[...] = acc

    halo = pl.BlockSpec((HALO, wc), lambda i, c: (jnp.maximum(i * (ts // HALO) - 1, 0), c))
    blk = pl.BlockSpec((ts, wc), lambda i, c: (i, c))
    return _pc(body, grid=(s // ts, CONV_COLS // wc),
               in_specs=[blk, halo, pl.BlockSpec((CONV_WIDTH, wc), lambda i, c: (0, c))],
               out_specs=blk, out_shape=_sds((s, CONV_COLS)), name=name,
               scratch=[pltpu.VMEM((HALO + ts, wc), F32)])(h, h, w)


def _conv_bwd(h, w, dout, name):
    s = h.shape[0]
    ts = _pick(s, (512, 256, 128))
    nt = s // ts
    wc = 512

    def body(x_ref, xhalo_ref, d_ref, dhalo_ref, w_ref, dx_ref, dw_ref, xext, dext):
        i = pl.program_id(1)

        @pl.when(i == 0)
        def _():
            dw_ref[...] = jnp.zeros_like(dw_ref)

        xext[pl.ds(0, HALO), :] = jnp.where(i > 0, xhalo_ref[...], 0.0)
        xext[pl.ds(HALO, ts), :] = x_ref[...]
        dext[pl.ds(0, ts), :] = d_ref[...]
        dext[pl.ds(ts, HALO), :] = jnp.where(i < nt - 1, dhalo_ref[...], 0.0)
        d = d_ref[...]
        acc = w_ref[CONV_WIDTH - 1:CONV_WIDTH, :] * d
        for j in range(CONV_WIDTH - 1):
            acc = acc + w_ref[j:j + 1, :] * dext[pl.ds(3 - j, ts), :]
        dx_ref[...] = acc
        for j in range(CONV_WIDTH):
            dw_ref[j:j + 1, :] += jnp.sum(d * xext[pl.ds(HALO - 3 + j, ts), :], 0, keepdims=True)

    blk = pl.BlockSpec((ts, wc), lambda c, i: (i, c))
    halo_prev = pl.BlockSpec((HALO, wc), lambda c, i: (jnp.maximum(i * (ts // HALO) - 1, 0), c))
    halo_next = pl.BlockSpec((HALO, wc), lambda c, i: (jnp.minimum((i + 1) * (ts // HALO), s // HALO - 1), c))
    wspec = pl.BlockSpec((CONV_WIDTH, wc), lambda c, i: (0, c))
    return _pc(body, grid=(CONV_COLS // wc, nt), in_specs=[blk, halo_prev, blk, halo_next, wspec],
               out_specs=[blk, wspec], out_shape=[_sds((s, CONV_COLS)), _sds((CONV_WIDTH, CONV_COLS))], name=name,
               scratch=[pltpu.VMEM((HALO + ts, wc), F32), pltpu.VMEM((ts + HALO, wc), F32)])(h, h, dout, dout, w)


def _silu(v):
    return v * jax.nn.sigmoid(v)


def _softplus(v):
    return jnp.maximum(v, 0.0) + jnp.log1p(jnp.exp(-jnp.abs(v)))


def _gdn_prep_head(cq, ck, cv, a, b, alog, dtb):
    q = _silu(cq)
    q = q * lax.rsqrt(jnp.sum(q * q, -1, keepdims=True) + 1e-6) * HEAD_DIM ** -0.5
    k = _silu(ck)
    k = k * lax.rsqrt(jnp.sum(k * k, -1, keepdims=True) + 1e-6)
    v = _silu(cv)
    g = -jnp.exp(alog) * _softplus(a + dtb)
    beta = jax.nn.sigmoid(b)
    ones = jnp.ones((1, HEAD_DIM), F32)
    return q, k, v, g * ones, beta * ones


def _onehot_lane(idx):
    return (lax.broadcasted_iota(jnp.int32, (1, LANES), 1) == idx).astype(F32)


def _lane_pick(row, idx):
    return jnp.sum(row * _onehot_lane(idx), -1, keepdims=True)


def _gdn_prep_fwd(conv, h, alog, dtb, name):
    s = h.shape[0]
    ts = _pick(s, (128,))
    w = MIX_WIDTH

    def body(c_ref, ab_ref, alog_ref, dtb_ref, q_ref, k_ref, v_ref, g_ref, b_ref):
        ab = ab_ref[...]
        for hh in range(N_MIX_HEADS):
            sl = slice(hh * HEAD_DIM, (hh + 1) * HEAD_DIM)
            cols = [slice(p * w + hh * HEAD_DIM, p * w + (hh + 1) * HEAD_DIM) for p in range(3)]
            outs = _gdn_prep_head(c_ref[:, cols[0]], c_ref[:, cols[1]], c_ref[:, cols[2]], _lane_pick(ab, hh),
                                  _lane_pick(ab, N_MIX_HEADS + hh), _lane_pick(alog_ref[...], hh),
                                  _lane_pick(dtb_ref[...], hh))
            for ref, val in zip((q_ref, k_ref, v_ref, g_ref, b_ref), outs):
                ref[:, sl] = val

    return _pc(body, grid=(s // ts,),
               in_specs=[_rows(ts, 3 * w), _rows(ts, LANES, (4 * w + MEM_WIDTH) // LANES), _whole((1, LANES)), _whole((1, LANES))],
               out_specs=[_rows(ts, w)] * 5, out_shape=[_sds((s, w))] * 5, name=name)(conv, h, alog, dtb)


def _gdn_prep_bwd(conv, h, alog, dtb, dq, dk, dv, dg, db, name):
    s = h.shape[0]
    ts = _pick(s, (128,))
    w = MIX_WIDTH

    def body(c_ref, ab_ref, alog_ref, dtb_ref, dq_ref, dk_ref, dv_ref, dg_ref, db_ref,
             dc_ref, dab_ref, dalog_ref, ddtb_ref):
        @pl.when(pl.program_id(0) == 0)
        def _():
            dalog_ref[...] = jnp.zeros_like(dalog_ref)
            ddtb_ref[...] = jnp.zeros_like(ddtb_ref)

        ab = ab_ref[...]
        dab = jnp.zeros_like(ab)
        dalog = jnp.zeros((1, LANES), F32)
        ddtb = jnp.zeros((1, LANES), F32)
        for hh in range(N_MIX_HEADS):
            sl = slice(hh * HEAD_DIM, (hh + 1) * HEAD_DIM)
            cols = [slice(p * w + hh * HEAD_DIM, p * w + (hh + 1) * HEAD_DIM) for p in range(3)]
            _, vjp = jax.vjp(_gdn_prep_head, c_ref[:, cols[0]], c_ref[:, cols[1]], c_ref[:, cols[2]], _lane_pick(ab, hh),
                             _lane_pick(ab, N_MIX_HEADS + hh), _lane_pick(alog_ref[...], hh), _lane_pick(dtb_ref[...], hh))
            dcq, dck, dcv, da, dbb, dal, ddt = vjp((dq_ref[:, sl], dk_ref[:, sl], dv_ref[:, sl], dg_ref[:, sl], db_ref[:, sl]))
            dc_ref[:, cols[0]] = dcq
            dc_ref[:, cols[1]] = dck
            dc_ref[:, cols[2]] = dcv
            dab = dab + da * _onehot_lane(hh) + dbb * _onehot_lane(N_MIX_HEADS + hh)
            dalog = dalog + dal * _onehot_lane(hh)
            ddtb = ddtb + ddt * _onehot_lane(hh)
        dab_ref[...] = dab
        dalog_ref[...] += dalog
        ddtb_ref[...] += ddtb

    return _pc(body, grid=(s // ts,),
               in_specs=[_rows(ts, 3 * w), _rows(ts, LANES, (4 * w + MEM_WIDTH) // LANES),
                         _whole((1, LANES)), _whole((1, LANES))] + [_rows(ts, w)] * 5,
               out_specs=[_rows(ts, 3 * w), _rows(ts, LANES), _whole((1, LANES)), _whole((1, LANES))],
               out_shape=[_sds((s, 3 * w)), _sds((s, LANES)), _sds((1, LANES)), _sds((1, LANES))],
               name=name)(conv, h, alog, dtb, dq, dk, dv, dg, db)


def _gdn_chunk(ops, state, q, k, v, gb, bb):
    c = CHUNK
    row = lax.broadcasted_iota(jnp.int32, (c, c), 0)
    col = lax.broadcasted_iota(jnp.int32, (c, c), 1)
    tril = row >= col
    strict = row > col
    eye = (row == col).astype(F32)
    gc = ops.nn(tril.astype(F32), gb, True)
    avg = jnp.full((c, HEAD_DIM), 1.0 / HEAD_DIM, F32)
    gci = ops.nt(gc, avg, True)
    gcj = ops.nt(avg, gc, True)
    decay = jnp.where(tril, jnp.exp(jnp.where(tril, gci - gcj, 0.0)), 0.0)
    kb = k * bb
    low = jnp.where(strict, ops.nt(kb, k) * decay, 0.0)
    inv = eye - low
    pw = low
    for _ in range(5):
        pw = ops.nn(pw, pw, True)
        inv = inv + ops.nn(inv, pw, True)
    ge = jnp.exp(gc)
    u = ops.nn(inv, v * bb, True)
    w = ops.nn(inv, kb * ge, True)
    a_qk = jnp.where(tril, ops.nt(q, k) * decay, 0.0)
    g_tot = jnp.sum(gb, 0, keepdims=True)
    k_dec = k * jnp.exp(g_tot - gc)
    v_new = u - ops.nn(w, state)
    o = ops.nn(q * ge, state) + ops.nn(a_qk, v_new)
    new_state = state * jnp.exp(g_tot) + ops.tn(k_dec, v_new)
    return new_state, o


GDN_HEADS_PER_STEP = 4


def _gdn_core_fwd(q, k, v, gb, bb, name):
    s = q.shape[0]
    nc = s // CHUNK
    hp = GDN_HEADS_PER_STEP
    w = hp * HEAD_DIM

    def body(q_ref, k_ref, v_ref, g_ref, b_ref, o_ref, st_ref, state):
        @pl.when(pl.program_id(1) == 0)
        def _():
            state[...] = jnp.zeros_like(state)

        for hh in range(hp):
            sl = slice(hh * HEAD_DIM, (hh + 1) * HEAD_DIM)
            st = state[hh]
            st_ref[0, hh] = st
            new_state, o = _gdn_chunk(_RawOps, st, q_ref[:, sl], k_ref[:, sl], v_ref[:, sl], g_ref[:, sl], b_ref[:, sl])
            state[hh] = new_state
            o_ref[:, sl] = o

    blk = pl.BlockSpec((CHUNK, w), lambda hg, c: (c, hg))
    return _pc(body, grid=(N_MIX_HEADS // hp, nc), in_specs=[blk] * 5,
               out_specs=[blk, pl.BlockSpec((1, hp, HEAD_DIM, HEAD_DIM), lambda hg, c: (c, hg, 0, 0))],
               out_shape=[_sds((s, MIX_WIDTH)), _sds((nc, N_MIX_HEADS, HEAD_DIM, HEAD_DIM))], name=name,
               scratch=[pltpu.VMEM((hp, HEAD_DIM, HEAD_DIM), F32)])(q, k, v, gb, bb)


def _gdn_core_bwd(q, k, v, gb, bb, states, do, name):
    s = q.shape[0]
    nc = s // CHUNK
    hp = GDN_HEADS_PER_STEP
    w = hp * HEAD_DIM

    def body(q_ref, k_ref, v_ref, g_ref, b_ref, st_ref, do_ref, dq_ref, dk_ref, dv_ref, dg_ref, db_ref, dstate):
        @pl.when(pl.program_id(1) == 0)
        def _():
            dstate[...] = jnp.zeros_like(dstate)

        for hh in range(hp):
            sl = slice(hh * HEAD_DIM, (hh + 1) * HEAD_DIM)
            _, vjp = jax.vjp(functools.partial(_gdn_chunk, _DiffOps), st_ref[0, hh], q_ref[:, sl], k_ref[:, sl],
                             v_ref[:, sl], g_ref[:, sl], b_ref[:, sl])
            dst, dq, dk, dv, dg, db = vjp((dstate[hh], do_ref[:, sl]))
            dstate[hh] = dst
            for ref, val in zip((dq_ref, dk_ref, dv_ref, dg_ref, db_ref), (dq, dk, dv, dg, db)):
                ref[:, sl] = val

    blk = pl.BlockSpec((CHUNK, w), lambda hg, c: (nc - 1 - c, hg))
    return _pc(body, grid=(N_MIX_HEADS // hp, nc),
               in_specs=[blk] * 5 + [pl.BlockSpec((1, hp, HEAD_DIM, HEAD_DIM), lambda hg, c: (nc - 1 - c, hg, 0, 0)), blk],
               out_specs=[blk] * 5, out_shape=[_sds((s, MIX_WIDTH))] * 5, name=name,
               scratch=[pltpu.VMEM((hp, HEAD_DIM, HEAD_DIM), F32)])(q, k, v, gb, bb, states, do)


def _gdn_out_head(o, z, g):
    return _rms(o, g) * _silu(z)


def _gdn_out_fwd(o, h, onorm, name):
    s = o.shape[0]
    ts = _pick(s, (256, 128))

    def body(o_ref, z_ref, g_ref, y_ref):
        for hh in range(N_MIX_HEADS):
            sl = slice(hh * HEAD_DIM, (hh + 1) * HEAD_DIM)
            y_ref[:, sl] = _gdn_out_head(o_ref[:, sl], z_ref[:, sl], g_ref[...])

    return _pc(body, grid=(s // ts,), in_specs=[_rows(ts, MIX_WIDTH), _rows(ts, MIX_WIDTH, 3), _whole((1, HEAD_DIM))],
               out_specs=_rows(ts, MIX_WIDTH), out_shape=_sds((s, MIX_WIDTH)), name=name)(o, h, onorm)


def _gdn_out_bwd(o, h, onorm, dcat, name):
    s = o.shape[0]
    ts = _pick(s, (256, 128))

    def body(o_ref, z_ref, g_ref, dy_ref, do_ref, dz_ref, dg_ref):
        @pl.when(pl.program_id(0) == 0)
        def _():
            dg_ref[...] = jnp.zeros_like(dg_ref)

        dgs = jnp.zeros((1, HEAD_DIM), F32)
        for hh in range(N_MIX_HEADS):
            sl = slice(hh * HEAD_DIM, (hh + 1) * HEAD_DIM)
            _, vjp = jax.vjp(_gdn_out_head, o_ref[:, sl], z_ref[:, sl], g_ref[...])
            do, dz, dg = vjp(dy_ref[:, sl])
            do_ref[:, sl] = do
            dz_ref[:, sl] = dz
            dgs = dgs + dg
        dg_ref[...] += dgs

    return _pc(body, grid=(s // ts,),
               in_specs=[_rows(ts, MIX_WIDTH), _rows(ts, MIX_WIDTH, 3), _whole((1, HEAD_DIM)), _rows(ts, MIX_WIDTH, 0)],
               out_specs=[_rows(ts, MIX_WIDTH), _rows(ts, MIX_WIDTH), _whole((1, HEAD_DIM))],
               out_shape=[_sds((s, MIX_WIDTH)), _sds((s, MIX_WIDTH)), _sds((1, HEAD_DIM))], name=name)(o, h, onorm, dcat)


def _row(v):
    return v.reshape(1, -1)


def _lane_row(v):
    return jnp.pad(v, (0, LANES - v.shape[0])).reshape(1, LANES)


def _rope_tables(positions):
    inv_freq = 1.0 / (ROPE_THETA ** (jnp.arange(0, QK_ROPE, 2, dtype=F32) / QK_ROPE))
    ang = positions.astype(F32)[:, None] * inv_freq
    cos, sin = jnp.cos(ang), jnp.sin(ang)
    z = jnp.zeros_like(cos)
    return jnp.concatenate([cos, z, cos, z], 1), jnp.concatenate([-sin, z, sin, z], 1)


def _local_step(x, mem, positions, loss_target, wts, small):
    cs, sn = _rope_tables(positions)
    saved = []
    for i in range(DEPTH):
        j = i // 2
        sv = {"x": x}
        sv["mem_kv"] = _mm(mem, wts["mem_w_kv"][i], name=f"l{i}_memkv")
        if i % 2 == 0:
            h = _mm(x, wts["mla_w_in"][j], name=f"l{i}_in")
            cqn, ckvn, krr = _mla_prep_fwd(h, _row(small["mla_q_norm"][j]), _row(small["mla_kv_norm"][j]), cs, sn, f"l{i}_mlaprep")
            q = _mm(cqn, wts["mla_w_uq"][j], name=f"l{i}_uq")
            kv = _mm(ckvn, wts["mla_w_ukv"][j], name=f"l{i}_ukv")
            mix, lse = _flash_fwd(q, kv, krr, cs, sn, f"l{i}_flash")
            sv.update(cqn=cqn, ckvn=ckvn, krr=krr, q=q, kv=kv, o=mix, lse=lse)
            qcol = 2
        else:
            h = _mm(x, wts["gdn_w_in"][j], name=f"l{i}_in")
            conv = _conv_fwd(h, small["gdn_conv"][j], f"l{i}_conv")
            qn, kn, vv, gb, bb = _gdn_prep_fwd(conv, h, _lane_row(small["gdn_a_log"][j]), _lane_row(small["gdn_dt_bias"][j]), f"l{i}_gdnprep")
            o, states = _gdn_core_fwd(qn, kn, vv, gb, bb, f"l{i}_gdncore")
            mix = _gdn_out_fwd(o, h, _row(small["gdn_o_norm"][j]), f"l{i}_gdnout")
            sv.update(conv=conv, qn=qn, kn=kn, vv=vv, gb=gb, bb=bb, o=o, states=states)
            qcol = 4 * MIX_WIDTH // MEM_WIDTH
        mem_o = _mem_fwd(h, qcol, sv["mem_kv"], f"l{i}_mem")
        cat = jnp.concatenate([mix, mem_o], 1)
        y = _mm(cat, wts["w_out"][i], name=f"l{i}_out")
        x1 = _ln_fwd(x, y, _row(small["ln1_g"][i]), _row(small["ln1_b"][i]), f"l{i}_ln1")
        h1 = _mm(x1, wts["mlp_w1"][i], name=f"l{i}_w1")
        ff = _mm(h1, wts["mlp_w2"][i], a_fn=_relu2, name=f"l{i}_w2")
        x2 = _ln_fwd(x1, ff, _row(small["ln2_g"][i]), _row(small["ln2_b"][i]), f"l{i}_ln2")
        sv.update(h=h, qcol=qcol, cat=cat, y=y, x1=x1, h1=h1, ff=ff)
        saved.append(sv)
        x = x2

    loss_row, dx = _loss_and_grad(x, loss_target, "loss")

    gw = {k: [None] * len(v) for k, v in wts.items()}
    gs = {k: [None] * v.shape[0] for k, v in small.items()}
    gdt = COMM_DTYPE
    for i in reversed(range(DEPTH)):
        j = i // 2
        sv = saved[i]
        dr2, dg, db = _ln_bwd(sv["x1"], sv["ff"], _row(small["ln2_g"][i]), _row(small["ln2_b"][i]), dx, f"l{i}_ln2b")
        gs["ln2_g"][i], gs["ln2_b"][i] = dg[0], db[0]
        dh1 = _mm(dr2, wts["mlp_w2"][i], tb=True, epi=_relu2_bwd, e=sv["h1"], name=f"l{i}_dh1")
        gw["mlp_w2"][i] = _mm(sv["h1"], dr2, ta=True, a_fn=_relu2, out_dtype=gdt, name=f"l{i}_dw2")
        gw["mlp_w1"][i] = _mm(sv["x1"], dh1, ta=True, out_dtype=gdt, name=f"l{i}_dw1")
        dx1 = _mm(dh1, wts["mlp_w1"][i], tb=True, epi=_add_alpha, e=dr2, name=f"l{i}_dx1")
        dr1, dg, db = _ln_bwd(sv["x"], sv["y"], _row(small["ln1_g"][i]), _row(small["ln1_b"][i]), dx1, f"l{i}_ln1b")
        gs["ln1_g"][i], gs["ln1_b"][i] = dg[0], db[0]
        gw["w_out"][i] = _mm(sv["cat"], dr1, ta=True, out_dtype=gdt, name=f"l{i}_dwout")
        dcat = _mm(dr1, wts["w_out"][i], tb=True, name=f"l{i}_dcat")
        h = sv["h"]
        dqmem, dmem_kv = _mem_bwd(h, sv["qcol"], sv["mem_kv"], dcat, f"l{i}_memb")
        gw["mem_w_kv"][i] = _mm(mem, dmem_kv, ta=True, out_dtype=gdt, name=f"l{i}_dwmem")
        zpad = jnp.zeros((h.shape[0], LANES), F32)
        if i % 2 == 0:
            qnw, kvnw = _row(small["mla_q_norm"][j]), _row(small["mla_kv_norm"][j])
            dq = _flash_bwd_q(sv["q"], sv["kv"], sv["krr"], cs, sn, sv["o"], sv["lse"], dcat, f"l{i}_flashbq")
            dkv, dkrr = _flash_bwd_kv(sv["q"], sv["kv"], sv["krr"], cs, sn, sv["o"], sv["lse"], dcat, f"l{i}_flashbkv")
            gw["mla_w_ukv"][j] = _mm(sv["ckvn"], dkv, ta=True, out_dtype=gdt, name=f"l{i}_dwukv")
            dckvn = _mm(dkv, wts["mla_w_ukv"][j], tb=True, name=f"l{i}_dckvn")
            gw["mla_w_uq"][j] = _mm(sv["cqn"], dq, ta=True, out_dtype=gdt, name=f"l{i}_dwuq")
            dcqn = _mm(dq, wts["mla_w_uq"][j], tb=True, name=f"l{i}_dcqn")
            dcq, dckv, dkr, dqn, dkvn = _mla_prep_bwd(h, qnw, kvnw, cs, sn, dcqn, dckvn, dkrr, f"l{i}_mlaprepb")
            gs["mla_q_norm"][j], gs["mla_kv_norm"][j] = dqn[0], dkvn[0]
            dh = jnp.concatenate([dcq, dckv, dqmem, dkr, zpad], 1)
            w_in = wts["mla_w_in"][j]
            key = "mla_w_in"
        else:
            alog, dtb = _lane_row(small["gdn_a_log"][j]), _lane_row(small["gdn_dt_bias"][j])
            do, dz, dgn = _gdn_out_bwd(sv["o"], h, _row(small["gdn_o_norm"][j]), dcat, f"l{i}_gdnoutb")
            gs["gdn_o_norm"][j] = dgn[0]
            dqn, dkn, dvv, dgb, dbb = _gdn_core_bwd(sv["qn"], sv["kn"], sv["vv"], sv["gb"], sv["bb"], sv["states"], do, f"l{i}_gdncoreb")
            dconv, dab, dalog, ddtb = _gdn_prep_bwd(sv["conv"], h, alog, dtb, dqn, dkn, dvv, dgb, dbb, f"l{i}_gdnprepb")
            gs["gdn_a_log"][j], gs["gdn_dt_bias"][j] = dalog[0, :N_MIX_HEADS], ddtb[0, :N_MIX_HEADS]
            dhqkv, dconvw = _conv_bwd(h, small["gdn_conv"][j], dconv, f"l{i}_convb")
            gs["gdn_conv"][j] = dconvw
            dh = jnp.concatenate([dhqkv, dz, dqmem, dab, zpad], 1)
            w_in = wts["gdn_w_in"][j]
            key = "gdn_w_in"
        gw[key][j] = _mm(sv["x"], dh, ta=True, out_dtype=gdt, name=f"l{i}_dwin")
        dx = _mm(dh, w_in, tb=True, epi=_add_alpha, e=dr1, name=f"l{i}_dx")
    gs = {k: jnp.stack(v) for k, v in gs.items()}
    return loss_row, dx, gw, gs


def _mla_in_to_kernel(w):
    z32 = jnp.zeros((w.shape[0], 32), w.dtype)
    z128 = jnp.zeros((w.shape[0], LANES), w.dtype)
    return jnp.concatenate([w[:, :1024], w[:, 1088:1600], w[:, 1024:1056], z32, w[:, 1056:1088], z32, z128], 1)


def _mla_in_from_kernel(g):
    return jnp.concatenate([g[:, :1024], g[:, 1536:1568], g[:, 1600:1632], g[:, 1024:1536]], 1)


def _uq_to_kernel(w):
    w3 = w.reshape(Q_LORA, N_MIX_HEADS, QK_NOPE + QK_ROPE)
    z = jnp.zeros((Q_LORA, N_MIX_HEADS, 32), w.dtype)
    return jnp.concatenate([w3[:, :, :128], w3[:, :, 128:160], z, w3[:, :, 160:192], z], 2).reshape(Q_LORA, N_MIX_HEADS * Q_HEAD_P)


def _uq_from_kernel(g):
    g3 = g.reshape(Q_LORA, N_MIX_HEADS, Q_HEAD_P)
    return jnp.concatenate([g3[:, :, :128], g3[:, :, 128:160], g3[:, :, 192:224]], 2).reshape(Q_LORA, -1)


def _gdn_in_to_kernel(w):
    z = jnp.zeros((w.shape[0], LANES - 2 * N_MIX_HEADS + LANES), w.dtype)
    return jnp.concatenate([w[:, :6144], w[:, 6168:6680], w[:, 6144:6168], z], 1)


def _gdn_in_from_kernel(g):
    return jnp.concatenate([g[:, :6144], g[:, 6656:6680], g[:, 6144:6656]], 1)


def _exchange(src, name, gather):
    shp = src.shape if gather else src.shape[1:]

    def body(src_ref, out_ref, send_sems, recv_sems, local_sem):
        x, y, c = lax.axis_index("x"), lax.axis_index("y"), lax.axis_index("c")
        me = 4 * x + 2 * y + c
        mine = pltpu.make_async_copy(src_ref if gather else src_ref.at[me], out_ref.at[me], local_sem)
        mine.start()
        sends, recvs = [], []
        for kk in range(1, N_DEV):
            px, py, pc = x ^ ((kk >> 2) & 1), y ^ ((kk >> 1) & 1), c ^ (kk & 1)
            peer = 4 * px + 2 * py + pc
            piece = src_ref if gather else src_ref.at[peer]
            sends.append(pltpu.make_async_remote_copy(
                src_ref=piece, dst_ref=out_ref.at[me], send_sem=send_sems.at[kk - 1], recv_sem=recv_sems.at[kk - 1],
                device_id=(px, py, pc), device_id_type=pl.DeviceIdType.MESH))
            recvs.append(pltpu.make_async_remote_copy(
                src_ref=piece, dst_ref=out_ref.at[peer], send_sem=send_sems.at[kk - 1], recv_sem=recv_sems.at[kk - 1],
                device_id=(px, py, pc), device_id_type=pl.DeviceIdType.MESH))
        for cp in sends:
            cp.start()
        for cp in recvs:
            cp.wait_recv()
        for cp in sends:
            cp.wait_send()
        mine.wait()

    hbm = pl.BlockSpec(memory_space=pltpu.HBM)
    return pl.pallas_call(
        body, in_specs=[hbm], out_specs=hbm, out_shape=_sds((N_DEV,) + tuple(shp), src.dtype), name=name,
        scratch_shapes=[pltpu.SemaphoreType.DMA((N_DEV - 1,)), pltpu.SemaphoreType.DMA((N_DEV - 1,)), pltpu.SemaphoreType.DMA],
    )(src)


def _adamw(parts, w, m, v, name):
    r, c = w.shape
    tr = _pick(r, (256, 128, 64, 32, 16, 8))

    def body(p_ref, w_ref, m_ref, v_ref, g_ref, d_ref, nm_ref, nv_ref):
        g = p_ref[0].astype(F32)
        for dd in range(1, N_DEV):
            g = g + p_ref[dd].astype(F32)
        nm = ADAM_B1 * m_ref[...] + (1.0 - ADAM_B1) * g
        nv = ADAM_B2 * v_ref[...] + (1.0 - ADAM_B2) * jnp.square(g)
        m_hat = nm / (1.0 - ADAM_B1 ** ADAM_STEP)
        v_hat = nv / (1.0 - ADAM_B2 ** ADAM_STEP)
        g_ref[...] = g
        d_ref[...] = -ADAM_LR * (m_hat / (jnp.sqrt(v_hat) + ADAM_EPS) + ADAM_WD * w_ref[...])
        nm_ref[...] = nm
        nv_ref[...] = nv

    blk = pl.BlockSpec((tr, c), lambda i: (i, 0))
    return _pc(body, grid=(r // tr,), in_specs=[pl.BlockSpec((N_DEV, tr, c), lambda i: (0, i, 0)), blk, blk, blk],
               out_specs=[blk] * 4, out_shape=[_sds((r, c))] * 4, name=name)(parts, w, m, v)


BIG = (("mla_w_in", 1), ("mla_w_uq", 2), ("mla_w_ukv", 2), ("gdn_w_in", 2), ("mem_w_kv", 1), ("w_out", 1),
       ("mlp_w1", 2), ("mlp_w2", 1), ("gdn_conv", 2))
SMALL = ("mla_q_norm", "mla_kv_norm", "gdn_a_log", "gdn_dt_bias", "gdn_o_norm", "ln1_g", "ln1_b", "ln2_g", "ln2_b")
PACK_COLS = 1024


def _unshard(pieces, axis):
    t = jnp.moveaxis(pieces, 0, axis)
    return t.reshape(t.shape[:axis] + (t.shape[axis] * t.shape[axis + 1],) + t.shape[axis + 2:])


def _shard(full, axis):
    t = full.reshape(full.shape[:axis] + (N_DEV, full.shape[axis] // N_DEV) + full.shape[axis + 1:])
    return jnp.moveaxis(t, axis, 0)


def _pack(arrs, lead, cols, mult):
    lead_shape = arrs[0].shape[:lead]
    flat = jnp.concatenate([a.reshape(lead_shape + (-1,)) for a in arrs], -1)
    n = flat.shape[-1]
    r = -(-n // (cols * mult)) * mult
    flat = jnp.pad(flat, [(0, 0)] * lead + [(0, r * cols - n)])
    return flat.reshape(lead_shape + (r, cols))


def _unpack(buf, lead, shapes):
    lead_shape = buf.shape[:lead]
    flat = buf.reshape(lead_shape + (-1,))
    out, off = [], 0
    for shp in shapes:
        n = 1
        for d in shp:
            n *= d
        out.append(flat[..., off:off + n].reshape(lead_shape + tuple(shp)))
        off += n
    return out


def kernel(x, mem, positions, mla_w_in, mla_q_norm, mla_w_uq, mla_kv_norm, mla_w_ukv, gdn_w_in, gdn_conv, gdn_a_log, gdn_dt_bias, gdn_o_norm, mem_w_kv, w_out, ln1_g, ln1_b, mlp_w1, mlp_w2, ln2_g, ln2_b, loss_target, m_mla_w_in, m_mla_q_norm, m_mla_w_uq, m_mla_kv_norm, m_mla_w_ukv, m_gdn_w_in, m_gdn_conv, m_gdn_a_log, m_gdn_dt_bias, m_gdn_o_norm, m_mem_w_kv, m_w_out, m_ln1_g, m_ln1_b, m_mlp_w1, m_mlp_w2, m_ln2_g, m_ln2_b, v_mla_w_in, v_mla_q_norm, v_mla_w_uq, v_mla_kv_norm, v_mla_w_ukv, v_gdn_w_in, v_gdn_conv, v_gdn_a_log, v_gdn_dt_bias, v_gdn_o_norm, v_mem_w_kv, v_w_out, v_ln1_g, v_ln1_b, v_mlp_w1, v_mlp_w2, v_ln2_g, v_ln2_b):
    args = dict(locals())
    w_loc = {n: args[n] for n, _ in BIG}
    m_loc = {n: args["m_" + n] for n, _ in BIG}
    v_loc = {n: args["v_" + n] for n, _ in BIG}
    small = {n: args[n] for n in SMALL}
    axis_of = dict(BIG)
    mm_names = [n for n, _ in BIG if n != "gdn_conv"]

    sent = _pack([w_loc[n].astype(COMM_DTYPE) for n in mm_names], 0, PACK_COLS, 16)
    got = _exchange(sent, "gather_weights", gather=True)
    full = dict(zip(mm_names, (_unshard(p, axis_of[n]) for n, p in
                               zip(mm_names, _unpack(got, 1, [w_loc[n].shape for n in mm_names])))))
    conv_sent = _pack([w_loc["gdn_conv"]], 0, LANES, 8)
    conv_got = _exchange(conv_sent, "gather_conv", gather=True)
    conv_full = _unshard(_unpack(conv_got, 1, [w_loc["gdn_conv"].shape])[0], 2)

    wts = {n: [full[n][l] for l in range(full[n].shape[0])] for n in mm_names}
    wts["mla_w_in"] = [_mla_in_to_kernel(w) for w in wts["mla_w_in"]]
    wts["mla_w_uq"] = [_uq_to_kernel(w) for w in wts["mla_w_uq"]]
    wts["gdn_w_in"] = [_gdn_in_to_kernel(w) for w in wts["gdn_w_in"]]
    small_in = dict(small, gdn_conv=conv_full)

    loss_row, grad_x, gw, gs = _local_step(x[0], mem[0], positions[0], loss_target[0], wts, small_in)
    loss = lax.psum(loss_row[0, 0], ("x", "y", "c"))

    gw["mla_w_in"] = [_mla_in_from_kernel(g) for g in gw["mla_w_in"]]
    gw["mla_w_uq"] = [_uq_from_kernel(g) for g in gw["mla_w_uq"]]
    gw["gdn_w_in"] = [_gdn_in_from_kernel(g) for g in gw["gdn_w_in"]]
    gfull = {n: jnp.stack(gw[n]) for n in mm_names}
    gfull["gdn_conv"] = gs.pop("gdn_conv").astype(COMM_DTYPE)
    names = [n for n, _ in BIG]
    g_sent = _pack([_shard(gfull[n], axis_of[n]) for n in names], 1, PACK_COLS, 256)
    g_got = _exchange(g_sent, "exchange_grads", gather=False)
    w_flat = _pack([w_loc[n] for n in names], 0, PACK_COLS, 256)
    m_flat = _pack([m_loc[n] for n in names], 0, PACK_COLS, 256)
    v_flat = _pack([v_loc[n] for n in names], 0, PACK_COLS, 256)
    big_out = [dict(zip(names, _unpack(o, 0, [w_loc[n].shape for n in names])))
               for o in _adamw(g_got, w_flat, m_flat, v_flat, "adamw_big")]

    s_sent = _pack([gs[n] for n in SMALL], 0, LANES, 8)
    s_got = _exchange(s_sent, "gather_small_grads", gather=True)
    small_out = [dict(zip(SMALL, _unpack(o, 0, [small[n].shape for n in SMALL])))
                 for o in _adamw(s_got, _pack([small[n] for n in SMALL], 0, LANES, 8),
                                 _pack([args["m_" + n] for n in SMALL], 0, LANES, 8),
                                 _pack([args["v_" + n] for n in SMALL], 0, LANES, 8), "adamw_small")]

    order = ["mla_w_in", "mla_q_norm", "mla_w_uq", "mla_kv_norm", "mla_w_ukv", "gdn_w_in", "gdn_conv", "gdn_a_log",
             "gdn_dt_bias", "gdn_o_norm", "mem_w_kv", "w_out", "ln1_g", "ln1_b", "mlp_w1", "mlp_w2", "ln2_g", "ln2_b"]
    outs = [loss, grad_x[None]]
    for kind in range(4):
        for n in order:
            outs.append(big_out[kind][n] if n in axis_of else small_out[kind][n])
    return tuple(outs)
```

```python
import functools

import jax
import jax.numpy as jnp
from jax import lax
from jax.experimental import pallas as pl
from jax.experimental.pallas import tpu as pltpu

F32 = jnp.float32
MXU_DTYPE = jnp.bfloat16
COMM_DTYPE = jnp.bfloat16

N_DEV = 8
D_MODEL = 2048
DEPTH = 4
HEAD_DIM = 128
N_MIX_HEADS = 12
N_MEM_HEADS = 4
MIX_WIDTH = N_MIX_HEADS * HEAD_DIM
MEM_WIDTH = N_MEM_HEADS * HEAD_DIM
Q_LORA = 512
KV_LORA = 512
QK_NOPE = 128
QK_ROPE = 64
ROPE_THETA = 10000.0
CONV_WIDTH = 4
CHUNK = 64
D_FF = 4 * D_MODEL
ALPHA = (2 * DEPTH) ** 0.25
LN_EPS = 1e-5
RMS_EPS = 1e-6
MLA_IN = Q_LORA + KV_LORA + QK_ROPE + MEM_WIDTH
GDN_IN = 4 * MIX_WIDTH + 2 * N_MIX_HEADS + MEM_WIDTH
MLA_IN_P = 1792
GDN_IN_P = 6912
Q_HEAD_P = 256
ATT_SCALE = (QK_NOPE + QK_ROPE) ** -0.5
ADAM_LR, ADAM_B1, ADAM_B2, ADAM_EPS, ADAM_WD, ADAM_STEP = 0.001, 0.9, 0.999, 1e-08, 0.01, 10
LANES = 128
VMEM_LIMIT = 56 * 1024 * 1024

NN = (((1,), (0,)), ((), ()))
NT = (((1,), (1,)), ((), ()))
TN = (((0,), (0,)), ((), ()))
HI = lax.Precision.HIGHEST


def _dot(a, b, dims, hi=False):
    if hi:
        return lax.dot_general(a, b, dims, precision=HI, preferred_element_type=F32)
    return lax.dot_general(a.astype(MXU_DTYPE), b.astype(MXU_DTYPE), dims, preferred_element_type=F32)


@functools.partial(jax.custom_vjp, nondiff_argnums=(2,))
def _nn_d(a, b, hi):
    return _dot(a, b, NN, hi)


@functools.partial(jax.custom_vjp, nondiff_argnums=(2,))
def _nt_d(a, b, hi):
    return _dot(a, b, NT, hi)


@functools.partial(jax.custom_vjp, nondiff_argnums=(2,))
def _tn_d(a, b, hi):
    return _dot(a, b, TN, hi)


_nn_d.defvjp(lambda a, b, hi: (_dot(a, b, NN, hi), (a, b)),
             lambda hi, r, g: (_nt_d(g, r[1], hi), _tn_d(r[0], g, hi)))
_nt_d.defvjp(lambda a, b, hi: (_dot(a, b, NT, hi), (a, b)),
             lambda hi, r, g: (_nn_d(g, r[1], hi), _tn_d(g, r[0], hi)))
_tn_d.defvjp(lambda a, b, hi: (_dot(a, b, TN, hi), (a, b)),
             lambda hi, r, g: (_nt_d(r[1], g, hi), _nn_d(r[0], g, hi)))


class _RawOps:
    nn = staticmethod(lambda a, b, hi=False: _dot(a, b, NN, hi))
    nt = staticmethod(lambda a, b, hi=False: _dot(a, b, NT, hi))
    tn = staticmethod(lambda a, b, hi=False: _dot(a, b, TN, hi))


class _DiffOps:
    nn = staticmethod(lambda a, b, hi=False: _nn_d(a, b, hi))
    nt = staticmethod(lambda a, b, hi=False: _nt_d(a, b, hi))
    tn = staticmethod(lambda a, b, hi=False: _tn_d(a, b, hi))


def _pick(dim, cands=(512, 384, 256, 128)):
    for c in cands:
        if dim % c == 0:
            return c
    return dim


def _pc(body, *, grid, in_specs, out_specs, out_shape, name, scratch=()):
    return pl.pallas_call(
        body, grid=grid, in_specs=in_specs, out_specs=out_specs, out_shape=out_shape,
        scratch_shapes=list(scratch), name=name,
        compiler_params=pltpu.CompilerParams(dimension_semantics=("arbitrary",) * len(grid),
                                             vmem_limit_bytes=VMEM_LIMIT))


def _rows(ts, w, cb=0):
    return pl.BlockSpec((ts, w), lambda i, *_: (i, cb))


def _whole(shape):
    return pl.BlockSpec(shape, lambda *_: (0,) * len(shape))


def _sds(shape, dtype=F32):
    return jax.ShapeDtypeStruct(shape, dtype)


def _mm(a, b, *, name, ta=False, tb=False, out_dtype=F32, a_fn=None, epi=None, e=None):
    m, k = (a.shape[1], a.shape[0]) if ta else a.shape
    n = b.shape[0] if tb else b.shape[1]
    assert k == (b.shape[1] if tb else b.shape[0]), (a.shape, b.shape, ta, tb)
    tm, tn, tk = _pick(m, (1024, 512, 384, 256, 128)), _pick(n, (1024, 512, 384, 256, 128)), _pick(k)
    nk = k // tk
    dims = (((0 if ta else 1,), (1 if tb else 0,)), ((), ()))

    def body(*refs):
        if e is None:
            a_ref, b_ref, o_ref, acc = refs
        else:
            a_ref, b_ref, e_ref, o_ref, acc = refs
        kk = pl.program_id(2)

        @pl.when(kk == 0)
        def _():
            acc[...] = jnp.zeros_like(acc)

        av = a_ref[...]
        if a_fn is not None:
            av = a_fn(av.astype(F32))
        acc[...] += lax.dot_general(av.astype(MXU_DTYPE), b_ref[...].astype(MXU_DTYPE), dims,
                                    preferred_element_type=F32)

        @pl.when(kk == nk - 1)
        def _():
            r = acc[...]
            if epi is not None:
                r = epi(r, e_ref[...].astype(F32))
            o_ref[...] = r.astype(out_dtype)

    a_spec = pl.BlockSpec((tk, tm), lambda i, j, kk: (kk, i)) if ta else pl.BlockSpec((tm, tk), lambda i, j, kk: (i, kk))
    b_spec = pl.BlockSpec((tn, tk), lambda i, j, kk: (j, kk)) if tb else pl.BlockSpec((tk, tn), lambda i, j, kk: (kk, j))
    o_spec = pl.BlockSpec((tm, tn), lambda i, j, kk: (i, j))
    ins, specs = [a, b], [a_spec, b_spec]
    if e is not None:
        assert e.shape == (m, n)
        ins.append(e)
        specs.append(o_spec)
    return _pc(body, grid=(m // tm, n // tn, nk), in_specs=specs, out_specs=o_spec,
               out_shape=_sds((m, n), out_dtype), name=name, scratch=[pltpu.VMEM((tm, tn), F32)])(*ins)


def _relu2(v):
    r = jnp.maximum(v, 0.0)
    return r * r


def _relu2_bwd(acc, h1):
    return acc * (2.0 * jnp.maximum(h1, 0.0))


def _add_alpha(acc, dr):
    return acc + ALPHA * dr


def _ln(r, g, b):
    mu = jnp.mean(r, -1, keepdims=True)
    var = jnp.mean(jnp.square(r - mu), -1, keepdims=True)
    return (r - mu) * lax.rsqrt(var + LN_EPS) * g + b


def _ln_fwd(x, y, g, b, name):
    s, d = x.shape
    ts = _pick(s, (256, 128))

    def body(x_ref, y_ref, g_ref, b_ref, o_ref):
        o_ref[...] = _ln(ALPHA * x_ref[...] + y_ref[...], g_ref[...], b_ref[...])

    return _pc(body, grid=(s // ts,), in_specs=[_rows(ts, d), _rows(ts, d), _whole((1, d)), _whole((1, d))],
               out_specs=_rows(ts, d), out_shape=_sds((s, d)), name=name)(x, y, g, b)


def _ln_bwd(x, y, g, b, dout, name):
    s, d = x.shape
    ts = _pick(s, (256, 128))

    def body(x_ref, y_ref, g_ref, b_ref, do_ref, dr_ref, dg_ref, db_ref):
        @pl.when(pl.program_id(0) == 0)
        def _():
            dg_ref[...] = jnp.zeros_like(dg_ref)
            db_ref[...] = jnp.zeros_like(db_ref)

        r = ALPHA * x_ref[...] + y_ref[...]
        _, vjp = jax.vjp(_ln, r, g_ref[...], b_ref[...])
        dr, dg, db = vjp(do_ref[...])
        dr_ref[...] = dr
        dg_ref[...] += dg
        db_ref[...] += db

    return _pc(body, grid=(s // ts,),
               in_specs=[_rows(ts, d), _rows(ts, d), _whole((1, d)), _whole((1, d)), _rows(ts, d)],
               out_specs=[_rows(ts, d), _whole((1, d)), _whole((1, d))],
               out_shape=[_sds((s, d)), _sds((1, d)), _sds((1, d))], name=name)(x, y, g, b, dout)


def _loss_and_grad(y, target, name):
    s, d = y.shape
    ts = _pick(s, (256, 128))

    def body(y_ref, t_ref, l_ref, dy_ref):
        @pl.when(pl.program_id(0) == 0)
        def _():
            l_ref[...] = jnp.zeros_like(l_ref)

        diff = y_ref[...] - t_ref[...]
        per_tok = jnp.mean(jnp.square(diff), -1, keepdims=True)
        l_ref[...] += 0.5 * jnp.sum(per_tok, 0, keepdims=True) * jnp.ones((1, LANES), F32)
        dy_ref[...] = diff * (1.0 / d)

    return _pc(body, grid=(s // ts,), in_specs=[_rows(ts, d), _rows(ts, d)],
               out_specs=[_whole((1, LANES)), _rows(ts, d)],
               out_shape=[_sds((1, LANES)), _sds((s, d))], name=name)(y, target)


def _rope(blk, cs, sn):
    return blk * cs + pltpu.roll(blk, 64, 1) * sn


def _rope_t(dblk, cs, sn):
    return dblk * cs + pltpu.roll(dblk * sn, 64, 1)


def _rms(v, g):
    return v * lax.rsqrt(jnp.mean(v * v, -1, keepdims=True) + RMS_EPS) * g


def _mla_prep_fwd(h, qn, kvn, cs, sn, name):
    s = h.shape[0]
    ts = _pick(s, (512, 256, 128))

    def body(cq_ref, ckv_ref, kr_ref, qn_ref, kvn_ref, cs_ref, sn_ref, cqn_ref, ckvn_ref, krr_ref):
        cqn_ref[...] = _rms(cq_ref[...], qn_ref[...])
        ckvn_ref[...] = _rms(ckv_ref[...], kvn_ref[...])
        krr_ref[...] = _rope(kr_ref[...], cs_ref[...], sn_ref[...])

    return _pc(body, grid=(s // ts,),
               in_specs=[_rows(ts, 512, 0), _rows(ts, 512, 1), _rows(ts, LANES, 12), _whole((1, 512)), _whole((1, 512)),
                         _rows(ts, LANES), _rows(ts, LANES)],
               out_specs=[_rows(ts, 512), _rows(ts, 512), _rows(ts, LANES)],
               out_shape=[_sds((s, 512)), _sds((s, 512)), _sds((s, LANES))], name=name)(h, h, h, qn, kvn, cs, sn)


def _mla_prep_bwd(h, qn, kvn, cs, sn, dcqn, dckvn, dkrr_heads, name):
    s = h.shape[0]
    ts = _pick(s, (512, 256, 128))

    def body(cq_ref, ckv_ref, qn_ref, kvn_ref, cs_ref, sn_ref, dcqn_ref, dckvn_ref, dkrr_ref,
             dcq_ref, dckv_ref, dkr_ref, dqn_ref, dkvn_ref):
        @pl.when(pl.program_id(0) == 0)
        def _():
            dqn_ref[...] = jnp.zeros_like(dqn_ref)
            dkvn_ref[...] = jnp.zeros_like(dkvn_ref)

        _, vjp = jax.vjp(_rms, cq_ref[...], qn_ref[...])
        dcq, dqn = vjp(dcqn_ref[...])
        dcq_ref[...] = dcq
        dqn_ref[...] += dqn
        _, vjp = jax.vjp(_rms, ckv_ref[...], kvn_ref[...])
        dckv, dkvn = vjp(dckvn_ref[...])
        dckv_ref[...] = dckv
        dkvn_ref[...] += dkvn
        dkrr = dkrr_ref[0]
        for hh in range(1, N_MIX_HEADS):
            dkrr = dkrr + dkrr_ref[hh]
        dkr_ref[...] = _rope_t(dkrr, cs_ref[...], sn_ref[...])

    heads3 = pl.BlockSpec((N_MIX_HEADS, ts, LANES), lambda i: (0, i, 0))
    return _pc(body, grid=(s // ts,),
               in_specs=[_rows(ts, 512, 0), _rows(ts, 512, 1), _whole((1, 512)), _whole((1, 512)),
                         _rows(ts, LANES), _rows(ts, LANES), _rows(ts, 512), _rows(ts, 512), heads3],
               out_specs=[_rows(ts, 512), _rows(ts, 512), _rows(ts, LANES), _whole((1, 512)), _whole((1, 512))],
               out_shape=[_sds((s, 512)), _sds((s, 512)), _sds((s, LANES)), _sds((1, 512)), _sds((1, 512))],
               name=name)(h, h, qn, kvn, cs, sn, dcqn, dckvn, dkrr_heads)


def _att_tiles(s):
    t = _pick(s, (512, 256, 128))
    return t, s // t


ATT_HEADS_PER_STEP = 2


def _tri_pairs(nb, k_major):
    pairs = [(i, j) for j in range(nb) for i in range(j, nb)] if k_major else [(i, j) for i in range(nb) for j in range(i + 1)]
    return jnp.asarray([p[0] for p in pairs], jnp.int32), jnp.asarray([p[1] for p in pairs], jnp.int32)


def _tril(t):
    return lax.broadcasted_iota(jnp.int32, (t, t), 1) <= lax.broadcasted_iota(jnp.int32, (t, t), 0)


def _pc_pairs(body, pairs, *, n_groups, in_specs, out_specs, out_shape, name, scratch, args):
    return pl.pallas_call(
        body, out_shape=out_shape, name=name,
        grid_spec=pltpu.PrefetchScalarGridSpec(num_scalar_prefetch=2, grid=(n_groups, pairs[0].shape[0]), in_specs=in_specs,
                                               out_specs=out_specs, scratch_shapes=list(scratch)),
        compiler_params=pltpu.CompilerParams(dimension_semantics=("arbitrary", "arbitrary"), vmem_limit_bytes=VMEM_LIMIT),
    )(*pairs, *args)


def _att_specs(t):
    qrow = lambda w: pl.BlockSpec((t, w), lambda h, p, qi, kj: (qi[p], h))
    krow = lambda w: pl.BlockSpec((t, w), lambda h, p, qi, kj: (kj[p], h))
    qtab = pl.BlockSpec((t, LANES), lambda h, p, qi, kj: (qi[p], 0))
    ktab = pl.BlockSpec((t, LANES), lambda h, p, qi, kj: (kj[p], 0))
    return qrow, krow, qtab, ktab


def _flash_fwd(q, kv, krr, cs, sn, name):
    s = q.shape[0]
    t, nb = _att_tiles(s)
    ah = ATT_HEADS_PER_STEP

    def body(qi_ref, kj_ref, q_ref, cs_ref, sn_ref, kv_ref, kr_ref, o_ref, lse_ref, qn_s, qr_s, m_s, l_s, acc_s):
        i, j = qi_ref[pl.program_id(1)], kj_ref[pl.program_id(1)]

        @pl.when(j == 0)
        def _():
            for hh in range(ah):
                c0 = hh * Q_HEAD_P
                qn_s[hh] = q_ref[:, c0:c0 + LANES].astype(MXU_DTYPE)
                qr_s[hh] = _rope(q_ref[:, c0 + LANES:c0 + Q_HEAD_P], cs_ref[...], sn_ref[...]).astype(MXU_DTYPE)
            m_s[...] = jnp.full_like(m_s, -jnp.inf)
            l_s[...] = jnp.zeros_like(l_s)
            acc_s[...] = jnp.zeros_like(acc_s)

        def update(masked):
            kr = kr_ref[...].astype(MXU_DTYPE)
            for hh in range(ah):
                c0 = hh * Q_HEAD_P
                sc = (_dot(qn_s[hh], kv_ref[:, c0:c0 + LANES], NT) + _dot(qr_s[hh], kr, NT)) * ATT_SCALE
                if masked:
                    sc = jnp.where(_tril(t), sc, -jnp.inf)
                m_old = m_s[hh]
                m_new = jnp.maximum(m_old, jnp.max(sc, -1, keepdims=True))
                p = jnp.exp(sc - m_new[:, :1])
                corr = jnp.exp(m_old - m_new)
                l_s[hh] = corr * l_s[hh] + jnp.sum(p, -1, keepdims=True)
                acc_s[hh] = corr * acc_s[hh] + _dot(p, kv_ref[:, c0 + LANES:c0 + Q_HEAD_P], NN)
                m_s[hh] = m_new

        @pl.when(j < i)
        def _():
            update(False)

        @pl.when(j == i)
        def _():
            update(True)
            for hh in range(ah):
                sl = slice(hh * LANES, (hh + 1) * LANES)
                o_ref[:, sl] = acc_s[hh] / l_s[hh]
                lse_ref[:, sl] = m_s[hh] + jnp.log(l_s[hh])

    qrow, krow, qtab, ktab = _att_specs(t)
    return _pc_pairs(body, _tri_pairs(nb, False), n_groups=N_MIX_HEADS // ah,
                     in_specs=[qrow(ah * Q_HEAD_P), qtab, qtab, krow(ah * Q_HEAD_P), ktab],
                     out_specs=[qrow(ah * LANES), qrow(ah * LANES)],
                     out_shape=[_sds((s, MIX_WIDTH)), _sds((s, MIX_WIDTH))], name=name,
                     scratch=[pltpu.VMEM((ah, t, LANES), MXU_DTYPE), pltpu.VMEM((ah, t, LANES), MXU_DTYPE),
                              pltpu.VMEM((ah, t, LANES), F32), pltpu.VMEM((ah, t, LANES), F32), pltpu.VMEM((ah, t, LANES), F32)],
                     args=(q, cs, sn, kv, krr))


def _flash_bwd_q(q, kv, krr, cs, sn, o, lse, dcat, name):
    s = q.shape[0]
    t, nb = _att_tiles(s)
    ah = ATT_HEADS_PER_STEP

    def body(qi_ref, kj_ref, q_ref, cs_ref, sn_ref, o_ref, lse_ref, do_ref, kv_ref, kr_ref, dq_ref,
             qn_s, qr_s, dl_s, an_s, ar_s):
        i, j = qi_ref[pl.program_id(1)], kj_ref[pl.program_id(1)]

        @pl.when(j == 0)
        def _():
            for hh in range(ah):
                c0 = hh * Q_HEAD_P
                sl = slice(hh * LANES, (hh + 1) * LANES)
                qn_s[hh] = q_ref[:, c0:c0 + LANES].astype(MXU_DTYPE)
                qr_s[hh] = _rope(q_ref[:, c0 + LANES:c0 + Q_HEAD_P], cs_ref[...], sn_ref[...]).astype(MXU_DTYPE)
                dl_s[hh] = jnp.sum(o_ref[:, sl] * do_ref[:, sl], -1, keepdims=True) * jnp.ones((1, LANES), F32)
            an_s[...] = jnp.zeros_like(an_s)
            ar_s[...] = jnp.zeros_like(ar_s)

        def update(masked):
            kr = kr_ref[...].astype(MXU_DTYPE)
            for hh in range(ah):
                c0 = hh * Q_HEAD_P
                sl = slice(hh * LANES, (hh + 1) * LANES)
                kn = kv_ref[:, c0:c0 + LANES].astype(MXU_DTYPE)
                sc = (_dot(qn_s[hh], kn, NT) + _dot(qr_s[hh], kr, NT)) * ATT_SCALE
                p = jnp.exp(sc - lse_ref[:, hh * LANES:hh * LANES + 1])
                if masked:
                    p = jnp.where(_tril(t), p, 0.0)
                dp = _dot(do_ref[:, sl], kv_ref[:, c0 + LANES:c0 + Q_HEAD_P], NT)
                ds = (p * (dp - dl_s[hh][:, :1]) * ATT_SCALE).astype(MXU_DTYPE)
                an_s[hh] += _dot(ds, kn, NN)
                ar_s[hh] += _dot(ds, kr, NN)

        @pl.when(j < i)
        def _():
            update(False)

        @pl.when(j == i)
        def _():
            update(True)
            for hh in range(ah):
                c0 = hh * Q_HEAD_P
                dq_ref[:, c0:c0 + LANES] = an_s[hh]
                dq_ref[:, c0 + LANES:c0 + Q_HEAD_P] = _rope_t(ar_s[hh], cs_ref[...], sn_ref[...])

    qrow, krow, qtab, ktab = _att_specs(t)
    return _pc_pairs(body, _tri_pairs(nb, False), n_groups=N_MIX_HEADS // ah,
                     in_specs=[qrow(ah * Q_HEAD_P), qtab, qtab, qrow(ah * LANES), qrow(ah * LANES), qrow(ah * LANES),
                               krow(ah * Q_HEAD_P), ktab],
                     out_specs=qrow(ah * Q_HEAD_P), out_shape=_sds((s, N_MIX_HEADS * Q_HEAD_P)), name=name,
                     scratch=[pltpu.VMEM((ah, t, LANES), MXU_DTYPE), pltpu.VMEM((ah, t, LANES), MXU_DTYPE),
                              pltpu.VMEM((ah, t, LANES), F32), pltpu.VMEM((ah, t, LANES), F32), pltpu.VMEM((ah, t, LANES), F32)],
                     args=(q, cs, sn, o, lse, dcat, kv, krr))


def _flash_bwd_kv(q, kv, krr, cs, sn, o, lse, dcat, name):
    s = q.shape[0]
    t, nb = _att_tiles(s)
    ah = ATT_HEADS_PER_STEP

    def body(qi_ref, kj_ref, kv_ref, kr_ref, q_ref, cs_ref, sn_ref, o_ref, lse_ref, do_ref, dkv_ref, dkr_ref,
             akn_s, av_s, akr_s):
        i, j = qi_ref[pl.program_id(1)], kj_ref[pl.program_id(1)]

        @pl.when(i == j)
        def _():
            akn_s[...] = jnp.zeros_like(akn_s)
            av_s[...] = jnp.zeros_like(av_s)
            akr_s[...] = jnp.zeros_like(akr_s)

        def update(masked):
            kr = kr_ref[...].astype(MXU_DTYPE)
            for hh in range(ah):
                c0 = hh * Q_HEAD_P
                sl = slice(hh * LANES, (hh + 1) * LANES)
                qn = q_ref[:, c0:c0 + LANES].astype(MXU_DTYPE)
                qr = _rope(q_ref[:, c0 + LANES:c0 + Q_HEAD_P], cs_ref[...], sn_ref[...]).astype(MXU_DTYPE)
                do = do_ref[:, sl]
                sc = (_dot(qn, kv_ref[:, c0:c0 + LANES], NT) + _dot(qr, kr, NT)) * ATT_SCALE
                p = jnp.exp(sc - lse_ref[:, hh * LANES:hh * LANES + 1])
                if masked:
                    p = jnp.where(_tril(t), p, 0.0)
                dp = _dot(do, kv_ref[:, c0 + LANES:c0 + Q_HEAD_P], NT)
                dl = jnp.sum(o_ref[:, sl] * do, -1, keepdims=True)
                ds = (p * (dp - dl) * ATT_SCALE).astype(MXU_DTYPE)
                av_s[hh] += _dot(p, do, TN)
                akn_s[hh] += _dot(ds, qn, TN)
                akr_s[hh] += _dot(ds, qr, TN)

        @pl.when(i > j)
        def _():
            update(False)

        @pl.when(i == j)
        def _():
            update(True)

        @pl.when(i == nb - 1)
        def _():
            for hh in range(ah):
                c0 = hh * Q_HEAD_P
                dkv_ref[:, c0:c0 + LANES] = akn_s[hh]
                dkv_ref[:, c0 + LANES:c0 + Q_HEAD_P] = av_s[hh]
                dkr_ref[hh] = akr_s[hh]

    qrow, krow, qtab, ktab = _att_specs(t)
    return _pc_pairs(body, _tri_pairs(nb, True), n_groups=N_MIX_HEADS // ah,
                     in_specs=[krow(ah * Q_HEAD_P), ktab, qrow(ah * Q_HEAD_P), qtab, qtab,
                               qrow(ah * LANES), qrow(ah * LANES), qrow(ah * LANES)],
                     out_specs=[krow(ah * Q_HEAD_P), pl.BlockSpec((ah, t, LANES), lambda h, p, qi, kj: (h, kj[p], 0))],
                     out_shape=[_sds((s, N_MIX_HEADS * Q_HEAD_P)), _sds((N_MIX_HEADS, s, LANES))], name=name,
                     scratch=[pltpu.VMEM((ah, t, LANES), F32)] * 3,
                     args=(kv, krr, q, cs, sn, o, lse, dcat))


def _mem_head(ops, qh, kh, vh):
    sc = ops.nt(qh, kh) * HEAD_DIM ** -0.5
    e = jnp.exp(sc - lax.stop_gradient(jnp.max(sc, -1, keepdims=True)))
    p = e / jnp.sum(e, -1, keepdims=True)
    return ops.nn(p, vh)


def _mem_fwd(h, qcol, mem_kv, name):
    s = h.shape[0]
    m = mem_kv.shape[0]
    ts = _pick(s, (512, 256, 128))

    def body(q_ref, kv_ref, o_ref):
        for hh in range(N_MEM_HEADS):
            sl = slice(hh * HEAD_DIM, (hh + 1) * HEAD_DIM)
            vsl = slice(MEM_WIDTH + hh * HEAD_DIM, MEM_WIDTH + (hh + 1) * HEAD_DIM)
            o_ref[:, sl] = _mem_head(_RawOps, q_ref[:, sl], kv_ref[:, sl], kv_ref[:, vsl])

    return _pc(body, grid=(s // ts,), in_specs=[_rows(ts, MEM_WIDTH, qcol), _whole((m, 2 * MEM_WIDTH))],
               out_specs=_rows(ts, MEM_WIDTH), out_shape=_sds((s, MEM_WIDTH)), name=name)(h, mem_kv)


def _mem_bwd(h, qcol, mem_kv, dcat, name):
    s = h.shape[0]
    m = mem_kv.shape[0]
    ts = _pick(s, (512, 256, 128))

    def body(q_ref, kv_ref, do_ref, dq_ref, dkv_ref):
        @pl.when(pl.program_id(0) == 0)
        def _():
            dkv_ref[...] = jnp.zeros_like(dkv_ref)

        for hh in range(N_MEM_HEADS):
            sl = slice(hh * HEAD_DIM, (hh + 1) * HEAD_DIM)
            vsl = slice(MEM_WIDTH + hh * HEAD_DIM, MEM_WIDTH + (hh + 1) * HEAD_DIM)
            _, vjp = jax.vjp(functools.partial(_mem_head, _DiffOps), q_ref[:, sl], kv_ref[:, sl], kv_ref[:, vsl])
            dq, dk, dv = vjp(do_ref[:, sl])
            dq_ref[:, sl] = dq
            dkv_ref[:, sl] += dk
            dkv_ref[:, vsl] += dv

    return _pc(body, grid=(s // ts,),
               in_specs=[_rows(ts, MEM_WIDTH, qcol), _whole((m, 2 * MEM_WIDTH)), _rows(ts, MEM_WIDTH, 3)],
               out_specs=[_rows(ts, MEM_WIDTH), _whole((m, 2 * MEM_WIDTH))],
               out_shape=[_sds((s, MEM_WIDTH)), _sds((m, 2 * MEM_WIDTH))], name=name)(h, mem_kv, dcat)


CONV_COLS = 3 * MIX_WIDTH
HALO = 8


def _conv_fwd(h, w, name):
    s = h.shape[0]
    ts = _pick(s, (512, 256, 128))
    wc = 512

    def body(x_ref, halo_ref, w_ref, o_ref, ext):
        i = pl.program_id(0)
        ext[pl.ds(0, HALO), :] = jnp.where(i > 0, halo_ref[...], 0.0)
        ext[pl.ds(HALO, ts), :] = x_ref[...]
        acc = w_ref[0:1, :] * ext[pl.ds(HALO - 3, ts), :]
        for j in range(1, CONV_WIDTH):
            acc = acc + w_ref[j:j + 1, :] * ext[pl.ds(HALO - 3 + j, ts), :]
        o_ref[...] = acc

    halo = pl.BlockSpec((HALO, wc), lambda i, c: (jnp.maximum(i * (ts // HALO) - 1, 0), c))
    blk = pl.BlockSpec((ts, wc), lambda i, c: (i, c))
    return _pc(body, grid=(s // ts, CONV_COLS // wc),
               in_specs=[blk, halo, pl.BlockSpec((CONV_WIDTH, wc), lambda i, c: (0, c))],
               out_specs=blk, out_shape=_sds((s, CONV_COLS)), name=name,
               scratch=[pltpu.VMEM((HALO + ts, wc), F32)])(h, h, w)


def _conv_bwd(h, w, dout, name):
    s = h.shape[0]
    ts = _pick(s, (512, 256, 128))
    nt = s // ts
    wc = 512

    def body(x_ref, xhalo_ref, d_ref, dhalo_ref, w_ref, dx_ref, dw_ref, xext, dext):
        i = pl.program_id(1)

        @pl.when(i == 0)
        def _():
            dw_ref[...] = jnp.zeros_like(dw_ref)

        xext[pl.ds(0, HALO), :] = jnp.where(i > 0, xhalo_ref[...], 0.0)
        xext[pl.ds(HALO, ts), :] = x_ref[...]
        dext[pl.ds(0, ts), :] = d_ref[...]
        dext[pl.ds(ts, HALO), :] = jnp.where(i < nt - 1, dhalo_ref[...], 0.0)
        d = d_ref[...]
        acc = w_ref[CONV_WIDTH - 1:CONV_WIDTH, :] * d
        for j in range(CONV_WIDTH - 1):
            acc = acc + w_ref[j:j + 1, :] * dext[pl.ds(3 - j, ts), :]
        dx_ref[...] = acc
        for j in range(CONV_WIDTH):
            dw_ref[j:j + 1, :] += jnp.sum(d * xext[pl.ds(HALO - 3 + j, ts), :], 0, keepdims=True)

    blk = pl.BlockSpec((ts, wc), lambda c, i: (i, c))
    halo_prev = pl.BlockSpec((HALO, wc), lambda c, i: (jnp.maximum(i * (ts // HALO) - 1, 0), c))
    halo_next = pl.BlockSpec((HALO, wc), lambda c, i: (jnp.minimum((i + 1) * (ts // HALO), s // HALO - 1), c))
    wspec = pl.BlockSpec((CONV_WIDTH, wc), lambda c, i: (0, c))
    return _pc(body, grid=(CONV_COLS // wc, nt), in_specs=[blk, halo_prev, blk, halo_next, wspec],
               out_specs=[blk, wspec], out_shape=[_sds((s, CONV_COLS)), _sds((CONV_WIDTH, CONV_COLS))], name=name,
               scratch=[pltpu.VMEM((HALO + ts, wc), F32), pltpu.VMEM((ts + HALO, wc), F32)])(h, h, dout, dout, w)


def _silu(v):
    return v * jax.nn.sigmoid(v)


def _softplus(v):
    return jnp.maximum(v, 0.0) + jnp.log1p(jnp.exp(-jnp.abs(v)))


def _gdn_prep_head(cq, ck, cv, a, b, alog, dtb):
    q = _silu(cq)
    q = q * lax.rsqrt(jnp.sum(q * q, -1, keepdims=True) + 1e-6) * HEAD_DIM ** -0.5
    k = _silu(ck)
    k = k * lax.rsqrt(jnp.sum(k * k, -1, keepdims=True) + 1e-6)
    v = _silu(cv)
    g = -jnp.exp(alog) * _softplus(a + dtb)
    beta = jax.nn.sigmoid(b)
    ones = jnp.ones((1, HEAD_DIM), F32)
    return q, k, v, g * ones, beta * ones


def _onehot_lane(idx):
    return (lax.broadcasted_iota(jnp.int32, (1, LANES), 1) == idx).astype(F32)


def _lane_pick(row, idx):
    return jnp.sum(row * _onehot_lane(idx), -1, keepdims=True)


def _gdn_prep_fwd(conv, h, alog, dtb, name):
    s = h.shape[0]
    ts = _pick(s, (128,))
    w = MIX_WIDTH

    def body(c_ref, ab_ref, alog_ref, dtb_ref, q_ref, k_ref, v_ref, g_ref, b_ref):
        ab = ab_ref[...]
        for hh in range(N_MIX_HEADS):
            sl = slice(hh * HEAD_DIM, (hh + 1) * HEAD_DIM)
            cols = [slice(p * w + hh * HEAD_DIM, p * w + (hh + 1) * HEAD_DIM) for p in range(3)]
            outs = _gdn_prep_head(c_ref[:, cols[0]], c_ref[:, cols[1]], c_ref[:, cols[2]], _lane_pick(ab, hh),
                                  _lane_pick(ab, N_MIX_HEADS + hh), _lane_pick(alog_ref[...], hh),
                                  _lane_pick(dtb_ref[...], hh))
            for ref, val in zip((q_ref, k_ref, v_ref, g_ref, b_ref), outs):
                ref[:, sl] = val

    return _pc(body, grid=(s // ts,),
               in_specs=[_rows(ts, 3 * w), _rows(ts, LANES, (4 * w + MEM_WIDTH) // LANES), _whole((1, LANES)), _whole((1, LANES))],
               out_specs=[_rows(ts, w)] * 5, out_shape=[_sds((s, w))] * 5, name=name)(conv, h, alog, dtb)


def _gdn_prep_bwd(conv, h, alog, dtb, dq, dk, dv, dg, db, name):
    s = h.shape[0]
    ts = _pick(s, (128,))
    w = MIX_WIDTH

    def body(c_ref, ab_ref, alog_ref, dtb_ref, dq_ref, dk_ref, dv_ref, dg_ref, db_ref,
             dc_ref, dab_ref, dalog_ref, ddtb_ref):
        @pl.when(pl.program_id(0) == 0)
        def _():
            dalog_ref[...] = jnp.zeros_like(dalog_ref)
            ddtb_ref[...] = jnp.zeros_like(ddtb_ref)

        ab = ab_ref[...]
        dab = jnp.zeros_like(ab)
        dalog = jnp.zeros((1, LANES), F32)
        ddtb = jnp.zeros((1, LANES), F32)
        for hh in range(N_MIX_HEADS):
            sl = slice(hh * HEAD_DIM, (hh + 1) * HEAD_DIM)
            cols = [slice(p * w + hh * HEAD_DIM, p * w + (hh + 1) * HEAD_DIM) for p in range(3)]
            _, vjp = jax.vjp(_gdn_prep_head, c_ref[:, cols[0]], c_ref[:, cols[1]], c_ref[:, cols[2]], _lane_pick(ab, hh),
                             _lane_pick(ab, N_MIX_HEADS + hh), _lane_pick(alog_ref[...], hh), _lane_pick(dtb_ref[...], hh))
            dcq, dck, dcv, da, dbb, dal, ddt = vjp((dq_ref[:, sl], dk_ref[:, sl], dv_ref[:, sl], dg_ref[:, sl], db_ref[:, sl]))
            dc_ref[:, cols[0]] = dcq
            dc_ref[:, cols[1]] = dck
            dc_ref[:, cols[2]] = dcv
            dab = dab + da * _onehot_lane(hh) + dbb * _onehot_lane(N_MIX_HEADS + hh)
            dalog = dalog + dal * _onehot_lane(hh)
            ddtb = ddtb + ddt * _onehot_lane(hh)
        dab_ref[...] = dab
        dalog_ref[...] += dalog
        ddtb_ref[...] += ddtb

    return _pc(body, grid=(s // ts,),
               in_specs=[_rows(ts, 3 * w), _rows(ts, LANES, (4 * w + MEM_WIDTH) // LANES),
                         _whole((1, LANES)), _whole((1, LANES))] + [_rows(ts, w)] * 5,
               out_specs=[_rows(ts, 3 * w), _rows(ts, LANES), _whole((1, LANES)), _whole((1, LANES))],
               out_shape=[_sds((s, 3 * w)), _sds((s, LANES)), _sds((1, LANES)), _sds((1, LANES))],
               name=name)(conv, h, alog, dtb, dq, dk, dv, dg, db)


def _gdn_chunk(ops, state, q, k, v, gb, bb):
    c = CHUNK
    row = lax.broadcasted_iota(jnp.int32, (c, c), 0)
    col = lax.broadcasted_iota(jnp.int32, (c, c), 1)
    tril = row >= col
    strict = row > col
    eye = (row == col).astype(F32)
    gc = ops.nn(tril.astype(F32), gb, True)
    avg = jnp.full((c, HEAD_DIM), 1.0 / HEAD_DIM, F32)
    gci = ops.nt(gc, avg, True)
    gcj = ops.nt(avg, gc, True)
    decay = jnp.where(tril, jnp.exp(jnp.where(tril, gci - gcj, 0.0)), 0.0)
    kb = k * bb
    low = jnp.where(strict, ops.nt(kb, k) * decay, 0.0)
    inv = eye - low
    pw = low
    for _ in range(5):
        pw = ops.nn(pw, pw, True)
        inv = inv + ops.nn(inv, pw, True)
    ge = jnp.exp(gc)
    u = ops.nn(inv, v * bb, True)
    w = ops.nn(inv, kb * ge, True)
    a_qk = jnp.where(tril, ops.nt(q, k) * decay, 0.0)
    g_tot = jnp.sum(gb, 0, keepdims=True)
    k_dec = k * jnp.exp(g_tot - gc)
    v_new = u - ops.nn(w, state)
    o = ops.nn(q * ge, state) + ops.nn(a_qk, v_new)
    new_state = state * jnp.exp(g_tot) + ops.tn(k_dec, v_new)
    return new_state, o


GDN_HEADS_FWD = 12
GDN_HEADS_BWD = 6


def _gdn_core_fwd(q, k, v, gb, bb, name):
    s = q.shape[0]
    nc = s // CHUNK
    hp = GDN_HEADS_FWD
    w = hp * HEAD_DIM

    def body(q_ref, k_ref, v_ref, g_ref, b_ref, o_ref, st_ref, state):
        @pl.when(pl.program_id(1) == 0)
        def _():
            state[...] = jnp.zeros_like(state)

        for hh in range(hp):
            sl = slice(hh * HEAD_DIM, (hh + 1) * HEAD_DIM)
            st = state[hh]
            st_ref[0, hh] = st
            new_state, o = _gdn_chunk(_RawOps, st, q_ref[:, sl], k_ref[:, sl], v_ref[:, sl], g_ref[:, sl], b_ref[:, sl])
            state[hh] = new_state
            o_ref[:, sl] = o

    blk = pl.BlockSpec((CHUNK, w), lambda hg, c: (c, hg))
    return _pc(body, grid=(N_MIX_HEADS // hp, nc), in_specs=[blk] * 5,
               out_specs=[blk, pl.BlockSpec((1, hp, HEAD_DIM, HEAD_DIM), lambda hg, c: (c, hg, 0, 0))],
               out_shape=[_sds((s, MIX_WIDTH)), _sds((nc, N_MIX_HEADS, HEAD_DIM, HEAD_DIM))], name=name,
               scratch=[pltpu.VMEM((hp, HEAD_DIM, HEAD_DIM), F32)])(q, k, v, gb, bb)


def _gdn_core_bwd(q, k, v, gb, bb, states, do, name):
    s = q.shape[0]
    nc = s // CHUNK
    hp = GDN_HEADS_BWD
    w = hp * HEAD_DIM

    def body(q_ref, k_ref, v_ref, g_ref, b_ref, st_ref, do_ref, dq_ref, dk_ref, dv_ref, dg_ref, db_ref, dstate):
        @pl.when(pl.program_id(1) == 0)
        def _():
            dstate[...] = jnp.zeros_like(dstate)

        for hh in range(hp):
            sl = slice(hh * HEAD_DIM, (hh + 1) * HEAD_DIM)
            _, vjp = jax.vjp(functools.partial(_gdn_chunk, _DiffOps), st_ref[0, hh], q_ref[:, sl], k_ref[:, sl],
                             v_ref[:, sl], g_ref[:, sl], b_ref[:, sl])
            dst, dq, dk, dv, dg, db = vjp((dstate[hh], do_ref[:, sl]))
            dstate[hh] = dst
            for ref, val in zip((dq_ref, dk_ref, dv_ref, dg_ref, db_ref), (dq, dk, dv, dg, db)):
                ref[:, sl] = val

    blk = pl.BlockSpec((CHUNK, w), lambda hg, c: (nc - 1 - c, hg))
    return _pc(body, grid=(N_MIX_HEADS // hp, nc),
               in_specs=[blk] * 5 + [pl.BlockSpec((1, hp, HEAD_DIM, HEAD_DIM), lambda hg, c: (nc - 1 - c, hg, 0, 0)), blk],
               out_specs=[blk] * 5, out_shape=[_sds((s, MIX_WIDTH))] * 5, name=name,
               scratch=[pltpu.VMEM((hp, HEAD_DIM, HEAD_DIM), F32)])(q, k, v, gb, bb, states, do)


def _gdn_out_head(o, z, g):
    return _rms(o, g) * _silu(z)


def _gdn_out_fwd(o, h, onorm, name):
    s = o.shape[0]
    ts = _pick(s, (256, 128))

    def body(o_ref, z_ref, g_ref, y_ref):
        for hh in range(N_MIX_HEADS):
            sl = slice(hh * HEAD_DIM, (hh + 1) * HEAD_DIM)
            y_ref[:, sl] = _gdn_out_head(o_ref[:, sl], z_ref[:, sl], g_ref[...])

    return _pc(body, grid=(s // ts,), in_specs=[_rows(ts, MIX_WIDTH), _rows(ts, MIX_WIDTH, 3), _whole((1, HEAD_DIM))],
               out_specs=_rows(ts, MIX_WIDTH), out_shape=_sds((s, MIX_WIDTH)), name=name)(o, h, onorm)


def _gdn_out_bwd(o, h, onorm, dcat, name):
    s = o.shape[0]
    ts = _pick(s, (256, 128))

    def body(o_ref, z_ref, g_ref, dy_ref, do_ref, dz_ref, dg_ref):
        @pl.when(pl.program_id(0) == 0)
        def _():
            dg_ref[...] = jnp.zeros_like(dg_ref)

        dgs = jnp.zeros((1, HEAD_DIM), F32)
        for hh in range(N_MIX_HEADS):
            sl = slice(hh * HEAD_DIM, (hh + 1) * HEAD_DIM)
            _, vjp = jax.vjp(_gdn_out_head, o_ref[:, sl], z_ref[:, sl], g_ref[...])
            do, dz, dg = vjp(dy_ref[:, sl])
            do_ref[:, sl] = do
            dz_ref[:, sl] = dz
            dgs = dgs + dg
        dg_ref[...] += dgs

    return _pc(body, grid=(s // ts,),
               in_specs=[_rows(ts, MIX_WIDTH), _rows(ts, MIX_WIDTH, 3), _whole((1, HEAD_DIM)), _rows(ts, MIX_WIDTH, 0)],
               out_specs=[_rows(ts, MIX_WIDTH), _rows(ts, MIX_WIDTH), _whole((1, HEAD_DIM))],
               out_shape=[_sds((s, MIX_WIDTH)), _sds((s, MIX_WIDTH)), _sds((1, HEAD_DIM))], name=name)(o, h, onorm, dcat)


def _row(v):
    return v.reshape(1, -1)


def _lane_row(v):
    return jnp.pad(v, (0, LANES - v.shape[0])).reshape(1, LANES)


def _rope_tables(positions):
    inv_freq = 1.0 / (ROPE_THETA ** (jnp.arange(0, QK_ROPE, 2, dtype=F32) / QK_ROPE))
    ang = positions.astype(F32)[:, None] * inv_freq
    cos, sin = jnp.cos(ang), jnp.sin(ang)
    z = jnp.zeros_like(cos)
    return jnp.concatenate([cos, z, cos, z], 1), jnp.concatenate([-sin, z, sin, z], 1)


def _local_step(x, mem, positions, loss_target, wts, small):
    cs, sn = _rope_tables(positions)
    saved = []
    for i in range(DEPTH):
        j = i // 2
        sv = {"x": x}
        sv["mem_kv"] = _mm(mem, wts["mem_w_kv"][i], name=f"l{i}_memkv")
        if i % 2 == 0:
            h = _mm(x, wts["mla_w_in"][j], name=f"l{i}_in")
            cqn, ckvn, krr = _mla_prep_fwd(h, _row(small["mla_q_norm"][j]), _row(small["mla_kv_norm"][j]), cs, sn, f"l{i}_mlaprep")
            q = _mm(cqn, wts["mla_w_uq"][j], name=f"l{i}_uq")
            kv = _mm(ckvn, wts["mla_w_ukv"][j], name=f"l{i}_ukv")
            mix, lse = _flash_fwd(q, kv, krr, cs, sn, f"l{i}_flash")
            sv.update(cqn=cqn, ckvn=ckvn, krr=krr, q=q, kv=kv, o=mix, lse=lse)
            qcol = 2
        else:
            h = _mm(x, wts["gdn_w_in"][j], name=f"l{i}_in")
            conv = _conv_fwd(h, small["gdn_conv"][j], f"l{i}_conv")
            qn, kn, vv, gb, bb = _gdn_prep_fwd(conv, h, _lane_row(small["gdn_a_log"][j]), _lane_row(small["gdn_dt_bias"][j]), f"l{i}_gdnprep")
            o, states = _gdn_core_fwd(qn, kn, vv, gb, bb, f"l{i}_gdncore")
            mix = _gdn_out_fwd(o, h, _row(small["gdn_o_norm"][j]), f"l{i}_gdnout")
            sv.update(conv=conv, qn=qn, kn=kn, vv=vv, gb=gb, bb=bb, o=o, states=states)
            qcol = 4 * MIX_WIDTH // MEM_WIDTH
        mem_o = _mem_fwd(h, qcol, sv["mem_kv"], f"l{i}_mem")
        cat = jnp.concatenate([mix, mem_o], 1)
        y = _mm(cat, wts["w_out"][i], name=f"l{i}_out")
        x1 = _ln_fwd(x, y, _row(small["ln1_g"][i]), _row(small["ln1_b"][i]), f"l{i}_ln1")
        h1 = _mm(x1, wts["mlp_w1"][i], name=f"l{i}_w1")
        ff = _mm(h1, wts["mlp_w2"][i], a_fn=_relu2, name=f"l{i}_w2")
        x2 = _ln_fwd(x1, ff, _row(small["ln2_g"][i]), _row(small["ln2_b"][i]), f"l{i}_ln2")
        sv.update(h=h, qcol=qcol, cat=cat, y=y, x1=x1, h1=h1, ff=ff)
        saved.append(sv)
        x = x2

    loss_row, dx = _loss_and_grad(x, loss_target, "loss")

    gw = {k: [None] * len(v) for k, v in wts.items()}
    gs = {k: [None] * v.shape[0] for k, v in small.items()}
    gdt = COMM_DTYPE
    for i in reversed(range(DEPTH)):
        j = i // 2
        sv = saved[i]
        dr2, dg, db = _ln_bwd(sv["x1"], sv["ff"], _row(small["ln2_g"][i]), _row(small["ln2_b"][i]), dx, f"l{i}_ln2b")
        gs["ln2_g"][i], gs["ln2_b"][i] = dg[0], db[0]
        dh1 = _mm(dr2, wts["mlp_w2"][i], tb=True, epi=_relu2_bwd, e=sv["h1"], name=f"l{i}_dh1")
        gw["mlp_w2"][i] = _mm(sv["h1"], dr2, ta=True, a_fn=_relu2, out_dtype=gdt, name=f"l{i}_dw2")
        gw["mlp_w1"][i] = _mm(sv["x1"], dh1, ta=True, out_dtype=gdt, name=f"l{i}_dw1")
        dx1 = _mm(dh1, wts["mlp_w1"][i], tb=True, epi=_add_alpha, e=dr2, name=f"l{i}_dx1")
        dr1, dg, db = _ln_bwd(sv["x"], sv["y"], _row(small["ln1_g"][i]), _row(small["ln1_b"][i]), dx1, f"l{i}_ln1b")
        gs["ln1_g"][i], gs["ln1_b"][i] = dg[0], db[0]
        gw["w_out"][i] = _mm(sv["cat"], dr1, ta=True, out_dtype=gdt, name=f"l{i}_dwout")
        dcat = _mm(dr1, wts["w_out"][i], tb=True, name=f"l{i}_dcat")
        h = sv["h"]
        dqmem, dmem_kv = _mem_bwd(h, sv["qcol"], sv["mem_kv"], dcat, f"l{i}_memb")
        gw["mem_w_kv"][i] = _mm(mem, dmem_kv, ta=True, out_dtype=gdt, name=f"l{i}_dwmem")
        zpad = jnp.zeros((h.shape[0], LANES), F32)
        if i % 2 == 0:
            qnw, kvnw = _row(small["mla_q_norm"][j]), _row(small["mla_kv_norm"][j])
            dq = _flash_bwd_q(sv["q"], sv["kv"], sv["krr"], cs, sn, sv["o"], sv["lse"], dcat, f"l{i}_flashbq")
            dkv, dkrr = _flash_bwd_kv(sv["q"], sv["kv"], sv["krr"], cs, sn, sv["o"], sv["lse"], dcat, f"l{i}_flashbkv")
            gw["mla_w_ukv"][j] = _mm(sv["ckvn"], dkv, ta=True, out_dtype=gdt, name=f"l{i}_dwukv")
            dckvn = _mm(dkv, wts["mla_w_ukv"][j], tb=True, name=f"l{i}_dckvn")
            gw["mla_w_uq"][j] = _mm(sv["cqn"], dq, ta=True, out_dtype=gdt, name=f"l{i}_dwuq")
            dcqn = _mm(dq, wts["mla_w_uq"][j], tb=True, name=f"l{i}_dcqn")
            dcq, dckv, dkr, dqn, dkvn = _mla_prep_bwd(h, qnw, kvnw, cs, sn, dcqn, dckvn, dkrr, f"l{i}_mlaprepb")
            gs["mla_q_norm"][j], gs["mla_kv_norm"][j] = dqn[0], dkvn[0]
            dh = jnp.concatenate([dcq, dckv, dqmem, dkr, zpad], 1)
            w_in = wts["mla_w_in"][j]
            key = "mla_w_in"
        else:
            alog, dtb = _lane_row(small["gdn_a_log"][j]), _lane_row(small["gdn_dt_bias"][j])
            do, dz, dgn = _gdn_out_bwd(sv["o"], h, _row(small["gdn_o_norm"][j]), dcat, f"l{i}_gdnoutb")
            gs["gdn_o_norm"][j] = dgn[0]
            dqn, dkn, dvv, dgb, dbb = _gdn_core_bwd(sv["qn"], sv["kn"], sv["vv"], sv["gb"], sv["bb"], sv["states"], do, f"l{i}_gdncoreb")
            dconv, dab, dalog, ddtb = _gdn_prep_bwd(sv["conv"], h, alog, dtb, dqn, dkn, dvv, dgb, dbb, f"l{i}_gdnprepb")
            gs["gdn_a_log"][j], gs["gdn_dt_bias"][j] = dalog[0, :N_MIX_HEADS], ddtb[0, :N_MIX_HEADS]
            dhqkv, dconvw = _conv_bwd(h, small["gdn_conv"][j], dconv, f"l{i}_convb")
            gs["gdn_conv"][j] = dconvw
            dh = jnp.concatenate([dhqkv, dz, dqmem, dab, zpad], 1)
            w_in = wts["gdn_w_in"][j]
            key = "gdn_w_in"
        gw[key][j] = _mm(sv["x"], dh, ta=True, out_dtype=gdt, name=f"l{i}_dwin")
        dx = _mm(dh, w_in, tb=True, epi=_add_alpha, e=dr1, name=f"l{i}_dx")
    gs = {k: jnp.stack(v) for k, v in gs.items()}
    return loss_row, dx, gw, gs


def _mla_in_to_kernel(w):
    z32 = jnp.zeros((w.shape[0], 32), w.dtype)
    z128 = jnp.zeros((w.shape[0], LANES), w.dtype)
    return jnp.concatenate([w[:, :1024], w[:, 1088:1600], w[:, 1024:1056], z32, w[:, 1056:1088], z32, z128], 1)


def _mla_in_from_kernel(g):
    return jnp.concatenate([g[:, :1024], g[:, 1536:1568], g[:, 1600:1632], g[:, 1024:1536]], 1)


def _uq_to_kernel(w):
    w3 = w.reshape(Q_LORA, N_MIX_HEADS, QK_NOPE + QK_ROPE)
    z = jnp.zeros((Q_LORA, N_MIX_HEADS, 32), w.dtype)
    return jnp.concatenate([w3[:, :, :128], w3[:, :, 128:160], z, w3[:, :, 160:192], z], 2).reshape(Q_LORA, N_MIX_HEADS * Q_HEAD_P)


def _uq_from_kernel(g):
    g3 = g.reshape(Q_LORA, N_MIX_HEADS, Q_HEAD_P)
    return jnp.concatenate([g3[:, :, :128], g3[:, :, 128:160], g3[:, :, 192:224]], 2).reshape(Q_LORA, -1)


def _gdn_in_to_kernel(w):
    z = jnp.zeros((w.shape[0], LANES - 2 * N_MIX_HEADS + LANES), w.dtype)
    return jnp.concatenate([w[:, :6144], w[:, 6168:6680], w[:, 6144:6168], z], 1)


def _gdn_in_from_kernel(g):
    return jnp.concatenate([g[:, :6144], g[:, 6656:6680], g[:, 6144:6656]], 1)


def _exchange(srcs, name, gather):
    n = len(srcs)

    def body(*refs):
        src_refs, out_refs = refs[:n], refs[n:2 * n]
        send_sems, recv_sems, local_sems = refs[2 * n:]
        x, y, c = lax.axis_index("x"), lax.axis_index("y"), lax.axis_index("c")
        me = 4 * x + 2 * y + c
        local, sends, recvs = [], [], []
        for a in range(n):
            src_ref, out_ref = src_refs[a], out_refs[a]
            local.append(pltpu.make_async_copy(src_ref if gather else src_ref.at[me], out_ref.at[me], local_sems.at[a]))
            for kk in range(1, N_DEV):
                px, py, pc = x ^ ((kk >> 2) & 1), y ^ ((kk >> 1) & 1), c ^ (kk & 1)
                peer = 4 * px + 2 * py + pc
                piece = src_ref if gather else src_ref.at[peer]
                sends.append(pltpu.make_async_remote_copy(
                    src_ref=piece, dst_ref=out_ref.at[me], send_sem=send_sems.at[a, kk - 1], recv_sem=recv_sems.at[a, kk - 1],
                    device_id=(px, py, pc), device_id_type=pl.DeviceIdType.MESH))
                recvs.append(pltpu.make_async_remote_copy(
                    src_ref=piece, dst_ref=out_ref.at[peer], send_sem=send_sems.at[a, kk - 1], recv_sem=recv_sems.at[a, kk - 1],
                    device_id=(px, py, pc), device_id_type=pl.DeviceIdType.MESH))
        for cp in local + sends:
            cp.start()
        for cp in recvs:
            cp.wait_recv()
        for cp in sends:
            cp.wait_send()
        for cp in local:
            cp.wait()

    hbm = pl.BlockSpec(memory_space=pltpu.HBM)
    shapes = [_sds((N_DEV,) + tuple(s.shape if gather else s.shape[1:]), s.dtype) for s in srcs]
    return pl.pallas_call(
        body, in_specs=[hbm] * n, out_specs=[hbm] * n, out_shape=shapes, name=name,
        scratch_shapes=[pltpu.SemaphoreType.DMA((n, N_DEV - 1)), pltpu.SemaphoreType.DMA((n, N_DEV - 1)),
                        pltpu.SemaphoreType.DMA((n,))],
    )(*srcs)


def _adamw(parts, w, m, v, name):
    r, c = w.shape
    tr = _pick(r, tuple(t for t in (256, 128, 64, 32, 16, 8) if t * c <= 256 * 1024))

    def body(p_ref, w_ref, m_ref, v_ref, g_ref, d_ref, nm_ref, nv_ref):
        g = p_ref[0].astype(F32)
        for dd in range(1, N_DEV):
            g = g + p_ref[dd].astype(F32)
        nm = ADAM_B1 * m_ref[...] + (1.0 - ADAM_B1) * g
        nv = ADAM_B2 * v_ref[...] + (1.0 - ADAM_B2) * jnp.square(g)
        m_hat = nm / (1.0 - ADAM_B1 ** ADAM_STEP)
        v_hat = nv / (1.0 - ADAM_B2 ** ADAM_STEP)
        g_ref[...] = g
        d_ref[...] = -ADAM_LR * (m_hat / (jnp.sqrt(v_hat) + ADAM_EPS) + ADAM_WD * w_ref[...])
        nm_ref[...] = nm
        nv_ref[...] = nv

    blk = pl.BlockSpec((tr, c), lambda i: (i, 0))
    return _pc(body, grid=(r // tr,), in_specs=[pl.BlockSpec((N_DEV, tr, c), lambda i: (0, i, 0)), blk, blk, blk],
               out_specs=[blk] * 4, out_shape=[_sds((r, c))] * 4, name=name)(parts, w, m, v)


BIG = (("mla_w_in", 1), ("mla_w_uq", 2), ("mla_w_ukv", 2), ("gdn_w_in", 2), ("mem_w_kv", 1), ("w_out", 1),
       ("mlp_w1", 2), ("mlp_w2", 1), ("gdn_conv", 2))
SMALL = ("mla_q_norm", "mla_kv_norm", "gdn_a_log", "gdn_dt_bias", "gdn_o_norm", "ln1_g", "ln1_b", "ln2_g", "ln2_b")
PACK_COLS = 1024


def _unshard(pieces, axis):
    t = jnp.moveaxis(pieces, 0, axis)
    return t.reshape(t.shape[:axis] + (t.shape[axis] * t.shape[axis + 1],) + t.shape[axis + 2:])


def _shard(full, axis):
    t = full.reshape(full.shape[:axis] + (N_DEV, full.shape[axis] // N_DEV) + full.shape[axis + 1:])
    return jnp.moveaxis(t, axis, 0)


def _pack(arrs, lead, cols, mult):
    lead_shape = arrs[0].shape[:lead]
    flat = jnp.concatenate([a.reshape(lead_shape + (-1,)) for a in arrs], -1)
    n = flat.shape[-1]
    r = -(-n // (cols * mult)) * mult
    flat = jnp.pad(flat, [(0, 0)] * lead + [(0, r * cols - n)])
    return flat.reshape(lead_shape + (r, cols))


def _unpack(buf, lead, shapes):
    lead_shape = buf.shape[:lead]
    flat = buf.reshape(lead_shape + (-1,))
    out, off = [], 0
    for shp in shapes:
        n = 1
        for d in shp:
            n *= d
        out.append(flat[..., off:off + n].reshape(lead_shape + tuple(shp)))
        off += n
    return out


def kernel(x, mem, positions, mla_w_in, mla_q_norm, mla_w_uq, mla_kv_norm, mla_w_ukv, gdn_w_in, gdn_conv, gdn_a_log, gdn_dt_bias, gdn_o_norm, mem_w_kv, w_out, ln1_g, ln1_b, mlp_w1, mlp_w2, ln2_g, ln2_b, loss_target, m_mla_w_in, m_mla_q_norm, m_mla_w_uq, m_mla_kv_norm, m_mla_w_ukv, m_gdn_w_in, m_gdn_conv, m_gdn_a_log, m_gdn_dt_bias, m_gdn_o_norm, m_mem_w_kv, m_w_out, m_ln1_g, m_ln1_b, m_mlp_w1, m_mlp_w2, m_ln2_g, m_ln2_b, v_mla_w_in, v_mla_q_norm, v_mla_w_uq, v_mla_kv_norm, v_mla_w_ukv, v_gdn_w_in, v_gdn_conv, v_gdn_a_log, v_gdn_dt_bias, v_gdn_o_norm, v_mem_w_kv, v_w_out, v_ln1_g, v_ln1_b, v_mlp_w1, v_mlp_w2, v_ln2_g, v_ln2_b):
    args = dict(locals())
    w_loc = {n: args[n] for n, _ in BIG}
    m_loc = {n: args["m_" + n] for n, _ in BIG}
    v_loc = {n: args["v_" + n] for n, _ in BIG}
    small = {n: args[n] for n in SMALL}
    axis_of = dict(BIG)
    mm_names = [n for n, _ in BIG if n != "gdn_conv"]

    names = [n for n, _ in BIG]
    got = _exchange([w_loc[n].astype(COMM_DTYPE) for n in mm_names] + [w_loc["gdn_conv"]], "gather_weights", gather=True)
    full = {n: _unshard(p, axis_of[n]) for n, p in zip(names, got)}
    conv_full = full.pop("gdn_conv")

    wts = {n: [full[n][l] for l in range(full[n].shape[0])] for n in mm_names}
    wts["mla_w_in"] = [_mla_in_to_kernel(w) for w in wts["mla_w_in"]]
    wts["mla_w_uq"] = [_uq_to_kernel(w) for w in wts["mla_w_uq"]]
    wts["gdn_w_in"] = [_gdn_in_to_kernel(w) for w in wts["gdn_w_in"]]
    small_in = dict(small, gdn_conv=conv_full)

    loss_row, grad_x, gw, gs = _local_step(x[0], mem[0], positions[0], loss_target[0], wts, small_in)
    loss = lax.psum(loss_row[0, 0], ("x", "y", "c"))

    gw["mla_w_in"] = [_mla_in_from_kernel(g) for g in gw["mla_w_in"]]
    gw["mla_w_uq"] = [_uq_from_kernel(g) for g in gw["mla_w_uq"]]
    gw["gdn_w_in"] = [_gdn_in_from_kernel(g) for g in gw["gdn_w_in"]]
    gfull = {n: jnp.stack(gw[n]) for n in mm_names}
    gfull["gdn_conv"] = gs.pop("gdn_conv").astype(COMM_DTYPE)
    g_got = _exchange([_shard(gfull[n], axis_of[n]) for n in names], "exchange_grads", gather=False)
    big_out = [{}, {}, {}, {}]
    for n, parts in zip(names, g_got):
        shp = w_loc[n].shape
        rows, cols = shp[0] * shp[1], shp[2]
        res = _adamw(parts.reshape(N_DEV, rows, cols), w_loc[n].reshape(rows, cols), m_loc[n].reshape(rows, cols),
                     v_loc[n].reshape(rows, cols), f"adamw_{n}")
        for kind in range(4):
            big_out[kind][n] = res[kind].reshape(shp)

    s_sent = _pack([gs[n] for n in SMALL], 0, LANES, 8)
    s_got = _exchange([s_sent], "gather_small_grads", gather=True)[0]
    small_out = [dict(zip(SMALL, _unpack(o, 0, [small[n].shape for n in SMALL])))
                 for o in _adamw(s_got, _pack([small[n] for n in SMALL], 0, LANES, 8),
                                 _pack([args["m_" + n] for n in SMALL], 0, LANES, 8),
                                 _pack([args["v_" + n] for n in SMALL], 0, LANES, 8), "adamw_small")]

    order = ["mla_w_in", "mla_q_norm", "mla_w_uq", "mla_kv_norm", "mla_w_ukv", "gdn_w_in", "gdn_conv", "gdn_a_log",
             "gdn_dt_bias", "gdn_o_norm", "mem_w_kv", "w_out", "ln1_g", "ln1_b", "mlp_w1", "mlp_w2", "ln2_g", "ln2_b"]
    outs = [loss, grad_x[None]]
    for kind in range(4):
        for n in order:
            outs.append(big_out[kind][n] if n in axis_of else small_out[kind][n])
    return tuple(outs)
```

```python
import functools

import jax
import jax.numpy as jnp
from jax import lax
from jax.experimental import pallas as pl
from jax.experimental.pallas import tpu as pltpu

F32 = jnp.float32
MXU_DTYPE = jnp.bfloat16
COMM_DTYPE = jnp.bfloat16

N_DEV = 8
D_MODEL = 2048
DEPTH = 4
HEAD_DIM = 128
N_MIX_HEADS = 12
N_MEM_HEADS = 4
MIX_WIDTH = N_MIX_HEADS * HEAD_DIM
MEM_WIDTH = N_MEM_HEADS * HEAD_DIM
Q_LORA = 512
KV_LORA = 512
QK_NOPE = 128
QK_ROPE = 64
ROPE_THETA = 10000.0
CONV_WIDTH = 4
CHUNK = 64
D_FF = 4 * D_MODEL
ALPHA = (2 * DEPTH) ** 0.25
LN_EPS = 1e-5
RMS_EPS = 1e-6
MLA_IN = Q_LORA + KV_LORA + QK_ROPE + MEM_WIDTH
GDN_IN = 4 * MIX_WIDTH + 2 * N_MIX_HEADS + MEM_WIDTH
MLA_IN_P = 1792
GDN_IN_P = 6912
Q_HEAD_P = 256
ATT_SCALE = (QK_NOPE + QK_ROPE) ** -0.5
ADAM_LR, ADAM_B1, ADAM_B2, ADAM_EPS, ADAM_WD, ADAM_STEP = 0.001, 0.9, 0.999, 1e-08, 0.01, 10
LANES = 128
VMEM_LIMIT = 56 * 1024 * 1024

NN = (((1,), (0,)), ((), ()))
NT = (((1,), (1,)), ((), ()))
TN = (((0,), (0,)), ((), ()))
HI = lax.Precision.HIGHEST


def _dot(a, b, dims, hi=False):
    if hi:
        return lax.dot_general(a, b, dims, precision=HI, preferred_element_type=F32)
    return lax.dot_general(a.astype(MXU_DTYPE), b.astype(MXU_DTYPE), dims, preferred_element_type=F32)


@functools.partial(jax.custom_vjp, nondiff_argnums=(2,))
def _nn_d(a, b, hi):
    return _dot(a, b, NN, hi)


@functools.partial(jax.custom_vjp, nondiff_argnums=(2,))
def _nt_d(a, b, hi):
    return _dot(a, b, NT, hi)


@functools.partial(jax.custom_vjp, nondiff_argnums=(2,))
def _tn_d(a, b, hi):
    return _dot(a, b, TN, hi)


_nn_d.defvjp(lambda a, b, hi: (_dot(a, b, NN, hi), (a, b)),
             lambda hi, r, g: (_nt_d(g, r[1], hi), _tn_d(r[0], g, hi)))
_nt_d.defvjp(lambda a, b, hi: (_dot(a, b, NT, hi), (a, b)),
             lambda hi, r, g: (_nn_d(g, r[1], hi), _tn_d(g, r[0], hi)))
_tn_d.defvjp(lambda a, b, hi: (_dot(a, b, TN, hi), (a, b)),
             lambda hi, r, g: (_nt_d(r[1], g, hi), _nn_d(r[0], g, hi)))


class _RawOps:
    nn = staticmethod(lambda a, b, hi=False: _dot(a, b, NN, hi))
    nt = staticmethod(lambda a, b, hi=False: _dot(a, b, NT, hi))
    tn = staticmethod(lambda a, b, hi=False: _dot(a, b, TN, hi))


class _DiffOps:
    nn = staticmethod(lambda a, b, hi=False: _nn_d(a, b, hi))
    nt = staticmethod(lambda a, b, hi=False: _nt_d(a, b, hi))
    tn = staticmethod(lambda a, b, hi=False: _tn_d(a, b, hi))


def _pick(dim, cands=(512, 384, 256, 128)):
    for c in cands:
        if dim % c == 0:
            return c
    return dim


def _pc(body, *, grid, in_specs, out_specs, out_shape, name, scratch=()):
    return pl.pallas_call(
        body, grid=grid, in_specs=in_specs, out_specs=out_specs, out_shape=out_shape,
        scratch_shapes=list(scratch), name=name,
        compiler_params=pltpu.CompilerParams(dimension_semantics=("arbitrary",) * len(grid),
                                             vmem_limit_bytes=VMEM_LIMIT))


def _rows(ts, w, cb=0):
    return pl.BlockSpec((ts, w), lambda i, *_: (i, cb))


def _whole(shape):
    return pl.BlockSpec(shape, lambda *_: (0,) * len(shape))


def _sds(shape, dtype=F32):
    return jax.ShapeDtypeStruct(shape, dtype)


def _mm(a, b, *, name, ta=False, tb=False, out_dtype=F32, a_fn=None, epi=None, e=None):
    m, k = (a.shape[1], a.shape[0]) if ta else a.shape
    n = b.shape[0] if tb else b.shape[1]
    assert k == (b.shape[1] if tb else b.shape[0]), (a.shape, b.shape, ta, tb)
    tm, tn = _pick(m, (1024, 512, 384, 256, 128)), _pick(n, (1024, 512, 384, 256, 128))
    tk = _pick(k, (1024, 512, 384, 256, 128))
    nk = k // tk
    dims = (((0 if ta else 1,), (1 if tb else 0,)), ((), ()))

    def body(*refs):
        if e is None:
            a_ref, b_ref, o_ref, acc = refs
        else:
            a_ref, b_ref, e_ref, o_ref, acc = refs
        kk = pl.program_id(2)

        @pl.when(kk == 0)
        def _():
            acc[...] = jnp.zeros_like(acc)

        av = a_ref[...]
        if a_fn is not None:
            av = a_fn(av.astype(F32))
        acc[...] += lax.dot_general(av.astype(MXU_DTYPE), b_ref[...].astype(MXU_DTYPE), dims,
                                    preferred_element_type=F32)

        @pl.when(kk == nk - 1)
        def _():
            r = acc[...]
            if epi is not None:
                r = epi(r, e_ref[...].astype(F32))
            o_ref[...] = r.astype(out_dtype)

    a_spec = pl.BlockSpec((tk, tm), lambda i, j, kk: (kk, i)) if ta else pl.BlockSpec((tm, tk), lambda i, j, kk: (i, kk))
    b_spec = pl.BlockSpec((tn, tk), lambda i, j, kk: (j, kk)) if tb else pl.BlockSpec((tk, tn), lambda i, j, kk: (kk, j))
    o_spec = pl.BlockSpec((tm, tn), lambda i, j, kk: (i, j))
    ins, specs = [a, b], [a_spec, b_spec]
    if e is not None:
        assert e.shape == (m, n)
        ins.append(e)
        specs.append(o_spec)
    return _pc(body, grid=(m // tm, n // tn, nk), in_specs=specs, out_specs=o_spec,
               out_shape=_sds((m, n), out_dtype), name=name, scratch=[pltpu.VMEM((tm, tn), F32)])(*ins)


def _relu2(v):
    r = jnp.maximum(v, 0.0)
    return r * r


def _relu2_bwd(acc, h1):
    return acc * (2.0 * jnp.maximum(h1, 0.0))


def _add_alpha(acc, dr):
    return acc + ALPHA * dr


def _ln(r, g, b):
    mu = jnp.mean(r, -1, keepdims=True)
    var = jnp.mean(jnp.square(r - mu), -1, keepdims=True)
    return (r - mu) * lax.rsqrt(var + LN_EPS) * g + b


def _ln_fwd(x, y, g, b, name):
    s, d = x.shape
    ts = _pick(s, (256, 128))

    def body(x_ref, y_ref, g_ref, b_ref, o_ref):
        o_ref[...] = _ln(ALPHA * x_ref[...] + y_ref[...], g_ref[...], b_ref[...])

    return _pc(body, grid=(s // ts,), in_specs=[_rows(ts, d), _rows(ts, d), _whole((1, d)), _whole((1, d))],
               out_specs=_rows(ts, d), out_shape=_sds((s, d)), name=name)(x, y, g, b)


def _ln_bwd(x, y, g, b, dout, name):
    s, d = x.shape
    ts = _pick(s, (256, 128))

    def body(x_ref, y_ref, g_ref, b_ref, do_ref, dr_ref, dg_ref, db_ref):
        @pl.when(pl.program_id(0) == 0)
        def _():
            dg_ref[...] = jnp.zeros_like(dg_ref)
            db_ref[...] = jnp.zeros_like(db_ref)

        r = ALPHA * x_ref[...] + y_ref[...]
        _, vjp = jax.vjp(_ln, r, g_ref[...], b_ref[...])
        dr, dg, db = vjp(do_ref[...])
        dr_ref[...] = dr
        dg_ref[...] += dg
        db_ref[...] += db

    return _pc(body, grid=(s // ts,),
               in_specs=[_rows(ts, d), _rows(ts, d), _whole((1, d)), _whole((1, d)), _rows(ts, d)],
               out_specs=[_rows(ts, d), _whole((1, d)), _whole((1, d))],
               out_shape=[_sds((s, d)), _sds((1, d)), _sds((1, d))], name=name)(x, y, g, b, dout)


def _loss_and_grad(y, target, name):
    s, d = y.shape
    ts = _pick(s, (256, 128))

    def body(y_ref, t_ref, l_ref, dy_ref):
        @pl.when(pl.program_id(0) == 0)
        def _():
            l_ref[...] = jnp.zeros_like(l_ref)

        diff = y_ref[...] - t_ref[...]
        per_tok = jnp.mean(jnp.square(diff), -1, keepdims=True)
        l_ref[...] += 0.5 * jnp.sum(per_tok, 0, keepdims=True) * jnp.ones((1, LANES), F32)
        dy_ref[...] = diff * (1.0 / d)

    return _pc(body, grid=(s // ts,), in_specs=[_rows(ts, d), _rows(ts, d)],
               out_specs=[_whole((1, LANES)), _rows(ts, d)],
               out_shape=[_sds((1, LANES)), _sds((s, d))], name=name)(y, target)


def _rope(blk, cs, sn):
    return blk * cs + pltpu.roll(blk, 64, 1) * sn


def _rope_t(dblk, cs, sn):
    return dblk * cs + pltpu.roll(dblk * sn, 64, 1)


def _rms(v, g):
    return v * lax.rsqrt(jnp.mean(v * v, -1, keepdims=True) + RMS_EPS) * g


def _mla_prep_fwd(h, qn, kvn, cs, sn, name):
    s = h.shape[0]
    ts = _pick(s, (512, 256, 128))

    def body(cq_ref, ckv_ref, kr_ref, qn_ref, kvn_ref, cs_ref, sn_ref, cqn_ref, ckvn_ref, krr_ref):
        cqn_ref[...] = _rms(cq_ref[...], qn_ref[...])
        ckvn_ref[...] = _rms(ckv_ref[...], kvn_ref[...])
        krr_ref[...] = _rope(kr_ref[...], cs_ref[...], sn_ref[...])

    return _pc(body, grid=(s // ts,),
               in_specs=[_rows(ts, 512, 0), _rows(ts, 512, 1), _rows(ts, LANES, 12), _whole((1, 512)), _whole((1, 512)),
                         _rows(ts, LANES), _rows(ts, LANES)],
               out_specs=[_rows(ts, 512), _rows(ts, 512), _rows(ts, LANES)],
               out_shape=[_sds((s, 512)), _sds((s, 512)), _sds((s, LANES))], name=name)(h, h, h, qn, kvn, cs, sn)


def _mla_prep_bwd(h, qn, kvn, cs, sn, dcqn, dckvn, dkrr_heads, name):
    s = h.shape[0]
    ts = _pick(s, (512, 256, 128))

    def body(cq_ref, ckv_ref, qn_ref, kvn_ref, cs_ref, sn_ref, dcqn_ref, dckvn_ref, dkrr_ref,
             dcq_ref, dckv_ref, dkr_ref, dqn_ref, dkvn_ref):
        @pl.when(pl.program_id(0) == 0)
        def _():
            dqn_ref[...] = jnp.zeros_like(dqn_ref)
            dkvn_ref[...] = jnp.zeros_like(dkvn_ref)

        _, vjp = jax.vjp(_rms, cq_ref[...], qn_ref[...])
        dcq, dqn = vjp(dcqn_ref[...])
        dcq_ref[...] = dcq
        dqn_ref[...] += dqn
        _, vjp = jax.vjp(_rms, ckv_ref[...], kvn_ref[...])
        dckv, dkvn = vjp(dckvn_ref[...])
        dckv_ref[...] = dckv
        dkvn_ref[...] += dkvn
        dkrr = dkrr_ref[0]
        for hh in range(1, N_MIX_HEADS):
            dkrr = dkrr + dkrr_ref[hh]
        dkr_ref[...] = _rope_t(dkrr, cs_ref[...], sn_ref[...])

    heads3 = pl.BlockSpec((N_MIX_HEADS, ts, LANES), lambda i: (0, i, 0))
    return _pc(body, grid=(s // ts,),
               in_specs=[_rows(ts, 512, 0), _rows(ts, 512, 1), _whole((1, 512)), _whole((1, 512)),
                         _rows(ts, LANES), _rows(ts, LANES), _rows(ts, 512), _rows(ts, 512), heads3],
               out_specs=[_rows(ts, 512), _rows(ts, 512), _rows(ts, LANES), _whole((1, 512)), _whole((1, 512))],
               out_shape=[_sds((s, 512)), _sds((s, 512)), _sds((s, LANES)), _sds((1, 512)), _sds((1, 512))],
               name=name)(h, h, qn, kvn, cs, sn, dcqn, dckvn, dkrr_heads)


def _att_tiles(s):
    t = _pick(s, (512, 256, 128))
    return t, s // t


ATT_HEADS_PER_STEP = 2


def _tri_pairs(nb, k_major):
    pairs = [(i, j) for j in range(nb) for i in range(j, nb)] if k_major else [(i, j) for i in range(nb) for j in range(i + 1)]
    return jnp.asarray([p[0] for p in pairs], jnp.int32), jnp.asarray([p[1] for p in pairs], jnp.int32)


def _tril(t):
    return lax.broadcasted_iota(jnp.int32, (t, t), 1) <= lax.broadcasted_iota(jnp.int32, (t, t), 0)


def _pc_pairs(body, pairs, *, n_groups, in_specs, out_specs, out_shape, name, scratch, args):
    return pl.pallas_call(
        body, out_shape=out_shape, name=name,
        grid_spec=pltpu.PrefetchScalarGridSpec(num_scalar_prefetch=2, grid=(n_groups, pairs[0].shape[0]), in_specs=in_specs,
                                               out_specs=out_specs, scratch_shapes=list(scratch)),
        compiler_params=pltpu.CompilerParams(dimension_semantics=("arbitrary", "arbitrary"), vmem_limit_bytes=VMEM_LIMIT),
    )(*pairs, *args)


def _att_specs(t):
    qrow = lambda w: pl.BlockSpec((t, w), lambda h, p, qi, kj: (qi[p], h))
    krow = lambda w: pl.BlockSpec((t, w), lambda h, p, qi, kj: (kj[p], h))
    qtab = pl.BlockSpec((t, LANES), lambda h, p, qi, kj: (qi[p], 0))
    ktab = pl.BlockSpec((t, LANES), lambda h, p, qi, kj: (kj[p], 0))
    return qrow, krow, qtab, ktab


def _flash_fwd(q, kv, krr, cs, sn, name):
    s = q.shape[0]
    t, nb = _att_tiles(s)
    ah = ATT_HEADS_PER_STEP

    def body(qi_ref, kj_ref, q_ref, cs_ref, sn_ref, kv_ref, kr_ref, o_ref, lse_ref, qn_s, qr_s, m_s, l_s, acc_s):
        i, j = qi_ref[pl.program_id(1)], kj_ref[pl.program_id(1)]

        @pl.when(j == 0)
        def _():
            for hh in range(ah):
                c0 = hh * Q_HEAD_P
                qn_s[hh] = q_ref[:, c0:c0 + LANES].astype(MXU_DTYPE)
                qr_s[hh] = _rope(q_ref[:, c0 + LANES:c0 + Q_HEAD_P], cs_ref[...], sn_ref[...]).astype(MXU_DTYPE)
            m_s[...] = jnp.full_like(m_s, -jnp.inf)
            l_s[...] = jnp.zeros_like(l_s)
            acc_s[...] = jnp.zeros_like(acc_s)

        def update(masked):
            kr = kr_ref[...].astype(MXU_DTYPE)
            for hh in range(ah):
                c0 = hh * Q_HEAD_P
                sc = (_dot(qn_s[hh], kv_ref[:, c0:c0 + LANES], NT) + _dot(qr_s[hh], kr, NT)) * ATT_SCALE
                if masked:
                    sc = jnp.where(_tril(t), sc, -jnp.inf)
                m_old = m_s[hh]
                m_new = jnp.maximum(m_old, jnp.max(sc, -1, keepdims=True))
                p = jnp.exp(sc - m_new[:, :1])
                corr = jnp.exp(m_old - m_new)
                l_s[hh] = corr * l_s[hh] + jnp.sum(p, -1, keepdims=True)
                acc_s[hh] = corr * acc_s[hh] + _dot(p, kv_ref[:, c0 + LANES:c0 + Q_HEAD_P], NN)
                m_s[hh] = m_new

        @pl.when(j < i)
        def _():
            update(False)

        @pl.when(j == i)
        def _():
            update(True)
            for hh in range(ah):
                sl = slice(hh * LANES, (hh + 1) * LANES)
                o_ref[:, sl] = acc_s[hh] / l_s[hh]
                lse_ref[:, sl] = m_s[hh] + jnp.log(l_s[hh])

    qrow, krow, qtab, ktab = _att_specs(t)
    return _pc_pairs(body, _tri_pairs(nb, False), n_groups=N_MIX_HEADS // ah,
                     in_specs=[qrow(ah * Q_HEAD_P), qtab, qtab, krow(ah * Q_HEAD_P), ktab],
                     out_specs=[qrow(ah * LANES), qrow(ah * LANES)],
                     out_shape=[_sds((s, MIX_WIDTH)), _sds((s, MIX_WIDTH))], name=name,
                     scratch=[pltpu.VMEM((ah, t, LANES), MXU_DTYPE), pltpu.VMEM((ah, t, LANES), MXU_DTYPE),
                              pltpu.VMEM((ah, t, LANES), F32), pltpu.VMEM((ah, t, LANES), F32), pltpu.VMEM((ah, t, LANES), F32)],
                     args=(q, cs, sn, kv, krr))


def _flash_bwd_q(q, kv, krr, cs, sn, o, lse, dcat, name):
    s = q.shape[0]
    t, nb = _att_tiles(s)
    ah = ATT_HEADS_PER_STEP

    def body(qi_ref, kj_ref, q_ref, cs_ref, sn_ref, o_ref, lse_ref, do_ref, kv_ref, kr_ref, dq_ref,
             qn_s, qr_s, dl_s, an_s, ar_s):
        i, j = qi_ref[pl.program_id(1)], kj_ref[pl.program_id(1)]

        @pl.when(j == 0)
        def _():
            for hh in range(ah):
                c0 = hh * Q_HEAD_P
                sl = slice(hh * LANES, (hh + 1) * LANES)
                qn_s[hh] = q_ref[:, c0:c0 + LANES].astype(MXU_DTYPE)
                qr_s[hh] = _rope(q_ref[:, c0 + LANES:c0 + Q_HEAD_P], cs_ref[...], sn_ref[...]).astype(MXU_DTYPE)
                dl_s[hh] = jnp.sum(o_ref[:, sl] * do_ref[:, sl], -1, keepdims=True) * jnp.ones((1, LANES), F32)
            an_s[...] = jnp.zeros_like(an_s)
            ar_s[...] = jnp.zeros_like(ar_s)

        def update(masked):
            kr = kr_ref[...].astype(MXU_DTYPE)
            for hh in range(ah):
                c0 = hh * Q_HEAD_P
                sl = slice(hh * LANES, (hh + 1) * LANES)
                kn = kv_ref[:, c0:c0 + LANES].astype(MXU_DTYPE)
                sc = (_dot(qn_s[hh], kn, NT) + _dot(qr_s[hh], kr, NT)) * ATT_SCALE
                p = jnp.exp(sc - lse_ref[:, hh * LANES:hh * LANES + 1])
                if masked:
                    p = jnp.where(_tril(t), p, 0.0)
                dp = _dot(do_ref[:, sl], kv_ref[:, c0 + LANES:c0 + Q_HEAD_P], NT)
                ds = (p * (dp - dl_s[hh][:, :1]) * ATT_SCALE).astype(MXU_DTYPE)
                an_s[hh] += _dot(ds, kn, NN)
                ar_s[hh] += _dot(ds, kr, NN)

        @pl.when(j < i)
        def _():
            update(False)

        @pl.when(j == i)
        def _():
            update(True)
            for hh in range(ah):
                c0 = hh * Q_HEAD_P
                dq_ref[:, c0:c0 + LANES] = an_s[hh]
                dq_ref[:, c0 + LANES:c0 + Q_HEAD_P] = _rope_t(ar_s[hh], cs_ref[...], sn_ref[...])

    qrow, krow, qtab, ktab = _att_specs(t)
    return _pc_pairs(body, _tri_pairs(nb, False), n_groups=N_MIX_HEADS // ah,
                     in_specs=[qrow(ah * Q_HEAD_P), qtab, qtab, qrow(ah * LANES), qrow(ah * LANES), qrow(ah * LANES),
                               krow(ah * Q_HEAD_P), ktab],
                     out_specs=qrow(ah * Q_HEAD_P), out_shape=_sds((s, N_MIX_HEADS * Q_HEAD_P)), name=name,
                     scratch=[pltpu.VMEM((ah, t, LANES), MXU_DTYPE), pltpu.VMEM((ah, t, LANES), MXU_DTYPE),
                              pltpu.VMEM((ah, t, LANES), F32), pltpu.VMEM((ah, t, LANES), F32), pltpu.VMEM((ah, t, LANES), F32)],
                     args=(q, cs, sn, o, lse, dcat, kv, krr))


def _flash_bwd_kv(q, kv, krr, cs, sn, o, lse, dcat, name):
    s = q.shape[0]
    t, nb = _att_tiles(s)
    ah = ATT_HEADS_PER_STEP

    def body(qi_ref, kj_ref, kv_ref, kr_ref, q_ref, cs_ref, sn_ref, o_ref, lse_ref, do_ref, dkv_ref, dkr_ref,
             akn_s, av_s, akr_s):
        i, j = qi_ref[pl.program_id(1)], kj_ref[pl.program_id(1)]

        @pl.when(i == j)
        def _():
            akn_s[...] = jnp.zeros_like(akn_s)
            av_s[...] = jnp.zeros_like(av_s)
            akr_s[...] = jnp.zeros_like(akr_s)

        def update(masked):
            kr = kr_ref[...].astype(MXU_DTYPE)
            for hh in range(ah):
                c0 = hh * Q_HEAD_P
                sl = slice(hh * LANES, (hh + 1) * LANES)
                qn = q_ref[:, c0:c0 + LANES].astype(MXU_DTYPE)
                qr = _rope(q_ref[:, c0 + LANES:c0 + Q_HEAD_P], cs_ref[...], sn_ref[...]).astype(MXU_DTYPE)
                do = do_ref[:, sl]
                sc = (_dot(qn, kv_ref[:, c0:c0 + LANES], NT) + _dot(qr, kr, NT)) * ATT_SCALE
                p = jnp.exp(sc - lse_ref[:, hh * LANES:hh * LANES + 1])
                if masked:
                    p = jnp.where(_tril(t), p, 0.0)
                dp = _dot(do, kv_ref[:, c0 + LANES:c0 + Q_HEAD_P], NT)
                dl = jnp.sum(o_ref[:, sl] * do, -1, keepdims=True)
                ds = (p * (dp - dl) * ATT_SCALE).astype(MXU_DTYPE)
                av_s[hh] += _dot(p, do, TN)
                akn_s[hh] += _dot(ds, qn, TN)
                akr_s[hh] += _dot(ds, qr, TN)

        @pl.when(i > j)
        def _():
            update(False)

        @pl.when(i == j)
        def _():
            update(True)

        @pl.when(i == nb - 1)
        def _():
            for hh in range(ah):
                c0 = hh * Q_HEAD_P
                dkv_ref[:, c0:c0 + LANES] = akn_s[hh]
                dkv_ref[:, c0 + LANES:c0 + Q_HEAD_P] = av_s[hh]
                dkr_ref[hh] = akr_s[hh]

    qrow, krow, qtab, ktab = _att_specs(t)
    return _pc_pairs(body, _tri_pairs(nb, True), n_groups=N_MIX_HEADS // ah,
                     in_specs=[krow(ah * Q_HEAD_P), ktab, qrow(ah * Q_HEAD_P), qtab, qtab,
                               qrow(ah * LANES), qrow(ah * LANES), qrow(ah * LANES)],
                     out_specs=[krow(ah * Q_HEAD_P), pl.BlockSpec((ah, t, LANES), lambda h, p, qi, kj: (h, kj[p], 0))],
                     out_shape=[_sds((s, N_MIX_HEADS * Q_HEAD_P)), _sds((N_MIX_HEADS, s, LANES))], name=name,
                     scratch=[pltpu.VMEM((ah, t, LANES), F32)] * 3,
                     args=(kv, krr, q, cs, sn, o, lse, dcat))


def _mem_head(ops, qh, kh, vh):
    sc = ops.nt(qh, kh) * HEAD_DIM ** -0.5
    e = jnp.exp(sc - lax.stop_gradient(jnp.max(sc, -1, keepdims=True)))
    p = e / jnp.sum(e, -1, keepdims=True)
    return ops.nn(p, vh)


def _mem_fwd(h, qcol, mem_kv, name):
    s = h.shape[0]
    m = mem_kv.shape[0]
    ts = _pick(s, (512, 256, 128))

    def body(q_ref, kv_ref, o_ref):
        for hh in range(N_MEM_HEADS):
            sl = slice(hh * HEAD_DIM, (hh + 1) * HEAD_DIM)
            vsl = slice(MEM_WIDTH + hh * HEAD_DIM, MEM_WIDTH + (hh + 1) * HEAD_DIM)
            o_ref[:, sl] = _mem_head(_RawOps, q_ref[:, sl], kv_ref[:, sl], kv_ref[:, vsl])

    return _pc(body, grid=(s // ts,), in_specs=[_rows(ts, MEM_WIDTH, qcol), _whole((m, 2 * MEM_WIDTH))],
               out_specs=_rows(ts, MEM_WIDTH), out_shape=_sds((s, MEM_WIDTH)), name=name)(h, mem_kv)


def _mem_bwd(h, qcol, mem_kv, dcat, name):
    s = h.shape[0]
    m = mem_kv.shape[0]
    ts = _pick(s, (512, 256, 128))

    def body(q_ref, kv_ref, do_ref, dq_ref, dkv_ref):
        @pl.when(pl.program_id(0) == 0)
        def _():
            dkv_ref[...] = jnp.zeros_like(dkv_ref)

        for hh in range(N_MEM_HEADS):
            sl = slice(hh * HEAD_DIM, (hh + 1) * HEAD_DIM)
            vsl = slice(MEM_WIDTH + hh * HEAD_DIM, MEM_WIDTH + (hh + 1) * HEAD_DIM)
            _, vjp = jax.vjp(functools.partial(_mem_head, _DiffOps), q_ref[:, sl], kv_ref[:, sl], kv_ref[:, vsl])
            dq, dk, dv = vjp(do_ref[:, sl])
            dq_ref[:, sl] = dq
            dkv_ref[:, sl] += dk
            dkv_ref[:, vsl] += dv

    return _pc(body, grid=(s // ts,),
               in_specs=[_rows(ts, MEM_WIDTH, qcol), _whole((m, 2 * MEM_WIDTH)), _rows(ts, MEM_WIDTH, 3)],
               out_specs=[_rows(ts, MEM_WIDTH), _whole((m, 2 * MEM_WIDTH))],
               out_shape=[_sds((s, MEM_WIDTH)), _sds((m, 2 * MEM_WIDTH))], name=name)(h, mem_kv, dcat)


CONV_COLS = 3 * MIX_WIDTH
HALO = 8


def _conv_fwd(h, w, name):
    s = h.shape[0]
    ts = _pick(s, (512, 256, 128))
    wc = 512

    def body(x_ref, halo_ref, w_ref, o_ref, ext):
        i = pl.program_id(0)
        ext[pl.ds(0, HALO), :] = jnp.where(i > 0, halo_ref[...], 0.0)
        ext[pl.ds(HALO, ts), :] = x_ref[...]
        acc = w_ref[0:1, :] * ext[pl.ds(HALO - 3, ts), :]
        for j in range(1, CONV_WIDTH):
            acc = acc + w_ref[j:j + 1, :] * ext[pl.ds(HALO - 3 + j, ts), :]
        o_ref[...] = acc

    halo = pl.BlockSpec((HALO, wc), lambda i, c: (jnp.maximum(i * (ts // HALO) - 1, 0), c))
    blk = pl.BlockSpec((ts, wc), lambda i, c: (i, c))
    return _pc(body, grid=(s // ts, CONV_COLS // wc),
               in_specs=[blk, halo, pl.BlockSpec((CONV_WIDTH, wc), lambda i, c: (0, c))],
               out_specs=blk, out_shape=_sds((s, CONV_COLS)), name=name,
               scratch=[pltpu.VMEM((HALO + ts, wc), F32)])(h, h, w)


def _conv_bwd(h, w, dout, name):
    s = h.shape[0]
    ts = _pick(s, (512, 256, 128))
    nt = s // ts
    wc = 512

    def body(x_ref, xhalo_ref, d_ref, dhalo_ref, w_ref, dx_ref, dw_ref, xext, dext):
        i = pl.program_id(1)

        @pl.when(i == 0)
        def _():
            dw_ref[...] = jnp.zeros_like(dw_ref)

        xext[pl.ds(0, HALO), :] = jnp.where(i > 0, xhalo_ref[...], 0.0)
        xext[pl.ds(HALO, ts), :] = x_ref[...]
        dext[pl.ds(0, ts), :] = d_ref[...]
        dext[pl.ds(ts, HALO), :] = jnp.where(i < nt - 1, dhalo_ref[...], 0.0)
        d = d_ref[...]
        acc = w_ref[CONV_WIDTH - 1:CONV_WIDTH, :] * d
        for j in range(CONV_WIDTH - 1):
            acc = acc + w_ref[j:j + 1, :] * dext[pl.ds(3 - j, ts), :]
        dx_ref[...] = acc
        for j in range(CONV_WIDTH):
            dw_ref[j:j + 1, :] += jnp.sum(d * xext[pl.ds(HALO - 3 + j, ts), :], 0, keepdims=True)

    blk = pl.BlockSpec((ts, wc), lambda c, i: (i, c))
    halo_prev = pl.BlockSpec((HALO, wc), lambda c, i: (jnp.maximum(i * (ts // HALO) - 1, 0), c))
    halo_next = pl.BlockSpec((HALO, wc), lambda c, i: (jnp.minimum((i + 1) * (ts // HALO), s // HALO - 1), c))
    wspec = pl.BlockSpec((CONV_WIDTH, wc), lambda c, i: (0, c))
    return _pc(body, grid=(CONV_COLS // wc, nt), in_specs=[blk, halo_prev, blk, halo_next, wspec],
               out_specs=[blk, wspec], out_shape=[_sds((s, CONV_COLS)), _sds((CONV_WIDTH, CONV_COLS))], name=name,
               scratch=[pltpu.VMEM((HALO + ts, wc), F32), pltpu.VMEM((ts + HALO, wc), F32)])(h, h, dout, dout, w)


def _silu(v):
    return v * jax.nn.sigmoid(v)


def _softplus(v):
    return jnp.maximum(v, 0.0) + jnp.log1p(jnp.exp(-jnp.abs(v)))


def _gdn_prep_head(cq, ck, cv, a, b, alog, dtb):
    q = _silu(cq)
    q = q * lax.rsqrt(jnp.sum(q * q, -1, keepdims=True) + 1e-6) * HEAD_DIM ** -0.5
    k = _silu(ck)
    k = k * lax.rsqrt(jnp.sum(k * k, -1, keepdims=True) + 1e-6)
    v = _silu(cv)
    g = -jnp.exp(alog) * _softplus(a + dtb)
    beta = jax.nn.sigmoid(b)
    ones = jnp.ones((1, HEAD_DIM), F32)
    return q, k, v, g * ones, beta * ones


def _onehot_lane(idx):
    return (lax.broadcasted_iota(jnp.int32, (1, LANES), 1) == idx).astype(F32)


def _lane_pick(row, idx):
    return jnp.sum(row * _onehot_lane(idx), -1, keepdims=True)


def _gdn_prep_fwd(conv, h, alog, dtb, name):
    s = h.shape[0]
    ts = _pick(s, (128,))
    w = MIX_WIDTH

    def body(c_ref, ab_ref, alog_ref, dtb_ref, q_ref, k_ref, v_ref, g_ref, b_ref):
        ab = ab_ref[...]
        for hh in range(N_MIX_HEADS):
            sl = slice(hh * HEAD_DIM, (hh + 1) * HEAD_DIM)
            cols = [slice(p * w + hh * HEAD_DIM, p * w + (hh + 1) * HEAD_DIM) for p in range(3)]
            outs = _gdn_prep_head(c_ref[:, cols[0]], c_ref[:, cols[1]], c_ref[:, cols[2]], _lane_pick(ab, hh),
                                  _lane_pick(ab, N_MIX_HEADS + hh), _lane_pick(alog_ref[...], hh),
                                  _lane_pick(dtb_ref[...], hh))
            for ref, val in zip((q_ref, k_ref, v_ref, g_ref, b_ref), outs):
                ref[:, sl] = val

    return _pc(body, grid=(s // ts,),
               in_specs=[_rows(ts, 3 * w), _rows(ts, LANES, (4 * w + MEM_WIDTH) // LANES), _whole((1, LANES)), _whole((1, LANES))],
               out_specs=[_rows(ts, w)] * 5, out_shape=[_sds((s, w))] * 5, name=name)(conv, h, alog, dtb)


def _gdn_prep_bwd(conv, h, alog, dtb, dq, dk, dv, dg, db, name):
    s = h.shape[0]
    ts = _pick(s, (128,))
    w = MIX_WIDTH

    def body(c_ref, ab_ref, alog_ref, dtb_ref, dq_ref, dk_ref, dv_ref, dg_ref, db_ref,
             dc_ref, dab_ref, dalog_ref, ddtb_ref):
        @pl.when(pl.program_id(0) == 0)
        def _():
            dalog_ref[...] = jnp.zeros_like(dalog_ref)
            ddtb_ref[...] = jnp.zeros_like(ddtb_ref)

        ab = ab_ref[...]
        dab = jnp.zeros_like(ab)
        dalog = jnp.zeros((1, LANES), F32)
        ddtb = jnp.zeros((1, LANES), F32)
        for hh in range(N_MIX_HEADS):
            sl = slice(hh * HEAD_DIM, (hh + 1) * HEAD_DIM)
            cols = [slice(p * w + hh * HEAD_DIM, p * w + (hh + 1) * HEAD_DIM) for p in range(3)]
            _, vjp = jax.vjp(_gdn_prep_head, c_ref[:, cols[0]], c_ref[:, cols[1]], c_ref[:, cols[2]], _lane_pick(ab, hh),
                             _lane_pick(ab, N_MIX_HEADS + hh), _lane_pick(alog_ref[...], hh), _lane_pick(dtb_ref[...], hh))
            dcq, dck, dcv, da, dbb, dal, ddt = vjp((dq_ref[:, sl], dk_ref[:, sl], dv_ref[:, sl], dg_ref[:, sl], db_ref[:, sl]))
            dc_ref[:, cols[0]] = dcq
            dc_ref[:, cols[1]] = dck
            dc_ref[:, cols[2]] = dcv
            dab = dab + da * _onehot_lane(hh) + dbb * _onehot_lane(N_MIX_HEADS + hh)
            dalog = dalog + dal * _onehot_lane(hh)
            ddtb = ddtb + ddt * _onehot_lane(hh)
        dab_ref[...] = dab
        dalog_ref[...] += dalog
        ddtb_ref[...] += ddtb

    return _pc(body, grid=(s // ts,),
               in_specs=[_rows(ts, 3 * w), _rows(ts, LANES, (4 * w + MEM_WIDTH) // LANES),
                         _whole((1, LANES)), _whole((1, LANES))] + [_rows(ts, w)] * 5,
               out_specs=[_rows(ts, 3 * w), _rows(ts, LANES), _whole((1, LANES)), _whole((1, LANES))],
               out_shape=[_sds((s, 3 * w)), _sds((s, LANES)), _sds((1, LANES)), _sds((1, LANES))],
               name=name)(conv, h, alog, dtb, dq, dk, dv, dg, db)


NNB = (((2,), (1,)), ((0,), (0,)))
NTB = (((2,), (2,)), ((0,), (0,)))
TNB = (((1,), (1,)), ((0,), (0,)))


def _dg(a, b, dims):
    return lax.dot_general(a, b, dims, preferred_element_type=F32)


def _dotb(a, b, dims, mode):
    a1 = a.astype(MXU_DTYPE)
    b1 = b.astype(MXU_DTYPE)
    if mode == 1:
        return _dg(a1, b1, dims)
    rb = b - b1.astype(F32)
    b2 = rb.astype(MXU_DTYPE)
    if mode == 3:
        a2 = (a - a1.astype(F32)).astype(MXU_DTYPE)
        return _dg(a1, b1, dims) + (_dg(a1, b2, dims) + _dg(a2, b1, dims))
    b3 = (rb - b2.astype(F32)).astype(MXU_DTYPE)
    return _dg(a1, b1, dims) + (_dg(a1, b2, dims) + _dg(a1, b3, dims))


@functools.partial(jax.custom_vjp, nondiff_argnums=(2,))
def _nnb(a, b, mode):
    return _dotb(a, b, NNB, mode)


@functools.partial(jax.custom_vjp, nondiff_argnums=(2,))
def _ntb(a, b, mode):
    return _dotb(a, b, NTB, mode)


@functools.partial(jax.custom_vjp, nondiff_argnums=(2,))
def _tnb(a, b, mode):
    return _dotb(a, b, TNB, mode)


_nnb.defvjp(lambda a, b, mode: (_dotb(a, b, NNB, mode), (a, b)),
            lambda mode, r, g: (_ntb(g, r[1], mode), _tnb(r[0], g, mode)))
_ntb.defvjp(lambda a, b, mode: (_dotb(a, b, NTB, mode), (a, b)),
            lambda mode, r, g: (_nnb(g, r[1], mode), _tnb(g, r[0], mode)))
_tnb.defvjp(lambda a, b, mode: (_dotb(a, b, TNB, mode), (a, b)),
            lambda mode, r, g: (_ntb(r[1], g, mode), _nnb(r[0], g, mode)))


def _trilb(h):
    t = lax.broadcasted_iota(jnp.int32, (CHUNK, CHUNK), 0) >= lax.broadcasted_iota(jnp.int32, (CHUNK, CHUNK), 1)
    return jnp.broadcast_to(t.astype(F32)[None], (h, CHUNK, CHUNK))


def _cumsum_rows_raw(gb):
    return _dotb(_trilb(gb.shape[0]), gb, NNB, "lhs")


_cumsum_rows = jax.custom_vjp(_cumsum_rows_raw)
_cumsum_rows.defvjp(lambda gb: (_cumsum_rows_raw(gb), None),
                    lambda _, g: (_dotb(_trilb(g.shape[0]), g, TNB, "lhs"),))


def _row_col_raw(gc):
    return gc[:, :, :CHUNK], jnp.swapaxes(gc, 1, 2)[:, :CHUNK, :]


def _row_col_bwd(_, g):
    part = g[0] + jnp.swapaxes(g[1], 1, 2)
    return (jnp.concatenate([part, jnp.zeros_like(part)], -1),)


_row_col = jax.custom_vjp(_row_col_raw)
_row_col.defvjp(lambda gc: (_row_col_raw(gc), None), _row_col_bwd)


def _gdn_chunk(diff, state, q, k, v, gb, bb):
    if diff:
        nn, nt, tn, cumsum, row_col = _nnb, _ntb, _tnb, _cumsum_rows, _row_col
    else:
        nn = lambda a, b, m: _dotb(a, b, NNB, m)
        nt = lambda a, b, m: _dotb(a, b, NTB, m)
        tn = lambda a, b, m: _dotb(a, b, TNB, m)
        cumsum, row_col = _cumsum_rows_raw, _row_col_raw
    c = CHUNK
    row = lax.broadcasted_iota(jnp.int32, (1, c, c), 1)
    col = lax.broadcasted_iota(jnp.int32, (1, c, c), 2)
    tril = row >= col
    strict = row > col
    eye = (row == col).astype(F32)
    gc = cumsum(gb)
    gci, gcj = row_col(gc)
    decay = jnp.where(tril, jnp.exp(jnp.where(tril, gci - gcj, 0.0)), 0.0)
    kb = k * bb
    low = jnp.where(strict, nt(kb, k, 1) * decay, 0.0)
    inv = eye - low
    pw = low
    for _ in range(5):
        pw = nn(pw, pw, 3)
        inv = inv + nn(inv, pw, 3)
    ge = jnp.exp(gc)
    u = nn(inv, v * bb, 3)
    w = nn(inv, kb * ge, 3)
    a_qk = jnp.where(tril, nt(q, k, 1) * decay, 0.0)
    g_tot = jnp.sum(gb, 1, keepdims=True)
    k_dec = k * jnp.exp(g_tot - gc)
    v_new = u - nn(w, state, 1)
    o = nn(q * ge, state, 1) + nn(a_qk, v_new, 1)
    new_state = state * jnp.exp(g_tot) + tn(k_dec, v_new, 1)
    return new_state, o


def _stack_heads(ref, hp):
    return jnp.stack([ref[:, hh * HEAD_DIM:(hh + 1) * HEAD_DIM] for hh in range(hp)])


GDN_HEADS_FWD = 12
GDN_HEADS_BWD = 12


def _gdn_core_fwd(q, k, v, gb, bb, name):
    s = q.shape[0]
    nc = s // CHUNK
    hp = GDN_HEADS_FWD
    w = hp * HEAD_DIM

    def body(q_ref, k_ref, v_ref, g_ref, b_ref, o_ref, st_ref, state):
        @pl.when(pl.program_id(1) == 0)
        def _():
            state[...] = jnp.zeros_like(state)

        st = state[...]
        st_ref[0] = st
        new_state, o = _gdn_chunk(False, st, *(_stack_heads(r, hp) for r in (q_ref, k_ref, v_ref, g_ref, b_ref)))
        state[...] = new_state
        for hh in range(hp):
            o_ref[:, hh * HEAD_DIM:(hh + 1) * HEAD_DIM] = o[hh]

    blk = pl.BlockSpec((CHUNK, w), lambda hg, c: (c, hg))
    return _pc(body, grid=(N_MIX_HEADS // hp, nc), in_specs=[blk] * 5,
               out_specs=[blk, pl.BlockSpec((1, hp, HEAD_DIM, HEAD_DIM), lambda hg, c: (c, hg, 0, 0))],
               out_shape=[_sds((s, MIX_WIDTH)), _sds((nc, N_MIX_HEADS, HEAD_DIM, HEAD_DIM))], name=name,
               scratch=[pltpu.VMEM((hp, HEAD_DIM, HEAD_DIM), F32)])(q, k, v, gb, bb)


def _gdn_core_bwd(q, k, v, gb, bb, states, do, name):
    s = q.shape[0]
    nc = s // CHUNK
    hp = GDN_HEADS_BWD
    w = hp * HEAD_DIM

    def body(q_ref, k_ref, v_ref, g_ref, b_ref, st_ref, do_ref, dq_ref, dk_ref, dv_ref, dg_ref, db_ref, dstate):
        @pl.when(pl.program_id(1) == 0)
        def _():
            dstate[...] = jnp.zeros_like(dstate)

        _, vjp = jax.vjp(functools.partial(_gdn_chunk, True), st_ref[0],
                         *(_stack_heads(r, hp) for r in (q_ref, k_ref, v_ref, g_ref, b_ref)))
        grads = vjp((dstate[...], _stack_heads(do_ref, hp)))
        dstate[...] = grads[0]
        for ref, val in zip((dq_ref, dk_ref, dv_ref, dg_ref, db_ref), grads[1:]):
            for hh in range(hp):
                ref[:, hh * HEAD_DIM:(hh + 1) * HEAD_DIM] = val[hh]

    blk = pl.BlockSpec((CHUNK, w), lambda hg, c: (nc - 1 - c, hg))
    return _pc(body, grid=(N_MIX_HEADS // hp, nc),
               in_specs=[blk] * 5 + [pl.BlockSpec((1, hp, HEAD_DIM, HEAD_DIM), lambda hg, c: (nc - 1 - c, hg, 0, 0)), blk],
               out_specs=[blk] * 5, out_shape=[_sds((s, MIX_WIDTH))] * 5, name=name,
               scratch=[pltpu.VMEM((hp, HEAD_DIM, HEAD_DIM), F32)])(q, k, v, gb, bb, states, do)


def _gdn_out_head(o, z, g):
    return _rms(o, g) * _silu(z)


def _gdn_out_fwd(o, h, onorm, name):
    s = o.shape[0]
    ts = _pick(s, (256, 128))

    def body(o_ref, z_ref, g_ref, y_ref):
        for hh in range(N_MIX_HEADS):
            sl = slice(hh * HEAD_DIM, (hh + 1) * HEAD_DIM)
            y_ref[:, sl] = _gdn_out_head(o_ref[:, sl], z_ref[:, sl], g_ref[...])

    return _pc(body, grid=(s // ts,), in_specs=[_rows(ts, MIX_WIDTH), _rows(ts, MIX_WIDTH, 3), _whole((1, HEAD_DIM))],
               out_specs=_rows(ts, MIX_WIDTH), out_shape=_sds((s, MIX_WIDTH)), name=name)(o, h, onorm)


def _gdn_out_bwd(o, h, onorm, dcat, name):
    s = o.shape[0]
    ts = _pick(s, (256, 128))

    def body(o_ref, z_ref, g_ref, dy_ref, do_ref, dz_ref, dg_ref):
        @pl.when(pl.program_id(0) == 0)
        def _():
            dg_ref[...] = jnp.zeros_like(dg_ref)

        dgs = jnp.zeros((1, HEAD_DIM), F32)
        for hh in range(N_MIX_HEADS):
            sl = slice(hh * HEAD_DIM, (hh + 1) * HEAD_DIM)
            _, vjp = jax.vjp(_gdn_out_head, o_ref[:, sl], z_ref[:, sl], g_ref[...])
            do, dz, dg = vjp(dy_ref[:, sl])
            do_ref[:, sl] = do
            dz_ref[:, sl] = dz
            dgs = dgs + dg
        dg_ref[...] += dgs

    return _pc(body, grid=(s // ts,),
               in_specs=[_rows(ts, MIX_WIDTH), _rows(ts, MIX_WIDTH, 3), _whole((1, HEAD_DIM)), _rows(ts, MIX_WIDTH, 0)],
               out_specs=[_rows(ts, MIX_WIDTH), _rows(ts, MIX_WIDTH), _whole((1, HEAD_DIM))],
               out_shape=[_sds((s, MIX_WIDTH)), _sds((s, MIX_WIDTH)), _sds((1, HEAD_DIM))], name=name)(o, h, onorm, dcat)


def _row(v):
    return v.reshape(1, -1)


def _lane_row(v):
    return jnp.pad(v, (0, LANES - v.shape[0])).reshape(1, LANES)


def _rope_tables(positions):
    inv_freq = 1.0 / (ROPE_THETA ** (jnp.arange(0, QK_ROPE, 2, dtype=F32) / QK_ROPE))
    ang = positions.astype(F32)[:, None] * inv_freq
    cos, sin = jnp.cos(ang), jnp.sin(ang)
    z = jnp.zeros_like(cos)
    return jnp.concatenate([cos, z, cos, z], 1), jnp.concatenate([-sin, z, sin, z], 1)


def _local_step(x, mem, positions, loss_target, wts, small):
    cs, sn = _rope_tables(positions)
    saved = []
    for i in range(DEPTH):
        j = i // 2
        sv = {"x": x}
        sv["mem_kv"] = _mm(mem, wts["mem_w_kv"][i], name=f"l{i}_memkv")
        if i % 2 == 0:
            h = _mm(x, wts["mla_w_in"][j], name=f"l{i}_in")
            cqn, ckvn, krr = _mla_prep_fwd(h, _row(small["mla_q_norm"][j]), _row(small["mla_kv_norm"][j]), cs, sn, f"l{i}_mlaprep")
            q = _mm(cqn, wts["mla_w_uq"][j], name=f"l{i}_uq")
            kv = _mm(ckvn, wts["mla_w_ukv"][j], name=f"l{i}_ukv")
            mix, lse = _flash_fwd(q, kv, krr, cs, sn, f"l{i}_flash")
            sv.update(cqn=cqn, ckvn=ckvn, krr=krr, q=q, kv=kv, o=mix, lse=lse)
            qcol = 2
        else:
            h = _mm(x, wts["gdn_w_in"][j], name=f"l{i}_in")
            conv = _conv_fwd(h, small["gdn_conv"][j], f"l{i}_conv")
            qn, kn, vv, gb, bb = _gdn_prep_fwd(conv, h, _lane_row(small["gdn_a_log"][j]), _lane_row(small["gdn_dt_bias"][j]), f"l{i}_gdnprep")
            o, states = _gdn_core_fwd(qn, kn, vv, gb, bb, f"l{i}_gdncore")
            mix = _gdn_out_fwd(o, h, _row(small["gdn_o_norm"][j]), f"l{i}_gdnout")
            sv.update(conv=conv, qn=qn, kn=kn, vv=vv, gb=gb, bb=bb, o=o, states=states)
            qcol = 4 * MIX_WIDTH // MEM_WIDTH
        mem_o = _mem_fwd(h, qcol, sv["mem_kv"], f"l{i}_mem")
        cat = jnp.concatenate([mix, mem_o], 1)
        y = _mm(cat, wts["w_out"][i], name=f"l{i}_out")
        x1 = _ln_fwd(x, y, _row(small["ln1_g"][i]), _row(small["ln1_b"][i]), f"l{i}_ln1")
        h1 = _mm(x1, wts["mlp_w1"][i], name=f"l{i}_w1")
        ff = _mm(h1, wts["mlp_w2"][i], a_fn=_relu2, name=f"l{i}_w2")
        x2 = _ln_fwd(x1, ff, _row(small["ln2_g"][i]), _row(small["ln2_b"][i]), f"l{i}_ln2")
        sv.update(h=h, qcol=qcol, cat=cat, y=y, x1=x1, h1=h1, ff=ff)
        saved.append(sv)
        x = x2

    loss_row, dx = _loss_and_grad(x, loss_target, "loss")

    gw = {k: [None] * len(v) for k, v in wts.items()}
    gs = {k: [None] * v.shape[0] for k, v in small.items()}
    gdt = COMM_DTYPE
    for i in reversed(range(DEPTH)):
        j = i // 2
        sv = saved[i]
        dr2, dg, db = _ln_bwd(sv["x1"], sv["ff"], _row(small["ln2_g"][i]), _row(small["ln2_b"][i]), dx, f"l{i}_ln2b")
        gs["ln2_g"][i], gs["ln2_b"][i] = dg[0], db[0]
        dh1 = _mm(dr2, wts["mlp_w2"][i], tb=True, epi=_relu2_bwd, e=sv["h1"], name=f"l{i}_dh1")
        gw["mlp_w2"][i] = _mm(sv["h1"], dr2, ta=True, a_fn=_relu2, out_dtype=gdt, name=f"l{i}_dw2")
        gw["mlp_w1"][i] = _mm(sv["x1"], dh1, ta=True, out_dtype=gdt, name=f"l{i}_dw1")
        dx1 = _mm(dh1, wts["mlp_w1"][i], tb=True, epi=_add_alpha, e=dr2, name=f"l{i}_dx1")
        dr1, dg, db = _ln_bwd(sv["x"], sv["y"], _row(small["ln1_g"][i]), _row(small["ln1_b"][i]), dx1, f"l{i}_ln1b")
        gs["ln1_g"][i], gs["ln1_b"][i] = dg[0], db[0]
        gw["w_out"][i] = _mm(sv["cat"], dr1, ta=True, out_dtype=gdt, name=f"l{i}_dwout")
        dcat = _mm(dr1, wts["w_out"][i], tb=True, name=f"l{i}_dcat")
        h = sv["h"]
        dqmem, dmem_kv = _mem_bwd(h, sv["qcol"], sv["mem_kv"], dcat, f"l{i}_memb")
        gw["mem_w_kv"][i] = _mm(mem, dmem_kv, ta=True, out_dtype=gdt, name=f"l{i}_dwmem")
        zpad = jnp.zeros((h.shape[0], LANES), F32)
        if i % 2 == 0:
            qnw, kvnw = _row(small["mla_q_norm"][j]), _row(small["mla_kv_norm"][j])
            dq = _flash_bwd_q(sv["q"], sv["kv"], sv["krr"], cs, sn, sv["o"], sv["lse"], dcat, f"l{i}_flashbq")
            dkv, dkrr = _flash_bwd_kv(sv["q"], sv["kv"], sv["krr"], cs, sn, sv["o"], sv["lse"], dcat, f"l{i}_flashbkv")
            gw["mla_w_ukv"][j] = _mm(sv["ckvn"], dkv, ta=True, out_dtype=gdt, name=f"l{i}_dwukv")
            dckvn = _mm(dkv, wts["mla_w_ukv"][j], tb=True, name=f"l{i}_dckvn")
            gw["mla_w_uq"][j] = _mm(sv["cqn"], dq, ta=True, out_dtype=gdt, name=f"l{i}_dwuq")
            dcqn = _mm(dq, wts["mla_w_uq"][j], tb=True, name=f"l{i}_dcqn")
            dcq, dckv, dkr, dqn, dkvn = _mla_prep_bwd(h, qnw, kvnw, cs, sn, dcqn, dckvn, dkrr, f"l{i}_mlaprepb")
            gs["mla_q_norm"][j], gs["mla_kv_norm"][j] = dqn[0], dkvn[0]
            dh = jnp.concatenate([dcq, dckv, dqmem, dkr, zpad], 1)
            w_in = wts["mla_w_in"][j]
            key = "mla_w_in"
        else:
            alog, dtb = _lane_row(small["gdn_a_log"][j]), _lane_row(small["gdn_dt_bias"][j])
            do, dz, dgn = _gdn_out_bwd(sv["o"], h, _row(small["gdn_o_norm"][j]), dcat, f"l{i}_gdnoutb")
            gs["gdn_o_norm"][j] = dgn[0]
            dqn, dkn, dvv, dgb, dbb = _gdn_core_bwd(sv["qn"], sv["kn"], sv["vv"], sv["gb"], sv["bb"], sv["states"], do, f"l{i}_gdncoreb")
            dconv, dab, dalog, ddtb = _gdn_prep_bwd(sv["conv"], h, alog, dtb, dqn, dkn, dvv, dgb, dbb, f"l{i}_gdnprepb")
            gs["gdn_a_log"][j], gs["gdn_dt_bias"][j] = dalog[0, :N_MIX_HEADS], ddtb[0, :N_MIX_HEADS]
            dhqkv, dconvw = _conv_bwd(h, small["gdn_conv"][j], dconv, f"l{i}_convb")
            gs["gdn_conv"][j] = dconvw
            dh = jnp.concatenate([dhqkv, dz, dqmem, dab, zpad], 1)
            w_in = wts["gdn_w_in"][j]
            key = "gdn_w_in"
        gw[key][j] = _mm(sv["x"], dh, ta=True, out_dtype=gdt, name=f"l{i}_dwin")
        dx = _mm(dh, w_in, tb=True, epi=_add_alpha, e=dr1, name=f"l{i}_dx")
    gs = {k: jnp.stack(v) for k, v in gs.items()}
    return loss_row, dx, gw, gs


def _mla_in_to_kernel(w):
    z32 = jnp.zeros((w.shape[0], 32), w.dtype)
    z128 = jnp.zeros((w.shape[0], LANES), w.dtype)
    return jnp.concatenate([w[:, :1024], w[:, 1088:1600], w[:, 1024:1056], z32, w[:, 1056:1088], z32, z128], 1)


def _mla_in_from_kernel(g):
    return jnp.concatenate([g[:, :1024], g[:, 1536:1568], g[:, 1600:1632], g[:, 1024:1536]], 1)


def _uq_to_kernel(w):
    w3 = w.reshape(Q_LORA, N_MIX_HEADS, QK_NOPE + QK_ROPE)
    z = jnp.zeros((Q_LORA, N_MIX_HEADS, 32), w.dtype)
    return jnp.concatenate([w3[:, :, :128], w3[:, :, 128:160], z, w3[:, :, 160:192], z], 2).reshape(Q_LORA, N_MIX_HEADS * Q_HEAD_P)


def _uq_from_kernel(g):
    g3 = g.reshape(Q_LORA, N_MIX_HEADS, Q_HEAD_P)
    return jnp.concatenate([g3[:, :, :128], g3[:, :, 128:160], g3[:, :, 192:224]], 2).reshape(Q_LORA, -1)


def _gdn_in_to_kernel(w):
    z = jnp.zeros((w.shape[0], LANES - 2 * N_MIX_HEADS + LANES), w.dtype)
    return jnp.concatenate([w[:, :6144], w[:, 6168:6680], w[:, 6144:6168], z], 1)


def _gdn_in_from_kernel(g):
    return jnp.concatenate([g[:, :6144], g[:, 6656:6680], g[:, 6144:6656]], 1)


def _exchange(srcs, name, gather):
    n = len(srcs)

    def body(*refs):
        src_refs, out_refs = refs[:n], refs[n:2 * n]
        send_sems, recv_sems, local_sems = refs[2 * n:]
        x, y, c = lax.axis_index("x"), lax.axis_index("y"), lax.axis_index("c")
        me = 4 * x + 2 * y + c
        local, sends, recvs = [], [], []
        for a in range(n):
            src_ref, out_ref = src_refs[a], out_refs[a]
            local.append(pltpu.make_async_copy(src_ref if gather else src_ref.at[me], out_ref.at[me], local_sems.at[a]))
            for kk in range(1, N_DEV):
                px, py, pc = x ^ ((kk >> 2) & 1), y ^ ((kk >> 1) & 1), c ^ (kk & 1)
                peer = 4 * px + 2 * py + pc
                piece = src_ref if gather else src_ref.at[peer]
                sends.append(pltpu.make_async_remote_copy(
                    src_ref=piece, dst_ref=out_ref.at[me], send_sem=send_sems.at[a, kk - 1], recv_sem=recv_sems.at[a, kk - 1],
                    device_id=(px, py, pc), device_id_type=pl.DeviceIdType.MESH))
                recvs.append(pltpu.make_async_remote_copy(
                    src_ref=piece, dst_ref=out_ref.at[peer], send_sem=send_sems.at[a, kk - 1], recv_sem=recv_sems.at[a, kk - 1],
                    device_id=(px, py, pc), device_id_type=pl.DeviceIdType.MESH))
        for cp in local + sends:
            cp.start()
        for cp in recvs:
            cp.wait_recv()
        for cp in sends:
            cp.wait_send()
        for cp in local:
            cp.wait()

    hbm = pl.BlockSpec(memory_space=pltpu.HBM)
    shapes = [_sds((N_DEV,) + tuple(s.shape if gather else s.shape[1:]), s.dtype) for s in srcs]
    return pl.pallas_call(
        body, in_specs=[hbm] * n, out_specs=[hbm] * n, out_shape=shapes, name=name,
        scratch_shapes=[pltpu.SemaphoreType.DMA((n, N_DEV - 1)), pltpu.SemaphoreType.DMA((n, N_DEV - 1)),
                        pltpu.SemaphoreType.DMA((n,))],
    )(*srcs)


def _adamw(parts, w, m, v, name):
    r, c = w.shape
    tr = _pick(r, tuple(t for t in (256, 128, 64, 32, 16, 8) if t * c <= 256 * 1024))

    def body(p_ref, w_ref, m_ref, v_ref, g_ref, d_ref, nm_ref, nv_ref):
        g = p_ref[0].astype(F32)
        for dd in range(1, N_DEV):
            g = g + p_ref[dd].astype(F32)
        nm = ADAM_B1 * m_ref[...] + (1.0 - ADAM_B1) * g
        nv = ADAM_B2 * v_ref[...] + (1.0 - ADAM_B2) * jnp.square(g)
        m_hat = nm / (1.0 - ADAM_B1 ** ADAM_STEP)
        v_hat = nv / (1.0 - ADAM_B2 ** ADAM_STEP)
        g_ref[...] = g
        d_ref[...] = -ADAM_LR * (m_hat / (jnp.sqrt(v_hat) + ADAM_EPS) + ADAM_WD * w_ref[...])
        nm_ref[...] = nm
        nv_ref[...] = nv

    blk = pl.BlockSpec((tr, c), lambda i: (i, 0))
    return _pc(body, grid=(r // tr,), in_specs=[pl.BlockSpec((N_DEV, tr, c), lambda i: (0, i, 0)), blk, blk, blk],
               out_specs=[blk] * 4, out_shape=[_sds((r, c))] * 4, name=name)(parts, w, m, v)


BIG = (("mla_w_in", 1), ("mla_w_uq", 2), ("mla_w_ukv", 2), ("gdn_w_in", 2), ("mem_w_kv", 1), ("w_out", 1),
       ("mlp_w1", 2), ("mlp_w2", 1), ("gdn_conv", 2))
SMALL = ("mla_q_norm", "mla_kv_norm", "gdn_a_log", "gdn_dt_bias", "gdn_o_norm", "ln1_g", "ln1_b", "ln2_g", "ln2_b")
PACK_COLS = 1024


def _unshard(pieces, axis):
    t = jnp.moveaxis(pieces, 0, axis)
    return t.reshape(t.shape[:axis] + (t.shape[axis] * t.shape[axis + 1],) + t.shape[axis + 2:])


def _shard(full, axis):
    t = full.reshape(full.shape[:axis] + (N_DEV, full.shape[axis] // N_DEV) + full.shape[axis + 1:])
    return jnp.moveaxis(t, axis, 0)


def _pack(arrs, lead, cols, mult):
    lead_shape = arrs[0].shape[:lead]
    flat = jnp.concatenate([a.reshape(lead_shape + (-1,)) for a in arrs], -1)
    n = flat.shape[-1]
    r = -(-n // (cols * mult)) * mult
    flat = jnp.pad(flat, [(0, 0)] * lead + [(0, r * cols - n)])
    return flat.reshape(lead_shape + (r, cols))


def _unpack(buf, lead, shapes):
    lead_shape = buf.shape[:lead]
    flat = buf.reshape(lead_shape + (-1,))
    out, off = [], 0
    for shp in shapes:
        n = 1
        for d in shp:
            n *= d
        out.append(flat[..., off:off + n].reshape(lead_shape + tuple(shp)))
        off += n
    return out


def kernel(x, mem, positions, mla_w_in, mla_q_norm, mla_w_uq, mla_kv_norm, mla_w_ukv, gdn_w_in, gdn_conv, gdn_a_log, gdn_dt_bias, gdn_o_norm, mem_w_kv, w_out, ln1_g, ln1_b, mlp_w1, mlp_w2, ln2_g, ln2_b, loss_target, m_mla_w_in, m_mla_q_norm, m_mla_w_uq, m_mla_kv_norm, m_mla_w_ukv, m_gdn_w_in, m_gdn_conv, m_gdn_a_log, m_gdn_dt_bias, m_gdn_o_norm, m_mem_w_kv, m_w_out, m_ln1_g, m_ln1_b, m_mlp_w1, m_mlp_w2, m_ln2_g, m_ln2_b, v_mla_w_in, v_mla_q_norm, v_mla_w_uq, v_mla_kv_norm, v_mla_w_ukv, v_gdn_w_in, v_gdn_conv, v_gdn_a_log, v_gdn_dt_bias, v_gdn_o_norm, v_mem_w_kv, v_w_out, v_ln1_g, v_ln1_b, v_mlp_w1, v_mlp_w2, v_ln2_g, v_ln2_b):
    args = dict(locals())
    w_loc = {n: args[n] for n, _ in BIG}
    m_loc = {n: args["m_" + n] for n, _ in BIG}
    v_loc = {n: args["v_" + n] for n, _ in BIG}
    small = {n: args[n] for n in SMALL}
    axis_of = dict(BIG)
    mm_names = [n for n, _ in BIG if n != "gdn_conv"]

    names = [n for n, _ in BIG]
    got = _exchange([w_loc[n].astype(COMM_DTYPE) for n in mm_names] + [w_loc["gdn_conv"]], "gather_weights", gather=True)
    full = {n: _unshard(p, axis_of[n]) for n, p in zip(names, got)}
    conv_full = full.pop("gdn_conv")

    wts = {n: [full[n][l] for l in range(full[n].shape[0])] for n in mm_names}
    wts["mla_w_in"] = [_mla_in_to_kernel(w) for w in wts["mla_w_in"]]
    wts["mla_w_uq"] = [_uq_to_kernel(w) for w in wts["mla_w_uq"]]
    wts["gdn_w_in"] = [_gdn_in_to_kernel(w) for w in wts["gdn_w_in"]]
    small_in = dict(small, gdn_conv=conv_full)

    loss_row, grad_x, gw, gs = _local_step(x[0], mem[0], positions[0], loss_target[0], wts, small_in)
    loss = lax.psum(loss_row[0, 0], ("x", "y", "c"))

    gw["mla_w_in"] = [_mla_in_from_kernel(g) for g in gw["mla_w_in"]]
    gw["mla_w_uq"] = [_uq_from_kernel(g) for g in gw["mla_w_uq"]]
    gw["gdn_w_in"] = [_gdn_in_from_kernel(g) for g in gw["gdn_w_in"]]
    gfull = {n: jnp.stack(gw[n]) for n in mm_names}
    gfull["gdn_conv"] = gs.pop("gdn_conv").astype(COMM_DTYPE)
    g_got = _exchange([_shard(gfull[n], axis_of[n]) for n in names], "exchange_grads", gather=False)
    big_out = [{}, {}, {}, {}]
    for n, parts in zip(names, g_got):
        shp = w_loc[n].shape
        rows, cols = shp[0] * shp[1], shp[2]
        res = _adamw(parts.reshape(N_DEV, rows, cols), w_loc[n].reshape(rows, cols), m_loc[n].reshape(rows, cols),
                     v_loc[n].reshape(rows, cols), f"adamw_{n}")
        for kind in range(4):
            big_out[kind][n] = res[kind].reshape(shp)

    s_sent = _pack([gs[n] for n in SMALL], 0, LANES, 8)
    s_got = _exchange([s_sent], "gather_small_grads", gather=True)[0]
    small_out = [dict(zip(SMALL, _unpack(o, 0, [small[n].shape for n in SMALL])))
                 for o in _adamw(s_got, _pack([small[n] for n in SMALL], 0, LANES, 8),
                                 _pack([args["m_" + n] for n in SMALL], 0, LANES, 8),
                                 _pack([args["v_" + n] for n in SMALL], 0, LANES, 8), "adamw_small")]

    order = ["mla_w_in", "mla_q_norm", "mla_w_uq", "mla_kv_norm", "mla_w_ukv", "gdn_w_in", "gdn_conv", "gdn_a_log",
             "gdn_dt_bias", "gdn_o_norm", "mem_w_kv", "w_out", "ln1_g", "ln1_b", "mlp_w1", "mlp_w2", "ln2_g", "ln2_b"]
    outs = [loss, grad_x[None]]
    for kind in range(4):
        for n in order:
            outs.append(big_out[kind][n] if n in axis_of else small_out[kind][n])
    return tuple(outs)
```

```python
import functools

import jax
import jax.numpy as jnp
from jax import lax
from jax.experimental import pallas as pl
from jax.experimental.pallas import tpu as pltpu

F32 = jnp.float32
MXU_DTYPE = jnp.bfloat16
COMM_DTYPE = jnp.bfloat16

N_DEV = 8
D_MODEL = 2048
DEPTH = 4
HEAD_DIM = 128
N_MIX_HEADS = 12
N_MEM_HEADS = 4
MIX_WIDTH = N_MIX_HEADS * HEAD_DIM
MEM_WIDTH = N_MEM_HEADS * HEAD_DIM
Q_LORA = 512
KV_LORA = 512
QK_NOPE = 128
QK_ROPE = 64
ROPE_THETA = 10000.0
CONV_WIDTH = 4
CHUNK = 64
D_FF = 4 * D_MODEL
ALPHA = (2 * DEPTH) ** 0.25
LN_EPS = 1e-5
RMS_EPS = 1e-6
MLA_IN = Q_LORA + KV_LORA + QK_ROPE + MEM_WIDTH
GDN_IN = 4 * MIX_WIDTH + 2 * N_MIX_HEADS + MEM_WIDTH
MLA_IN_P = 1792
GDN_IN_P = 6912
Q_HEAD_P = 256
ATT_SCALE = (QK_NOPE + QK_ROPE) ** -0.5
ADAM_LR, ADAM_B1, ADAM_B2, ADAM_EPS, ADAM_WD, ADAM_STEP = 0.001, 0.9, 0.999, 1e-08, 0.01, 10
LANES = 128
VMEM_LIMIT = 56 * 1024 * 1024

NN = (((1,), (0,)), ((), ()))
NT = (((1,), (1,)), ((), ()))
TN = (((0,), (0,)), ((), ()))
HI = lax.Precision.HIGHEST


def _dot(a, b, dims, hi=False):
    if hi:
        return lax.dot_general(a, b, dims, precision=HI, preferred_element_type=F32)
    return lax.dot_general(a.astype(MXU_DTYPE), b.astype(MXU_DTYPE), dims, preferred_element_type=F32)


@functools.partial(jax.custom_vjp, nondiff_argnums=(2,))
def _nn_d(a, b, hi):
    return _dot(a, b, NN, hi)


@functools.partial(jax.custom_vjp, nondiff_argnums=(2,))
def _nt_d(a, b, hi):
    return _dot(a, b, NT, hi)


@functools.partial(jax.custom_vjp, nondiff_argnums=(2,))
def _tn_d(a, b, hi):
    return _dot(a, b, TN, hi)


_nn_d.defvjp(lambda a, b, hi: (_dot(a, b, NN, hi), (a, b)),
             lambda hi, r, g: (_nt_d(g, r[1], hi), _tn_d(r[0], g, hi)))
_nt_d.defvjp(lambda a, b, hi: (_dot(a, b, NT, hi), (a, b)),
             lambda hi, r, g: (_nn_d(g, r[1], hi), _tn_d(g, r[0], hi)))
_tn_d.defvjp(lambda a, b, hi: (_dot(a, b, TN, hi), (a, b)),
             lambda hi, r, g: (_nt_d(r[1], g, hi), _nn_d(r[0], g, hi)))


class _RawOps:
    nn = staticmethod(lambda a, b, hi=False: _dot(a, b, NN, hi))
    nt = staticmethod(lambda a, b, hi=False: _dot(a, b, NT, hi))
    tn = staticmethod(lambda a, b, hi=False: _dot(a, b, TN, hi))


class _DiffOps:
    nn = staticmethod(lambda a, b, hi=False: _nn_d(a, b, hi))
    nt = staticmethod(lambda a, b, hi=False: _nt_d(a, b, hi))
    tn = staticmethod(lambda a, b, hi=False: _tn_d(a, b, hi))


def _pick(dim, cands=(512, 384, 256, 128)):
    for c in cands:
        if dim % c == 0:
            return c
    return dim


def _pc(body, *, grid, in_specs, out_specs, out_shape, name, scratch=()):
    return pl.pallas_call(
        body, grid=grid, in_specs=in_specs, out_specs=out_specs, out_shape=out_shape,
        scratch_shapes=list(scratch), name=name,
        compiler_params=pltpu.CompilerParams(dimension_semantics=("arbitrary",) * len(grid),
                                             vmem_limit_bytes=VMEM_LIMIT))


def _rows(ts, w, cb=0):
    return pl.BlockSpec((ts, w), lambda i, *_: (i, cb))


def _whole(shape):
    return pl.BlockSpec(shape, lambda *_: (0,) * len(shape))


def _sds(shape, dtype=F32):
    return jax.ShapeDtypeStruct(shape, dtype)


def _mm(a, b, *, name, ta=False, tb=False, out_dtype=F32, a_fn=None, epi=None, e=None):
    m, k = (a.shape[1], a.shape[0]) if ta else a.shape
    n = b.shape[0] if tb else b.shape[1]
    assert k == (b.shape[1] if tb else b.shape[0]), (a.shape, b.shape, ta, tb)
    wide = (1024, 768, 512, 384, 256, 128)
    if k <= 2048 and not ta:
        tm, tn, tk = _pick(m, (512, 256, 128)), _pick(n, wide), k
    else:
        tm, tn, tk = _pick(m, wide), _pick(n, wide), _pick(k, wide)
    nk = k // tk
    dims = (((0 if ta else 1,), (1 if tb else 0,)), ((), ()))

    def body(*refs):
        if e is None:
            a_ref, b_ref, o_ref, acc = refs
        else:
            a_ref, b_ref, e_ref, o_ref, acc = refs
        kk = pl.program_id(2)

        @pl.when(kk == 0)
        def _():
            acc[...] = jnp.zeros_like(acc)

        av = a_ref[...]
        if a_fn is not None:
            av = a_fn(av.astype(F32))
        acc[...] += lax.dot_general(av.astype(MXU_DTYPE), b_ref[...].astype(MXU_DTYPE), dims,
                                    preferred_element_type=F32)

        @pl.when(kk == nk - 1)
        def _():
            r = acc[...]
            if epi is not None:
                r = epi(r, e_ref[...].astype(F32))
            o_ref[...] = r.astype(out_dtype)

    a_spec = pl.BlockSpec((tk, tm), lambda i, j, kk: (kk, i)) if ta else pl.BlockSpec((tm, tk), lambda i, j, kk: (i, kk))
    b_spec = pl.BlockSpec((tn, tk), lambda i, j, kk: (j, kk)) if tb else pl.BlockSpec((tk, tn), lambda i, j, kk: (kk, j))
    o_spec = pl.BlockSpec((tm, tn), lambda i, j, kk: (i, j))
    ins, specs = [a, b], [a_spec, b_spec]
    if e is not None:
        assert e.shape == (m, n)
        ins.append(e)
        specs.append(o_spec)
    return _pc(body, grid=(m // tm, n // tn, nk), in_specs=specs, out_specs=o_spec,
               out_shape=_sds((m, n), out_dtype), name=name, scratch=[pltpu.VMEM((tm, tn), F32)])(*ins)


def _relu2(v):
    r = jnp.maximum(v, 0.0)
    return r * r


def _relu2_bwd(acc, h1):
    return acc * (2.0 * jnp.maximum(h1, 0.0))


def _add_alpha(acc, dr):
    return acc + ALPHA * dr


def _ln(r, g, b):
    mu = jnp.mean(r, -1, keepdims=True)
    var = jnp.mean(jnp.square(r - mu), -1, keepdims=True)
    return (r - mu) * lax.rsqrt(var + LN_EPS) * g + b


def _ln_fwd(x, y, g, b, name):
    s, d = x.shape
    ts = _pick(s, (256, 128))

    def body(x_ref, y_ref, g_ref, b_ref, o_ref):
        o_ref[...] = _ln(ALPHA * x_ref[...] + y_ref[...], g_ref[...], b_ref[...])

    return _pc(body, grid=(s // ts,), in_specs=[_rows(ts, d), _rows(ts, d), _whole((1, d)), _whole((1, d))],
               out_specs=_rows(ts, d), out_shape=_sds((s, d)), name=name)(x, y, g, b)


def _ln_bwd(x, y, g, b, dout, name):
    s, d = x.shape
    ts = _pick(s, (256, 128))

    def body(x_ref, y_ref, g_ref, b_ref, do_ref, dr_ref, dg_ref, db_ref):
        @pl.when(pl.program_id(0) == 0)
        def _():
            dg_ref[...] = jnp.zeros_like(dg_ref)
            db_ref[...] = jnp.zeros_like(db_ref)

        r = ALPHA * x_ref[...] + y_ref[...]
        _, vjp = jax.vjp(_ln, r, g_ref[...], b_ref[...])
        dr, dg, db = vjp(do_ref[...])
        dr_ref[...] = dr
        dg_ref[...] += dg
        db_ref[...] += db

    return _pc(body, grid=(s // ts,),
               in_specs=[_rows(ts, d), _rows(ts, d), _whole((1, d)), _whole((1, d)), _rows(ts, d)],
               out_specs=[_rows(ts, d), _whole((1, d)), _whole((1, d))],
               out_shape=[_sds((s, d)), _sds((1, d)), _sds((1, d))], name=name)(x, y, g, b, dout)


def _loss_and_grad(y, target, name):
    s, d = y.shape
    ts = _pick(s, (256, 128))

    def body(y_ref, t_ref, l_ref, dy_ref):
        @pl.when(pl.program_id(0) == 0)
        def _():
            l_ref[...] = jnp.zeros_like(l_ref)

        diff = y_ref[...] - t_ref[...]
        per_tok = jnp.mean(jnp.square(diff), -1, keepdims=True)
        l_ref[...] += 0.5 * jnp.sum(per_tok, 0, keepdims=True) * jnp.ones((1, LANES), F32)
        dy_ref[...] = diff * (1.0 / d)

    return _pc(body, grid=(s // ts,), in_specs=[_rows(ts, d), _rows(ts, d)],
               out_specs=[_whole((1, LANES)), _rows(ts, d)],
               out_shape=[_sds((1, LANES)), _sds((s, d))], name=name)(y, target)


def _rope(blk, cs, sn):
    return blk * cs + pltpu.roll(blk, 64, 1) * sn


def _rope_t(dblk, cs, sn):
    return dblk * cs + pltpu.roll(dblk * sn, 64, 1)


def _rms(v, g):
    return v * lax.rsqrt(jnp.mean(v * v, -1, keepdims=True) + RMS_EPS) * g


def _mla_prep_fwd(h, qn, kvn, cs, sn, name):
    s = h.shape[0]
    ts = _pick(s, (512, 256, 128))

    def body(cq_ref, ckv_ref, kr_ref, qn_ref, kvn_ref, cs_ref, sn_ref, cqn_ref, ckvn_ref, krr_ref):
        cqn_ref[...] = _rms(cq_ref[...], qn_ref[...])
        ckvn_ref[...] = _rms(ckv_ref[...], kvn_ref[...])
        krr_ref[...] = _rope(kr_ref[...], cs_ref[...], sn_ref[...])

    return _pc(body, grid=(s // ts,),
               in_specs=[_rows(ts, 512, 0), _rows(ts, 512, 1), _rows(ts, LANES, 12), _whole((1, 512)), _whole((1, 512)),
                         _rows(ts, LANES), _rows(ts, LANES)],
               out_specs=[_rows(ts, 512), _rows(ts, 512), _rows(ts, LANES)],
               out_shape=[_sds((s, 512)), _sds((s, 512)), _sds((s, LANES))], name=name)(h, h, h, qn, kvn, cs, sn)


def _mla_prep_bwd(h, qn, kvn, cs, sn, dcqn, dckvn, dkrr_heads, name):
    s = h.shape[0]
    ts = _pick(s, (512, 256, 128))

    def body(cq_ref, ckv_ref, qn_ref, kvn_ref, cs_ref, sn_ref, dcqn_ref, dckvn_ref, dkrr_ref,
             dcq_ref, dckv_ref, dkr_ref, dqn_ref, dkvn_ref):
        @pl.when(pl.program_id(0) == 0)
        def _():
            dqn_ref[...] = jnp.zeros_like(dqn_ref)
            dkvn_ref[...] = jnp.zeros_like(dkvn_ref)

        _, vjp = jax.vjp(_rms, cq_ref[...], qn_ref[...])
        dcq, dqn = vjp(dcqn_ref[...])
        dcq_ref[...] = dcq
        dqn_ref[...] += dqn
        _, vjp = jax.vjp(_rms, ckv_ref[...], kvn_ref[...])
        dckv, dkvn = vjp(dckvn_ref[...])
        dckv_ref[...] = dckv
        dkvn_ref[...] += dkvn
        dkrr = dkrr_ref[0]
        for hh in range(1, N_MIX_HEADS):
            dkrr = dkrr + dkrr_ref[hh]
        dkr_ref[...] = _rope_t(dkrr, cs_ref[...], sn_ref[...])

    heads3 = pl.BlockSpec((N_MIX_HEADS, ts, LANES), lambda i: (0, i, 0))
    return _pc(body, grid=(s // ts,),
               in_specs=[_rows(ts, 512, 0), _rows(ts, 512, 1), _whole((1, 512)), _whole((1, 512)),
                         _rows(ts, LANES), _rows(ts, LANES), _rows(ts, 512), _rows(ts, 512), heads3],
               out_specs=[_rows(ts, 512), _rows(ts, 512), _rows(ts, LANES), _whole((1, 512)), _whole((1, 512))],
               out_shape=[_sds((s, 512)), _sds((s, 512)), _sds((s, LANES)), _sds((1, 512)), _sds((1, 512))],
               name=name)(h, h, qn, kvn, cs, sn, dcqn, dckvn, dkrr_heads)


def _att_tiles(s):
    t = _pick(s, (512, 256, 128))
    return t, s // t


ATT_HEADS_PER_STEP = 2


def _tri_pairs(nb, k_major):
    pairs = [(i, j) for j in range(nb) for i in range(j, nb)] if k_major else [(i, j) for i in range(nb) for j in range(i + 1)]
    return jnp.asarray([p[0] for p in pairs], jnp.int32), jnp.asarray([p[1] for p in pairs], jnp.int32)


ATT_STRIP = 64


def _tril_rows(r, n, t):
    return lax.broadcasted_iota(jnp.int32, (n, t), 1) <= r + lax.broadcasted_iota(jnp.int32, (n, t), 0)


def _pc_pairs(body, pairs, *, n_groups, in_specs, out_specs, out_shape, name, scratch, args):
    return pl.pallas_call(
        body, out_shape=out_shape, name=name,
        grid_spec=pltpu.PrefetchScalarGridSpec(num_scalar_prefetch=2, grid=(n_groups, pairs[0].shape[0]), in_specs=in_specs,
                                               out_specs=out_specs, scratch_shapes=list(scratch)),
        compiler_params=pltpu.CompilerParams(dimension_semantics=("arbitrary", "arbitrary"), vmem_limit_bytes=VMEM_LIMIT),
    )(*pairs, *args)


def _att_specs(t):
    qrow = lambda w: pl.BlockSpec((t, w), lambda h, p, qi, kj: (qi[p], h))
    krow = lambda w: pl.BlockSpec((t, w), lambda h, p, qi, kj: (kj[p], h))
    qtab = pl.BlockSpec((t, LANES), lambda h, p, qi, kj: (qi[p], 0))
    ktab = pl.BlockSpec((t, LANES), lambda h, p, qi, kj: (kj[p], 0))
    return qrow, krow, qtab, ktab


def _flash_fwd(q, kv, krr, cs, sn, name):
    s = q.shape[0]
    t, nb = _att_tiles(s)
    ah = ATT_HEADS_PER_STEP

    def body(qi_ref, kj_ref, q_ref, cs_ref, sn_ref, kv_ref, kr_ref, o_ref, lse_ref, qn_s, qr_s, m_s, l_s, acc_s, s_scr, p_scr):
        i, j = qi_ref[pl.program_id(1)], kj_ref[pl.program_id(1)]

        @pl.when(j == 0)
        def _():
            for hh in range(ah):
                c0 = hh * Q_HEAD_P
                qn_s[hh] = (q_ref[:, c0:c0 + LANES] * ATT_SCALE).astype(MXU_DTYPE)
                qr_s[hh] = (_rope(q_ref[:, c0 + LANES:c0 + Q_HEAD_P], cs_ref[...], sn_ref[...]) * ATT_SCALE).astype(MXU_DTYPE)
            m_s[...] = jnp.full_like(m_s, -jnp.inf)
            l_s[...] = jnp.zeros_like(l_s)
            acc_s[...] = jnp.zeros_like(acc_s)

        def update(masked):
            kr = kr_ref[...].astype(MXU_DTYPE)
            for hh in range(ah):
                c0 = hh * Q_HEAD_P
                s_scr[hh] = _dot(qn_s[hh], kv_ref[:, c0:c0 + LANES], NT) + _dot(qr_s[hh], kr, NT)
                for r in range(0, t, ATT_STRIP):
                    rows = slice(r, r + ATT_STRIP)
                    sc = s_scr[hh, rows, :]
                    if masked:
                        sc = jnp.where(_tril_rows(r, ATT_STRIP, t), sc, -jnp.inf)
                    m_old = m_s[hh, rows, :]
                    m_new = jnp.maximum(m_old, jnp.max(sc, -1, keepdims=True))
                    p = jnp.exp(sc - m_new[:, :1])
                    corr = jnp.exp(m_old - m_new)
                    l_s[hh, rows, :] = corr * l_s[hh, rows, :] + jnp.sum(p, -1, keepdims=True)
                    acc_s[hh, rows, :] = corr * acc_s[hh, rows, :]
                    m_s[hh, rows, :] = m_new
                    p_scr[hh, rows, :] = p.astype(MXU_DTYPE)
                acc_s[hh] += _dot(p_scr[hh], kv_ref[:, c0 + LANES:c0 + Q_HEAD_P], NN)

        @pl.when(j < i)
        def _():
            update(False)

        @pl.when(j == i)
        def _():
            update(True)
            for hh in range(ah):
                sl = slice(hh * LANES, (hh + 1) * LANES)
                o_ref[:, sl] = acc_s[hh] / l_s[hh]
                lse_ref[:, sl] = m_s[hh] + jnp.log(l_s[hh])

    qrow, krow, qtab, ktab = _att_specs(t)
    return _pc_pairs(body, _tri_pairs(nb, False), n_groups=N_MIX_HEADS // ah,
                     in_specs=[qrow(ah * Q_HEAD_P), qtab, qtab, krow(ah * Q_HEAD_P), ktab],
                     out_specs=[qrow(ah * LANES), qrow(ah * LANES)],
                     out_shape=[_sds((s, MIX_WIDTH)), _sds((s, MIX_WIDTH))], name=name,
                     scratch=[pltpu.VMEM((ah, t, LANES), MXU_DTYPE), pltpu.VMEM((ah, t, LANES), MXU_DTYPE),
                              pltpu.VMEM((ah, t, LANES), F32), pltpu.VMEM((ah, t, LANES), F32), pltpu.VMEM((ah, t, LANES), F32),
                              pltpu.VMEM((ah, t, t), F32), pltpu.VMEM((ah, t, t), MXU_DTYPE)],
                     args=(q, cs, sn, kv, krr))


def _flash_bwd_q(q, kv, krr, cs, sn, o, lse, dcat, name):
    s = q.shape[0]
    t, nb = _att_tiles(s)
    ah = ATT_HEADS_PER_STEP

    def body(qi_ref, kj_ref, q_ref, cs_ref, sn_ref, o_ref, lse_ref, do_ref, kv_ref, kr_ref, dq_ref,
             qn_s, qr_s, dl_s, an_s, ar_s, s_scr, dp_scr, ds_scr):
        i, j = qi_ref[pl.program_id(1)], kj_ref[pl.program_id(1)]

        @pl.when(j == 0)
        def _():
            for hh in range(ah):
                c0 = hh * Q_HEAD_P
                sl = slice(hh * LANES, (hh + 1) * LANES)
                qn_s[hh] = (q_ref[:, c0:c0 + LANES] * ATT_SCALE).astype(MXU_DTYPE)
                qr_s[hh] = (_rope(q_ref[:, c0 + LANES:c0 + Q_HEAD_P], cs_ref[...], sn_ref[...]) * ATT_SCALE).astype(MXU_DTYPE)
                dl_s[hh] = jnp.sum(o_ref[:, sl] * do_ref[:, sl], -1, keepdims=True) * jnp.ones((1, LANES), F32)
            an_s[...] = jnp.zeros_like(an_s)
            ar_s[...] = jnp.zeros_like(ar_s)

        def update(masked):
            kr = kr_ref[...].astype(MXU_DTYPE)
            for hh in range(ah):
                c0 = hh * Q_HEAD_P
                sl = slice(hh * LANES, (hh + 1) * LANES)
                kn = kv_ref[:, c0:c0 + LANES].astype(MXU_DTYPE)
                s_scr[hh] = _dot(qn_s[hh], kn, NT) + _dot(qr_s[hh], kr, NT)
                dp_scr[hh] = _dot(do_ref[:, sl], kv_ref[:, c0 + LANES:c0 + Q_HEAD_P], NT)
                for r in range(0, t, ATT_STRIP):
                    rows = slice(r, r + ATT_STRIP)
                    p = jnp.exp(s_scr[hh, rows, :] - lse_ref[rows, hh * LANES:hh * LANES + 1])
                    if masked:
                        p = jnp.where(_tril_rows(r, ATT_STRIP, t), p, 0.0)
                    ds_scr[hh, rows, :] = (p * (dp_scr[hh, rows, :] - dl_s[hh, rows, :1])).astype(MXU_DTYPE)
                an_s[hh] += _dot(ds_scr[hh], kn, NN)
                ar_s[hh] += _dot(ds_scr[hh], kr, NN)

        @pl.when(j < i)
        def _():
            update(False)

        @pl.when(j == i)
        def _():
            update(True)
            for hh in range(ah):
                c0 = hh * Q_HEAD_P
                dq_ref[:, c0:c0 + LANES] = an_s[hh] * ATT_SCALE
                dq_ref[:, c0 + LANES:c0 + Q_HEAD_P] = _rope_t(ar_s[hh] * ATT_SCALE, cs_ref[...], sn_ref[...])

    qrow, krow, qtab, ktab = _att_specs(t)
    return _pc_pairs(body, _tri_pairs(nb, False), n_groups=N_MIX_HEADS // ah,
                     in_specs=[qrow(ah * Q_HEAD_P), qtab, qtab, qrow(ah * LANES), qrow(ah * LANES), qrow(ah * LANES),
                               krow(ah * Q_HEAD_P), ktab],
                     out_specs=qrow(ah * Q_HEAD_P), out_shape=_sds((s, N_MIX_HEADS * Q_HEAD_P)), name=name,
                     scratch=[pltpu.VMEM((ah, t, LANES), MXU_DTYPE), pltpu.VMEM((ah, t, LANES), MXU_DTYPE),
                              pltpu.VMEM((ah, t, LANES), F32), pltpu.VMEM((ah, t, LANES), F32), pltpu.VMEM((ah, t, LANES), F32),
                              pltpu.VMEM((ah, t, t), F32), pltpu.VMEM((ah, t, t), F32), pltpu.VMEM((ah, t, t), MXU_DTYPE)],
                     args=(q, cs, sn, o, lse, dcat, kv, krr))


def _flash_bwd_kv(q, kv, krr, cs, sn, o, lse, dcat, name):
    s = q.shape[0]
    t, nb = _att_tiles(s)
    ah = ATT_HEADS_PER_STEP

    def body(qi_ref, kj_ref, kv_ref, kr_ref, q_ref, cs_ref, sn_ref, o_ref, lse_ref, do_ref, dkv_ref, dkr_ref,
             akn_s, av_s, akr_s, s_scr, dp_scr, p_scr, ds_scr):
        i, j = qi_ref[pl.program_id(1)], kj_ref[pl.program_id(1)]

        @pl.when(i == j)
        def _():
            akn_s[...] = jnp.zeros_like(akn_s)
            av_s[...] = jnp.zeros_like(av_s)
            akr_s[...] = jnp.zeros_like(akr_s)

        def update(masked):
            kr = kr_ref[...].astype(MXU_DTYPE)
            for hh in range(ah):
                c0 = hh * Q_HEAD_P
                sl = slice(hh * LANES, (hh + 1) * LANES)
                qn = (q_ref[:, c0:c0 + LANES] * ATT_SCALE).astype(MXU_DTYPE)
                qr = (_rope(q_ref[:, c0 + LANES:c0 + Q_HEAD_P], cs_ref[...], sn_ref[...]) * ATT_SCALE).astype(MXU_DTYPE)
                do = do_ref[:, sl].astype(MXU_DTYPE)
                s_scr[hh] = _dot(qn, kv_ref[:, c0:c0 + LANES], NT) + _dot(qr, kr, NT)
                dp_scr[hh] = _dot(do, kv_ref[:, c0 + LANES:c0 + Q_HEAD_P], NT)
                for r in range(0, t, ATT_STRIP):
                    rows = slice(r, r + ATT_STRIP)
                    p = jnp.exp(s_scr[hh, rows, :] - lse_ref[rows, hh * LANES:hh * LANES + 1])
                    if masked:
                        p = jnp.where(_tril_rows(r, ATT_STRIP, t), p, 0.0)
                    dl = jnp.sum(o_ref[rows, sl] * do_ref[rows, sl], -1, keepdims=True)
                    p_scr[hh, rows, :] = p.astype(MXU_DTYPE)
                    ds_scr[hh, rows, :] = (p * (dp_scr[hh, rows, :] - dl)).astype(MXU_DTYPE)
                av_s[hh] += _dot(p_scr[hh], do, TN)
                akn_s[hh] += _dot(ds_scr[hh], qn, TN)
                akr_s[hh] += _dot(ds_scr[hh], qr, TN)

        @pl.when(i > j)
        def _():
            update(False)

        @pl.when(i == j)
        def _():
            update(True)

        @pl.when(i == nb - 1)
        def _():
            for hh in range(ah):
                c0 = hh * Q_HEAD_P
                dkv_ref[:, c0:c0 + LANES] = akn_s[hh]
                dkv_ref[:, c0 + LANES:c0 + Q_HEAD_P] = av_s[hh]
                dkr_ref[hh] = akr_s[hh]

    qrow, krow, qtab, ktab = _att_specs(t)
    return _pc_pairs(body, _tri_pairs(nb, True), n_groups=N_MIX_HEADS // ah,
                     in_specs=[krow(ah * Q_HEAD_P), ktab, qrow(ah * Q_HEAD_P), qtab, qtab,
                               qrow(ah * LANES), qrow(ah * LANES), qrow(ah * LANES)],
                     out_specs=[krow(ah * Q_HEAD_P), pl.BlockSpec((ah, t, LANES), lambda h, p, qi, kj: (h, kj[p], 0))],
                     out_shape=[_sds((s, N_MIX_HEADS * Q_HEAD_P)), _sds((N_MIX_HEADS, s, LANES))], name=name,
                     scratch=[pltpu.VMEM((ah, t, LANES), F32)] * 3
                     + [pltpu.VMEM((ah, t, t), F32), pltpu.VMEM((ah, t, t), F32),
                        pltpu.VMEM((ah, t, t), MXU_DTYPE), pltpu.VMEM((ah, t, t), MXU_DTYPE)],
                     args=(kv, krr, q, cs, sn, o, lse, dcat))


def _mem_head(ops, qh, kh, vh):
    sc = ops.nt(qh, kh) * HEAD_DIM ** -0.5
    e = jnp.exp(sc - lax.stop_gradient(jnp.max(sc, -1, keepdims=True)))
    p = e / jnp.sum(e, -1, keepdims=True)
    return ops.nn(p, vh)


def _mem_fwd(h, qcol, mem_kv, name):
    s = h.shape[0]
    m = mem_kv.shape[0]
    ts = _pick(s, (512, 256, 128))

    def body(q_ref, kv_ref, o_ref):
        for hh in range(N_MEM_HEADS):
            sl = slice(hh * HEAD_DIM, (hh + 1) * HEAD_DIM)
            vsl = slice(MEM_WIDTH + hh * HEAD_DIM, MEM_WIDTH + (hh + 1) * HEAD_DIM)
            o_ref[:, sl] = _mem_head(_RawOps, q_ref[:, sl], kv_ref[:, sl], kv_ref[:, vsl])

    return _pc(body, grid=(s // ts,), in_specs=[_rows(ts, MEM_WIDTH, qcol), _whole((m, 2 * MEM_WIDTH))],
               out_specs=_rows(ts, MEM_WIDTH), out_shape=_sds((s, MEM_WIDTH)), name=name)(h, mem_kv)


def _mem_bwd(h, qcol, mem_kv, dcat, name):
    s = h.shape[0]
    m = mem_kv.shape[0]
    ts = _pick(s, (512, 256, 128))

    def body(q_ref, kv_ref, do_ref, dq_ref, dkv_ref):
        @pl.when(pl.program_id(0) == 0)
        def _():
            dkv_ref[...] = jnp.zeros_like(dkv_ref)

        for hh in range(N_MEM_HEADS):
            sl = slice(hh * HEAD_DIM, (hh + 1) * HEAD_DIM)
            vsl = slice(MEM_WIDTH + hh * HEAD_DIM, MEM_WIDTH + (hh + 1) * HEAD_DIM)
            _, vjp = jax.vjp(functools.partial(_mem_head, _DiffOps), q_ref[:, sl], kv_ref[:, sl], kv_ref[:, vsl])
            dq, dk, dv = vjp(do_ref[:, sl])
            dq_ref[:, sl] = dq
            dkv_ref[:, sl] += dk
            dkv_ref[:, vsl] += dv

    return _pc(body, grid=(s // ts,),
               in_specs=[_rows(ts, MEM_WIDTH, qcol), _whole((m, 2 * MEM_WIDTH)), _rows(ts, MEM_WIDTH, 3)],
               out_specs=[_rows(ts, MEM_WIDTH), _whole((m, 2 * MEM_WIDTH))],
               out_shape=[_sds((s, MEM_WIDTH)), _sds((m, 2 * MEM_WIDTH))], name=name)(h, mem_kv, dcat)


CONV_COLS = 3 * MIX_WIDTH
HALO = 8


def _conv_fwd(h, w, name):
    s = h.shape[0]
    ts = _pick(s, (512, 256, 128))
    wc = 512

    def body(x_ref, halo_ref, w_ref, o_ref, ext):
        i = pl.program_id(0)
        ext[pl.ds(0, HALO), :] = jnp.where(i > 0, halo_ref[...], 0.0)
        ext[pl.ds(HALO, ts), :] = x_ref[...]
        acc = w_ref[0:1, :] * ext[pl.ds(HALO - 3, ts), :]
        for j in range(1, CONV_WIDTH):
            acc = acc + w_ref[j:j + 1, :] * ext[pl.ds(HALO - 3 + j, ts), :]
        o_ref[...] = acc

    halo = pl.BlockSpec((HALO, wc), lambda i, c: (jnp.maximum(i * (ts // HALO) - 1, 0), c))
    blk = pl.BlockSpec((ts, wc), lambda i, c: (i, c))
    return _pc(body, grid=(s // ts, CONV_COLS // wc),
               in_specs=[blk, halo, pl.BlockSpec((CONV_WIDTH, wc), lambda i, c: (0, c))],
               out_specs=blk, out_shape=_sds((s, CONV_COLS)), name=name,
               scratch=[pltpu.VMEM((HALO + ts, wc), F32)])(h, h, w)


def _conv_bwd(h, w, dout, name):
    s = h.shape[0]
    ts = _pick(s, (512, 256, 128))
    nt = s // ts
    wc = 512

    def body(x_ref, xhalo_ref, d_ref, dhalo_ref, w_ref, dx_ref, dw_ref, xext, dext):
        i = pl.program_id(1)

        @pl.when(i == 0)
        def _():
            dw_ref[...] = jnp.zeros_like(dw_ref)

        xext[pl.ds(0, HALO), :] = jnp.where(i > 0, xhalo_ref[...], 0.0)
        xext[pl.ds(HALO, ts), :] = x_ref[...]
        dext[pl.ds(0, ts), :] = d_ref[...]
        dext[pl.ds(ts, HALO), :] = jnp.where(i < nt - 1, dhalo_ref[...], 0.0)
        d = d_ref[...]
        acc = w_ref[CONV_WIDTH - 1:CONV_WIDTH, :] * d
        for j in range(CONV_WIDTH - 1):
            acc = acc + w_ref[j:j + 1, :] * dext[pl.ds(3 - j, ts), :]
        dx_ref[...] = acc
        for j in range(CONV_WIDTH):
            dw_ref[j:j + 1, :] += jnp.sum(d * xext[pl.ds(HALO - 3 + j, ts), :], 0, keepdims=True)

    blk = pl.BlockSpec((ts, wc), lambda c, i: (i, c))
    halo_prev = pl.BlockSpec((HALO, wc), lambda c, i: (jnp.maximum(i * (ts // HALO) - 1, 0), c))
    halo_next = pl.BlockSpec((HALO, wc), lambda c, i: (jnp.minimum((i + 1) * (ts // HALO), s // HALO - 1), c))
    wspec = pl.BlockSpec((CONV_WIDTH, wc), lambda c, i: (0, c))
    return _pc(body, grid=(CONV_COLS // wc, nt), in_specs=[blk, halo_prev, blk, halo_next, wspec],
               out_specs=[blk, wspec], out_shape=[_sds((s, CONV_COLS)), _sds((CONV_WIDTH, CONV_COLS))], name=name,
               scratch=[pltpu.VMEM((HALO + ts, wc), F32), pltpu.VMEM((ts + HALO, wc), F32)])(h, h, dout, dout, w)


def _silu(v):
    return v * jax.nn.sigmoid(v)


def _softplus(v):
    return jnp.maximum(v, 0.0) + jnp.log1p(jnp.exp(-jnp.abs(v)))


def _gdn_prep_head(cq, ck, cv, a, b, alog, dtb):
    q = _silu(cq)
    q = q * lax.rsqrt(jnp.sum(q * q, -1, keepdims=True) + 1e-6) * HEAD_DIM ** -0.5
    k = _silu(ck)
    k = k * lax.rsqrt(jnp.sum(k * k, -1, keepdims=True) + 1e-6)
    v = _silu(cv)
    g = -jnp.exp(alog) * _softplus(a + dtb)
    beta = jax.nn.sigmoid(b)
    ones = jnp.ones((1, HEAD_DIM), F32)
    return q, k, v, g * ones, beta * ones


def _onehot_lane(idx):
    return (lax.broadcasted_iota(jnp.int32, (1, LANES), 1) == idx).astype(F32)


def _lane_pick(row, idx):
    return jnp.sum(row * _onehot_lane(idx), -1, keepdims=True)


def _gdn_prep_fwd(conv, h, alog, dtb, name):
    s = h.shape[0]
    ts = _pick(s, (128,))
    w = MIX_WIDTH

    def body(c_ref, ab_ref, alog_ref, dtb_ref, q_ref, k_ref, v_ref, g_ref, b_ref):
        ab = ab_ref[...]
        for hh in range(N_MIX_HEADS):
            sl = slice(hh * HEAD_DIM, (hh + 1) * HEAD_DIM)
            cols = [slice(p * w + hh * HEAD_DIM, p * w + (hh + 1) * HEAD_DIM) for p in range(3)]
            outs = _gdn_prep_head(c_ref[:, cols[0]], c_ref[:, cols[1]], c_ref[:, cols[2]], _lane_pick(ab, hh),
                                  _lane_pick(ab, N_MIX_HEADS + hh), _lane_pick(alog_ref[...], hh),
                                  _lane_pick(dtb_ref[...], hh))
            for ref, val in zip((q_ref, k_ref, v_ref, g_ref, b_ref), outs):
                ref[:, sl] = val

    return _pc(body, grid=(s // ts,),
               in_specs=[_rows(ts, 3 * w), _rows(ts, LANES, (4 * w + MEM_WIDTH) // LANES), _whole((1, LANES)), _whole((1, LANES))],
               out_specs=[_rows(ts, w)] * 5, out_shape=[_sds((s, w))] * 5, name=name)(conv, h, alog, dtb)


def _gdn_prep_bwd(conv, h, alog, dtb, dq, dk, dv, dg, db, name):
    s = h.shape[0]
    ts = _pick(s, (128,))
    w = MIX_WIDTH

    def body(c_ref, ab_ref, alog_ref, dtb_ref, dq_ref, dk_ref, dv_ref, dg_ref, db_ref,
             dc_ref, dab_ref, dalog_ref, ddtb_ref):
        @pl.when(pl.program_id(0) == 0)
        def _():
            dalog_ref[...] = jnp.zeros_like(dalog_ref)
            ddtb_ref[...] = jnp.zeros_like(ddtb_ref)

        ab = ab_ref[...]
        dab = jnp.zeros_like(ab)
        dalog = jnp.zeros((1, LANES), F32)
        ddtb = jnp.zeros((1, LANES), F32)
        for hh in range(N_MIX_HEADS):
            sl = slice(hh * HEAD_DIM, (hh + 1) * HEAD_DIM)
            cols = [slice(p * w + hh * HEAD_DIM, p * w + (hh + 1) * HEAD_DIM) for p in range(3)]
            _, vjp = jax.vjp(_gdn_prep_head, c_ref[:, cols[0]], c_ref[:, cols[1]], c_ref[:, cols[2]], _lane_pick(ab, hh),
                             _lane_pick(ab, N_MIX_HEADS + hh), _lane_pick(alog_ref[...], hh), _lane_pick(dtb_ref[...], hh))
            dcq, dck, dcv, da, dbb, dal, ddt = vjp((dq_ref[:, sl], dk_ref[:, sl], dv_ref[:, sl], dg_ref[:, sl], db_ref[:, sl]))
            dc_ref[:, cols[0]] = dcq
            dc_ref[:, cols[1]] = dck
            dc_ref[:, cols[2]] = dcv
            dab = dab + da * _onehot_lane(hh) + dbb * _onehot_lane(N_MIX_HEADS + hh)
            dalog = dalog + dal * _onehot_lane(hh)
            ddtb = ddtb + ddt * _onehot_lane(hh)
        dab_ref[...] = dab
        dalog_ref[...] += dalog
        ddtb_ref[...] += ddtb

    return _pc(body, grid=(s // ts,),
               in_specs=[_rows(ts, 3 * w), _rows(ts, LANES, (4 * w + MEM_WIDTH) // LANES),
                         _whole((1, LANES)), _whole((1, LANES))] + [_rows(ts, w)] * 5,
               out_specs=[_rows(ts, 3 * w), _rows(ts, LANES), _whole((1, LANES)), _whole((1, LANES))],
               out_shape=[_sds((s, 3 * w)), _sds((s, LANES)), _sds((1, LANES)), _sds((1, LANES))],
               name=name)(conv, h, alog, dtb, dq, dk, dv, dg, db)


NNB = (((2,), (1,)), ((0,), (0,)))
NTB = (((2,), (2,)), ((0,), (0,)))
TNB = (((1,), (1,)), ((0,), (0,)))


def _dg(a, b, dims):
    return lax.dot_general(a, b, dims, preferred_element_type=F32)


def _dotb(a, b, dims, mode):
    a1 = a.astype(MXU_DTYPE)
    b1 = b.astype(MXU_DTYPE)
    if mode == 1:
        return _dg(a1, b1, dims)
    rb = b - b1.astype(F32)
    b2 = rb.astype(MXU_DTYPE)
    if mode == 3:
        a2 = (a - a1.astype(F32)).astype(MXU_DTYPE)
        return _dg(a1, b1, dims) + (_dg(a1, b2, dims) + _dg(a2, b1, dims))
    b3 = (rb - b2.astype(F32)).astype(MXU_DTYPE)
    return _dg(a1, b1, dims) + (_dg(a1, b2, dims) + _dg(a1, b3, dims))


@functools.partial(jax.custom_vjp, nondiff_argnums=(2,))
def _nnb(a, b, mode):
    return _dotb(a, b, NNB, mode)


@functools.partial(jax.custom_vjp, nondiff_argnums=(2,))
def _ntb(a, b, mode):
    return _dotb(a, b, NTB, mode)


@functools.partial(jax.custom_vjp, nondiff_argnums=(2,))
def _tnb(a, b, mode):
    return _dotb(a, b, TNB, mode)


_nnb.defvjp(lambda a, b, mode: (_dotb(a, b, NNB, mode), (a, b)),
            lambda mode, r, g: (_ntb(g, r[1], mode), _tnb(r[0], g, mode)))
_ntb.defvjp(lambda a, b, mode: (_dotb(a, b, NTB, mode), (a, b)),
            lambda mode, r, g: (_nnb(g, r[1], mode), _tnb(g, r[0], mode)))
_tnb.defvjp(lambda a, b, mode: (_dotb(a, b, TNB, mode), (a, b)),
            lambda mode, r, g: (_ntb(r[1], g, mode), _nnb(r[0], g, mode)))


def _trilb(h):
    t = lax.broadcasted_iota(jnp.int32, (CHUNK, CHUNK), 0) >= lax.broadcasted_iota(jnp.int32, (CHUNK, CHUNK), 1)
    return jnp.broadcast_to(t.astype(F32)[None], (h, CHUNK, CHUNK))


def _cumsum_rows_raw(gb):
    return _dotb(_trilb(gb.shape[0]), gb, NNB, "lhs")


_cumsum_rows = jax.custom_vjp(_cumsum_rows_raw)
_cumsum_rows.defvjp(lambda gb: (_cumsum_rows_raw(gb), None),
                    lambda _, g: (_dotb(_trilb(g.shape[0]), g, TNB, "lhs"),))


def _row_col_raw(gc):
    return gc[:, :, :CHUNK], jnp.swapaxes(gc, 1, 2)[:, :CHUNK, :]


def _row_col_bwd(_, g):
    part = g[0] + jnp.swapaxes(g[1], 1, 2)
    return (jnp.concatenate([part, jnp.zeros_like(part)], -1),)


_row_col = jax.custom_vjp(_row_col_raw)
_row_col.defvjp(lambda gc: (_row_col_raw(gc), None), _row_col_bwd)


def _gdn_chunk(diff, state, q, k, v, gb, bb):
    if diff:
        nn, nt, tn, cumsum, row_col = _nnb, _ntb, _tnb, _cumsum_rows, _row_col
    else:
        nn = lambda a, b, m: _dotb(a, b, NNB, m)
        nt = lambda a, b, m: _dotb(a, b, NTB, m)
        tn = lambda a, b, m: _dotb(a, b, TNB, m)
        cumsum, row_col = _cumsum_rows_raw, _row_col_raw
    c = CHUNK
    row = lax.broadcasted_iota(jnp.int32, (1, c, c), 1)
    col = lax.broadcasted_iota(jnp.int32, (1, c, c), 2)
    tril = row >= col
    strict = row > col
    eye = (row == col).astype(F32)
    gc = cumsum(gb)
    gci, gcj = row_col(gc)
    decay = jnp.where(tril, jnp.exp(jnp.where(tril, gci - gcj, 0.0)), 0.0)
    kb = k * bb
    low = jnp.where(strict, nt(kb, k, 1) * decay, 0.0)
    inv = eye - low
    pw = low
    for _ in range(5):
        pw = nn(pw, pw, 3)
        inv = inv + nn(inv, pw, 3)
    ge = jnp.exp(gc)
    u = nn(inv, v * bb, 3)
    w = nn(inv, kb * ge, 3)
    a_qk = jnp.where(tril, nt(q, k, 1) * decay, 0.0)
    g_tot = jnp.sum(gb, 1, keepdims=True)
    k_dec = k * jnp.exp(g_tot - gc)
    v_new = u - nn(w, state, 1)
    o = nn(q * ge, state, 1) + nn(a_qk, v_new, 1)
    new_state = state * jnp.exp(g_tot) + tn(k_dec, v_new, 1)
    return new_state, o


def _stack_heads(ref, hp):
    return jnp.stack([ref[:, hh * HEAD_DIM:(hh + 1) * HEAD_DIM] for hh in range(hp)])


GDN_HEADS_FWD = 12
GDN_HEADS_BWD = 12


def _gdn_core_fwd(q, k, v, gb, bb, name):
    s = q.shape[0]
    nc = s // CHUNK
    hp = GDN_HEADS_FWD
    w = hp * HEAD_DIM

    def body(q_ref, k_ref, v_ref, g_ref, b_ref, o_ref, st_ref, state):
        @pl.when(pl.program_id(1) == 0)
        def _():
            state[...] = jnp.zeros_like(state)

        st = state[...]
        st_ref[0] = st
        new_state, o = _gdn_chunk(False, st, *(_stack_heads(r, hp) for r in (q_ref, k_ref, v_ref, g_ref, b_ref)))
        state[...] = new_state
        for hh in range(hp):
            o_ref[:, hh * HEAD_DIM:(hh + 1) * HEAD_DIM] = o[hh]

    blk = pl.BlockSpec((CHUNK, w), lambda hg, c: (c, hg))
    return _pc(body, grid=(N_MIX_HEADS // hp, nc), in_specs=[blk] * 5,
               out_specs=[blk, pl.BlockSpec((1, hp, HEAD_DIM, HEAD_DIM), lambda hg, c: (c, hg, 0, 0))],
               out_shape=[_sds((s, MIX_WIDTH)), _sds((nc, N_MIX_HEADS, HEAD_DIM, HEAD_DIM))], name=name,
               scratch=[pltpu.VMEM((hp, HEAD_DIM, HEAD_DIM), F32)])(q, k, v, gb, bb)


def _gdn_core_bwd(q, k, v, gb, bb, states, do, name):
    s = q.shape[0]
    nc = s // CHUNK
    hp = GDN_HEADS_BWD
    w = hp * HEAD_DIM

    def body(q_ref, k_ref, v_ref, g_ref, b_ref, st_ref, do_ref, dq_ref, dk_ref, dv_ref, dg_ref, db_ref, dstate):
        @pl.when(pl.program_id(1) == 0)
        def _():
            dstate[...] = jnp.zeros_like(dstate)

        _, vjp = jax.vjp(functools.partial(_gdn_chunk, True), st_ref[0],
                         *(_stack_heads(r, hp) for r in (q_ref, k_ref, v_ref, g_ref, b_ref)))
        grads = vjp((dstate[...], _stack_heads(do_ref, hp)))
        dstate[...] = grads[0]
        for ref, val in zip((dq_ref, dk_ref, dv_ref, dg_ref, db_ref), grads[1:]):
            for hh in range(hp):
                ref[:, hh * HEAD_DIM:(hh + 1) * HEAD_DIM] = val[hh]

    blk = pl.BlockSpec((CHUNK, w), lambda hg, c: (nc - 1 - c, hg))
    return _pc(body, grid=(N_MIX_HEADS // hp, nc),
               in_specs=[blk] * 5 + [pl.BlockSpec((1, hp, HEAD_DIM, HEAD_DIM), lambda hg, c: (nc - 1 - c, hg, 0, 0)), blk],
               out_specs=[blk] * 5, out_shape=[_sds((s, MIX_WIDTH))] * 5, name=name,
               scratch=[pltpu.VMEM((hp, HEAD_DIM, HEAD_DIM), F32)])(q, k, v, gb, bb, states, do)


def _gdn_out_head(o, z, g):
    return _rms(o, g) * _silu(z)


def _gdn_out_fwd(o, h, onorm, name):
    s = o.shape[0]
    ts = _pick(s, (256, 128))

    def body(o_ref, z_ref, g_ref, y_ref):
        for hh in range(N_MIX_HEADS):
            sl = slice(hh * HEAD_DIM, (hh + 1) * HEAD_DIM)
            y_ref[:, sl] = _gdn_out_head(o_ref[:, sl], z_ref[:, sl], g_ref[...])

    return _pc(body, grid=(s // ts,), in_specs=[_rows(ts, MIX_WIDTH), _rows(ts, MIX_WIDTH, 3), _whole((1, HEAD_DIM))],
               out_specs=_rows(ts, MIX_WIDTH), out_shape=_sds((s, MIX_WIDTH)), name=name)(o, h, onorm)


def _gdn_out_bwd(o, h, onorm, dcat, name):
    s = o.shape[0]
    ts = _pick(s, (256, 128))

    def body(o_ref, z_ref, g_ref, dy_ref, do_ref, dz_ref, dg_ref):
        @pl.when(pl.program_id(0) == 0)
        def _():
            dg_ref[...] = jnp.zeros_like(dg_ref)

        dgs = jnp.zeros((1, HEAD_DIM), F32)
        for hh in range(N_MIX_HEADS):
            sl = slice(hh * HEAD_DIM, (hh + 1) * HEAD_DIM)
            _, vjp = jax.vjp(_gdn_out_head, o_ref[:, sl], z_ref[:, sl], g_ref[...])
            do, dz, dg = vjp(dy_ref[:, sl])
            do_ref[:, sl] = do
            dz_ref[:, sl] = dz
            dgs = dgs + dg
        dg_ref[...] += dgs

    return _pc(body, grid=(s // ts,),
               in_specs=[_rows(ts, MIX_WIDTH), _rows(ts, MIX_WIDTH, 3), _whole((1, HEAD_DIM)), _rows(ts, MIX_WIDTH, 0)],
               out_specs=[_rows(ts, MIX_WIDTH), _rows(ts, MIX_WIDTH), _whole((1, HEAD_DIM))],
               out_shape=[_sds((s, MIX_WIDTH)), _sds((s, MIX_WIDTH)), _sds((1, HEAD_DIM))], name=name)(o, h, onorm, dcat)


def _row(v):
    return v.reshape(1, -1)


def _lane_row(v):
    return jnp.pad(v, (0, LANES - v.shape[0])).reshape(1, LANES)


def _rope_tables(positions):
    inv_freq = 1.0 / (ROPE_THETA ** (jnp.arange(0, QK_ROPE, 2, dtype=F32) / QK_ROPE))
    ang = positions.astype(F32)[:, None] * inv_freq
    cos, sin = jnp.cos(ang), jnp.sin(ang)
    z = jnp.zeros_like(cos)
    return jnp.concatenate([cos, z, cos, z], 1), jnp.concatenate([-sin, z, sin, z], 1)


def _local_step(x, mem, positions, loss_target, wts, small):
    cs, sn = _rope_tables(positions)
    saved = []
    for i in range(DEPTH):
        j = i // 2
        sv = {"x": x}
        sv["mem_kv"] = _mm(mem, wts["mem_w_kv"][i], name=f"l{i}_memkv")
        if i % 2 == 0:
            h = _mm(x, wts["mla_w_in"][j], name=f"l{i}_in")
            cqn, ckvn, krr = _mla_prep_fwd(h, _row(small["mla_q_norm"][j]), _row(small["mla_kv_norm"][j]), cs, sn, f"l{i}_mlaprep")
            q = _mm(cqn, wts["mla_w_uq"][j], name=f"l{i}_uq")
            kv = _mm(ckvn, wts["mla_w_ukv"][j], name=f"l{i}_ukv")
            mix, lse = _flash_fwd(q, kv, krr, cs, sn, f"l{i}_flash")
            sv.update(cqn=cqn, ckvn=ckvn, krr=krr, q=q, kv=kv, o=mix, lse=lse)
            qcol = 2
        else:
            h = _mm(x, wts["gdn_w_in"][j], name=f"l{i}_in")
            conv = _conv_fwd(h, small["gdn_conv"][j], f"l{i}_conv")
            qn, kn, vv, gb, bb = _gdn_prep_fwd(conv, h, _lane_row(small["gdn_a_log"][j]), _lane_row(small["gdn_dt_bias"][j]), f"l{i}_gdnprep")
            o, states = _gdn_core_fwd(qn, kn, vv, gb, bb, f"l{i}_gdncore")
            mix = _gdn_out_fwd(o, h, _row(small["gdn_o_norm"][j]), f"l{i}_gdnout")
            sv.update(conv=conv, qn=qn, kn=kn, vv=vv, gb=gb, bb=bb, o=o, states=states)
            qcol = 4 * MIX_WIDTH // MEM_WIDTH
        mem_o = _mem_fwd(h, qcol, sv["mem_kv"], f"l{i}_mem")
        cat = jnp.concatenate([mix, mem_o], 1)
        y = _mm(cat, wts["w_out"][i], name=f"l{i}_out")
        x1 = _ln_fwd(x, y, _row(small["ln1_g"][i]), _row(small["ln1_b"][i]), f"l{i}_ln1")
        h1 = _mm(x1, wts["mlp_w1"][i], name=f"l{i}_w1")
        ff = _mm(h1, wts["mlp_w2"][i], a_fn=_relu2, name=f"l{i}_w2")
        x2 = _ln_fwd(x1, ff, _row(small["ln2_g"][i]), _row(small["ln2_b"][i]), f"l{i}_ln2")
        sv.update(h=h, qcol=qcol, cat=cat, y=y, x1=x1, h1=h1, ff=ff)
        saved.append(sv)
        x = x2

    loss_row, dx = _loss_and_grad(x, loss_target, "loss")

    gw = {k: [None] * len(v) for k, v in wts.items()}
    gs = {k: [None] * v.shape[0] for k, v in small.items()}
    gdt = COMM_DTYPE
    for i in reversed(range(DEPTH)):
        j = i // 2
        sv = saved[i]
        dr2, dg, db = _ln_bwd(sv["x1"], sv["ff"], _row(small["ln2_g"][i]), _row(small["ln2_b"][i]), dx, f"l{i}_ln2b")
        gs["ln2_g"][i], gs["ln2_b"][i] = dg[0], db[0]
        dh1 = _mm(dr2, wts["mlp_w2"][i], tb=True, epi=_relu2_bwd, e=sv["h1"], name=f"l{i}_dh1")
        gw["mlp_w2"][i] = _mm(sv["h1"], dr2, ta=True, a_fn=_relu2, out_dtype=gdt, name=f"l{i}_dw2")
        gw["mlp_w1"][i] = _mm(sv["x1"], dh1, ta=True, out_dtype=gdt, name=f"l{i}_dw1")
        dx1 = _mm(dh1, wts["mlp_w1"][i], tb=True, epi=_add_alpha, e=dr2, name=f"l{i}_dx1")
        dr1, dg, db = _ln_bwd(sv["x"], sv["y"], _row(small["ln1_g"][i]), _row(small["ln1_b"][i]), dx1, f"l{i}_ln1b")
        gs["ln1_g"][i], gs["ln1_b"][i] = dg[0], db[0]
        gw["w_out"][i] = _mm(sv["cat"], dr1, ta=True, out_dtype=gdt, name=f"l{i}_dwout")
        dcat = _mm(dr1, wts["w_out"][i], tb=True, name=f"l{i}_dcat")
        h = sv["h"]
        dqmem, dmem_kv = _mem_bwd(h, sv["qcol"], sv["mem_kv"], dcat, f"l{i}_memb")
        gw["mem_w_kv"][i] = _mm(mem, dmem_kv, ta=True, out_dtype=gdt, name=f"l{i}_dwmem")
        zpad = jnp.zeros((h.shape[0], LANES), F32)
        if i % 2 == 0:
            qnw, kvnw = _row(small["mla_q_norm"][j]), _row(small["mla_kv_norm"][j])
            dq = _flash_bwd_q(sv["q"], sv["kv"], sv["krr"], cs, sn, sv["o"], sv["lse"], dcat, f"l{i}_flashbq")
            dkv, dkrr = _flash_bwd_kv(sv["q"], sv["kv"], sv["krr"], cs, sn, sv["o"], sv["lse"], dcat, f"l{i}_flashbkv")
            gw["mla_w_ukv"][j] = _mm(sv["ckvn"], dkv, ta=True, out_dtype=gdt, name=f"l{i}_dwukv")
            dckvn = _mm(dkv, wts["mla_w_ukv"][j], tb=True, name=f"l{i}_dckvn")
            gw["mla_w_uq"][j] = _mm(sv["cqn"], dq, ta=True, out_dtype=gdt, name=f"l{i}_dwuq")
            dcqn = _mm(dq, wts["mla_w_uq"][j], tb=True, name=f"l{i}_dcqn")
            dcq, dckv, dkr, dqn, dkvn = _mla_prep_bwd(h, qnw, kvnw, cs, sn, dcqn, dckvn, dkrr, f"l{i}_mlaprepb")
            gs["mla_q_norm"][j], gs["mla_kv_norm"][j] = dqn[0], dkvn[0]
            dh = jnp.concatenate([dcq, dckv, dqmem, dkr, zpad], 1)
            w_in = wts["mla_w_in"][j]
            key = "mla_w_in"
        else:
            alog, dtb = _lane_row(small["gdn_a_log"][j]), _lane_row(small["gdn_dt_bias"][j])
            do, dz, dgn = _gdn_out_bwd(sv["o"], h, _row(small["gdn_o_norm"][j]), dcat, f"l{i}_gdnoutb")
            gs["gdn_o_norm"][j] = dgn[0]
            dqn, dkn, dvv, dgb, dbb = _gdn_core_bwd(sv["qn"], sv["kn"], sv["vv"], sv["gb"], sv["bb"], sv["states"], do, f"l{i}_gdncoreb")
            dconv, dab, dalog, ddtb = _gdn_prep_bwd(sv["conv"], h, alog, dtb, dqn, dkn, dvv, dgb, dbb, f"l{i}_gdnprepb")
            gs["gdn_a_log"][j], gs["gdn_dt_bias"][j] = dalog[0, :N_MIX_HEADS], ddtb[0, :N_MIX_HEADS]
            dhqkv, dconvw = _conv_bwd(h, small["gdn_conv"][j], dconv, f"l{i}_convb")
            gs["gdn_conv"][j] = dconvw
            dh = jnp.concatenate([dhqkv, dz, dqmem, dab, zpad], 1)
            w_in = wts["gdn_w_in"][j]
            key = "gdn_w_in"
        gw[key][j] = _mm(sv["x"], dh, ta=True, out_dtype=gdt, name=f"l{i}_dwin")
        dx = _mm(dh, w_in, tb=True, epi=_add_alpha, e=dr1, name=f"l{i}_dx")
    gs = {k: jnp.stack(v) for k, v in gs.items()}
    return loss_row, dx, gw, gs


def _mla_in_to_kernel(w):
    z32 = jnp.zeros((w.shape[0], 32), w.dtype)
    z128 = jnp.zeros((w.shape[0], LANES), w.dtype)
    return jnp.concatenate([w[:, :1024], w[:, 1088:1600], w[:, 1024:1056], z32, w[:, 1056:1088], z32, z128], 1)


def _mla_in_from_kernel(g):
    return jnp.concatenate([g[:, :1024], g[:, 1536:1568], g[:, 1600:1632], g[:, 1024:1536]], 1)


def _uq_to_kernel(w):
    w3 = w.reshape(Q_LORA, N_MIX_HEADS, QK_NOPE + QK_ROPE)
    z = jnp.zeros((Q_LORA, N_MIX_HEADS, 32), w.dtype)
    return jnp.concatenate([w3[:, :, :128], w3[:, :, 128:160], z, w3[:, :, 160:192], z], 2).reshape(Q_LORA, N_MIX_HEADS * Q_HEAD_P)


def _uq_from_kernel(g):
    g3 = g.reshape(Q_LORA, N_MIX_HEADS, Q_HEAD_P)
    return jnp.concatenate([g3[:, :, :128], g3[:, :, 128:160], g3[:, :, 192:224]], 2).reshape(Q_LORA, -1)


def _gdn_in_to_kernel(w):
    z = jnp.zeros((w.shape[0], LANES - 2 * N_MIX_HEADS + LANES), w.dtype)
    return jnp.concatenate([w[:, :6144], w[:, 6168:6680], w[:, 6144:6168], z], 1)


def _gdn_in_from_kernel(g):
    return jnp.concatenate([g[:, :6144], g[:, 6656:6680], g[:, 6144:6656]], 1)


def _exchange(srcs, name, gather):
    n = len(srcs)

    def body(*refs):
        src_refs, out_refs = refs[:n], refs[n:2 * n]
        send_sems, recv_sems, local_sems = refs[2 * n:]
        x, y, c = lax.axis_index("x"), lax.axis_index("y"), lax.axis_index("c")
        me = 4 * x + 2 * y + c
        local, sends, recvs = [], [], []
        for a in range(n):
            src_ref, out_ref = src_refs[a], out_refs[a]
            local.append(pltpu.make_async_copy(src_ref if gather else src_ref.at[me], out_ref.at[me], local_sems.at[a]))
            for kk in range(1, N_DEV):
                px, py, pc = x ^ ((kk >> 2) & 1), y ^ ((kk >> 1) & 1), c ^ (kk & 1)
                peer = 4 * px + 2 * py + pc
                piece = src_ref if gather else src_ref.at[peer]
                sends.append(pltpu.make_async_remote_copy(
                    src_ref=piece, dst_ref=out_ref.at[me], send_sem=send_sems.at[a, kk - 1], recv_sem=recv_sems.at[a, kk - 1],
                    device_id=(px, py, pc), device_id_type=pl.DeviceIdType.MESH))
                recvs.append(pltpu.make_async_remote_copy(
                    src_ref=piece, dst_ref=out_ref.at[peer], send_sem=send_sems.at[a, kk - 1], recv_sem=recv_sems.at[a, kk - 1],
                    device_id=(px, py, pc), device_id_type=pl.DeviceIdType.MESH))
        for cp in local + sends:
            cp.start()
        for cp in recvs:
            cp.wait_recv()
        for cp in sends:
            cp.wait_send()
        for cp in local:
            cp.wait()

    hbm = pl.BlockSpec(memory_space=pltpu.HBM)
    shapes = [_sds((N_DEV,) + tuple(s.shape if gather else s.shape[1:]), s.dtype) for s in srcs]
    return pl.pallas_call(
        body, in_specs=[hbm] * n, out_specs=[hbm] * n, out_shape=shapes, name=name,
        scratch_shapes=[pltpu.SemaphoreType.DMA((n, N_DEV - 1)), pltpu.SemaphoreType.DMA((n, N_DEV - 1)),
                        pltpu.SemaphoreType.DMA((n,))],
    )(*srcs)


def _adamw(parts, w, m, v, name):
    r, c = w.shape
    n_parts = parts.shape[0]
    tr = _pick(r, tuple(t for t in (256, 128, 64, 32, 16, 8) if t * c <= 256 * 1024))

    def body(p_ref, w_ref, m_ref, v_ref, g_ref, d_ref, nm_ref, nv_ref):
        g = p_ref[0].astype(F32)
        for dd in range(1, n_parts):
            g = g + p_ref[dd].astype(F32)
        nm = ADAM_B1 * m_ref[...] + (1.0 - ADAM_B1) * g
        nv = ADAM_B2 * v_ref[...] + (1.0 - ADAM_B2) * jnp.square(g)
        m_hat = nm / (1.0 - ADAM_B1 ** ADAM_STEP)
        v_hat = nv / (1.0 - ADAM_B2 ** ADAM_STEP)
        g_ref[...] = g
        d_ref[...] = -ADAM_LR * (m_hat / (jnp.sqrt(v_hat) + ADAM_EPS) + ADAM_WD * w_ref[...])
        nm_ref[...] = nm
        nv_ref[...] = nv

    blk = pl.BlockSpec((tr, c), lambda i: (i, 0))
    return _pc(body, grid=(r // tr,), in_specs=[pl.BlockSpec((n_parts, tr, c), lambda i: (0, i, 0)), blk, blk, blk],
               out_specs=[blk] * 4, out_shape=[_sds((r, c))] * 4, name=name)(parts, w, m, v)


N_CHIPS = 4
MESH_IDS = pl.DeviceIdType.MESH


def _place():
    x, y, c = lax.axis_index("x"), lax.axis_index("y"), lax.axis_index("c")
    return x, y, c, [(1 - x, y), (x, 1 - y), (1 - x, 1 - y)]


def _gather_two_level(srcs, name):
    n = len(srcs)

    def body(*refs):
        src_refs, out_refs = refs[:n], refs[n:2 * n]
        send_sems, recv_sems, local_sems = refs[2 * n:]
        x, y, c, chips = _place()
        sib = (x, y, 1 - c)

        def copy(a, k, block, to, src=None):
            rows = out_refs[a].at[4 * block[0] + 2 * block[1] + block[2]]
            return pltpu.make_async_remote_copy(src_ref=rows if src is None else src, dst_ref=rows,
                                                send_sem=send_sems.at[a, k], recv_sem=recv_sems.at[a, k],
                                                device_id=to, device_id_type=MESH_IDS)

        local = [pltpu.make_async_copy(src_refs[a], out_refs[a].at[4 * x + 2 * y + c], local_sems.at[a]) for a in range(n)]
        first = []
        for a in range(n):
            first.append(copy(a, 0, (x, y, c), sib, src_refs[a]))
            first += [copy(a, 1 + j, (x, y, c), (*chip, c), src_refs[a]) for j, chip in enumerate(chips)]
        for cp in local + first:
            cp.start()
        passed = []
        for a in range(n):
            for j, chip in enumerate(chips):
                copy(a, 1 + j, (*chip, c), (x, y, c)).wait_recv()
                passed.append(copy(a, 4 + j, (*chip, c), sib))
                passed[-1].start()
        for a in range(n):
            copy(a, 0, (x, y, 1 - c), (x, y, c)).wait_recv()
            for j, chip in enumerate(chips):
                copy(a, 4 + j, (*chip, 1 - c), (x, y, c)).wait_recv()
        for cp in first + passed:
            cp.wait_send()
        for cp in local:
            cp.wait()

    hbm = pl.BlockSpec(memory_space=pltpu.HBM)
    return pl.pallas_call(
        body, in_specs=[hbm] * n, out_specs=[hbm] * n, name=name,
        out_shape=[_sds((N_DEV,) + tuple(s.shape), s.dtype) for s in srcs],
        scratch_shapes=[pltpu.SemaphoreType.DMA((n, 7)), pltpu.SemaphoreType.DMA((n, 7)), pltpu.SemaphoreType.DMA((n,))],
    )(*srcs)


def _pair_exchange(srcs, name):
    n = len(srcs)

    def body(*refs):
        src_refs, out_refs = refs[:n], refs[n:2 * n]
        send_sems, recv_sems = refs[2 * n:]
        x, y, c, _ = _place()
        sends = []
        for a in range(n):
            for ch in range(N_CHIPS):
                sends.append(pltpu.make_async_remote_copy(
                    src_ref=src_refs[a].at[2 * ch + (1 - c)], dst_ref=out_refs[a].at[ch],
                    send_sem=send_sems.at[a, ch], recv_sem=recv_sems.at[a, ch], device_id=(x, y, 1 - c),
                    device_id_type=MESH_IDS))
        for cp in sends:
            cp.start()
        for cp in sends:
            cp.wait_recv()
        for cp in sends:
            cp.wait_send()

    hbm = pl.BlockSpec(memory_space=pltpu.HBM)
    return pl.pallas_call(
        body, in_specs=[hbm] * n, out_specs=[hbm] * n, name=name,
        out_shape=[_sds((N_CHIPS,) + tuple(s.shape[1:]), s.dtype) for s in srcs],
        scratch_shapes=[pltpu.SemaphoreType.DMA((n, N_CHIPS)), pltpu.SemaphoreType.DMA((n, N_CHIPS))],
    )(*srcs)


def _pair_sum(mine, theirs, my_c, name):
    _, r, c = mine.shape
    tr = _pick(r, tuple(t for t in (512, 256, 128, 64, 32, 16, 8) if t * c <= 512 * 1024))

    def body(c_ref, a_ref, b_ref, o_ref):
        o_ref[...] = (a_ref[...].astype(F32) + b_ref[...].astype(F32)).astype(o_ref.dtype)

    return pl.pallas_call(
        body, out_shape=_sds((N_CHIPS, r, c), mine.dtype), name=name,
        grid_spec=pltpu.PrefetchScalarGridSpec(
            num_scalar_prefetch=1, grid=(N_CHIPS, r // tr),
            in_specs=[pl.BlockSpec((1, tr, c), lambda ch, i, cc: (2 * ch + cc[0], i, 0)),
                      pl.BlockSpec((1, tr, c), lambda ch, i, cc: (ch, i, 0))],
            out_specs=pl.BlockSpec((1, tr, c), lambda ch, i, cc: (ch, i, 0))),
        compiler_params=pltpu.CompilerParams(dimension_semantics=("arbitrary", "arbitrary"), vmem_limit_bytes=VMEM_LIMIT),
    )(my_c, mine, theirs)


def _chip_exchange(srcs, name):
    n = len(srcs)

    def body(*refs):
        src_refs, out_refs = refs[:n], refs[n:2 * n]
        send_sems, recv_sems, local_sems = refs[2 * n:]
        x, y, c, chips = _place()
        my_chip = 2 * x + y
        local = [pltpu.make_async_copy(src_refs[a].at[my_chip], out_refs[a].at[my_chip], local_sems.at[a]) for a in range(n)]
        sends, recvs = [], []
        for a in range(n):
            for j, chip in enumerate(chips):
                ch = 2 * chip[0] + chip[1]
                sends.append(pltpu.make_async_remote_copy(
                    src_ref=src_refs[a].at[ch], dst_ref=out_refs[a].at[my_chip], send_sem=send_sems.at[a, j],
                    recv_sem=recv_sems.at[a, j], device_id=(*chip, c), device_id_type=MESH_IDS))
                recvs.append(pltpu.make_async_remote_copy(
                    src_ref=src_refs[a].at[ch], dst_ref=out_refs[a].at[ch], send_sem=send_sems.at[a, j],
                    recv_sem=recv_sems.at[a, j], device_id=(*chip, c), device_id_type=MESH_IDS))
        for cp in local + sends:
            cp.start()
        for cp in recvs:
            cp.wait_recv()
        for cp in sends:
            cp.wait_send()
        for cp in local:
            cp.wait()

    hbm = pl.BlockSpec(memory_space=pltpu.HBM)
    return pl.pallas_call(
        body, in_specs=[hbm] * n, out_specs=[hbm] * n, name=name,
        out_shape=[_sds(tuple(s.shape), s.dtype) for s in srcs],
        scratch_shapes=[pltpu.SemaphoreType.DMA((n, 3)), pltpu.SemaphoreType.DMA((n, 3)), pltpu.SemaphoreType.DMA((n,))],
    )(*srcs)


BIG = (("mla_w_in", 1), ("mla_w_uq", 2), ("mla_w_ukv", 2), ("gdn_w_in", 2), ("mem_w_kv", 1), ("w_out", 1),
       ("mlp_w1", 2), ("mlp_w2", 1), ("gdn_conv", 2))
SMALL = ("mla_q_norm", "mla_kv_norm", "gdn_a_log", "gdn_dt_bias", "gdn_o_norm", "ln1_g", "ln1_b", "ln2_g", "ln2_b")
PACK_COLS = 1024


def _unshard(pieces, axis):
    t = jnp.moveaxis(pieces, 0, axis)
    return t.reshape(t.shape[:axis] + (t.shape[axis] * t.shape[axis + 1],) + t.shape[axis + 2:])


def _shard(full, axis):
    t = full.reshape(full.shape[:axis] + (N_DEV, full.shape[axis] // N_DEV) + full.shape[axis + 1:])
    return jnp.moveaxis(t, axis, 0)


def _pack(arrs, lead, cols, mult):
    lead_shape = arrs[0].shape[:lead]
    flat = jnp.concatenate([a.reshape(lead_shape + (-1,)) for a in arrs], -1)
    n = flat.shape[-1]
    r = -(-n // (cols * mult)) * mult
    flat = jnp.pad(flat, [(0, 0)] * lead + [(0, r * cols - n)])
    return flat.reshape(lead_shape + (r, cols))


def _unpack(buf, lead, shapes):
    lead_shape = buf.shape[:lead]
    flat = buf.reshape(lead_shape + (-1,))
    out, off = [], 0
    for shp in shapes:
        n = 1
        for d in shp:
            n *= d
        out.append(flat[..., off:off + n].reshape(lead_shape + tuple(shp)))
        off += n
    return out


def kernel(x, mem, positions, mla_w_in, mla_q_norm, mla_w_uq, mla_kv_norm, mla_w_ukv, gdn_w_in, gdn_conv, gdn_a_log, gdn_dt_bias, gdn_o_norm, mem_w_kv, w_out, ln1_g, ln1_b, mlp_w1, mlp_w2, ln2_g, ln2_b, loss_target, m_mla_w_in, m_mla_q_norm, m_mla_w_uq, m_mla_kv_norm, m_mla_w_ukv, m_gdn_w_in, m_gdn_conv, m_gdn_a_log, m_gdn_dt_bias, m_gdn_o_norm, m_mem_w_kv, m_w_out, m_ln1_g, m_ln1_b, m_mlp_w1, m_mlp_w2, m_ln2_g, m_ln2_b, v_mla_w_in, v_mla_q_norm, v_mla_w_uq, v_mla_kv_norm, v_mla_w_ukv, v_gdn_w_in, v_gdn_conv, v_gdn_a_log, v_gdn_dt_bias, v_gdn_o_norm, v_mem_w_kv, v_w_out, v_ln1_g, v_ln1_b, v_mlp_w1, v_mlp_w2, v_ln2_g, v_ln2_b):
    args = dict(locals())
    w_loc = {n: args[n] for n, _ in BIG}
    m_loc = {n: args["m_" + n] for n, _ in BIG}
    v_loc = {n: args["v_" + n] for n, _ in BIG}
    small = {n: args[n] for n in SMALL}
    axis_of = dict(BIG)
    mm_names = [n for n, _ in BIG if n != "gdn_conv"]

    names = [n for n, _ in BIG]
    got = _gather_two_level([w_loc[n].astype(COMM_DTYPE) for n in mm_names] + [w_loc["gdn_conv"]], "gather_weights")
    full = {n: _unshard(p, axis_of[n]) for n, p in zip(names, got)}
    conv_full = full.pop("gdn_conv")

    wts = {n: [full[n][l] for l in range(full[n].shape[0])] for n in mm_names}
    wts["mla_w_in"] = [_mla_in_to_kernel(w) for w in wts["mla_w_in"]]
    wts["mla_w_uq"] = [_uq_to_kernel(w) for w in wts["mla_w_uq"]]
    wts["gdn_w_in"] = [_gdn_in_to_kernel(w) for w in wts["gdn_w_in"]]
    small_in = dict(small, gdn_conv=conv_full)

    loss_row, grad_x, gw, gs = _local_step(x[0], mem[0], positions[0], loss_target[0], wts, small_in)
    loss = lax.psum(loss_row[0, 0], ("x", "y", "c"))

    gw["mla_w_in"] = [_mla_in_from_kernel(g) for g in gw["mla_w_in"]]
    gw["mla_w_uq"] = [_uq_from_kernel(g) for g in gw["mla_w_uq"]]
    gw["gdn_w_in"] = [_gdn_in_from_kernel(g) for g in gw["gdn_w_in"]]
    gfull = {n: jnp.stack(gw[n]) for n in mm_names}
    gfull["gdn_conv"] = gs.pop("gdn_conv").astype(COMM_DTYPE)
    dims = {n: (w_loc[n].shape[0] * w_loc[n].shape[1], w_loc[n].shape[2]) for n in names}
    g_sent = [_shard(gfull[n], axis_of[n]).reshape((N_DEV,) + dims[n]) for n in names]
    g_sibling = _pair_exchange(g_sent, "grads_pair")
    my_c = lax.axis_index("c").astype(jnp.int32).reshape(1)
    g_pairs = [_pair_sum(a, b, my_c, f"pair_sum_{n}") for n, a, b in zip(names, g_sent, g_sibling)]
    g_got = _chip_exchange(g_pairs, "grads_chips")
    big_out = [{}, {}, {}, {}]
    for n, parts in zip(names, g_got):
        shp = w_loc[n].shape
        rows, cols = dims[n]
        res = _adamw(parts, w_loc[n].reshape(rows, cols), m_loc[n].reshape(rows, cols),
                     v_loc[n].reshape(rows, cols), f"adamw_{n}")
        for kind in range(4):
            big_out[kind][n] = res[kind].reshape(shp)

    s_sent = _pack([gs[n] for n in SMALL], 0, LANES, 8)
    s_got = _exchange([s_sent], "gather_small_grads", gather=True)[0]
    small_out = [dict(zip(SMALL, _unpack(o, 0, [small[n].shape for n in SMALL])))
                 for o in _adamw(s_got, _pack([small[n] for n in SMALL], 0, LANES, 8),
                                 _pack([args["m_" + n] for n in SMALL], 0, LANES, 8),
                                 _pack([args["v_" + n] for n in SMALL], 0, LANES, 8), "adamw_small")]

    order = ["mla_w_in", "mla_q_norm", "mla_w_uq", "mla_kv_norm", "mla_w_ukv", "gdn_w_in", "gdn_conv", "gdn_a_log",
             "gdn_dt_bias", "gdn_o_norm", "mem_w_kv", "w_out", "ln1_g", "ln1_b", "mlp_w1", "mlp_w2", "ln2_g", "ln2_b"]
    outs = [loss, grad_x[None]]
    for kind in range(4):
        for n in order:
            outs.append(big_out[kind][n] if n in axis_of else small_out[kind][n])
    return tuple(outs)
```

```python
import functools

import jax
import jax.numpy as jnp
from jax import lax
from jax.experimental import pallas as pl
from jax.experimental.pallas import tpu as pltpu

F32 = jnp.float32
MXU_DTYPE = jnp.bfloat16
COMM_DTYPE = jnp.bfloat16

N_DEV = 8
D_MODEL = 2048
DEPTH = 4
HEAD_DIM = 128
N_MIX_HEADS = 12
N_MEM_HEADS = 4
MIX_WIDTH = N_MIX_HEADS * HEAD_DIM
MEM_WIDTH = N_MEM_HEADS * HEAD_DIM
Q_LORA = 512
KV_LORA = 512
QK_NOPE = 128
QK_ROPE = 64
ROPE_THETA = 10000.0
CONV_WIDTH = 4
CHUNK = 64
D_FF = 4 * D_MODEL
ALPHA = (2 * DEPTH) ** 0.25
LN_EPS = 1e-5
RMS_EPS = 1e-6
MLA_IN = Q_LORA + KV_LORA + QK_ROPE + MEM_WIDTH
GDN_IN = 4 * MIX_WIDTH + 2 * N_MIX_HEADS + MEM_WIDTH
MLA_IN_P = 1792
GDN_IN_P = 6912
Q_HEAD_P = 256
ATT_SCALE = (QK_NOPE + QK_ROPE) ** -0.5
ADAM_LR, ADAM_B1, ADAM_B2, ADAM_EPS, ADAM_WD, ADAM_STEP = 0.001, 0.9, 0.999, 1e-08, 0.01, 10
LANES = 128
VMEM_LIMIT = 56 * 1024 * 1024

NN = (((1,), (0,)), ((), ()))
NT = (((1,), (1,)), ((), ()))
TN = (((0,), (0,)), ((), ()))
HI = lax.Precision.HIGHEST


def _dot(a, b, dims, hi=False):
    if hi:
        return lax.dot_general(a, b, dims, precision=HI, preferred_element_type=F32)
    return lax.dot_general(a.astype(MXU_DTYPE), b.astype(MXU_DTYPE), dims, preferred_element_type=F32)


@functools.partial(jax.custom_vjp, nondiff_argnums=(2,))
def _nn_d(a, b, hi):
    return _dot(a, b, NN, hi)


@functools.partial(jax.custom_vjp, nondiff_argnums=(2,))
def _nt_d(a, b, hi):
    return _dot(a, b, NT, hi)


@functools.partial(jax.custom_vjp, nondiff_argnums=(2,))
def _tn_d(a, b, hi):
    return _dot(a, b, TN, hi)


_nn_d.defvjp(lambda a, b, hi: (_dot(a, b, NN, hi), (a, b)),
             lambda hi, r, g: (_nt_d(g, r[1], hi), _tn_d(r[0], g, hi)))
_nt_d.defvjp(lambda a, b, hi: (_dot(a, b, NT, hi), (a, b)),
             lambda hi, r, g: (_nn_d(g, r[1], hi), _tn_d(g, r[0], hi)))
_tn_d.defvjp(lambda a, b, hi: (_dot(a, b, TN, hi), (a, b)),
             lambda hi, r, g: (_nt_d(r[1], g, hi), _nn_d(r[0], g, hi)))


class _RawOps:
    nn = staticmethod(lambda a, b, hi=False: _dot(a, b, NN, hi))
    nt = staticmethod(lambda a, b, hi=False: _dot(a, b, NT, hi))
    tn = staticmethod(lambda a, b, hi=False: _dot(a, b, TN, hi))


class _DiffOps:
    nn = staticmethod(lambda a, b, hi=False: _nn_d(a, b, hi))
    nt = staticmethod(lambda a, b, hi=False: _nt_d(a, b, hi))
    tn = staticmethod(lambda a, b, hi=False: _tn_d(a, b, hi))


def _pick(dim, cands=(512, 384, 256, 128)):
    for c in cands:
        if dim % c == 0:
            return c
    return dim


def _pc(body, *, grid, in_specs, out_specs, out_shape, name, scratch=()):
    return pl.pallas_call(
        body, grid=grid, in_specs=in_specs, out_specs=out_specs, out_shape=out_shape,
        scratch_shapes=list(scratch), name=name,
        compiler_params=pltpu.CompilerParams(dimension_semantics=("arbitrary",) * len(grid),
                                             vmem_limit_bytes=VMEM_LIMIT))


def _rows(ts, w, cb=0):
    return pl.BlockSpec((ts, w), lambda i, *_: (i, cb))


def _whole(shape):
    return pl.BlockSpec(shape, lambda *_: (0,) * len(shape))


def _sds(shape, dtype=F32):
    return jax.ShapeDtypeStruct(shape, dtype)


def _mm(a, b, *, name, ta=False, tb=False, out_dtype=F32, a_fn=None, epi=None, e=None):
    m, k = (a.shape[1], a.shape[0]) if ta else a.shape
    n = b.shape[0] if tb else b.shape[1]
    assert k == (b.shape[1] if tb else b.shape[0]), (a.shape, b.shape, ta, tb)
    wide = (1024, 768, 512, 384, 256, 128)
    if k <= 2048 and not ta:
        tm, tn, tk = _pick(m, (512, 256, 128)), _pick(n, wide), k
    else:
        tm, tn, tk = _pick(m, wide), _pick(n, wide), _pick(k, wide)
    nk = k // tk
    dims = (((0 if ta else 1,), (1 if tb else 0,)), ((), ()))

    def body(*refs):
        if e is None:
            a_ref, b_ref, o_ref, acc = refs
        else:
            a_ref, b_ref, e_ref, o_ref, acc = refs
        kk = pl.program_id(2)

        @pl.when(kk == 0)
        def _():
            acc[...] = jnp.zeros_like(acc)

        av = a_ref[...]
        if a_fn is not None:
            av = a_fn(av.astype(F32))
        acc[...] += lax.dot_general(av.astype(MXU_DTYPE), b_ref[...].astype(MXU_DTYPE), dims,
                                    preferred_element_type=F32)

        @pl.when(kk == nk - 1)
        def _():
            r = acc[...]
            if epi is not None:
                r = epi(r, e_ref[...].astype(F32))
            o_ref[...] = r.astype(out_dtype)

    a_spec = pl.BlockSpec((tk, tm), lambda i, j, kk: (kk, i)) if ta else pl.BlockSpec((tm, tk), lambda i, j, kk: (i, kk))
    b_spec = pl.BlockSpec((tn, tk), lambda i, j, kk: (j, kk)) if tb else pl.BlockSpec((tk, tn), lambda i, j, kk: (kk, j))
    o_spec = pl.BlockSpec((tm, tn), lambda i, j, kk: (i, j))
    ins, specs = [a, b], [a_spec, b_spec]
    if e is not None:
        assert e.shape == (m, n)
        ins.append(e)
        specs.append(o_spec)
    return _pc(body, grid=(m // tm, n // tn, nk), in_specs=specs, out_specs=o_spec,
               out_shape=_sds((m, n), out_dtype), name=name, scratch=[pltpu.VMEM((tm, tn), F32)])(*ins)


def _relu2(v):
    r = jnp.maximum(v, 0.0)
    return r * r


def _relu2_bwd(acc, h1):
    return acc * (2.0 * jnp.maximum(h1, 0.0))


def _add_alpha(acc, dr):
    return acc + ALPHA * dr


def _ln(r, g, b):
    mu = jnp.mean(r, -1, keepdims=True)
    var = jnp.mean(jnp.square(r - mu), -1, keepdims=True)
    return (r - mu) * lax.rsqrt(var + LN_EPS) * g + b


def _ln_fwd(x, y, g, b, name):
    s, d = x.shape
    ts = _pick(s, (256, 128))

    def body(x_ref, y_ref, g_ref, b_ref, o_ref):
        o_ref[...] = _ln(ALPHA * x_ref[...] + y_ref[...], g_ref[...], b_ref[...])

    return _pc(body, grid=(s // ts,), in_specs=[_rows(ts, d), _rows(ts, d), _whole((1, d)), _whole((1, d))],
               out_specs=_rows(ts, d), out_shape=_sds((s, d)), name=name)(x, y, g, b)


def _ln_bwd(x, y, g, b, dout, name):
    s, d = x.shape
    ts = _pick(s, (256, 128))

    def body(x_ref, y_ref, g_ref, b_ref, do_ref, dr_ref, dg_ref, db_ref):
        @pl.when(pl.program_id(0) == 0)
        def _():
            dg_ref[...] = jnp.zeros_like(dg_ref)
            db_ref[...] = jnp.zeros_like(db_ref)

        r = ALPHA * x_ref[...] + y_ref[...]
        _, vjp = jax.vjp(_ln, r, g_ref[...], b_ref[...])
        dr, dg, db = vjp(do_ref[...])
        dr_ref[...] = dr
        dg_ref[...] += dg
        db_ref[...] += db

    return _pc(body, grid=(s // ts,),
               in_specs=[_rows(ts, d), _rows(ts, d), _whole((1, d)), _whole((1, d)), _rows(ts, d)],
               out_specs=[_rows(ts, d), _whole((1, d)), _whole((1, d))],
               out_shape=[_sds((s, d)), _sds((1, d)), _sds((1, d))], name=name)(x, y, g, b, dout)


def _loss_and_grad(y, target, name):
    s, d = y.shape
    ts = _pick(s, (256, 128))

    def body(y_ref, t_ref, l_ref, dy_ref):
        @pl.when(pl.program_id(0) == 0)
        def _():
            l_ref[...] = jnp.zeros_like(l_ref)

        diff = y_ref[...] - t_ref[...]
        per_tok = jnp.mean(jnp.square(diff), -1, keepdims=True)
        l_ref[...] += 0.5 * jnp.sum(per_tok, 0, keepdims=True) * jnp.ones((1, LANES), F32)
        dy_ref[...] = diff * (1.0 / d)

    return _pc(body, grid=(s // ts,), in_specs=[_rows(ts, d), _rows(ts, d)],
               out_specs=[_whole((1, LANES)), _rows(ts, d)],
               out_shape=[_sds((1, LANES)), _sds((s, d))], name=name)(y, target)


def _rope(blk, cs, sn):
    return blk * cs + pltpu.roll(blk, 64, 1) * sn


def _rope_t(dblk, cs, sn):
    return dblk * cs + pltpu.roll(dblk * sn, 64, 1)


def _rms(v, g):
    return v * lax.rsqrt(jnp.mean(v * v, -1, keepdims=True) + RMS_EPS) * g


def _mla_prep_fwd(h, qn, kvn, cs, sn, name):
    s = h.shape[0]
    ts = _pick(s, (512, 256, 128))

    def body(cq_ref, ckv_ref, kr_ref, qn_ref, kvn_ref, cs_ref, sn_ref, cqn_ref, ckvn_ref, krr_ref):
        cqn_ref[...] = _rms(cq_ref[...], qn_ref[...])
        ckvn_ref[...] = _rms(ckv_ref[...], kvn_ref[...])
        krr_ref[...] = _rope(kr_ref[...], cs_ref[...], sn_ref[...])

    return _pc(body, grid=(s // ts,),
               in_specs=[_rows(ts, 512, 0), _rows(ts, 512, 1), _rows(ts, LANES, 12), _whole((1, 512)), _whole((1, 512)),
                         _rows(ts, LANES), _rows(ts, LANES)],
               out_specs=[_rows(ts, 512), _rows(ts, 512), _rows(ts, LANES)],
               out_shape=[_sds((s, 512)), _sds((s, 512)), _sds((s, LANES))], name=name)(h, h, h, qn, kvn, cs, sn)


def _mla_prep_bwd(h, qn, kvn, cs, sn, dcqn, dckvn, dkrr_heads, name):
    s = h.shape[0]
    ts = _pick(s, (512, 256, 128))

    def body(cq_ref, ckv_ref, qn_ref, kvn_ref, cs_ref, sn_ref, dcqn_ref, dckvn_ref, dkrr_ref,
             dcq_ref, dckv_ref, dkr_ref, dqn_ref, dkvn_ref):
        @pl.when(pl.program_id(0) == 0)
        def _():
            dqn_ref[...] = jnp.zeros_like(dqn_ref)
            dkvn_ref[...] = jnp.zeros_like(dkvn_ref)

        _, vjp = jax.vjp(_rms, cq_ref[...], qn_ref[...])
        dcq, dqn = vjp(dcqn_ref[...])
        dcq_ref[...] = dcq
        dqn_ref[...] += dqn
        _, vjp = jax.vjp(_rms, ckv_ref[...], kvn_ref[...])
        dckv, dkvn = vjp(dckvn_ref[...])
        dckv_ref[...] = dckv
        dkvn_ref[...] += dkvn
        dkrr = dkrr_ref[0]
        for hh in range(1, N_MIX_HEADS):
            dkrr = dkrr + dkrr_ref[hh]
        dkr_ref[...] = _rope_t(dkrr, cs_ref[...], sn_ref[...])

    heads3 = pl.BlockSpec((N_MIX_HEADS, ts, LANES), lambda i: (0, i, 0))
    return _pc(body, grid=(s // ts,),
               in_specs=[_rows(ts, 512, 0), _rows(ts, 512, 1), _whole((1, 512)), _whole((1, 512)),
                         _rows(ts, LANES), _rows(ts, LANES), _rows(ts, 512), _rows(ts, 512), heads3],
               out_specs=[_rows(ts, 512), _rows(ts, 512), _rows(ts, LANES), _whole((1, 512)), _whole((1, 512))],
               out_shape=[_sds((s, 512)), _sds((s, 512)), _sds((s, LANES)), _sds((1, 512)), _sds((1, 512))],
               name=name)(h, h, qn, kvn, cs, sn, dcqn, dckvn, dkrr_heads)


def _att_tiles(s):
    t = _pick(s, (512, 256, 128))
    return t, s // t


ATT_HEADS_PER_STEP = 2


def _tri_pairs(nb, k_major):
    pairs = [(i, j) for j in range(nb) for i in range(j, nb)] if k_major else [(i, j) for i in range(nb) for j in range(i + 1)]
    return jnp.asarray([p[0] for p in pairs], jnp.int32), jnp.asarray([p[1] for p in pairs], jnp.int32)


ATT_STRIP = 64


def _scaled_q(q_ref, c0, cs_ref, sn_ref, dst, hh):
    dst[hh, :, :LANES] = (q_ref[:, c0:c0 + LANES] * ATT_SCALE).astype(MXU_DTYPE)
    dst[hh, :, LANES:] = (_rope(q_ref[:, c0 + LANES:c0 + Q_HEAD_P], cs_ref[...], sn_ref[...]) * ATT_SCALE).astype(MXU_DTYPE)


def _cat_k(kv_ref, c0, kr, dst, hh):
    dst[hh, :, :LANES] = kv_ref[:, c0:c0 + LANES].astype(MXU_DTYPE)
    dst[hh, :, LANES:] = kr


def _tril_rows(r, n, t):
    return lax.broadcasted_iota(jnp.int32, (n, t), 1) <= r + lax.broadcasted_iota(jnp.int32, (n, t), 0)


def _pc_pairs(body, pairs, *, n_groups, in_specs, out_specs, out_shape, name, scratch, args):
    return pl.pallas_call(
        body, out_shape=out_shape, name=name,
        grid_spec=pltpu.PrefetchScalarGridSpec(num_scalar_prefetch=2, grid=(n_groups, pairs[0].shape[0]), in_specs=in_specs,
                                               out_specs=out_specs, scratch_shapes=list(scratch)),
        compiler_params=pltpu.CompilerParams(dimension_semantics=("arbitrary", "arbitrary"), vmem_limit_bytes=VMEM_LIMIT),
    )(*pairs, *args)


def _att_specs(t):
    qrow = lambda w: pl.BlockSpec((t, w), lambda h, p, qi, kj: (qi[p], h))
    krow = lambda w: pl.BlockSpec((t, w), lambda h, p, qi, kj: (kj[p], h))
    qtab = pl.BlockSpec((t, LANES), lambda h, p, qi, kj: (qi[p], 0))
    ktab = pl.BlockSpec((t, LANES), lambda h, p, qi, kj: (kj[p], 0))
    return qrow, krow, qtab, ktab


def _flash_fwd(q, kv, krr, cs, sn, name):
    s = q.shape[0]
    t, nb = _att_tiles(s)
    ah = ATT_HEADS_PER_STEP

    def body(qi_ref, kj_ref, q_ref, cs_ref, sn_ref, kv_ref, kr_ref, o_ref, lse_ref, q_s, m_s, l_s, acc_s, s_scr, p_scr):
        i, j = qi_ref[pl.program_id(1)], kj_ref[pl.program_id(1)]

        @pl.when(j == 0)
        def _():
            for hh in range(ah):
                _scaled_q(q_ref, hh * Q_HEAD_P, cs_ref, sn_ref, q_s, hh)
            m_s[...] = jnp.full_like(m_s, -jnp.inf)
            l_s[...] = jnp.zeros_like(l_s)
            acc_s[...] = jnp.zeros_like(acc_s)

        def update(masked):
            kr = kr_ref[...].astype(MXU_DTYPE)
            for hh in range(ah):
                c0 = hh * Q_HEAD_P
                s_scr[hh] = _dot(q_s[hh, :, :LANES], kv_ref[:, c0:c0 + LANES], NT) + _dot(q_s[hh, :, LANES:], kr, NT)
                for r in range(0, t, ATT_STRIP):
                    rows = slice(r, r + ATT_STRIP)
                    sc = s_scr[hh, rows, :]
                    if masked:
                        sc = jnp.where(_tril_rows(r, ATT_STRIP, t), sc, -jnp.inf)
                    m_old = m_s[hh, rows, :]
                    m_new = jnp.maximum(m_old, jnp.max(sc, -1, keepdims=True))
                    p = jnp.exp(sc - m_new[:, :1])
                    corr = jnp.exp(m_old - m_new)
                    l_s[hh, rows, :] = corr * l_s[hh, rows, :] + jnp.sum(p, -1, keepdims=True)
                    acc_s[hh, rows, :] = corr * acc_s[hh, rows, :]
                    m_s[hh, rows, :] = m_new
                    p_scr[hh, rows, :] = p.astype(MXU_DTYPE)
                acc_s[hh] += _dot(p_scr[hh], kv_ref[:, c0 + LANES:c0 + Q_HEAD_P], NN)

        @pl.when(j < i)
        def _():
            update(False)

        @pl.when(j == i)
        def _():
            update(True)
            for hh in range(ah):
                sl = slice(hh * LANES, (hh + 1) * LANES)
                o_ref[:, sl] = acc_s[hh] / l_s[hh]
                lse_ref[:, sl] = m_s[hh] + jnp.log(l_s[hh])

    qrow, krow, qtab, ktab = _att_specs(t)
    return _pc_pairs(body, _tri_pairs(nb, False), n_groups=N_MIX_HEADS // ah,
                     in_specs=[qrow(ah * Q_HEAD_P), qtab, qtab, krow(ah * Q_HEAD_P), ktab],
                     out_specs=[qrow(ah * LANES), qrow(ah * LANES)],
                     out_shape=[_sds((s, MIX_WIDTH)), _sds((s, MIX_WIDTH))], name=name,
                     scratch=[pltpu.VMEM((ah, t, Q_HEAD_P), MXU_DTYPE),
                              pltpu.VMEM((ah, t, LANES), F32), pltpu.VMEM((ah, t, LANES), F32), pltpu.VMEM((ah, t, LANES), F32),
                              pltpu.VMEM((ah, t, t), F32), pltpu.VMEM((ah, t, t), MXU_DTYPE)],
                     args=(q, cs, sn, kv, krr))


def _flash_bwd_q(q, kv, krr, cs, sn, o, lse, dcat, name):
    s = q.shape[0]
    t, nb = _att_tiles(s)
    ah = ATT_HEADS_PER_STEP

    def body(qi_ref, kj_ref, q_ref, cs_ref, sn_ref, o_ref, lse_ref, do_ref, kv_ref, kr_ref, dq_ref,
             q_s, k_s, dl_s, aq_s, s_scr, dp_scr, ds_scr):
        i, j = qi_ref[pl.program_id(1)], kj_ref[pl.program_id(1)]

        @pl.when(j == 0)
        def _():
            for hh in range(ah):
                sl = slice(hh * LANES, (hh + 1) * LANES)
                _scaled_q(q_ref, hh * Q_HEAD_P, cs_ref, sn_ref, q_s, hh)
                dl_s[hh] = jnp.sum(o_ref[:, sl] * do_ref[:, sl], -1, keepdims=True) * jnp.ones((1, LANES), F32)
            aq_s[...] = jnp.zeros_like(aq_s)

        def update(masked):
            kr = kr_ref[...].astype(MXU_DTYPE)
            for hh in range(ah):
                c0 = hh * Q_HEAD_P
                sl = slice(hh * LANES, (hh + 1) * LANES)
                _cat_k(kv_ref, c0, kr, k_s, hh)
                s_scr[hh] = _dot(q_s[hh], k_s[hh], NT)
                dp_scr[hh] = _dot(do_ref[:, sl], kv_ref[:, c0 + LANES:c0 + Q_HEAD_P], NT)
                for r in range(0, t, ATT_STRIP):
                    rows = slice(r, r + ATT_STRIP)
                    p = jnp.exp(s_scr[hh, rows, :] - lse_ref[rows, hh * LANES:hh * LANES + 1])
                    if masked:
                        p = jnp.where(_tril_rows(r, ATT_STRIP, t), p, 0.0)
                    ds_scr[hh, rows, :] = (p * (dp_scr[hh, rows, :] - dl_s[hh, rows, :1])).astype(MXU_DTYPE)
                aq_s[hh] += _dot(ds_scr[hh], k_s[hh], NN)

        @pl.when(j < i)
        def _():
            update(False)

        @pl.when(j == i)
        def _():
            update(True)
            for hh in range(ah):
                c0 = hh * Q_HEAD_P
                dq_ref[:, c0:c0 + LANES] = aq_s[hh, :, :LANES] * ATT_SCALE
                dq_ref[:, c0 + LANES:c0 + Q_HEAD_P] = _rope_t(aq_s[hh, :, LANES:] * ATT_SCALE, cs_ref[...], sn_ref[...])

    qrow, krow, qtab, ktab = _att_specs(t)
    return _pc_pairs(body, _tri_pairs(nb, False), n_groups=N_MIX_HEADS // ah,
                     in_specs=[qrow(ah * Q_HEAD_P), qtab, qtab, qrow(ah * LANES), qrow(ah * LANES), qrow(ah * LANES),
                               krow(ah * Q_HEAD_P), ktab],
                     out_specs=qrow(ah * Q_HEAD_P), out_shape=_sds((s, N_MIX_HEADS * Q_HEAD_P)), name=name,
                     scratch=[pltpu.VMEM((ah, t, Q_HEAD_P), MXU_DTYPE), pltpu.VMEM((ah, t, Q_HEAD_P), MXU_DTYPE),
                              pltpu.VMEM((ah, t, LANES), F32), pltpu.VMEM((ah, t, Q_HEAD_P), F32),
                              pltpu.VMEM((ah, t, t), F32), pltpu.VMEM((ah, t, t), F32), pltpu.VMEM((ah, t, t), MXU_DTYPE)],
                     args=(q, cs, sn, o, lse, dcat, kv, krr))


def _flash_bwd_kv(q, kv, krr, cs, sn, o, lse, dcat, name):
    s = q.shape[0]
    t, nb = _att_tiles(s)
    ah = ATT_HEADS_PER_STEP

    def body(qi_ref, kj_ref, kv_ref, kr_ref, q_ref, cs_ref, sn_ref, o_ref, lse_ref, do_ref, dkv_ref, dkr_ref,
             ak_s, av_s, q_s, k_s, s_scr, dp_scr, p_scr, ds_scr):
        i, j = qi_ref[pl.program_id(1)], kj_ref[pl.program_id(1)]

        @pl.when(i == j)
        def _():
            ak_s[...] = jnp.zeros_like(ak_s)
            av_s[...] = jnp.zeros_like(av_s)

        def update(masked):
            kr = kr_ref[...].astype(MXU_DTYPE)
            for hh in range(ah):
                c0 = hh * Q_HEAD_P
                sl = slice(hh * LANES, (hh + 1) * LANES)
                _scaled_q(q_ref, c0, cs_ref, sn_ref, q_s, hh)
                _cat_k(kv_ref, c0, kr, k_s, hh)
                do = do_ref[:, sl].astype(MXU_DTYPE)
                s_scr[hh] = _dot(q_s[hh], k_s[hh], NT)
                dp_scr[hh] = _dot(do, kv_ref[:, c0 + LANES:c0 + Q_HEAD_P], NT)
                for r in range(0, t, ATT_STRIP):
                    rows = slice(r, r + ATT_STRIP)
                    p = jnp.exp(s_scr[hh, rows, :] - lse_ref[rows, hh * LANES:hh * LANES + 1])
                    if masked:
                        p = jnp.where(_tril_rows(r, ATT_STRIP, t), p, 0.0)
                    dl = jnp.sum(o_ref[rows, sl] * do_ref[rows, sl], -1, keepdims=True)
                    p_scr[hh, rows, :] = p.astype(MXU_DTYPE)
                    ds_scr[hh, rows, :] = (p * (dp_scr[hh, rows, :] - dl)).astype(MXU_DTYPE)
                av_s[hh] += _dot(p_scr[hh], do, TN)
                ak_s[hh] += _dot(ds_scr[hh], q_s[hh], TN)

        @pl.when(i > j)
        def _():
            update(False)

        @pl.when(i == j)
        def _():
            update(True)

        @pl.when(i == nb - 1)
        def _():
            for hh in range(ah):
                c0 = hh * Q_HEAD_P
                dkv_ref[:, c0:c0 + LANES] = ak_s[hh, :, :LANES]
                dkv_ref[:, c0 + LANES:c0 + Q_HEAD_P] = av_s[hh]
                dkr_ref[hh] = ak_s[hh, :, LANES:]

    qrow, krow, qtab, ktab = _att_specs(t)
    return _pc_pairs(body, _tri_pairs(nb, True), n_groups=N_MIX_HEADS // ah,
                     in_specs=[krow(ah * Q_HEAD_P), ktab, qrow(ah * Q_HEAD_P), qtab, qtab,
                               qrow(ah * LANES), qrow(ah * LANES), qrow(ah * LANES)],
                     out_specs=[krow(ah * Q_HEAD_P), pl.BlockSpec((ah, t, LANES), lambda h, p, qi, kj: (h, kj[p], 0))],
                     out_shape=[_sds((s, N_MIX_HEADS * Q_HEAD_P)), _sds((N_MIX_HEADS, s, LANES))], name=name,
                     scratch=[pltpu.VMEM((ah, t, Q_HEAD_P), F32), pltpu.VMEM((ah, t, LANES), F32),
                              pltpu.VMEM((ah, t, Q_HEAD_P), MXU_DTYPE), pltpu.VMEM((ah, t, Q_HEAD_P), MXU_DTYPE),
                              pltpu.VMEM((ah, t, t), F32), pltpu.VMEM((ah, t, t), F32),
                              pltpu.VMEM((ah, t, t), MXU_DTYPE), pltpu.VMEM((ah, t, t), MXU_DTYPE)],
                     args=(kv, krr, q, cs, sn, o, lse, dcat))


def _mem_head(ops, qh, kh, vh):
    sc = ops.nt(qh, kh) * HEAD_DIM ** -0.5
    e = jnp.exp(sc - lax.stop_gradient(jnp.max(sc, -1, keepdims=True)))
    p = e / jnp.sum(e, -1, keepdims=True)
    return ops.nn(p, vh)


def _mem_fwd(h, qcol, mem_kv, name):
    s = h.shape[0]
    m = mem_kv.shape[0]
    ts = _pick(s, (512, 256, 128))

    def body(q_ref, kv_ref, o_ref):
        for hh in range(N_MEM_HEADS):
            sl = slice(hh * HEAD_DIM, (hh + 1) * HEAD_DIM)
            vsl = slice(MEM_WIDTH + hh * HEAD_DIM, MEM_WIDTH + (hh + 1) * HEAD_DIM)
            o_ref[:, sl] = _mem_head(_RawOps, q_ref[:, sl], kv_ref[:, sl], kv_ref[:, vsl])

    return _pc(body, grid=(s // ts,), in_specs=[_rows(ts, MEM_WIDTH, qcol), _whole((m, 2 * MEM_WIDTH))],
               out_specs=_rows(ts, MEM_WIDTH), out_shape=_sds((s, MEM_WIDTH)), name=name)(h, mem_kv)


def _mem_bwd(h, qcol, mem_kv, dcat, name):
    s = h.shape[0]
    m = mem_kv.shape[0]
    ts = _pick(s, (512, 256, 128))

    def body(q_ref, kv_ref, do_ref, dq_ref, dkv_ref):
        @pl.when(pl.program_id(0) == 0)
        def _():
            dkv_ref[...] = jnp.zeros_like(dkv_ref)

        for hh in range(N_MEM_HEADS):
            sl = slice(hh * HEAD_DIM, (hh + 1) * HEAD_DIM)
            vsl = slice(MEM_WIDTH + hh * HEAD_DIM, MEM_WIDTH + (hh + 1) * HEAD_DIM)
            _, vjp = jax.vjp(functools.partial(_mem_head, _DiffOps), q_ref[:, sl], kv_ref[:, sl], kv_ref[:, vsl])
            dq, dk, dv = vjp(do_ref[:, sl])
            dq_ref[:, sl] = dq
            dkv_ref[:, sl] += dk
            dkv_ref[:, vsl] += dv

    return _pc(body, grid=(s // ts,),
               in_specs=[_rows(ts, MEM_WIDTH, qcol), _whole((m, 2 * MEM_WIDTH)), _rows(ts, MEM_WIDTH, 3)],
               out_specs=[_rows(ts, MEM_WIDTH), _whole((m, 2 * MEM_WIDTH))],
               out_shape=[_sds((s, MEM_WIDTH)), _sds((m, 2 * MEM_WIDTH))], name=name)(h, mem_kv, dcat)


CONV_COLS = 3 * MIX_WIDTH
HALO = 8


def _conv_fwd(h, w, name):
    s = h.shape[0]
    ts = _pick(s, (512, 256, 128))
    wc = 512

    def body(x_ref, halo_ref, w_ref, o_ref, ext):
        i = pl.program_id(0)
        ext[pl.ds(0, HALO), :] = jnp.where(i > 0, halo_ref[...], 0.0)
        ext[pl.ds(HALO, ts), :] = x_ref[...]
        acc = w_ref[0:1, :] * ext[pl.ds(HALO - 3, ts), :]
        for j in range(1, CONV_WIDTH):
            acc = acc + w_ref[j:j + 1, :] * ext[pl.ds(HALO - 3 + j, ts), :]
        o_ref[...] = acc

    halo = pl.BlockSpec((HALO, wc), lambda i, c: (jnp.maximum(i * (ts // HALO) - 1, 0), c))
    blk = pl.BlockSpec((ts, wc), lambda i, c: (i, c))
    return _pc(body, grid=(s // ts, CONV_COLS // wc),
               in_specs=[blk, halo, pl.BlockSpec((CONV_WIDTH, wc), lambda i, c: (0, c))],
               out_specs=blk, out_shape=_sds((s, CONV_COLS)), name=name,
               scratch=[pltpu.VMEM((HALO + ts, wc), F32)])(h, h, w)


def _conv_bwd(h, w, dout, name):
    s = h.shape[0]
    ts = _pick(s, (512, 256, 128))
    nt = s // ts
    wc = 512

    def body(x_ref, xhalo_ref, d_ref, dhalo_ref, w_ref, dx_ref, dw_ref, xext, dext):
        i = pl.program_id(1)

        @pl.when(i == 0)
        def _():
            dw_ref[...] = jnp.zeros_like(dw_ref)

        xext[pl.ds(0, HALO), :] = jnp.where(i > 0, xhalo_ref[...], 0.0)
        xext[pl.ds(HALO, ts), :] = x_ref[...]
        dext[pl.ds(0, ts), :] = d_ref[...]
        dext[pl.ds(ts, HALO), :] = jnp.where(i < nt - 1, dhalo_ref[...], 0.0)
        d = d_ref[...]
        acc = w_ref[CONV_WIDTH - 1:CONV_WIDTH, :] * d
        for j in range(CONV_WIDTH - 1):
            acc = acc + w_ref[j:j + 1, :] * dext[pl.ds(3 - j, ts), :]
        dx_ref[...] = acc
        for j in range(CONV_WIDTH):
            dw_ref[j:j + 1, :] += jnp.sum(d * xext[pl.ds(HALO - 3 + j, ts), :], 0, keepdims=True)

    blk = pl.BlockSpec((ts, wc), lambda c, i: (i, c))
    halo_prev = pl.BlockSpec((HALO, wc), lambda c, i: (jnp.maximum(i * (ts // HALO) - 1, 0), c))
    halo_next = pl.BlockSpec((HALO, wc), lambda c, i: (jnp.minimum((i + 1) * (ts // HALO), s // HALO - 1), c))
    wspec = pl.BlockSpec((CONV_WIDTH, wc), lambda c, i: (0, c))
    return _pc(body, grid=(CONV_COLS // wc, nt), in_specs=[blk, halo_prev, blk, halo_next, wspec],
               out_specs=[blk, wspec], out_shape=[_sds((s, CONV_COLS)), _sds((CONV_WIDTH, CONV_COLS))], name=name,
               scratch=[pltpu.VMEM((HALO + ts, wc), F32), pltpu.VMEM((ts + HALO, wc), F32)])(h, h, dout, dout, w)


def _silu(v):
    return v * jax.nn.sigmoid(v)


def _softplus(v):
    return jnp.maximum(v, 0.0) + jnp.log1p(jnp.exp(-jnp.abs(v)))


def _gdn_prep_head(cq, ck, cv, a, b, alog, dtb):
    q = _silu(cq)
    q = q * lax.rsqrt(jnp.sum(q * q, -1, keepdims=True) + 1e-6) * HEAD_DIM ** -0.5
    k = _silu(ck)
    k = k * lax.rsqrt(jnp.sum(k * k, -1, keepdims=True) + 1e-6)
    v = _silu(cv)
    g = -jnp.exp(alog) * _softplus(a + dtb)
    beta = jax.nn.sigmoid(b)
    ones = jnp.ones((1, HEAD_DIM), F32)
    return q, k, v, g * ones, beta * ones


def _onehot_lane(idx):
    return (lax.broadcasted_iota(jnp.int32, (1, LANES), 1) == idx).astype(F32)


def _lane_pick(row, idx):
    return jnp.sum(row * _onehot_lane(idx), -1, keepdims=True)


def _gdn_prep_fwd(conv, h, alog, dtb, name):
    s = h.shape[0]
    ts = _pick(s, (128,))
    w = MIX_WIDTH

    def body(c_ref, ab_ref, alog_ref, dtb_ref, q_ref, k_ref, v_ref, g_ref, b_ref):
        ab = ab_ref[...]
        for hh in range(N_MIX_HEADS):
            sl = slice(hh * HEAD_DIM, (hh + 1) * HEAD_DIM)
            cols = [slice(p * w + hh * HEAD_DIM, p * w + (hh + 1) * HEAD_DIM) for p in range(3)]
            outs = _gdn_prep_head(c_ref[:, cols[0]], c_ref[:, cols[1]], c_ref[:, cols[2]], _lane_pick(ab, hh),
                                  _lane_pick(ab, N_MIX_HEADS + hh), _lane_pick(alog_ref[...], hh),
                                  _lane_pick(dtb_ref[...], hh))
            for ref, val in zip((q_ref, k_ref, v_ref, g_ref, b_ref), outs):
                ref[:, sl] = val

    return _pc(body, grid=(s // ts,),
               in_specs=[_rows(ts, 3 * w), _rows(ts, LANES, (4 * w + MEM_WIDTH) // LANES), _whole((1, LANES)), _whole((1, LANES))],
               out_specs=[_rows(ts, w)] * 5, out_shape=[_sds((s, w))] * 5, name=name)(conv, h, alog, dtb)


def _gdn_prep_bwd(conv, h, alog, dtb, dq, dk, dv, dg, db, name):
    s = h.shape[0]
    ts = _pick(s, (128,))
    w = MIX_WIDTH

    def body(c_ref, ab_ref, alog_ref, dtb_ref, dq_ref, dk_ref, dv_ref, dg_ref, db_ref,
             dc_ref, dab_ref, dalog_ref, ddtb_ref):
        @pl.when(pl.program_id(0) == 0)
        def _():
            dalog_ref[...] = jnp.zeros_like(dalog_ref)
            ddtb_ref[...] = jnp.zeros_like(ddtb_ref)

        ab = ab_ref[...]
        dab = jnp.zeros_like(ab)
        dalog = jnp.zeros((1, LANES), F32)
        ddtb = jnp.zeros((1, LANES), F32)
        for hh in range(N_MIX_HEADS):
            sl = slice(hh * HEAD_DIM, (hh + 1) * HEAD_DIM)
            cols = [slice(p * w + hh * HEAD_DIM, p * w + (hh + 1) * HEAD_DIM) for p in range(3)]
            _, vjp = jax.vjp(_gdn_prep_head, c_ref[:, cols[0]], c_ref[:, cols[1]], c_ref[:, cols[2]], _lane_pick(ab, hh),
                             _lane_pick(ab, N_MIX_HEADS + hh), _lane_pick(alog_ref[...], hh), _lane_pick(dtb_ref[...], hh))
            dcq, dck, dcv, da, dbb, dal, ddt = vjp((dq_ref[:, sl], dk_ref[:, sl], dv_ref[:, sl], dg_ref[:, sl], db_ref[:, sl]))
            dc_ref[:, cols[0]] = dcq
            dc_ref[:, cols[1]] = dck
            dc_ref[:, cols[2]] = dcv
            dab = dab + da * _onehot_lane(hh) + dbb * _onehot_lane(N_MIX_HEADS + hh)
            dalog = dalog + dal * _onehot_lane(hh)
            ddtb = ddtb + ddt * _onehot_lane(hh)
        dab_ref[...] = dab
        dalog_ref[...] += dalog
        ddtb_ref[...] += ddtb

    return _pc(body, grid=(s // ts,),
               in_specs=[_rows(ts, 3 * w), _rows(ts, LANES, (4 * w + MEM_WIDTH) // LANES),
                         _whole((1, LANES)), _whole((1, LANES))] + [_rows(ts, w)] * 5,
               out_specs=[_rows(ts, 3 * w), _rows(ts, LANES), _whole((1, LANES)), _whole((1, LANES))],
               out_shape=[_sds((s, 3 * w)), _sds((s, LANES)), _sds((1, LANES)), _sds((1, LANES))],
               name=name)(conv, h, alog, dtb, dq, dk, dv, dg, db)


NNB = (((2,), (1,)), ((0,), (0,)))
NTB = (((2,), (2,)), ((0,), (0,)))
TNB = (((1,), (1,)), ((0,), (0,)))


def _dg(a, b, dims):
    return lax.dot_general(a, b, dims, preferred_element_type=F32)


def _dotb(a, b, dims, mode):
    a1 = a.astype(MXU_DTYPE)
    b1 = b.astype(MXU_DTYPE)
    if mode == 1:
        return _dg(a1, b1, dims)
    rb = b - b1.astype(F32)
    b2 = rb.astype(MXU_DTYPE)
    if mode == 3:
        a2 = (a - a1.astype(F32)).astype(MXU_DTYPE)
        return _dg(a1, b1, dims) + (_dg(a1, b2, dims) + _dg(a2, b1, dims))
    b3 = (rb - b2.astype(F32)).astype(MXU_DTYPE)
    return _dg(a1, b1, dims) + (_dg(a1, b2, dims) + _dg(a1, b3, dims))


@functools.partial(jax.custom_vjp, nondiff_argnums=(2,))
def _nnb(a, b, mode):
    return _dotb(a, b, NNB, mode)


@functools.partial(jax.custom_vjp, nondiff_argnums=(2,))
def _ntb(a, b, mode):
    return _dotb(a, b, NTB, mode)


@functools.partial(jax.custom_vjp, nondiff_argnums=(2,))
def _tnb(a, b, mode):
    return _dotb(a, b, TNB, mode)


_nnb.defvjp(lambda a, b, mode: (_dotb(a, b, NNB, mode), (a, b)),
            lambda mode, r, g: (_ntb(g, r[1], mode), _tnb(r[0], g, mode)))
_ntb.defvjp(lambda a, b, mode: (_dotb(a, b, NTB, mode), (a, b)),
            lambda mode, r, g: (_nnb(g, r[1], mode), _tnb(g, r[0], mode)))
_tnb.defvjp(lambda a, b, mode: (_dotb(a, b, TNB, mode), (a, b)),
            lambda mode, r, g: (_ntb(r[1], g, mode), _nnb(r[0], g, mode)))


def _trilb(h):
    t = lax.broadcasted_iota(jnp.int32, (CHUNK, CHUNK), 0) >= lax.broadcasted_iota(jnp.int32, (CHUNK, CHUNK), 1)
    return jnp.broadcast_to(t.astype(F32)[None], (h, CHUNK, CHUNK))


def _cumsum_rows_raw(gb):
    return _dotb(_trilb(gb.shape[0]), gb, NNB, "lhs")


_cumsum_rows = jax.custom_vjp(_cumsum_rows_raw)
_cumsum_rows.defvjp(lambda gb: (_cumsum_rows_raw(gb), None),
                    lambda _, g: (_dotb(_trilb(g.shape[0]), g, TNB, "lhs"),))


def _row_col_raw(gc):
    return gc[:, :, :CHUNK], jnp.swapaxes(gc, 1, 2)[:, :CHUNK, :]


def _row_col_bwd(_, g):
    part = g[0] + jnp.swapaxes(g[1], 1, 2)
    return (jnp.concatenate([part, jnp.zeros_like(part)], -1),)


_row_col = jax.custom_vjp(_row_col_raw)
_row_col.defvjp(lambda gc: (_row_col_raw(gc), None), _row_col_bwd)


def _gdn_chunk(diff, state, q, k, v, gb, bb):
    if diff:
        nn, nt, tn, cumsum, row_col = _nnb, _ntb, _tnb, _cumsum_rows, _row_col
    else:
        nn = lambda a, b, m: _dotb(a, b, NNB, m)
        nt = lambda a, b, m: _dotb(a, b, NTB, m)
        tn = lambda a, b, m: _dotb(a, b, TNB, m)
        cumsum, row_col = _cumsum_rows_raw, _row_col_raw
    c = CHUNK
    row = lax.broadcasted_iota(jnp.int32, (1, c, c), 1)
    col = lax.broadcasted_iota(jnp.int32, (1, c, c), 2)
    tril = row >= col
    strict = row > col
    eye = (row == col).astype(F32)
    gc = cumsum(gb)
    gci, gcj = row_col(gc)
    decay = jnp.where(tril, jnp.exp(jnp.where(tril, gci - gcj, 0.0)), 0.0)
    kb = k * bb
    low = jnp.where(strict, nt(kb, k, 1) * decay, 0.0)
    inv = eye - low
    pw = low
    for _ in range(5):
        pw = nn(pw, pw, 3)
        inv = inv + nn(inv, pw, 3)
    ge = jnp.exp(gc)
    u = nn(inv, v * bb, 3)
    w = nn(inv, kb * ge, 3)
    a_qk = jnp.where(tril, nt(q, k, 1) * decay, 0.0)
    g_tot = jnp.sum(gb, 1, keepdims=True)
    k_dec = k * jnp.exp(g_tot - gc)
    v_new = u - nn(w, state, 1)
    o = nn(q * ge, state, 1) + nn(a_qk, v_new, 1)
    new_state = state * jnp.exp(g_tot) + tn(k_dec, v_new, 1)
    return new_state, o


def _stack_heads(ref, hp):
    return jnp.stack([ref[:, hh * HEAD_DIM:(hh + 1) * HEAD_DIM] for hh in range(hp)])


GDN_HEADS_FWD = 12
GDN_HEADS_BWD = 12


def _gdn_core_fwd(q, k, v, gb, bb, name):
    s = q.shape[0]
    nc = s // CHUNK
    hp = GDN_HEADS_FWD
    w = hp * HEAD_DIM

    def body(q_ref, k_ref, v_ref, g_ref, b_ref, o_ref, st_ref, state):
        @pl.when(pl.program_id(1) == 0)
        def _():
            state[...] = jnp.zeros_like(state)

        st = state[...]
        st_ref[0] = st
        new_state, o = _gdn_chunk(False, st, *(_stack_heads(r, hp) for r in (q_ref, k_ref, v_ref, g_ref, b_ref)))
        state[...] = new_state
        for hh in range(hp):
            o_ref[:, hh * HEAD_DIM:(hh + 1) * HEAD_DIM] = o[hh]

    blk = pl.BlockSpec((CHUNK, w), lambda hg, c: (c, hg))
    return _pc(body, grid=(N_MIX_HEADS // hp, nc), in_specs=[blk] * 5,
               out_specs=[blk, pl.BlockSpec((1, hp, HEAD_DIM, HEAD_DIM), lambda hg, c: (c, hg, 0, 0))],
               out_shape=[_sds((s, MIX_WIDTH)), _sds((nc, N_MIX_HEADS, HEAD_DIM, HEAD_DIM))], name=name,
               scratch=[pltpu.VMEM((hp, HEAD_DIM, HEAD_DIM), F32)])(q, k, v, gb, bb)


def _gdn_core_bwd(q, k, v, gb, bb, states, do, name):
    s = q.shape[0]
    nc = s // CHUNK
    hp = GDN_HEADS_BWD
    w = hp * HEAD_DIM

    def body(q_ref, k_ref, v_ref, g_ref, b_ref, st_ref, do_ref, dq_ref, dk_ref, dv_ref, dg_ref, db_ref, dstate):
        @pl.when(pl.program_id(1) == 0)
        def _():
            dstate[...] = jnp.zeros_like(dstate)

        _, vjp = jax.vjp(functools.partial(_gdn_chunk, True), st_ref[0],
                         *(_stack_heads(r, hp) for r in (q_ref, k_ref, v_ref, g_ref, b_ref)))
        grads = vjp((dstate[...], _stack_heads(do_ref, hp)))
        dstate[...] = grads[0]
        for ref, val in zip((dq_ref, dk_ref, dv_ref, dg_ref, db_ref), grads[1:]):
            for hh in range(hp):
                ref[:, hh * HEAD_DIM:(hh + 1) * HEAD_DIM] = val[hh]

    blk = pl.BlockSpec((CHUNK, w), lambda hg, c: (nc - 1 - c, hg))
    return _pc(body, grid=(N_MIX_HEADS // hp, nc),
               in_specs=[blk] * 5 + [pl.BlockSpec((1, hp, HEAD_DIM, HEAD_DIM), lambda hg, c: (nc - 1 - c, hg, 0, 0)), blk],
               out_specs=[blk] * 5, out_shape=[_sds((s, MIX_WIDTH))] * 5, name=name,
               scratch=[pltpu.VMEM((hp, HEAD_DIM, HEAD_DIM), F32)])(q, k, v, gb, bb, states, do)


def _gdn_out_head(o, z, g):
    return _rms(o, g) * _silu(z)


def _gdn_out_fwd(o, h, onorm, name):
    s = o.shape[0]
    ts = _pick(s, (256, 128))

    def body(o_ref, z_ref, g_ref, y_ref):
        for hh in range(N_MIX_HEADS):
            sl = slice(hh * HEAD_DIM, (hh + 1) * HEAD_DIM)
            y_ref[:, sl] = _gdn_out_head(o_ref[:, sl], z_ref[:, sl], g_ref[...])

    return _pc(body, grid=(s // ts,), in_specs=[_rows(ts, MIX_WIDTH), _rows(ts, MIX_WIDTH, 3), _whole((1, HEAD_DIM))],
               out_specs=_rows(ts, MIX_WIDTH), out_shape=_sds((s, MIX_WIDTH)), name=name)(o, h, onorm)


def _gdn_out_bwd(o, h, onorm, dcat, name):
    s = o.shape[0]
    ts = _pick(s, (256, 128))

    def body(o_ref, z_ref, g_ref, dy_ref, do_ref, dz_ref, dg_ref):
        @pl.when(pl.program_id(0) == 0)
        def _():
            dg_ref[...] = jnp.zeros_like(dg_ref)

        dgs = jnp.zeros((1, HEAD_DIM), F32)
        for hh in range(N_MIX_HEADS):
            sl = slice(hh * HEAD_DIM, (hh + 1) * HEAD_DIM)
            _, vjp = jax.vjp(_gdn_out_head, o_ref[:, sl], z_ref[:, sl], g_ref[...])
            do, dz, dg = vjp(dy_ref[:, sl])
            do_ref[:, sl] = do
            dz_ref[:, sl] = dz
            dgs = dgs + dg
        dg_ref[...] += dgs

    return _pc(body, grid=(s // ts,),
               in_specs=[_rows(ts, MIX_WIDTH), _rows(ts, MIX_WIDTH, 3), _whole((1, HEAD_DIM)), _rows(ts, MIX_WIDTH, 0)],
               out_specs=[_rows(ts, MIX_WIDTH), _rows(ts, MIX_WIDTH), _whole((1, HEAD_DIM))],
               out_shape=[_sds((s, MIX_WIDTH)), _sds((s, MIX_WIDTH)), _sds((1, HEAD_DIM))], name=name)(o, h, onorm, dcat)


def _row(v):
    return v.reshape(1, -1)


def _lane_row(v):
    return jnp.pad(v, (0, LANES - v.shape[0])).reshape(1, LANES)


def _rope_tables(positions):
    inv_freq = 1.0 / (ROPE_THETA ** (jnp.arange(0, QK_ROPE, 2, dtype=F32) / QK_ROPE))
    ang = positions.astype(F32)[:, None] * inv_freq
    cos, sin = jnp.cos(ang), jnp.sin(ang)
    z = jnp.zeros_like(cos)
    return jnp.concatenate([cos, z, cos, z], 1), jnp.concatenate([-sin, z, sin, z], 1)


def _local_step(x, mem, positions, loss_target, wts, small):
    cs, sn = _rope_tables(positions)
    saved = []
    for i in range(DEPTH):
        j = i // 2
        sv = {"x": x}
        sv["mem_kv"] = _mm(mem, wts["mem_w_kv"][i], name=f"l{i}_memkv")
        if i % 2 == 0:
            h = _mm(x, wts["mla_w_in"][j], name=f"l{i}_in")
            cqn, ckvn, krr = _mla_prep_fwd(h, _row(small["mla_q_norm"][j]), _row(small["mla_kv_norm"][j]), cs, sn, f"l{i}_mlaprep")
            q = _mm(cqn, wts["mla_w_uq"][j], name=f"l{i}_uq")
            kv = _mm(ckvn, wts["mla_w_ukv"][j], out_dtype=MXU_DTYPE, name=f"l{i}_ukv")
            mix, lse = _flash_fwd(q, kv, krr, cs, sn, f"l{i}_flash")
            sv.update(cqn=cqn, ckvn=ckvn, krr=krr, q=q, kv=kv, o=mix, lse=lse)
            qcol = 2
        else:
            h = _mm(x, wts["gdn_w_in"][j], name=f"l{i}_in")
            conv = _conv_fwd(h, small["gdn_conv"][j], f"l{i}_conv")
            qn, kn, vv, gb, bb = _gdn_prep_fwd(conv, h, _lane_row(small["gdn_a_log"][j]), _lane_row(small["gdn_dt_bias"][j]), f"l{i}_gdnprep")
            o, states = _gdn_core_fwd(qn, kn, vv, gb, bb, f"l{i}_gdncore")
            mix = _gdn_out_fwd(o, h, _row(small["gdn_o_norm"][j]), f"l{i}_gdnout")
            sv.update(conv=conv, qn=qn, kn=kn, vv=vv, gb=gb, bb=bb, o=o, states=states)
            qcol = 4 * MIX_WIDTH // MEM_WIDTH
        mem_o = _mem_fwd(h, qcol, sv["mem_kv"], f"l{i}_mem")
        cat = jnp.concatenate([mix, mem_o], 1)
        y = _mm(cat, wts["w_out"][i], name=f"l{i}_out")
        x1 = _ln_fwd(x, y, _row(small["ln1_g"][i]), _row(small["ln1_b"][i]), f"l{i}_ln1")
        h1 = _mm(x1, wts["mlp_w1"][i], name=f"l{i}_w1")
        ff = _mm(h1, wts["mlp_w2"][i], a_fn=_relu2, name=f"l{i}_w2")
        x2 = _ln_fwd(x1, ff, _row(small["ln2_g"][i]), _row(small["ln2_b"][i]), f"l{i}_ln2")
        sv.update(h=h, qcol=qcol, cat=cat, y=y, x1=x1, h1=h1, ff=ff)
        saved.append(sv)
        x = x2

    loss_row, dx = _loss_and_grad(x, loss_target, "loss")

    gw = {k: [None] * len(v) for k, v in wts.items()}
    gs = {k: [None] * v.shape[0] for k, v in small.items()}
    gdt = COMM_DTYPE
    for i in reversed(range(DEPTH)):
        j = i // 2
        sv = saved[i]
        dr2, dg, db = _ln_bwd(sv["x1"], sv["ff"], _row(small["ln2_g"][i]), _row(small["ln2_b"][i]), dx, f"l{i}_ln2b")
        gs["ln2_g"][i], gs["ln2_b"][i] = dg[0], db[0]
        dh1 = _mm(dr2, wts["mlp_w2"][i], tb=True, epi=_relu2_bwd, e=sv["h1"], name=f"l{i}_dh1")
        gw["mlp_w2"][i] = _mm(sv["h1"], dr2, ta=True, a_fn=_relu2, out_dtype=gdt, name=f"l{i}_dw2")
        gw["mlp_w1"][i] = _mm(sv["x1"], dh1, ta=True, out_dtype=gdt, name=f"l{i}_dw1")
        dx1 = _mm(dh1, wts["mlp_w1"][i], tb=True, epi=_add_alpha, e=dr2, name=f"l{i}_dx1")
        dr1, dg, db = _ln_bwd(sv["x"], sv["y"], _row(small["ln1_g"][i]), _row(small["ln1_b"][i]), dx1, f"l{i}_ln1b")
        gs["ln1_g"][i], gs["ln1_b"][i] = dg[0], db[0]
        gw["w_out"][i] = _mm(sv["cat"], dr1, ta=True, out_dtype=gdt, name=f"l{i}_dwout")
        dcat = _mm(dr1, wts["w_out"][i], tb=True, name=f"l{i}_dcat")
        h = sv["h"]
        dqmem, dmem_kv = _mem_bwd(h, sv["qcol"], sv["mem_kv"], dcat, f"l{i}_memb")
        gw["mem_w_kv"][i] = _mm(mem, dmem_kv, ta=True, out_dtype=gdt, name=f"l{i}_dwmem")
        zpad = jnp.zeros((h.shape[0], LANES), F32)
        if i % 2 == 0:
            qnw, kvnw = _row(small["mla_q_norm"][j]), _row(small["mla_kv_norm"][j])
            dq = _flash_bwd_q(sv["q"], sv["kv"], sv["krr"], cs, sn, sv["o"], sv["lse"], dcat, f"l{i}_flashbq")
            dkv, dkrr = _flash_bwd_kv(sv["q"], sv["kv"], sv["krr"], cs, sn, sv["o"], sv["lse"], dcat, f"l{i}_flashbkv")
            gw["mla_w_ukv"][j] = _mm(sv["ckvn"], dkv, ta=True, out_dtype=gdt, name=f"l{i}_dwukv")
            dckvn = _mm(dkv, wts["mla_w_ukv"][j], tb=True, name=f"l{i}_dckvn")
            gw["mla_w_uq"][j] = _mm(sv["cqn"], dq, ta=True, out_dtype=gdt, name=f"l{i}_dwuq")
            dcqn = _mm(dq, wts["mla_w_uq"][j], tb=True, name=f"l{i}_dcqn")
            dcq, dckv, dkr, dqn, dkvn = _mla_prep_bwd(h, qnw, kvnw, cs, sn, dcqn, dckvn, dkrr, f"l{i}_mlaprepb")
            gs["mla_q_norm"][j], gs["mla_kv_norm"][j] = dqn[0], dkvn[0]
            dh = jnp.concatenate([dcq, dckv, dqmem, dkr, zpad], 1)
            w_in = wts["mla_w_in"][j]
            key = "mla_w_in"
        else:
            alog, dtb = _lane_row(small["gdn_a_log"][j]), _lane_row(small["gdn_dt_bias"][j])
            do, dz, dgn = _gdn_out_bwd(sv["o"], h, _row(small["gdn_o_norm"][j]), dcat, f"l{i}_gdnoutb")
            gs["gdn_o_norm"][j] = dgn[0]
            dqn, dkn, dvv, dgb, dbb = _gdn_core_bwd(sv["qn"], sv["kn"], sv["vv"], sv["gb"], sv["bb"], sv["states"], do, f"l{i}_gdncoreb")
            dconv, dab, dalog, ddtb = _gdn_prep_bwd(sv["conv"], h, alog, dtb, dqn, dkn, dvv, dgb, dbb, f"l{i}_gdnprepb")
            gs["gdn_a_log"][j], gs["gdn_dt_bias"][j] = dalog[0, :N_MIX_HEADS], ddtb[0, :N_MIX_HEADS]
            dhqkv, dconvw = _conv_bwd(h, small["gdn_conv"][j], dconv, f"l{i}_convb")
            gs["gdn_conv"][j] = dconvw
            dh = jnp.concatenate([dhqkv, dz, dqmem, dab, zpad], 1)
            w_in = wts["gdn_w_in"][j]
            key = "gdn_w_in"
        gw[key][j] = _mm(sv["x"], dh, ta=True, out_dtype=gdt, name=f"l{i}_dwin")
        dx = _mm(dh, w_in, tb=True, epi=_add_alpha, e=dr1, name=f"l{i}_dx")
    gs = {k: jnp.stack(v) for k, v in gs.items()}
    return loss_row, dx, gw, gs


def _mla_in_to_kernel(w):
    z32 = jnp.zeros((w.shape[0], 32), w.dtype)
    z128 = jnp.zeros((w.shape[0], LANES), w.dtype)
    return jnp.concatenate([w[:, :1024], w[:, 1088:1600], w[:, 1024:1056], z32, w[:, 1056:1088], z32, z128], 1)


def _mla_in_from_kernel(g):
    return jnp.concatenate([g[:, :1024], g[:, 1536:1568], g[:, 1600:1632], g[:, 1024:1536]], 1)


def _uq_to_kernel(w):
    w3 = w.reshape(Q_LORA, N_MIX_HEADS, QK_NOPE + QK_ROPE)
    z = jnp.zeros((Q_LORA, N_MIX_HEADS, 32), w.dtype)
    return jnp.concatenate([w3[:, :, :128], w3[:, :, 128:160], z, w3[:, :, 160:192], z], 2).reshape(Q_LORA, N_MIX_HEADS * Q_HEAD_P)


def _uq_from_kernel(g):
    g3 = g.reshape(Q_LORA, N_MIX_HEADS, Q_HEAD_P)
    return jnp.concatenate([g3[:, :, :128], g3[:, :, 128:160], g3[:, :, 192:224]], 2).reshape(Q_LORA, -1)


def _gdn_in_to_kernel(w):
    z = jnp.zeros((w.shape[0], LANES - 2 * N_MIX_HEADS + LANES), w.dtype)
    return jnp.concatenate([w[:, :6144], w[:, 6168:6680], w[:, 6144:6168], z], 1)


def _gdn_in_from_kernel(g):
    return jnp.concatenate([g[:, :6144], g[:, 6656:6680], g[:, 6144:6656]], 1)


def _exchange(srcs, name, gather):
    n = len(srcs)

    def body(*refs):
        src_refs, out_refs = refs[:n], refs[n:2 * n]
        send_sems, recv_sems, local_sems = refs[2 * n:]
        x, y, c = lax.axis_index("x"), lax.axis_index("y"), lax.axis_index("c")
        me = 4 * x + 2 * y + c
        local, sends, recvs = [], [], []
        for a in range(n):
            src_ref, out_ref = src_refs[a], out_refs[a]
            local.append(pltpu.make_async_copy(src_ref if gather else src_ref.at[me], out_ref.at[me], local_sems.at[a]))
            for kk in range(1, N_DEV):
                px, py, pc = x ^ ((kk >> 2) & 1), y ^ ((kk >> 1) & 1), c ^ (kk & 1)
                peer = 4 * px + 2 * py + pc
                piece = src_ref if gather else src_ref.at[peer]
                sends.append(pltpu.make_async_remote_copy(
                    src_ref=piece, dst_ref=out_ref.at[me], send_sem=send_sems.at[a, kk - 1], recv_sem=recv_sems.at[a, kk - 1],
                    device_id=(px, py, pc), device_id_type=pl.DeviceIdType.MESH))
                recvs.append(pltpu.make_async_remote_copy(
                    src_ref=piece, dst_ref=out_ref.at[peer], send_sem=send_sems.at[a, kk - 1], recv_sem=recv_sems.at[a, kk - 1],
                    device_id=(px, py, pc), device_id_type=pl.DeviceIdType.MESH))
        for cp in local + sends:
            cp.start()
        for cp in recvs:
            cp.wait_recv()
        for cp in sends:
            cp.wait_send()
        for cp in local:
            cp.wait()

    hbm = pl.BlockSpec(memory_space=pltpu.HBM)
    shapes = [_sds((N_DEV,) + tuple(s.shape if gather else s.shape[1:]), s.dtype) for s in srcs]
    return pl.pallas_call(
        body, in_specs=[hbm] * n, out_specs=[hbm] * n, out_shape=shapes, name=name,
        scratch_shapes=[pltpu.SemaphoreType.DMA((n, N_DEV - 1)), pltpu.SemaphoreType.DMA((n, N_DEV - 1)),
                        pltpu.SemaphoreType.DMA((n,))],
    )(*srcs)


def _adamw(parts, w, m, v, name):
    r, c = w.shape
    n_parts = parts.shape[0]
    tr = _pick(r, tuple(t for t in (256, 128, 64, 32, 16, 8) if t * c <= 256 * 1024))

    def body(p_ref, w_ref, m_ref, v_ref, g_ref, d_ref, nm_ref, nv_ref):
        g = p_ref[0].astype(F32)
        for dd in range(1, n_parts):
            g = g + p_ref[dd].astype(F32)
        nm = ADAM_B1 * m_ref[...] + (1.0 - ADAM_B1) * g
        nv = ADAM_B2 * v_ref[...] + (1.0 - ADAM_B2) * jnp.square(g)
        m_hat = nm / (1.0 - ADAM_B1 ** ADAM_STEP)
        v_hat = nv / (1.0 - ADAM_B2 ** ADAM_STEP)
        g_ref[...] = g
        d_ref[...] = -ADAM_LR * (m_hat / (jnp.sqrt(v_hat) + ADAM_EPS) + ADAM_WD * w_ref[...])
        nm_ref[...] = nm
        nv_ref[...] = nv

    blk = pl.BlockSpec((tr, c), lambda i: (i, 0))
    return _pc(body, grid=(r // tr,), in_specs=[pl.BlockSpec((n_parts, tr, c), lambda i: (0, i, 0)), blk, blk, blk],
               out_specs=[blk] * 4, out_shape=[_sds((r, c))] * 4, name=name)(parts, w, m, v)


N_CHIPS = 4
MESH_IDS = pl.DeviceIdType.MESH


def _place():
    x, y, c = lax.axis_index("x"), lax.axis_index("y"), lax.axis_index("c")
    return x, y, c, [(1 - x, y), (x, 1 - y), (1 - x, 1 - y)]


def _gather_two_level(srcs, name):
    n = len(srcs)

    def body(*refs):
        src_refs, out_refs = refs[:n], refs[n:2 * n]
        send_sems, recv_sems, local_sems = refs[2 * n:]
        x, y, c, chips = _place()
        sib = (x, y, 1 - c)

        def copy(a, k, block, to, src=None):
            rows = out_refs[a].at[4 * block[0] + 2 * block[1] + block[2]]
            return pltpu.make_async_remote_copy(src_ref=rows if src is None else src, dst_ref=rows,
                                                send_sem=send_sems.at[a, k], recv_sem=recv_sems.at[a, k],
                                                device_id=to, device_id_type=MESH_IDS)

        local = [pltpu.make_async_copy(src_refs[a], out_refs[a].at[4 * x + 2 * y + c], local_sems.at[a]) for a in range(n)]
        first = []
        for a in range(n):
            first.append(copy(a, 0, (x, y, c), sib, src_refs[a]))
            first += [copy(a, 1 + j, (x, y, c), (*chip, c), src_refs[a]) for j, chip in enumerate(chips)]
        for cp in local + first:
            cp.start()
        passed = []
        for a in range(n):
            for j, chip in enumerate(chips):
                copy(a, 1 + j, (*chip, c), (x, y, c)).wait_recv()
                passed.append(copy(a, 4 + j, (*chip, c), sib))
                passed[-1].start()
        for a in range(n):
            copy(a, 0, (x, y, 1 - c), (x, y, c)).wait_recv()
            for j, chip in enumerate(chips):
                copy(a, 4 + j, (*chip, 1 - c), (x, y, c)).wait_recv()
        for cp in first + passed:
            cp.wait_send()
        for cp in local:
            cp.wait()

    hbm = pl.BlockSpec(memory_space=pltpu.HBM)
    return pl.pallas_call(
        body, in_specs=[hbm] * n, out_specs=[hbm] * n, name=name,
        out_shape=[_sds((N_DEV,) + tuple(s.shape), s.dtype) for s in srcs],
        scratch_shapes=[pltpu.SemaphoreType.DMA((n, 7)), pltpu.SemaphoreType.DMA((n, 7)), pltpu.SemaphoreType.DMA((n,))],
    )(*srcs)


def _pair_exchange(srcs, name):
    n = len(srcs)

    def body(*refs):
        src_refs, out_refs = refs[:n], refs[n:2 * n]
        send_sems, recv_sems = refs[2 * n:]
        x, y, c, _ = _place()
        sends = []
        for a in range(n):
            for ch in range(N_CHIPS):
                sends.append(pltpu.make_async_remote_copy(
                    src_ref=src_refs[a].at[2 * ch + (1 - c)], dst_ref=out_refs[a].at[ch],
                    send_sem=send_sems.at[a, ch], recv_sem=recv_sems.at[a, ch], device_id=(x, y, 1 - c),
                    device_id_type=MESH_IDS))
        for cp in sends:
            cp.start()
        for cp in sends:
            cp.wait_recv()
        for cp in sends:
            cp.wait_send()

    hbm = pl.BlockSpec(memory_space=pltpu.HBM)
    return pl.pallas_call(
        body, in_specs=[hbm] * n, out_specs=[hbm] * n, name=name,
        out_shape=[_sds((N_CHIPS,) + tuple(s.shape[1:]), s.dtype) for s in srcs],
        scratch_shapes=[pltpu.SemaphoreType.DMA((n, N_CHIPS)), pltpu.SemaphoreType.DMA((n, N_CHIPS))],
    )(*srcs)


def _pair_sum(mine, theirs, my_c, name):
    _, r, c = mine.shape
    tr = _pick(r, tuple(t for t in (512, 256, 128, 64, 32, 16, 8) if t * c <= 512 * 1024))

    def body(c_ref, a_ref, b_ref, o_ref):
        o_ref[...] = (a_ref[...].astype(F32) + b_ref[...].astype(F32)).astype(o_ref.dtype)

    return pl.pallas_call(
        body, out_shape=_sds((N_CHIPS, r, c), mine.dtype), name=name,
        grid_spec=pltpu.PrefetchScalarGridSpec(
            num_scalar_prefetch=1, grid=(N_CHIPS, r // tr),
            in_specs=[pl.BlockSpec((1, tr, c), lambda ch, i, cc: (2 * ch + cc[0], i, 0)),
                      pl.BlockSpec((1, tr, c), lambda ch, i, cc: (ch, i, 0))],
            out_specs=pl.BlockSpec((1, tr, c), lambda ch, i, cc: (ch, i, 0))),
        compiler_params=pltpu.CompilerParams(dimension_semantics=("arbitrary", "arbitrary"), vmem_limit_bytes=VMEM_LIMIT),
    )(my_c, mine, theirs)


def _chip_exchange(srcs, name):
    n = len(srcs)

    def body(*refs):
        src_refs, out_refs = refs[:n], refs[n:2 * n]
        send_sems, recv_sems, local_sems = refs[2 * n:]
        x, y, c, chips = _place()
        my_chip = 2 * x + y
        local = [pltpu.make_async_copy(src_refs[a].at[my_chip], out_refs[a].at[my_chip], local_sems.at[a]) for a in range(n)]
        sends, recvs = [], []
        for a in range(n):
            for j, chip in enumerate(chips):
                ch = 2 * chip[0] + chip[1]
                sends.append(pltpu.make_async_remote_copy(
                    src_ref=src_refs[a].at[ch], dst_ref=out_refs[a].at[my_chip], send_sem=send_sems.at[a, j],
                    recv_sem=recv_sems.at[a, j], device_id=(*chip, c), device_id_type=MESH_IDS))
                recvs.append(pltpu.make_async_remote_copy(
                    src_ref=src_refs[a].at[ch], dst_ref=out_refs[a].at[ch], send_sem=send_sems.at[a, j],
                    recv_sem=recv_sems.at[a, j], device_id=(*chip, c), device_id_type=MESH_IDS))
        for cp in local + sends:
            cp.start()
        for cp in recvs:
            cp.wait_recv()
        for cp in sends:
            cp.wait_send()
        for cp in local:
            cp.wait()

    hbm = pl.BlockSpec(memory_space=pltpu.HBM)
    return pl.pallas_call(
        body, in_specs=[hbm] * n, out_specs=[hbm] * n, name=name,
        out_shape=[_sds(tuple(s.shape), s.dtype) for s in srcs],
        scratch_shapes=[pltpu.SemaphoreType.DMA((n, 3)), pltpu.SemaphoreType.DMA((n, 3)), pltpu.SemaphoreType.DMA((n,))],
    )(*srcs)


BIG = (("mla_w_in", 1), ("mla_w_uq", 2), ("mla_w_ukv", 2), ("gdn_w_in", 2), ("mem_w_kv", 1), ("w_out", 1),
       ("mlp_w1", 2), ("mlp_w2", 1), ("gdn_conv", 2))
SMALL = ("mla_q_norm", "mla_kv_norm", "gdn_a_log", "gdn_dt_bias", "gdn_o_norm", "ln1_g", "ln1_b", "ln2_g", "ln2_b")
PACK_COLS = 1024


def _unshard(pieces, axis):
    t = jnp.moveaxis(pieces, 0, axis)
    return t.reshape(t.shape[:axis] + (t.shape[axis] * t.shape[axis + 1],) + t.shape[axis + 2:])


def _shard(full, axis):
    t = full.reshape(full.shape[:axis] + (N_DEV, full.shape[axis] // N_DEV) + full.shape[axis + 1:])
    return jnp.moveaxis(t, axis, 0)


def _pack(arrs, lead, cols, mult):
    lead_shape = arrs[0].shape[:lead]
    flat = jnp.concatenate([a.reshape(lead_shape + (-1,)) for a in arrs], -1)
    n = flat.shape[-1]
    r = -(-n // (cols * mult)) * mult
    flat = jnp.pad(flat, [(0, 0)] * lead + [(0, r * cols - n)])
    return flat.reshape(lead_shape + (r, cols))


def _unpack(buf, lead, shapes):
    lead_shape = buf.shape[:lead]
    flat = buf.reshape(lead_shape + (-1,))
    out, off = [], 0
    for shp in shapes:
        n = 1
        for d in shp:
            n *= d
        out.append(flat[..., off:off + n].reshape(lead_shape + tuple(shp)))
        off += n
    return out


def kernel(x, mem, positions, mla_w_in, mla_q_norm, mla_w_uq, mla_kv_norm, mla_w_ukv, gdn_w_in, gdn_conv, gdn_a_log, gdn_dt_bias, gdn_o_norm, mem_w_kv, w_out, ln1_g, ln1_b, mlp_w1, mlp_w2, ln2_g, ln2_b, loss_target, m_mla_w_in, m_mla_q_norm, m_mla_w_uq, m_mla_kv_norm, m_mla_w_ukv, m_gdn_w_in, m_gdn_conv, m_gdn_a_log, m_gdn_dt_bias, m_gdn_o_norm, m_mem_w_kv, m_w_out, m_ln1_g, m_ln1_b, m_mlp_w1, m_mlp_w2, m_ln2_g, m_ln2_b, v_mla_w_in, v_mla_q_norm, v_mla_w_uq, v_mla_kv_norm, v_mla_w_ukv, v_gdn_w_in, v_gdn_conv, v_gdn_a_log, v_gdn_dt_bias, v_gdn_o_norm, v_mem_w_kv, v_w_out, v_ln1_g, v_ln1_b, v_mlp_w1, v_mlp_w2, v_ln2_g, v_ln2_b):
    args = dict(locals())
    w_loc = {n: args[n] for n, _ in BIG}
    m_loc = {n: args["m_" + n] for n, _ in BIG}
    v_loc = {n: args["v_" + n] for n, _ in BIG}
    small = {n: args[n] for n in SMALL}
    axis_of = dict(BIG)
    mm_names = [n for n, _ in BIG if n != "gdn_conv"]

    names = [n for n, _ in BIG]
    got = _gather_two_level([w_loc[n].astype(COMM_DTYPE) for n in mm_names] + [w_loc["gdn_conv"]], "gather_weights")
    full = {n: _unshard(p, axis_of[n]) for n, p in zip(names, got)}
    conv_full = full.pop("gdn_conv")

    wts = {n: [full[n][l] for l in range(full[n].shape[0])] for n in mm_names}
    wts["mla_w_in"] = [_mla_in_to_kernel(w) for w in wts["mla_w_in"]]
    wts["mla_w_uq"] = [_uq_to_kernel(w) for w in wts["mla_w_uq"]]
    wts["gdn_w_in"] = [_gdn_in_to_kernel(w) for w in wts["gdn_w_in"]]
    small_in = dict(small, gdn_conv=conv_full)

    loss_row, grad_x, gw, gs = _local_step(x[0], mem[0], positions[0], loss_target[0], wts, small_in)
    loss = lax.psum(loss_row[0, 0], ("x", "y", "c"))

    gw["mla_w_in"] = [_mla_in_from_kernel(g) for g in gw["mla_w_in"]]
    gw["mla_w_uq"] = [_uq_from_kernel(g) for g in gw["mla_w_uq"]]
    gw["gdn_w_in"] = [_gdn_in_from_kernel(g) for g in gw["gdn_w_in"]]
    gfull = {n: jnp.stack(gw[n]) for n in mm_names}
    gfull["gdn_conv"] = gs.pop("gdn_conv").astype(COMM_DTYPE)
    dims = {n: (w_loc[n].shape[0] * w_loc[n].shape[1], w_loc[n].shape[2]) for n in names}
    g_sent = [_shard(gfull[n], axis_of[n]).reshape((N_DEV,) + dims[n]) for n in names]
    g_sibling = _pair_exchange(g_sent, "grads_pair")
    my_c = lax.axis_index("c").astype(jnp.int32).reshape(1)
    g_pairs = [_pair_sum(a, b, my_c, f"pair_sum_{n}") for n, a, b in zip(names, g_sent, g_sibling)]
    g_got = _chip_exchange(g_pairs, "grads_chips")
    big_out = [{}, {}, {}, {}]
    for n, parts in zip(names, g_got):
        shp = w_loc[n].shape
        rows, cols = dims[n]
        res = _adamw(parts, w_loc[n].reshape(rows, cols), m_loc[n].reshape(rows, cols),
                     v_loc[n].reshape(rows, cols), f"adamw_{n}")
        for kind in range(4):
            big_out[kind][n] = res[kind].reshape(shp)

    s_sent = _pack([gs[n] for n in SMALL], 0, LANES, 8)
    s_got = _exchange([s_sent], "gather_small_grads", gather=True)[0]
    small_out = [dict(zip(SMALL, _unpack(o, 0, [small[n].shape for n in SMALL])))
                 for o in _adamw(s_got, _pack([small[n] for n in SMALL], 0, LANES, 8),
                                 _pack([args["m_" + n] for n in SMALL], 0, LANES, 8),
                                 _pack([args["v_" + n] for n in SMALL], 0, LANES, 8), "adamw_small")]

    order = ["mla_w_in", "mla_q_norm", "mla_w_uq", "mla_kv_norm", "mla_w_ukv", "gdn_w_in", "gdn_conv", "gdn_a_log",
             "gdn_dt_bias", "gdn_o_norm", "mem_w_kv", "w_out", "ln1_g", "ln1_b", "mlp_w1", "mlp_w2", "ln2_g", "ln2_b"]
    outs = [loss, grad_x[None]]
    for kind in range(4):
        for n in order:
            outs.append(big_out[kind][n] if n in axis_of else small_out[kind][n])
    return tuple(outs)
```

```python
import functools

import jax
import jax.numpy as jnp
from jax import lax
from jax.experimental import pallas as pl
from jax.experimental.pallas import tpu as pltpu

F32 = jnp.float32
MXU_DTYPE = jnp.bfloat16
COMM_DTYPE = jnp.bfloat16

N_DEV = 8
D_MODEL = 2048
DEPTH = 4
HEAD_DIM = 128
N_MIX_HEADS = 12
N_MEM_HEADS = 4
MIX_WIDTH = N_MIX_HEADS * HEAD_DIM
MEM_WIDTH = N_MEM_HEADS * HEAD_DIM
Q_LORA = 512
KV_LORA = 512
QK_NOPE = 128
QK_ROPE = 64
ROPE_THETA = 10000.0
CONV_WIDTH = 4
CHUNK = 64
D_FF = 4 * D_MODEL
ALPHA = (2 * DEPTH) ** 0.25
LN_EPS = 1e-5
RMS_EPS = 1e-6
MLA_IN = Q_LORA + KV_LORA + QK_ROPE + MEM_WIDTH
GDN_IN = 4 * MIX_WIDTH + 2 * N_MIX_HEADS + MEM_WIDTH
MLA_IN_P = 1792
GDN_IN_P = 6912
Q_HEAD_P = 256
ATT_SCALE = (QK_NOPE + QK_ROPE) ** -0.5
ADAM_LR, ADAM_B1, ADAM_B2, ADAM_EPS, ADAM_WD, ADAM_STEP = 0.001, 0.9, 0.999, 1e-08, 0.01, 10
LANES = 128
VMEM_LIMIT = 56 * 1024 * 1024

NN = (((1,), (0,)), ((), ()))
NT = (((1,), (1,)), ((), ()))
TN = (((0,), (0,)), ((), ()))
HI = lax.Precision.HIGHEST


def _dot(a, b, dims, hi=False):
    if hi:
        return lax.dot_general(a, b, dims, precision=HI, preferred_element_type=F32)
    return lax.dot_general(a.astype(MXU_DTYPE), b.astype(MXU_DTYPE), dims, preferred_element_type=F32)


@functools.partial(jax.custom_vjp, nondiff_argnums=(2,))
def _nn_d(a, b, hi):
    return _dot(a, b, NN, hi)


@functools.partial(jax.custom_vjp, nondiff_argnums=(2,))
def _nt_d(a, b, hi):
    return _dot(a, b, NT, hi)


@functools.partial(jax.custom_vjp, nondiff_argnums=(2,))
def _tn_d(a, b, hi):
    return _dot(a, b, TN, hi)


_nn_d.defvjp(lambda a, b, hi: (_dot(a, b, NN, hi), (a, b)),
             lambda hi, r, g: (_nt_d(g, r[1], hi), _tn_d(r[0], g, hi)))
_nt_d.defvjp(lambda a, b, hi: (_dot(a, b, NT, hi), (a, b)),
             lambda hi, r, g: (_nn_d(g, r[1], hi), _tn_d(g, r[0], hi)))
_tn_d.defvjp(lambda a, b, hi: (_dot(a, b, TN, hi), (a, b)),
             lambda hi, r, g: (_nt_d(r[1], g, hi), _nn_d(r[0], g, hi)))


class _RawOps:
    nn = staticmethod(lambda a, b, hi=False: _dot(a, b, NN, hi))
    nt = staticmethod(lambda a, b, hi=False: _dot(a, b, NT, hi))
    tn = staticmethod(lambda a, b, hi=False: _dot(a, b, TN, hi))


class _DiffOps:
    nn = staticmethod(lambda a, b, hi=False: _nn_d(a, b, hi))
    nt = staticmethod(lambda a, b, hi=False: _nt_d(a, b, hi))
    tn = staticmethod(lambda a, b, hi=False: _tn_d(a, b, hi))


def _pick(dim, cands=(512, 384, 256, 128)):
    for c in cands:
        if dim % c == 0:
            return c
    return dim


def _pc(body, *, grid, in_specs, out_specs, out_shape, name, scratch=()):
    return pl.pallas_call(
        body, grid=grid, in_specs=in_specs, out_specs=out_specs, out_shape=out_shape,
        scratch_shapes=list(scratch), name=name,
        compiler_params=pltpu.CompilerParams(dimension_semantics=("arbitrary",) * len(grid),
                                             vmem_limit_bytes=VMEM_LIMIT))


def _rows(ts, w, cb=0):
    return pl.BlockSpec((ts, w), lambda i, *_: (i, cb))


def _whole(shape):
    return pl.BlockSpec(shape, lambda *_: (0,) * len(shape))


def _sds(shape, dtype=F32):
    return jax.ShapeDtypeStruct(shape, dtype)


def _mm(a, b, *, name, ta=False, tb=False, out_dtype=F32, a_fn=None, epi=None, e=None):
    m, k = (a.shape[1], a.shape[0]) if ta else a.shape
    n = b.shape[0] if tb else b.shape[1]
    assert k == (b.shape[1] if tb else b.shape[0]), (a.shape, b.shape, ta, tb)
    wide = (1024, 768, 512, 384, 256, 128)
    if k <= 2048 and not ta:
        tm, tn, tk = _pick(m, (512, 256, 128)), _pick(n, wide), k
    else:
        tm, tn, tk = _pick(m, wide), _pick(n, wide), _pick(k, wide)
    nk = k // tk
    dims = (((0 if ta else 1,), (1 if tb else 0,)), ((), ()))

    def body(*refs):
        if e is None:
            a_ref, b_ref, o_ref, acc = refs
        else:
            a_ref, b_ref, e_ref, o_ref, acc = refs
        kk = pl.program_id(2)

        @pl.when(kk == 0)
        def _():
            acc[...] = jnp.zeros_like(acc)

        av = a_ref[...]
        if a_fn is not None:
            av = a_fn(av.astype(F32))
        acc[...] += lax.dot_general(av.astype(MXU_DTYPE), b_ref[...].astype(MXU_DTYPE), dims,
                                    preferred_element_type=F32)

        @pl.when(kk == nk - 1)
        def _():
            r = acc[...]
            if epi is not None:
                r = epi(r, e_ref[...].astype(F32))
            o_ref[...] = r.astype(out_dtype)

    a_spec = pl.BlockSpec((tk, tm), lambda i, j, kk: (kk, i)) if ta else pl.BlockSpec((tm, tk), lambda i, j, kk: (i, kk))
    b_spec = pl.BlockSpec((tn, tk), lambda i, j, kk: (j, kk)) if tb else pl.BlockSpec((tk, tn), lambda i, j, kk: (kk, j))
    o_spec = pl.BlockSpec((tm, tn), lambda i, j, kk: (i, j))
    ins, specs = [a, b], [a_spec, b_spec]
    if e is not None:
        assert e.shape == (m, n)
        ins.append(e)
        specs.append(o_spec)
    return _pc(body, grid=(m // tm, n // tn, nk), in_specs=specs, out_specs=o_spec,
               out_shape=_sds((m, n), out_dtype), name=name, scratch=[pltpu.VMEM((tm, tn), F32)])(*ins)


def _relu2(v):
    r = jnp.maximum(v, 0.0)
    return r * r


def _relu2_bwd(acc, h1):
    return acc * (2.0 * jnp.maximum(h1, 0.0))


def _add_alpha(acc, dr):
    return acc + ALPHA * dr


def _ln(r, g, b):
    mu = jnp.mean(r, -1, keepdims=True)
    var = jnp.mean(jnp.square(r - mu), -1, keepdims=True)
    return (r - mu) * lax.rsqrt(var + LN_EPS) * g + b


def _ln_fwd(x, y, g, b, name):
    s, d = x.shape
    ts = _pick(s, (256, 128))

    def body(x_ref, y_ref, g_ref, b_ref, o_ref):
        o_ref[...] = _ln(ALPHA * x_ref[...] + y_ref[...], g_ref[...], b_ref[...])

    return _pc(body, grid=(s // ts,), in_specs=[_rows(ts, d), _rows(ts, d), _whole((1, d)), _whole((1, d))],
               out_specs=_rows(ts, d), out_shape=_sds((s, d)), name=name)(x, y, g, b)


def _ln_bwd(x, y, g, b, dout, name):
    s, d = x.shape
    ts = _pick(s, (256, 128))

    def body(x_ref, y_ref, g_ref, b_ref, do_ref, dr_ref, dg_ref, db_ref):
        @pl.when(pl.program_id(0) == 0)
        def _():
            dg_ref[...] = jnp.zeros_like(dg_ref)
            db_ref[...] = jnp.zeros_like(db_ref)

        r = ALPHA * x_ref[...] + y_ref[...]
        _, vjp = jax.vjp(_ln, r, g_ref[...], b_ref[...])
        dr, dg, db = vjp(do_ref[...])
        dr_ref[...] = dr
        dg_ref[...] += dg
        db_ref[...] += db

    return _pc(body, grid=(s // ts,),
               in_specs=[_rows(ts, d), _rows(ts, d), _whole((1, d)), _whole((1, d)), _rows(ts, d)],
               out_specs=[_rows(ts, d), _whole((1, d)), _whole((1, d))],
               out_shape=[_sds((s, d)), _sds((1, d)), _sds((1, d))], name=name)(x, y, g, b, dout)


def _loss_and_grad(y, target, name):
    s, d = y.shape
    ts = _pick(s, (256, 128))

    def body(y_ref, t_ref, l_ref, dy_ref):
        @pl.when(pl.program_id(0) == 0)
        def _():
            l_ref[...] = jnp.zeros_like(l_ref)

        diff = y_ref[...] - t_ref[...]
        per_tok = jnp.mean(jnp.square(diff), -1, keepdims=True)
        l_ref[...] += 0.5 * jnp.sum(per_tok, 0, keepdims=True) * jnp.ones((1, LANES), F32)
        dy_ref[...] = diff * (1.0 / d)

    return _pc(body, grid=(s // ts,), in_specs=[_rows(ts, d), _rows(ts, d)],
               out_specs=[_whole((1, LANES)), _rows(ts, d)],
               out_shape=[_sds((1, LANES)), _sds((s, d))], name=name)(y, target)


def _rope(blk, cs, sn):
    return blk * cs + pltpu.roll(blk, 64, 1) * sn


def _rope_t(dblk, cs, sn):
    return dblk * cs + pltpu.roll(dblk * sn, 64, 1)


def _rms(v, g):
    return v * lax.rsqrt(jnp.mean(v * v, -1, keepdims=True) + RMS_EPS) * g


def _mla_prep_fwd(h, qn, kvn, cs, sn, name):
    s = h.shape[0]
    ts = _pick(s, (512, 256, 128))

    def body(cq_ref, ckv_ref, kr_ref, qn_ref, kvn_ref, cs_ref, sn_ref, cqn_ref, ckvn_ref, krr_ref):
        cqn_ref[...] = _rms(cq_ref[...], qn_ref[...])
        ckvn_ref[...] = _rms(ckv_ref[...], kvn_ref[...])
        krr_ref[...] = _rope(kr_ref[...], cs_ref[...], sn_ref[...])

    return _pc(body, grid=(s // ts,),
               in_specs=[_rows(ts, 512, 0), _rows(ts, 512, 1), _rows(ts, LANES, 12), _whole((1, 512)), _whole((1, 512)),
                         _rows(ts, LANES), _rows(ts, LANES)],
               out_specs=[_rows(ts, 512), _rows(ts, 512), _rows(ts, LANES)],
               out_shape=[_sds((s, 512)), _sds((s, 512)), _sds((s, LANES))], name=name)(h, h, h, qn, kvn, cs, sn)


def _mla_prep_bwd(h, qn, kvn, cs, sn, dcqn, dckvn, dkrr_heads, name):
    s = h.shape[0]
    ts = _pick(s, (512, 256, 128))

    def body(cq_ref, ckv_ref, qn_ref, kvn_ref, cs_ref, sn_ref, dcqn_ref, dckvn_ref, dkrr_ref,
             dcq_ref, dckv_ref, dkr_ref, dqn_ref, dkvn_ref):
        @pl.when(pl.program_id(0) == 0)
        def _():
            dqn_ref[...] = jnp.zeros_like(dqn_ref)
            dkvn_ref[...] = jnp.zeros_like(dkvn_ref)

        _, vjp = jax.vjp(_rms, cq_ref[...], qn_ref[...])
        dcq, dqn = vjp(dcqn_ref[...])
        dcq_ref[...] = dcq
        dqn_ref[...] += dqn
        _, vjp = jax.vjp(_rms, ckv_ref[...], kvn_ref[...])
        dckv, dkvn = vjp(dckvn_ref[...])
        dckv_ref[...] = dckv
        dkvn_ref[...] += dkvn
        dkrr = dkrr_ref[0]
        for hh in range(1, N_MIX_HEADS):
            dkrr = dkrr + dkrr_ref[hh]
        dkr_ref[...] = _rope_t(dkrr, cs_ref[...], sn_ref[...])

    heads3 = pl.BlockSpec((N_MIX_HEADS, ts, LANES), lambda i: (0, i, 0))
    return _pc(body, grid=(s // ts,),
               in_specs=[_rows(ts, 512, 0), _rows(ts, 512, 1), _whole((1, 512)), _whole((1, 512)),
                         _rows(ts, LANES), _rows(ts, LANES), _rows(ts, 512), _rows(ts, 512), heads3],
               out_specs=[_rows(ts, 512), _rows(ts, 512), _rows(ts, LANES), _whole((1, 512)), _whole((1, 512))],
               out_shape=[_sds((s, 512)), _sds((s, 512)), _sds((s, LANES)), _sds((1, 512)), _sds((1, 512))],
               name=name)(h, h, qn, kvn, cs, sn, dcqn, dckvn, dkrr_heads)


def _att_tiles(s):
    t = _pick(s, (512, 256, 128))
    return t, s // t


ATT_HEADS_PER_STEP = 6


def _tri_pairs(nb, k_major):
    pairs = [(i, j) for j in range(nb) for i in range(j, nb)] if k_major else [(i, j) for i in range(nb) for j in range(i + 1)]
    return jnp.asarray([p[0] for p in pairs], jnp.int32), jnp.asarray([p[1] for p in pairs], jnp.int32)


ATT_STRIP = 64


def _scaled_q(q_ref, c0, cs_ref, sn_ref, dst, hh):
    dst[hh, :, :LANES] = (q_ref[:, c0:c0 + LANES] * ATT_SCALE).astype(MXU_DTYPE)
    dst[hh, :, LANES:] = (_rope(q_ref[:, c0 + LANES:c0 + Q_HEAD_P], cs_ref[...], sn_ref[...]) * ATT_SCALE).astype(MXU_DTYPE)


def _cat_k(kv_ref, c0, kr, dst, hh):
    dst[hh, :, :LANES] = kv_ref[:, c0:c0 + LANES].astype(MXU_DTYPE)
    dst[hh, :, LANES:] = kr


def _tril_rows(r, n, t):
    return lax.broadcasted_iota(jnp.int32, (n, t), 1) <= r + lax.broadcasted_iota(jnp.int32, (n, t), 0)


def _pc_pairs(body, pairs, *, n_groups, in_specs, out_specs, out_shape, name, scratch, args):
    return pl.pallas_call(
        body, out_shape=out_shape, name=name,
        grid_spec=pltpu.PrefetchScalarGridSpec(num_scalar_prefetch=2, grid=(n_groups, pairs[0].shape[0]), in_specs=in_specs,
                                               out_specs=out_specs, scratch_shapes=list(scratch)),
        compiler_params=pltpu.CompilerParams(dimension_semantics=("arbitrary", "arbitrary"), vmem_limit_bytes=VMEM_LIMIT),
    )(*pairs, *args)


def _att_specs(t):
    qrow = lambda w: pl.BlockSpec((t, w), lambda h, p, qi, kj: (qi[p], h))
    krow = lambda w: pl.BlockSpec((t, w), lambda h, p, qi, kj: (kj[p], h))
    qtab = pl.BlockSpec((t, LANES), lambda h, p, qi, kj: (qi[p], 0))
    ktab = pl.BlockSpec((t, LANES), lambda h, p, qi, kj: (kj[p], 0))
    return qrow, krow, qtab, ktab


def _flash_fwd(q, kv, krr, cs, sn, name):
    s = q.shape[0]
    t, nb = _att_tiles(s)
    ah = ATT_HEADS_PER_STEP

    def body(qi_ref, kj_ref, q_ref, cs_ref, sn_ref, kv_ref, kr_ref, o_ref, lse_ref, q_s, m_s, l_s, acc_s, s_scr, p_scr):
        i, j = qi_ref[pl.program_id(1)], kj_ref[pl.program_id(1)]

        @pl.when(j == 0)
        def _():
            for hh in range(ah):
                _scaled_q(q_ref, hh * Q_HEAD_P, cs_ref, sn_ref, q_s, hh)
            m_s[...] = jnp.full_like(m_s, -jnp.inf)
            l_s[...] = jnp.zeros_like(l_s)
            acc_s[...] = jnp.zeros_like(acc_s)

        def update(masked):
            kr = kr_ref[...].astype(MXU_DTYPE)
            for hh in range(ah):
                c0 = hh * Q_HEAD_P
                s_scr[hh] = _dot(q_s[hh, :, :LANES], kv_ref[:, c0:c0 + LANES], NT) + _dot(q_s[hh, :, LANES:], kr, NT)
                for r in range(0, t, ATT_STRIP):
                    rows = slice(r, r + ATT_STRIP)
                    sc = s_scr[hh, rows, :]
                    if masked:
                        sc = jnp.where(_tril_rows(r, ATT_STRIP, t), sc, -jnp.inf)
                    m_old = m_s[hh, rows, :]
                    m_new = jnp.maximum(m_old, jnp.max(sc, -1, keepdims=True))
                    p = jnp.exp(sc - m_new[:, :1])
                    corr = jnp.exp(m_old - m_new)
                    l_s[hh, rows, :] = corr * l_s[hh, rows, :] + jnp.sum(p, -1, keepdims=True)
                    acc_s[hh, rows, :] = corr * acc_s[hh, rows, :]
                    m_s[hh, rows, :] = m_new
                    p_scr[hh, rows, :] = p.astype(MXU_DTYPE)
                acc_s[hh] += _dot(p_scr[hh], kv_ref[:, c0 + LANES:c0 + Q_HEAD_P], NN)

        @pl.when(j < i)
        def _():
            update(False)

        @pl.when(j == i)
        def _():
            update(True)
            for hh in range(ah):
                sl = slice(hh * LANES, (hh + 1) * LANES)
                o_ref[:, sl] = acc_s[hh] / l_s[hh]
                lse_ref[:, sl] = m_s[hh] + jnp.log(l_s[hh])

    qrow, krow, qtab, ktab = _att_specs(t)
    return _pc_pairs(body, _tri_pairs(nb, False), n_groups=N_MIX_HEADS // ah,
                     in_specs=[qrow(ah * Q_HEAD_P), qtab, qtab, krow(ah * Q_HEAD_P), ktab],
                     out_specs=[qrow(ah * LANES), qrow(ah * LANES)],
                     out_shape=[_sds((s, MIX_WIDTH)), _sds((s, MIX_WIDTH))], name=name,
                     scratch=[pltpu.VMEM((ah, t, Q_HEAD_P), MXU_DTYPE),
                              pltpu.VMEM((ah, t, LANES), F32), pltpu.VMEM((ah, t, LANES), F32), pltpu.VMEM((ah, t, LANES), F32),
                              pltpu.VMEM((ah, t, t), F32), pltpu.VMEM((ah, t, t), MXU_DTYPE)],
                     args=(q, cs, sn, kv, krr))


def _flash_bwd_q(q, kv, krr, cs, sn, o, lse, dcat, name):
    s = q.shape[0]
    t, nb = _att_tiles(s)
    ah = ATT_HEADS_PER_STEP

    def body(qi_ref, kj_ref, q_ref, cs_ref, sn_ref, o_ref, lse_ref, do_ref, kv_ref, kr_ref, dq_ref,
             q_s, k_s, dl_s, aq_s, s_scr, dp_scr, ds_scr):
        i, j = qi_ref[pl.program_id(1)], kj_ref[pl.program_id(1)]

        @pl.when(j == 0)
        def _():
            for hh in range(ah):
                sl = slice(hh * LANES, (hh + 1) * LANES)
                _scaled_q(q_ref, hh * Q_HEAD_P, cs_ref, sn_ref, q_s, hh)
                dl_s[hh] = jnp.sum(o_ref[:, sl] * do_ref[:, sl], -1, keepdims=True) * jnp.ones((1, LANES), F32)
            aq_s[...] = jnp.zeros_like(aq_s)

        def update(masked):
            kr = kr_ref[...].astype(MXU_DTYPE)
            for hh in range(ah):
                c0 = hh * Q_HEAD_P
                sl = slice(hh * LANES, (hh + 1) * LANES)
                _cat_k(kv_ref, c0, kr, k_s, hh)
                s_scr[hh] = _dot(q_s[hh], k_s[hh], NT)
                dp_scr[hh] = _dot(do_ref[:, sl], kv_ref[:, c0 + LANES:c0 + Q_HEAD_P], NT)
                for r in range(0, t, ATT_STRIP):
                    rows = slice(r, r + ATT_STRIP)
                    p = jnp.exp(s_scr[hh, rows, :] - lse_ref[rows, hh * LANES:hh * LANES + 1])
                    if masked:
                        p = jnp.where(_tril_rows(r, ATT_STRIP, t), p, 0.0)
                    ds_scr[hh, rows, :] = (p * (dp_scr[hh, rows, :] - dl_s[hh, rows, :1])).astype(MXU_DTYPE)
                aq_s[hh] += _dot(ds_scr[hh], k_s[hh], NN)

        @pl.when(j < i)
        def _():
            update(False)

        @pl.when(j == i)
        def _():
            update(True)
            for hh in range(ah):
                c0 = hh * Q_HEAD_P
                dq_ref[:, c0:c0 + LANES] = aq_s[hh, :, :LANES] * ATT_SCALE
                dq_ref[:, c0 + LANES:c0 + Q_HEAD_P] = _rope_t(aq_s[hh, :, LANES:] * ATT_SCALE, cs_ref[...], sn_ref[...])

    qrow, krow, qtab, ktab = _att_specs(t)
    return _pc_pairs(body, _tri_pairs(nb, False), n_groups=N_MIX_HEADS // ah,
                     in_specs=[qrow(ah * Q_HEAD_P), qtab, qtab, qrow(ah * LANES), qrow(ah * LANES), qrow(ah * LANES),
                               krow(ah * Q_HEAD_P), ktab],
                     out_specs=qrow(ah * Q_HEAD_P), out_shape=_sds((s, N_MIX_HEADS * Q_HEAD_P)), name=name,
                     scratch=[pltpu.VMEM((ah, t, Q_HEAD_P), MXU_DTYPE), pltpu.VMEM((ah, t, Q_HEAD_P), MXU_DTYPE),
                              pltpu.VMEM((ah, t, LANES), F32), pltpu.VMEM((ah, t, Q_HEAD_P), F32),
                              pltpu.VMEM((ah, t, t), F32), pltpu.VMEM((ah, t, t), F32), pltpu.VMEM((ah, t, t), MXU_DTYPE)],
                     args=(q, cs, sn, o, lse, dcat, kv, krr))


def _flash_bwd_kv(q, kv, krr, cs, sn, o, lse, dcat, name):
    s = q.shape[0]
    t, nb = _att_tiles(s)
    ah = ATT_HEADS_PER_STEP

    def body(qi_ref, kj_ref, kv_ref, kr_ref, q_ref, cs_ref, sn_ref, o_ref, lse_ref, do_ref, dkv_ref, dkr_ref,
             ak_s, av_s, q_s, k_s, s_scr, dp_scr, p_scr, ds_scr):
        i, j = qi_ref[pl.program_id(1)], kj_ref[pl.program_id(1)]

        @pl.when(i == j)
        def _():
            ak_s[...] = jnp.zeros_like(ak_s)
            av_s[...] = jnp.zeros_like(av_s)

        def update(masked):
            kr = kr_ref[...].astype(MXU_DTYPE)
            for hh in range(ah):
                c0 = hh * Q_HEAD_P
                sl = slice(hh * LANES, (hh + 1) * LANES)
                _scaled_q(q_ref, c0, cs_ref, sn_ref, q_s, hh)
                _cat_k(kv_ref, c0, kr, k_s, hh)
                do = do_ref[:, sl].astype(MXU_DTYPE)
                s_scr[hh] = _dot(q_s[hh], k_s[hh], NT)
                dp_scr[hh] = _dot(do, kv_ref[:, c0 + LANES:c0 + Q_HEAD_P], NT)
                for r in range(0, t, ATT_STRIP):
                    rows = slice(r, r + ATT_STRIP)
                    p = jnp.exp(s_scr[hh, rows, :] - lse_ref[rows, hh * LANES:hh * LANES + 1])
                    if masked:
                        p = jnp.where(_tril_rows(r, ATT_STRIP, t), p, 0.0)
                    dl = jnp.sum(o_ref[rows, sl] * do_ref[rows, sl], -1, keepdims=True)
                    p_scr[hh, rows, :] = p.astype(MXU_DTYPE)
                    ds_scr[hh, rows, :] = (p * (dp_scr[hh, rows, :] - dl)).astype(MXU_DTYPE)
                av_s[hh] += _dot(p_scr[hh], do, TN)
                ak_s[hh] += _dot(ds_scr[hh], q_s[hh], TN)

        @pl.when(i > j)
        def _():
            update(False)

        @pl.when(i == j)
        def _():
            update(True)

        @pl.when(i == nb - 1)
        def _():
            for hh in range(ah):
                c0 = hh * Q_HEAD_P
                dkv_ref[:, c0:c0 + LANES] = ak_s[hh, :, :LANES]
                dkv_ref[:, c0 + LANES:c0 + Q_HEAD_P] = av_s[hh]
                dkr_ref[hh] = ak_s[hh, :, LANES:]

    qrow, krow, qtab, ktab = _att_specs(t)
    return _pc_pairs(body, _tri_pairs(nb, True), n_groups=N_MIX_HEADS // ah,
                     in_specs=[krow(ah * Q_HEAD_P), ktab, qrow(ah * Q_HEAD_P), qtab, qtab,
                               qrow(ah * LANES), qrow(ah * LANES), qrow(ah * LANES)],
                     out_specs=[krow(ah * Q_HEAD_P), pl.BlockSpec((ah, t, LANES), lambda h, p, qi, kj: (h, kj[p], 0))],
                     out_shape=[_sds((s, N_MIX_HEADS * Q_HEAD_P)), _sds((N_MIX_HEADS, s, LANES))], name=name,
                     scratch=[pltpu.VMEM((ah, t, Q_HEAD_P), F32), pltpu.VMEM((ah, t, LANES), F32),
                              pltpu.VMEM((ah, t, Q_HEAD_P), MXU_DTYPE), pltpu.VMEM((ah, t, Q_HEAD_P), MXU_DTYPE),
                              pltpu.VMEM((ah, t, t), F32), pltpu.VMEM((ah, t, t), F32),
                              pltpu.VMEM((ah, t, t), MXU_DTYPE), pltpu.VMEM((ah, t, t), MXU_DTYPE)],
                     args=(kv, krr, q, cs, sn, o, lse, dcat))


def _mem_head(ops, qh, kh, vh):
    sc = ops.nt(qh, kh) * HEAD_DIM ** -0.5
    e = jnp.exp(sc - lax.stop_gradient(jnp.max(sc, -1, keepdims=True)))
    p = e / jnp.sum(e, -1, keepdims=True)
    return ops.nn(p, vh)


def _mem_fwd(h, qcol, mem_kv, name):
    s = h.shape[0]
    m = mem_kv.shape[0]
    ts = _pick(s, (512, 256, 128))

    def body(q_ref, kv_ref, o_ref):
        for hh in range(N_MEM_HEADS):
            sl = slice(hh * HEAD_DIM, (hh + 1) * HEAD_DIM)
            vsl = slice(MEM_WIDTH + hh * HEAD_DIM, MEM_WIDTH + (hh + 1) * HEAD_DIM)
            o_ref[:, sl] = _mem_head(_RawOps, q_ref[:, sl], kv_ref[:, sl], kv_ref[:, vsl])

    return _pc(body, grid=(s // ts,), in_specs=[_rows(ts, MEM_WIDTH, qcol), _whole((m, 2 * MEM_WIDTH))],
               out_specs=_rows(ts, MEM_WIDTH), out_shape=_sds((s, MEM_WIDTH)), name=name)(h, mem_kv)


def _mem_bwd(h, qcol, mem_kv, dcat, name):
    s = h.shape[0]
    m = mem_kv.shape[0]
    ts = _pick(s, (512, 256, 128))

    def body(q_ref, kv_ref, do_ref, dq_ref, dkv_ref):
        @pl.when(pl.program_id(0) == 0)
        def _():
            dkv_ref[...] = jnp.zeros_like(dkv_ref)

        for hh in range(N_MEM_HEADS):
            sl = slice(hh * HEAD_DIM, (hh + 1) * HEAD_DIM)
            vsl = slice(MEM_WIDTH + hh * HEAD_DIM, MEM_WIDTH + (hh + 1) * HEAD_DIM)
            _, vjp = jax.vjp(functools.partial(_mem_head, _DiffOps), q_ref[:, sl], kv_ref[:, sl], kv_ref[:, vsl])
            dq, dk, dv = vjp(do_ref[:, sl])
            dq_ref[:, sl] = dq
            dkv_ref[:, sl] += dk
            dkv_ref[:, vsl] += dv

    return _pc(body, grid=(s // ts,),
               in_specs=[_rows(ts, MEM_WIDTH, qcol), _whole((m, 2 * MEM_WIDTH)), _rows(ts, MEM_WIDTH, 3)],
               out_specs=[_rows(ts, MEM_WIDTH), _whole((m, 2 * MEM_WIDTH))],
               out_shape=[_sds((s, MEM_WIDTH)), _sds((m, 2 * MEM_WIDTH))], name=name)(h, mem_kv, dcat)


CONV_COLS = 3 * MIX_WIDTH
HALO = 8


def _conv_fwd(h, w, name):
    s = h.shape[0]
    ts = _pick(s, (512, 256, 128))
    wc = 512

    def body(x_ref, halo_ref, w_ref, o_ref, ext):
        i = pl.program_id(0)
        ext[pl.ds(0, HALO), :] = jnp.where(i > 0, halo_ref[...], 0.0)
        ext[pl.ds(HALO, ts), :] = x_ref[...]
        acc = w_ref[0:1, :] * ext[pl.ds(HALO - 3, ts), :]
        for j in range(1, CONV_WIDTH):
            acc = acc + w_ref[j:j + 1, :] * ext[pl.ds(HALO - 3 + j, ts), :]
        o_ref[...] = acc

    halo = pl.BlockSpec((HALO, wc), lambda i, c: (jnp.maximum(i * (ts // HALO) - 1, 0), c))
    blk = pl.BlockSpec((ts, wc), lambda i, c: (i, c))
    return _pc(body, grid=(s // ts, CONV_COLS // wc),
               in_specs=[blk, halo, pl.BlockSpec((CONV_WIDTH, wc), lambda i, c: (0, c))],
               out_specs=blk, out_shape=_sds((s, CONV_COLS)), name=name,
               scratch=[pltpu.VMEM((HALO + ts, wc), F32)])(h, h, w)


def _conv_bwd(h, w, dout, name):
    s = h.shape[0]
    ts = _pick(s, (512, 256, 128))
    nt = s // ts
    wc = 512

    def body(x_ref, xhalo_ref, d_ref, dhalo_ref, w_ref, dx_ref, dw_ref, xext, dext):
        i = pl.program_id(1)

        @pl.when(i == 0)
        def _():
            dw_ref[...] = jnp.zeros_like(dw_ref)

        xext[pl.ds(0, HALO), :] = jnp.where(i > 0, xhalo_ref[...], 0.0)
        xext[pl.ds(HALO, ts), :] = x_ref[...]
        dext[pl.ds(0, ts), :] = d_ref[...]
        dext[pl.ds(ts, HALO), :] = jnp.where(i < nt - 1, dhalo_ref[...], 0.0)
        d = d_ref[...]
        acc = w_ref[CONV_WIDTH - 1:CONV_WIDTH, :] * d
        for j in range(CONV_WIDTH - 1):
            acc = acc + w_ref[j:j + 1, :] * dext[pl.ds(3 - j, ts), :]
        dx_ref[...] = acc
        for j in range(CONV_WIDTH):
            dw_ref[j:j + 1, :] += jnp.sum(d * xext[pl.ds(HALO - 3 + j, ts), :], 0, keepdims=True)

    blk = pl.BlockSpec((ts, wc), lambda c, i: (i, c))
    halo_prev = pl.BlockSpec((HALO, wc), lambda c, i: (jnp.maximum(i * (ts // HALO) - 1, 0), c))
    halo_next = pl.BlockSpec((HALO, wc), lambda c, i: (jnp.minimum((i + 1) * (ts // HALO), s // HALO - 1), c))
    wspec = pl.BlockSpec((CONV_WIDTH, wc), lambda c, i: (0, c))
    return _pc(body, grid=(CONV_COLS // wc, nt), in_specs=[blk, halo_prev, blk, halo_next, wspec],
               out_specs=[blk, wspec], out_shape=[_sds((s, CONV_COLS)), _sds((CONV_WIDTH, CONV_COLS))], name=name,
               scratch=[pltpu.VMEM((HALO + ts, wc), F32), pltpu.VMEM((ts + HALO, wc), F32)])(h, h, dout, dout, w)


def _silu(v):
    return v * jax.nn.sigmoid(v)


def _softplus(v):
    return jnp.maximum(v, 0.0) + jnp.log1p(jnp.exp(-jnp.abs(v)))


def _gdn_prep_head(cq, ck, cv, a, b, alog, dtb):
    q = _silu(cq)
    q = q * lax.rsqrt(jnp.sum(q * q, -1, keepdims=True) + 1e-6) * HEAD_DIM ** -0.5
    k = _silu(ck)
    k = k * lax.rsqrt(jnp.sum(k * k, -1, keepdims=True) + 1e-6)
    v = _silu(cv)
    g = -jnp.exp(alog) * _softplus(a + dtb)
    beta = jax.nn.sigmoid(b)
    ones = jnp.ones((1, HEAD_DIM), F32)
    return q, k, v, g * ones, beta * ones


def _onehot_lane(idx):
    return (lax.broadcasted_iota(jnp.int32, (1, LANES), 1) == idx).astype(F32)


def _lane_pick(row, idx):
    return jnp.sum(row * _onehot_lane(idx), -1, keepdims=True)


def _gdn_prep_fwd(conv, h, alog, dtb, name):
    s = h.shape[0]
    ts = _pick(s, (128,))
    w = MIX_WIDTH

    def body(c_ref, ab_ref, alog_ref, dtb_ref, q_ref, k_ref, v_ref, g_ref, b_ref):
        ab = ab_ref[...]
        for hh in range(N_MIX_HEADS):
            sl = slice(hh * HEAD_DIM, (hh + 1) * HEAD_DIM)
            cols = [slice(p * w + hh * HEAD_DIM, p * w + (hh + 1) * HEAD_DIM) for p in range(3)]
            outs = _gdn_prep_head(c_ref[:, cols[0]], c_ref[:, cols[1]], c_ref[:, cols[2]], _lane_pick(ab, hh),
                                  _lane_pick(ab, N_MIX_HEADS + hh), _lane_pick(alog_ref[...], hh),
                                  _lane_pick(dtb_ref[...], hh))
            for ref, val in zip((q_ref, k_ref, v_ref, g_ref, b_ref), outs):
                ref[:, sl] = val

    return _pc(body, grid=(s // ts,),
               in_specs=[_rows(ts, 3 * w), _rows(ts, LANES, (4 * w + MEM_WIDTH) // LANES), _whole((1, LANES)), _whole((1, LANES))],
               out_specs=[_rows(ts, w)] * 5, out_shape=[_sds((s, w))] * 5, name=name)(conv, h, alog, dtb)


def _gdn_prep_bwd(conv, h, alog, dtb, dq, dk, dv, dg, db, name):
    s = h.shape[0]
    ts = _pick(s, (128,))
    w = MIX_WIDTH

    def body(c_ref, ab_ref, alog_ref, dtb_ref, dq_ref, dk_ref, dv_ref, dg_ref, db_ref,
             dc_ref, dab_ref, dalog_ref, ddtb_ref):
        @pl.when(pl.program_id(0) == 0)
        def _():
            dalog_ref[...] = jnp.zeros_like(dalog_ref)
            ddtb_ref[...] = jnp.zeros_like(ddtb_ref)

        ab = ab_ref[...]
        dab = jnp.zeros_like(ab)
        dalog = jnp.zeros((1, LANES), F32)
        ddtb = jnp.zeros((1, LANES), F32)
        for hh in range(N_MIX_HEADS):
            sl = slice(hh * HEAD_DIM, (hh + 1) * HEAD_DIM)
            cols = [slice(p * w + hh * HEAD_DIM, p * w + (hh + 1) * HEAD_DIM) for p in range(3)]
            _, vjp = jax.vjp(_gdn_prep_head, c_ref[:, cols[0]], c_ref[:, cols[1]], c_ref[:, cols[2]], _lane_pick(ab, hh),
                             _lane_pick(ab, N_MIX_HEADS + hh), _lane_pick(alog_ref[...], hh), _lane_pick(dtb_ref[...], hh))
            dcq, dck, dcv, da, dbb, dal, ddt = vjp((dq_ref[:, sl], dk_ref[:, sl], dv_ref[:, sl], dg_ref[:, sl], db_ref[:, sl]))
            dc_ref[:, cols[0]] = dcq
            dc_ref[:, cols[1]] = dck
            dc_ref[:, cols[2]] = dcv
            dab = dab + da * _onehot_lane(hh) + dbb * _onehot_lane(N_MIX_HEADS + hh)
            dalog = dalog + dal * _onehot_lane(hh)
            ddtb = ddtb + ddt * _onehot_lane(hh)
        dab_ref[...] = dab
        dalog_ref[...] += dalog
        ddtb_ref[...] += ddtb

    return _pc(body, grid=(s // ts,),
               in_specs=[_rows(ts, 3 * w), _rows(ts, LANES, (4 * w + MEM_WIDTH) // LANES),
                         _whole((1, LANES)), _whole((1, LANES))] + [_rows(ts, w)] * 5,
               out_specs=[_rows(ts, 3 * w), _rows(ts, LANES), _whole((1, LANES)), _whole((1, LANES))],
               out_shape=[_sds((s, 3 * w)), _sds((s, LANES)), _sds((1, LANES)), _sds((1, LANES))],
               name=name)(conv, h, alog, dtb, dq, dk, dv, dg, db)


NNB = (((2,), (1,)), ((0,), (0,)))
NTB = (((2,), (2,)), ((0,), (0,)))
TNB = (((1,), (1,)), ((0,), (0,)))


def _dg(a, b, dims):
    return lax.dot_general(a, b, dims, preferred_element_type=F32)


def _dotb(a, b, dims, mode):
    a1 = a.astype(MXU_DTYPE)
    b1 = b.astype(MXU_DTYPE)
    if mode == 1:
        return _dg(a1, b1, dims)
    rb = b - b1.astype(F32)
    b2 = rb.astype(MXU_DTYPE)
    if mode == 3:
        a2 = (a - a1.astype(F32)).astype(MXU_DTYPE)
        return _dg(a1, b1, dims) + (_dg(a1, b2, dims) + _dg(a2, b1, dims))
    b3 = (rb - b2.astype(F32)).astype(MXU_DTYPE)
    return _dg(a1, b1, dims) + (_dg(a1, b2, dims) + _dg(a1, b3, dims))


@functools.partial(jax.custom_vjp, nondiff_argnums=(2,))
def _nnb(a, b, mode):
    return _dotb(a, b, NNB, mode)


@functools.partial(jax.custom_vjp, nondiff_argnums=(2,))
def _ntb(a, b, mode):
    return _dotb(a, b, NTB, mode)


@functools.partial(jax.custom_vjp, nondiff_argnums=(2,))
def _tnb(a, b, mode):
    return _dotb(a, b, TNB, mode)


_nnb.defvjp(lambda a, b, mode: (_dotb(a, b, NNB, mode), (a, b)),
            lambda mode, r, g: (_ntb(g, r[1], mode), _tnb(r[0], g, mode)))
_ntb.defvjp(lambda a, b, mode: (_dotb(a, b, NTB, mode), (a, b)),
            lambda mode, r, g: (_nnb(g, r[1], mode), _tnb(g, r[0], mode)))
_tnb.defvjp(lambda a, b, mode: (_dotb(a, b, TNB, mode), (a, b)),
            lambda mode, r, g: (_ntb(r[1], g, mode), _nnb(r[0], g, mode)))


def _trilb(h):
    t = lax.broadcasted_iota(jnp.int32, (CHUNK, CHUNK), 0) >= lax.broadcasted_iota(jnp.int32, (CHUNK, CHUNK), 1)
    return jnp.broadcast_to(t.astype(F32)[None], (h, CHUNK, CHUNK))


def _cumsum_rows_raw(gb):
    return _dotb(_trilb(gb.shape[0]), gb, NNB, "lhs")


_cumsum_rows = jax.custom_vjp(_cumsum_rows_raw)
_cumsum_rows.defvjp(lambda gb: (_cumsum_rows_raw(gb), None),
                    lambda _, g: (_dotb(_trilb(g.shape[0]), g, TNB, "lhs"),))


def _row_col_raw(gc):
    return gc[:, :, :CHUNK], jnp.swapaxes(gc, 1, 2)[:, :CHUNK, :]


def _row_col_bwd(_, g):
    part = g[0] + jnp.swapaxes(g[1], 1, 2)
    return (jnp.concatenate([part, jnp.zeros_like(part)], -1),)


_row_col = jax.custom_vjp(_row_col_raw)
_row_col.defvjp(lambda gc: (_row_col_raw(gc), None), _row_col_bwd)


def _gdn_chunk(diff, state, q, k, v, gb, bb):
    if diff:
        nn, nt, tn, cumsum, row_col = _nnb, _ntb, _tnb, _cumsum_rows, _row_col
    else:
        nn = lambda a, b, m: _dotb(a, b, NNB, m)
        nt = lambda a, b, m: _dotb(a, b, NTB, m)
        tn = lambda a, b, m: _dotb(a, b, TNB, m)
        cumsum, row_col = _cumsum_rows_raw, _row_col_raw
    c = CHUNK
    row = lax.broadcasted_iota(jnp.int32, (1, c, c), 1)
    col = lax.broadcasted_iota(jnp.int32, (1, c, c), 2)
    tril = row >= col
    strict = row > col
    eye = (row == col).astype(F32)
    gc = cumsum(gb)
    gci, gcj = row_col(gc)
    decay = jnp.where(tril, jnp.exp(jnp.where(tril, gci - gcj, 0.0)), 0.0)
    kb = k * bb
    low = jnp.where(strict, nt(kb, k, 1) * decay, 0.0)
    inv = eye - low
    pw = low
    for _ in range(5):
        pw = nn(pw, pw, 3)
        inv = inv + nn(inv, pw, 3)
    ge = jnp.exp(gc)
    u = nn(inv, v * bb, 3)
    w = nn(inv, kb * ge, 3)
    a_qk = jnp.where(tril, nt(q, k, 1) * decay, 0.0)
    g_tot = jnp.sum(gb, 1, keepdims=True)
    k_dec = k * jnp.exp(g_tot - gc)
    v_new = u - nn(w, state, 1)
    o = nn(q * ge, state, 1) + nn(a_qk, v_new, 1)
    new_state = state * jnp.exp(g_tot) + tn(k_dec, v_new, 1)
    return new_state, o


def _stack_heads(ref, hp):
    return jnp.stack([ref[:, hh * HEAD_DIM:(hh + 1) * HEAD_DIM] for hh in range(hp)])


GDN_HEADS_FWD = 12
GDN_HEADS_BWD = 12


def _gdn_core_fwd(q, k, v, gb, bb, name):
    s = q.shape[0]
    nc = s // CHUNK
    hp = GDN_HEADS_FWD
    w = hp * HEAD_DIM

    def body(q_ref, k_ref, v_ref, g_ref, b_ref, o_ref, st_ref, state):
        @pl.when(pl.program_id(1) == 0)
        def _():
            state[...] = jnp.zeros_like(state)

        st = state[...]
        st_ref[0] = st
        new_state, o = _gdn_chunk(False, st, *(_stack_heads(r, hp) for r in (q_ref, k_ref, v_ref, g_ref, b_ref)))
        state[...] = new_state
        for hh in range(hp):
            o_ref[:, hh * HEAD_DIM:(hh + 1) * HEAD_DIM] = o[hh]

    blk = pl.BlockSpec((CHUNK, w), lambda hg, c: (c, hg))
    return _pc(body, grid=(N_MIX_HEADS // hp, nc), in_specs=[blk] * 5,
               out_specs=[blk, pl.BlockSpec((1, hp, HEAD_DIM, HEAD_DIM), lambda hg, c: (c, hg, 0, 0))],
               out_shape=[_sds((s, MIX_WIDTH)), _sds((nc, N_MIX_HEADS, HEAD_DIM, HEAD_DIM))], name=name,
               scratch=[pltpu.VMEM((hp, HEAD_DIM, HEAD_DIM), F32)])(q, k, v, gb, bb)


def _gdn_core_bwd(q, k, v, gb, bb, states, do, name):
    s = q.shape[0]
    nc = s // CHUNK
    hp = GDN_HEADS_BWD
    w = hp * HEAD_DIM

    def body(q_ref, k_ref, v_ref, g_ref, b_ref, st_ref, do_ref, dq_ref, dk_ref, dv_ref, dg_ref, db_ref, dstate):
        @pl.when(pl.program_id(1) == 0)
        def _():
            dstate[...] = jnp.zeros_like(dstate)

        _, vjp = jax.vjp(functools.partial(_gdn_chunk, True), st_ref[0],
                         *(_stack_heads(r, hp) for r in (q_ref, k_ref, v_ref, g_ref, b_ref)))
        grads = vjp((dstate[...], _stack_heads(do_ref, hp)))
        dstate[...] = grads[0]
        for ref, val in zip((dq_ref, dk_ref, dv_ref, dg_ref, db_ref), grads[1:]):
            for hh in range(hp):
                ref[:, hh * HEAD_DIM:(hh + 1) * HEAD_DIM] = val[hh]

    blk = pl.BlockSpec((CHUNK, w), lambda hg, c: (nc - 1 - c, hg))
    return _pc(body, grid=(N_MIX_HEADS // hp, nc),
               in_specs=[blk] * 5 + [pl.BlockSpec((1, hp, HEAD_DIM, HEAD_DIM), lambda hg, c: (nc - 1 - c, hg, 0, 0)), blk],
               out_specs=[blk] * 5, out_shape=[_sds((s, MIX_WIDTH))] * 5, name=name,
               scratch=[pltpu.VMEM((hp, HEAD_DIM, HEAD_DIM), F32)])(q, k, v, gb, bb, states, do)


def _gdn_out_head(o, z, g):
    return _rms(o, g) * _silu(z)


def _gdn_out_fwd(o, h, onorm, name):
    s = o.shape[0]
    ts = _pick(s, (256, 128))

    def body(o_ref, z_ref, g_ref, y_ref):
        for hh in range(N_MIX_HEADS):
            sl = slice(hh * HEAD_DIM, (hh + 1) * HEAD_DIM)
            y_ref[:, sl] = _gdn_out_head(o_ref[:, sl], z_ref[:, sl], g_ref[...])

    return _pc(body, grid=(s // ts,), in_specs=[_rows(ts, MIX_WIDTH), _rows(ts, MIX_WIDTH, 3), _whole((1, HEAD_DIM))],
               out_specs=_rows(ts, MIX_WIDTH), out_shape=_sds((s, MIX_WIDTH)), name=name)(o, h, onorm)


def _gdn_out_bwd(o, h, onorm, dcat, name):
    s = o.shape[0]
    ts = _pick(s, (256, 128))

    def body(o_ref, z_ref, g_ref, dy_ref, do_ref, dz_ref, dg_ref):
        @pl.when(pl.program_id(0) == 0)
        def _():
            dg_ref[...] = jnp.zeros_like(dg_ref)

        dgs = jnp.zeros((1, HEAD_DIM), F32)
        for hh in range(N_MIX_HEADS):
            sl = slice(hh * HEAD_DIM, (hh + 1) * HEAD_DIM)
            _, vjp = jax.vjp(_gdn_out_head, o_ref[:, sl], z_ref[:, sl], g_ref[...])
            do, dz, dg = vjp(dy_ref[:, sl])
            do_ref[:, sl] = do
            dz_ref[:, sl] = dz
            dgs = dgs + dg
        dg_ref[...] += dgs

    return _pc(body, grid=(s // ts,),
               in_specs=[_rows(ts, MIX_WIDTH), _rows(ts, MIX_WIDTH, 3), _whole((1, HEAD_DIM)), _rows(ts, MIX_WIDTH, 0)],
               out_specs=[_rows(ts, MIX_WIDTH), _rows(ts, MIX_WIDTH), _whole((1, HEAD_DIM))],
               out_shape=[_sds((s, MIX_WIDTH)), _sds((s, MIX_WIDTH)), _sds((1, HEAD_DIM))], name=name)(o, h, onorm, dcat)


def _row(v):
    return v.reshape(1, -1)


def _lane_row(v):
    return jnp.pad(v, (0, LANES - v.shape[0])).reshape(1, LANES)


def _rope_tables(positions):
    inv_freq = 1.0 / (ROPE_THETA ** (jnp.arange(0, QK_ROPE, 2, dtype=F32) / QK_ROPE))
    ang = positions.astype(F32)[:, None] * inv_freq
    cos, sin = jnp.cos(ang), jnp.sin(ang)
    z = jnp.zeros_like(cos)
    return jnp.concatenate([cos, z, cos, z], 1), jnp.concatenate([-sin, z, sin, z], 1)


def _local_step(x, mem, positions, loss_target, wts, small):
    cs, sn = _rope_tables(positions)
    saved = []
    for i in range(DEPTH):
        j = i // 2
        sv = {"x": x}
        sv["mem_kv"] = _mm(mem, wts["mem_w_kv"][i], name=f"l{i}_memkv")
        if i % 2 == 0:
            h = _mm(x, wts["mla_w_in"][j], name=f"l{i}_in")
            cqn, ckvn, krr = _mla_prep_fwd(h, _row(small["mla_q_norm"][j]), _row(small["mla_kv_norm"][j]), cs, sn, f"l{i}_mlaprep")
            q = _mm(cqn, wts["mla_w_uq"][j], name=f"l{i}_uq")
            kv = _mm(ckvn, wts["mla_w_ukv"][j], out_dtype=MXU_DTYPE, name=f"l{i}_ukv")
            mix, lse = _flash_fwd(q, kv, krr, cs, sn, f"l{i}_flash")
            sv.update(cqn=cqn, ckvn=ckvn, krr=krr, q=q, kv=kv, o=mix, lse=lse)
            qcol = 2
        else:
            h = _mm(x, wts["gdn_w_in"][j], name=f"l{i}_in")
            conv = _conv_fwd(h, small["gdn_conv"][j], f"l{i}_conv")
            qn, kn, vv, gb, bb = _gdn_prep_fwd(conv, h, _lane_row(small["gdn_a_log"][j]), _lane_row(small["gdn_dt_bias"][j]), f"l{i}_gdnprep")
            o, states = _gdn_core_fwd(qn, kn, vv, gb, bb, f"l{i}_gdncore")
            mix = _gdn_out_fwd(o, h, _row(small["gdn_o_norm"][j]), f"l{i}_gdnout")
            sv.update(conv=conv, qn=qn, kn=kn, vv=vv, gb=gb, bb=bb, o=o, states=states)
            qcol = 4 * MIX_WIDTH // MEM_WIDTH
        mem_o = _mem_fwd(h, qcol, sv["mem_kv"], f"l{i}_mem")
        cat = jnp.concatenate([mix, mem_o], 1)
        y = _mm(cat, wts["w_out"][i], name=f"l{i}_out")
        x1 = _ln_fwd(x, y, _row(small["ln1_g"][i]), _row(small["ln1_b"][i]), f"l{i}_ln1")
        h1 = _mm(x1, wts["mlp_w1"][i], name=f"l{i}_w1")
        ff = _mm(h1, wts["mlp_w2"][i], a_fn=_relu2, name=f"l{i}_w2")
        x2 = _ln_fwd(x1, ff, _row(small["ln2_g"][i]), _row(small["ln2_b"][i]), f"l{i}_ln2")
        sv.update(h=h, qcol=qcol, cat=cat, y=y, x1=x1, h1=h1, ff=ff)
        saved.append(sv)
        x = x2

    loss_row, dx = _loss_and_grad(x, loss_target, "loss")

    gw = {k: [None] * len(v) for k, v in wts.items()}
    gs = {k: [None] * v.shape[0] for k, v in small.items()}
    gdt = COMM_DTYPE
    for i in reversed(range(DEPTH)):
        j = i // 2
        sv = saved[i]
        dr2, dg, db = _ln_bwd(sv["x1"], sv["ff"], _row(small["ln2_g"][i]), _row(small["ln2_b"][i]), dx, f"l{i}_ln2b")
        gs["ln2_g"][i], gs["ln2_b"][i] = dg[0], db[0]
        dh1 = _mm(dr2, wts["mlp_w2"][i], tb=True, epi=_relu2_bwd, e=sv["h1"], name=f"l{i}_dh1")
        gw["mlp_w2"][i] = _mm(sv["h1"], dr2, ta=True, a_fn=_relu2, out_dtype=gdt, name=f"l{i}_dw2")
        gw["mlp_w1"][i] = _mm(sv["x1"], dh1, ta=True, out_dtype=gdt, name=f"l{i}_dw1")
        dx1 = _mm(dh1, wts["mlp_w1"][i], tb=True, epi=_add_alpha, e=dr2, name=f"l{i}_dx1")
        dr1, dg, db = _ln_bwd(sv["x"], sv["y"], _row(small["ln1_g"][i]), _row(small["ln1_b"][i]), dx1, f"l{i}_ln1b")
        gs["ln1_g"][i], gs["ln1_b"][i] = dg[0], db[0]
        gw["w_out"][i] = _mm(sv["cat"], dr1, ta=True, out_dtype=gdt, name=f"l{i}_dwout")
        dcat = _mm(dr1, wts["w_out"][i], tb=True, name=f"l{i}_dcat")
        h = sv["h"]
        dqmem, dmem_kv = _mem_bwd(h, sv["qcol"], sv["mem_kv"], dcat, f"l{i}_memb")
        gw["mem_w_kv"][i] = _mm(mem, dmem_kv, ta=True, out_dtype=gdt, name=f"l{i}_dwmem")
        zpad = jnp.zeros((h.shape[0], LANES), F32)
        if i % 2 == 0:
            qnw, kvnw = _row(small["mla_q_norm"][j]), _row(small["mla_kv_norm"][j])
            dq = _flash_bwd_q(sv["q"], sv["kv"], sv["krr"], cs, sn, sv["o"], sv["lse"], dcat, f"l{i}_flashbq")
            dkv, dkrr = _flash_bwd_kv(sv["q"], sv["kv"], sv["krr"], cs, sn, sv["o"], sv["lse"], dcat, f"l{i}_flashbkv")
            gw["mla_w_ukv"][j] = _mm(sv["ckvn"], dkv, ta=True, out_dtype=gdt, name=f"l{i}_dwukv")
            dckvn = _mm(dkv, wts["mla_w_ukv"][j], tb=True, name=f"l{i}_dckvn")
            gw["mla_w_uq"][j] = _mm(sv["cqn"], dq, ta=True, out_dtype=gdt, name=f"l{i}_dwuq")
            dcqn = _mm(dq, wts["mla_w_uq"][j], tb=True, name=f"l{i}_dcqn")
            dcq, dckv, dkr, dqn, dkvn = _mla_prep_bwd(h, qnw, kvnw, cs, sn, dcqn, dckvn, dkrr, f"l{i}_mlaprepb")
            gs["mla_q_norm"][j], gs["mla_kv_norm"][j] = dqn[0], dkvn[0]
            dh = jnp.concatenate([dcq, dckv, dqmem, dkr, zpad], 1)
            w_in = wts["mla_w_in"][j]
            key = "mla_w_in"
        else:
            alog, dtb = _lane_row(small["gdn_a_log"][j]), _lane_row(small["gdn_dt_bias"][j])
            do, dz, dgn = _gdn_out_bwd(sv["o"], h, _row(small["gdn_o_norm"][j]), dcat, f"l{i}_gdnoutb")
            gs["gdn_o_norm"][j] = dgn[0]
            dqn, dkn, dvv, dgb, dbb = _gdn_core_bwd(sv["qn"], sv["kn"], sv["vv"], sv["gb"], sv["bb"], sv["states"], do, f"l{i}_gdncoreb")
            dconv, dab, dalog, ddtb = _gdn_prep_bwd(sv["conv"], h, alog, dtb, dqn, dkn, dvv, dgb, dbb, f"l{i}_gdnprepb")
            gs["gdn_a_log"][j], gs["gdn_dt_bias"][j] = dalog[0, :N_MIX_HEADS], ddtb[0, :N_MIX_HEADS]
            dhqkv, dconvw = _conv_bwd(h, small["gdn_conv"][j], dconv, f"l{i}_convb")
            gs["gdn_conv"][j] = dconvw
            dh = jnp.concatenate([dhqkv, dz, dqmem, dab, zpad], 1)
            w_in = wts["gdn_w_in"][j]
            key = "gdn_w_in"
        gw[key][j] = _mm(sv["x"], dh, ta=True, out_dtype=gdt, name=f"l{i}_dwin")
        dx = _mm(dh, w_in, tb=True, epi=_add_alpha, e=dr1, name=f"l{i}_dx")
    gs = {k: jnp.stack(v) for k, v in gs.items()}
    return loss_row, dx, gw, gs


def _mla_in_to_kernel(w):
    z32 = jnp.zeros((w.shape[0], 32), w.dtype)
    z128 = jnp.zeros((w.shape[0], LANES), w.dtype)
    return jnp.concatenate([w[:, :1024], w[:, 1088:1600], w[:, 1024:1056], z32, w[:, 1056:1088], z32, z128], 1)


def _mla_in_from_kernel(g):
    return jnp.concatenate([g[:, :1024], g[:, 1536:1568], g[:, 1600:1632], g[:, 1024:1536]], 1)


def _uq_to_kernel(w):
    w3 = w.reshape(Q_LORA, N_MIX_HEADS, QK_NOPE + QK_ROPE)
    z = jnp.zeros((Q_LORA, N_MIX_HEADS, 32), w.dtype)
    return jnp.concatenate([w3[:, :, :128], w3[:, :, 128:160], z, w3[:, :, 160:192], z], 2).reshape(Q_LORA, N_MIX_HEADS * Q_HEAD_P)


def _uq_from_kernel(g):
    g3 = g.reshape(Q_LORA, N_MIX_HEADS, Q_HEAD_P)
    return jnp.concatenate([g3[:, :, :128], g3[:, :, 128:160], g3[:, :, 192:224]], 2).reshape(Q_LORA, -1)


def _gdn_in_to_kernel(w):
    z = jnp.zeros((w.shape[0], LANES - 2 * N_MIX_HEADS + LANES), w.dtype)
    return jnp.concatenate([w[:, :6144], w[:, 6168:6680], w[:, 6144:6168], z], 1)


def _gdn_in_from_kernel(g):
    return jnp.concatenate([g[:, :6144], g[:, 6656:6680], g[:, 6144:6656]], 1)


def _exchange(srcs, name, gather):
    n = len(srcs)

    def body(*refs):
        src_refs, out_refs = refs[:n], refs[n:2 * n]
        send_sems, recv_sems, local_sems = refs[2 * n:]
        x, y, c = lax.axis_index("x"), lax.axis_index("y"), lax.axis_index("c")
        me = 4 * x + 2 * y + c
        local, sends, recvs = [], [], []
        for a in range(n):
            src_ref, out_ref = src_refs[a], out_refs[a]
            local.append(pltpu.make_async_copy(src_ref if gather else src_ref.at[me], out_ref.at[me], local_sems.at[a]))
            for kk in range(1, N_DEV):
                px, py, pc = x ^ ((kk >> 2) & 1), y ^ ((kk >> 1) & 1), c ^ (kk & 1)
                peer = 4 * px + 2 * py + pc
                piece = src_ref if gather else src_ref.at[peer]
                sends.append(pltpu.make_async_remote_copy(
                    src_ref=piece, dst_ref=out_ref.at[me], send_sem=send_sems.at[a, kk - 1], recv_sem=recv_sems.at[a, kk - 1],
                    device_id=(px, py, pc), device_id_type=pl.DeviceIdType.MESH))
                recvs.append(pltpu.make_async_remote_copy(
                    src_ref=piece, dst_ref=out_ref.at[peer], send_sem=send_sems.at[a, kk - 1], recv_sem=recv_sems.at[a, kk - 1],
                    device_id=(px, py, pc), device_id_type=pl.DeviceIdType.MESH))
        for cp in local + sends:
            cp.start()
        for cp in recvs:
            cp.wait_recv()
        for cp in sends:
            cp.wait_send()
        for cp in local:
            cp.wait()

    hbm = pl.BlockSpec(memory_space=pltpu.HBM)
    shapes = [_sds((N_DEV,) + tuple(s.shape if gather else s.shape[1:]), s.dtype) for s in srcs]
    return pl.pallas_call(
        body, in_specs=[hbm] * n, out_specs=[hbm] * n, out_shape=shapes, name=name,
        scratch_shapes=[pltpu.SemaphoreType.DMA((n, N_DEV - 1)), pltpu.SemaphoreType.DMA((n, N_DEV - 1)),
                        pltpu.SemaphoreType.DMA((n,))],
    )(*srcs)


def _adamw(parts, w, m, v, name):
    r, c = w.shape
    n_parts = parts.shape[0]
    tr = _pick(r, tuple(t for t in (256, 128, 64, 32, 16, 8) if t * c <= 256 * 1024))

    def body(p_ref, w_ref, m_ref, v_ref, g_ref, d_ref, nm_ref, nv_ref):
        g = p_ref[0].astype(F32)
        for dd in range(1, n_parts):
            g = g + p_ref[dd].astype(F32)
        nm = ADAM_B1 * m_ref[...] + (1.0 - ADAM_B1) * g
        nv = ADAM_B2 * v_ref[...] + (1.0 - ADAM_B2) * jnp.square(g)
        m_hat = nm / (1.0 - ADAM_B1 ** ADAM_STEP)
        v_hat = nv / (1.0 - ADAM_B2 ** ADAM_STEP)
        g_ref[...] = g
        d_ref[...] = -ADAM_LR * (m_hat / (jnp.sqrt(v_hat) + ADAM_EPS) + ADAM_WD * w_ref[...])
        nm_ref[...] = nm
        nv_ref[...] = nv

    blk = pl.BlockSpec((tr, c), lambda i: (i, 0))
    return _pc(body, grid=(r // tr,), in_specs=[pl.BlockSpec((n_parts, tr, c), lambda i: (0, i, 0)), blk, blk, blk],
               out_specs=[blk] * 4, out_shape=[_sds((r, c))] * 4, name=name)(parts, w, m, v)


N_CHIPS = 4
MESH_IDS = pl.DeviceIdType.MESH


def _place():
    x, y, c = lax.axis_index("x"), lax.axis_index("y"), lax.axis_index("c")
    return x, y, c, [(1 - x, y), (x, 1 - y), (1 - x, 1 - y)]


def _gather_two_level(srcs, name):
    n = len(srcs)

    def body(*refs):
        src_refs, out_refs = refs[:n], refs[n:2 * n]
        send_sems, recv_sems, local_sems = refs[2 * n:]
        x, y, c, chips = _place()
        sib = (x, y, 1 - c)

        def copy(a, k, block, to, src=None):
            rows = out_refs[a].at[4 * block[0] + 2 * block[1] + block[2]]
            return pltpu.make_async_remote_copy(src_ref=rows if src is None else src, dst_ref=rows,
                                                send_sem=send_sems.at[a, k], recv_sem=recv_sems.at[a, k],
                                                device_id=to, device_id_type=MESH_IDS)

        local = [pltpu.make_async_copy(src_refs[a], out_refs[a].at[4 * x + 2 * y + c], local_sems.at[a]) for a in range(n)]
        first = []
        for a in range(n):
            first.append(copy(a, 0, (x, y, c), sib, src_refs[a]))
            first += [copy(a, 1 + j, (x, y, c), (*chip, c), src_refs[a]) for j, chip in enumerate(chips)]
        for cp in local + first:
            cp.start()
        passed = []
        for a in range(n):
            for j, chip in enumerate(chips):
                copy(a, 1 + j, (*chip, c), (x, y, c)).wait_recv()
                passed.append(copy(a, 4 + j, (*chip, c), sib))
                passed[-1].start()
        for a in range(n):
            copy(a, 0, (x, y, 1 - c), (x, y, c)).wait_recv()
            for j, chip in enumerate(chips):
                copy(a, 4 + j, (*chip, 1 - c), (x, y, c)).wait_recv()
        for cp in first + passed:
            cp.wait_send()
        for cp in local:
            cp.wait()

    hbm = pl.BlockSpec(memory_space=pltpu.HBM)
    return pl.pallas_call(
        body, in_specs=[hbm] * n, out_specs=[hbm] * n, name=name,
        out_shape=[_sds((N_DEV,) + tuple(s.shape), s.dtype) for s in srcs],
        scratch_shapes=[pltpu.SemaphoreType.DMA((n, 7)), pltpu.SemaphoreType.DMA((n, 7)), pltpu.SemaphoreType.DMA((n,))],
    )(*srcs)


def _pair_exchange(srcs, name):
    n = len(srcs)

    def body(*refs):
        src_refs, out_refs = refs[:n], refs[n:2 * n]
        send_sems, recv_sems = refs[2 * n:]
        x, y, c, _ = _place()
        sends = []
        for a in range(n):
            for ch in range(N_CHIPS):
                sends.append(pltpu.make_async_remote_copy(
                    src_ref=src_refs[a].at[2 * ch + (1 - c)], dst_ref=out_refs[a].at[ch],
                    send_sem=send_sems.at[a, ch], recv_sem=recv_sems.at[a, ch], device_id=(x, y, 1 - c),
                    device_id_type=MESH_IDS))
        for cp in sends:
            cp.start()
        for cp in sends:
            cp.wait_recv()
        for cp in sends:
            cp.wait_send()

    hbm = pl.BlockSpec(memory_space=pltpu.HBM)
    return pl.pallas_call(
        body, in_specs=[hbm] * n, out_specs=[hbm] * n, name=name,
        out_shape=[_sds((N_CHIPS,) + tuple(s.shape[1:]), s.dtype) for s in srcs],
        scratch_shapes=[pltpu.SemaphoreType.DMA((n, N_CHIPS)), pltpu.SemaphoreType.DMA((n, N_CHIPS))],
    )(*srcs)


def _pair_sum(mine, theirs, my_c, name):
    _, r, c = mine.shape
    tr = _pick(r, tuple(t for t in (512, 256, 128, 64, 32, 16, 8) if t * c <= 512 * 1024))

    def body(c_ref, a_ref, b_ref, o_ref):
        o_ref[...] = (a_ref[...].astype(F32) + b_ref[...].astype(F32)).astype(o_ref.dtype)

    return pl.pallas_call(
        body, out_shape=_sds((N_CHIPS, r, c), mine.dtype), name=name,
        grid_spec=pltpu.PrefetchScalarGridSpec(
            num_scalar_prefetch=1, grid=(N_CHIPS, r // tr),
            in_specs=[pl.BlockSpec((1, tr, c), lambda ch, i, cc: (2 * ch + cc[0], i, 0)),
                      pl.BlockSpec((1, tr, c), lambda ch, i, cc: (ch, i, 0))],
            out_specs=pl.BlockSpec((1, tr, c), lambda ch, i, cc: (ch, i, 0))),
        compiler_params=pltpu.CompilerParams(dimension_semantics=("arbitrary", "arbitrary"), vmem_limit_bytes=VMEM_LIMIT),
    )(my_c, mine, theirs)


def _chip_exchange(srcs, name):
    n = len(srcs)

    def body(*refs):
        src_refs, out_refs = refs[:n], refs[n:2 * n]
        send_sems, recv_sems, local_sems = refs[2 * n:]
        x, y, c, chips = _place()
        my_chip = 2 * x + y
        local = [pltpu.make_async_copy(src_refs[a].at[my_chip], out_refs[a].at[my_chip], local_sems.at[a]) for a in range(n)]
        sends, recvs = [], []
        for a in range(n):
            for j, chip in enumerate(chips):
                ch = 2 * chip[0] + chip[1]
                sends.append(pltpu.make_async_remote_copy(
                    src_ref=src_refs[a].at[ch], dst_ref=out_refs[a].at[my_chip], send_sem=send_sems.at[a, j],
                    recv_sem=recv_sems.at[a, j], device_id=(*chip, c), device_id_type=MESH_IDS))
                recvs.append(pltpu.make_async_remote_copy(
                    src_ref=src_refs[a].at[ch], dst_ref=out_refs[a].at[ch], send_sem=send_sems.at[a, j],
                    recv_sem=recv_sems.at[a, j], device_id=(*chip, c), device_id_type=MESH_IDS))
        for cp in local + sends:
            cp.start()
        for cp in recvs:
            cp.wait_recv()
        for cp in sends:
            cp.wait_send()
        for cp in local:
            cp.wait()

    hbm = pl.BlockSpec(memory_space=pltpu.HBM)
    return pl.pallas_call(
        body, in_specs=[hbm] * n, out_specs=[hbm] * n, name=name,
        out_shape=[_sds(tuple(s.shape), s.dtype) for s in srcs],
        scratch_shapes=[pltpu.SemaphoreType.DMA((n, 3)), pltpu.SemaphoreType.DMA((n, 3)), pltpu.SemaphoreType.DMA((n,))],
    )(*srcs)


BIG = (("mla_w_in", 1), ("mla_w_uq", 2), ("mla_w_ukv", 2), ("gdn_w_in", 2), ("mem_w_kv", 1), ("w_out", 1),
       ("mlp_w1", 2), ("mlp_w2", 1), ("gdn_conv", 2))
SMALL = ("mla_q_norm", "mla_kv_norm", "gdn_a_log", "gdn_dt_bias", "gdn_o_norm", "ln1_g", "ln1_b", "ln2_g", "ln2_b")
PACK_COLS = 1024


def _unshard(pieces, axis):
    t = jnp.moveaxis(pieces, 0, axis)
    return t.reshape(t.shape[:axis] + (t.shape[axis] * t.shape[axis + 1],) + t.shape[axis + 2:])


def _shard(full, axis):
    t = full.reshape(full.shape[:axis] + (N_DEV, full.shape[axis] // N_DEV) + full.shape[axis + 1:])
    return jnp.moveaxis(t, axis, 0)


def _pack(arrs, lead, cols, mult):
    lead_shape = arrs[0].shape[:lead]
    flat = jnp.concatenate([a.reshape(lead_shape + (-1,)) for a in arrs], -1)
    n = flat.shape[-1]
    r = -(-n // (cols * mult)) * mult
    flat = jnp.pad(flat, [(0, 0)] * lead + [(0, r * cols - n)])
    return flat.reshape(lead_shape + (r, cols))


def _unpack(buf, lead, shapes):
    lead_shape = buf.shape[:lead]
    flat = buf.reshape(lead_shape + (-1,))
    out, off = [], 0
    for shp in shapes:
        n = 1
        for d in shp:
            n *= d
        out.append(flat[..., off:off + n].reshape(lead_shape + tuple(shp)))
        off += n
    return out


def kernel(x, mem, positions, mla_w_in, mla_q_norm, mla_w_uq, mla_kv_norm, mla_w_ukv, gdn_w_in, gdn_conv, gdn_a_log, gdn_dt_bias, gdn_o_norm, mem_w_kv, w_out, ln1_g, ln1_b, mlp_w1, mlp_w2, ln2_g, ln2_b, loss_target, m_mla_w_in, m_mla_q_norm, m_mla_w_uq, m_mla_kv_norm, m_mla_w_ukv, m_gdn_w_in, m_gdn_conv, m_gdn_a_log, m_gdn_dt_bias, m_gdn_o_norm, m_mem_w_kv, m_w_out, m_ln1_g, m_ln1_b, m_mlp_w1, m_mlp_w2, m_ln2_g, m_ln2_b, v_mla_w_in, v_mla_q_norm, v_mla_w_uq, v_mla_kv_norm, v_mla_w_ukv, v_gdn_w_in, v_gdn_conv, v_gdn_a_log, v_gdn_dt_bias, v_gdn_o_norm, v_mem_w_kv, v_w_out, v_ln1_g, v_ln1_b, v_mlp_w1, v_mlp_w2, v_ln2_g, v_ln2_b):
    args = dict(locals())
    w_loc = {n: args[n] for n, _ in BIG}
    m_loc = {n: args["m_" + n] for n, _ in BIG}
    v_loc = {n: args["v_" + n] for n, _ in BIG}
    small = {n: args[n] for n in SMALL}
    axis_of = dict(BIG)
    mm_names = [n for n, _ in BIG if n != "gdn_conv"]

    names = [n for n, _ in BIG]
    got = _gather_two_level([w_loc[n].astype(COMM_DTYPE) for n in mm_names] + [w_loc["gdn_conv"]], "gather_weights")
    full = {n: _unshard(p, axis_of[n]) for n, p in zip(names, got)}
    conv_full = full.pop("gdn_conv")

    wts = {n: [full[n][l] for l in range(full[n].shape[0])] for n in mm_names}
    wts["mla_w_in"] = [_mla_in_to_kernel(w) for w in wts["mla_w_in"]]
    wts["mla_w_uq"] = [_uq_to_kernel(w) for w in wts["mla_w_uq"]]
    wts["gdn_w_in"] = [_gdn_in_to_kernel(w) for w in wts["gdn_w_in"]]
    small_in = dict(small, gdn_conv=conv_full)

    loss_row, grad_x, gw, gs = _local_step(x[0], mem[0], positions[0], loss_target[0], wts, small_in)
    loss = lax.psum(loss_row[0, 0], ("x", "y", "c"))

    gw["mla_w_in"] = [_mla_in_from_kernel(g) for g in gw["mla_w_in"]]
    gw["mla_w_uq"] = [_uq_from_kernel(g) for g in gw["mla_w_uq"]]
    gw["gdn_w_in"] = [_gdn_in_from_kernel(g) for g in gw["gdn_w_in"]]
    gfull = {n: jnp.stack(gw[n]) for n in mm_names}
    gfull["gdn_conv"] = gs.pop("gdn_conv").astype(COMM_DTYPE)
    dims = {n: (w_loc[n].shape[0] * w_loc[n].shape[1], w_loc[n].shape[2]) for n in names}
    g_sent = [_shard(gfull[n], axis_of[n]).reshape((N_DEV,) + dims[n]) for n in names]
    g_sibling = _pair_exchange(g_sent, "grads_pair")
    my_c = lax.axis_index("c").astype(jnp.int32).reshape(1)
    g_pairs = [_pair_sum(a, b, my_c, f"pair_sum_{n}") for n, a, b in zip(names, g_sent, g_sibling)]
    g_got = _chip_exchange(g_pairs, "grads_chips")
    big_out = [{}, {}, {}, {}]
    for n, parts in zip(names, g_got):
        shp = w_loc[n].shape
        rows, cols = dims[n]
        res = _adamw(parts, w_loc[n].reshape(rows, cols), m_loc[n].reshape(rows, cols),
                     v_loc[n].reshape(rows, cols), f"adamw_{n}")
        for kind in range(4):
            big_out[kind][n] = res[kind].reshape(shp)

    s_sent = _pack([gs[n] for n in SMALL], 0, LANES, 8)
    s_got = _exchange([s_sent], "gather_small_grads", gather=True)[0]
    small_out = [dict(zip(SMALL, _unpack(o, 0, [small[n].shape for n in SMALL])))
                 for o in _adamw(s_got, _pack([small[n] for n in SMALL], 0, LANES, 8),
                                 _pack([args["m_" + n] for n in SMALL], 0, LANES, 8),
                                 _pack([args["v_" + n] for n in SMALL], 0, LANES, 8), "adamw_small")]

    order = ["mla_w_in", "mla_q_norm", "mla_w_uq", "mla_kv_norm", "mla_w_ukv", "gdn_w_in", "gdn_conv", "gdn_a_log",
             "gdn_dt_bias", "gdn_o_norm", "mem_w_kv", "w_out", "ln1_g", "ln1_b", "mlp_w1", "mlp_w2", "ln2_g", "ln2_b"]
    outs = [loss, grad_x[None]]
    for kind in range(4):
        for n in order:
            outs.append(big_out[kind][n] if n in axis_of else small_out[kind][n])
    return tuple(outs)
```

```python
import functools

import jax
import jax.numpy as jnp
from jax import lax
from jax.experimental import pallas as pl
from jax.experimental.pallas import tpu as pltpu

F32 = jnp.float32
MXU_DTYPE = jnp.bfloat16
COMM_DTYPE = jnp.bfloat16

N_DEV = 8
D_MODEL = 2048
DEPTH = 4
HEAD_DIM = 128
N_MIX_HEADS = 12
N_MEM_HEADS = 4
MIX_WIDTH = N_MIX_HEADS * HEAD_DIM
MEM_WIDTH = N_MEM_HEADS * HEAD_DIM
Q_LORA = 512
KV_LORA = 512
QK_NOPE = 128
QK_ROPE = 64
ROPE_THETA = 10000.0
CONV_WIDTH = 4
CHUNK = 64
D_FF = 4 * D_MODEL
ALPHA = (2 * DEPTH) ** 0.25
LN_EPS = 1e-5
RMS_EPS = 1e-6
MLA_IN = Q_LORA + KV_LORA + QK_ROPE + MEM_WIDTH
GDN_IN = 4 * MIX_WIDTH + 2 * N_MIX_HEADS + MEM_WIDTH
MLA_IN_P = 1792
GDN_IN_P = 6912
Q_HEAD_P = 256
ATT_SCALE = (QK_NOPE + QK_ROPE) ** -0.5
ADAM_LR, ADAM_B1, ADAM_B2, ADAM_EPS, ADAM_WD, ADAM_STEP = 0.001, 0.9, 0.999, 1e-08, 0.01, 10
LANES = 128
VMEM_LIMIT = 56 * 1024 * 1024

NN = (((1,), (0,)), ((), ()))
NT = (((1,), (1,)), ((), ()))
TN = (((0,), (0,)), ((), ()))
HI = lax.Precision.HIGHEST


def _dot(a, b, dims, hi=False):
    if hi:
        return lax.dot_general(a, b, dims, precision=HI, preferred_element_type=F32)
    return lax.dot_general(a.astype(MXU_DTYPE), b.astype(MXU_DTYPE), dims, preferred_element_type=F32)


@functools.partial(jax.custom_vjp, nondiff_argnums=(2,))
def _nn_d(a, b, hi):
    return _dot(a, b, NN, hi)


@functools.partial(jax.custom_vjp, nondiff_argnums=(2,))
def _nt_d(a, b, hi):
    return _dot(a, b, NT, hi)


@functools.partial(jax.custom_vjp, nondiff_argnums=(2,))
def _tn_d(a, b, hi):
    return _dot(a, b, TN, hi)


_nn_d.defvjp(lambda a, b, hi: (_dot(a, b, NN, hi), (a, b)),
             lambda hi, r, g: (_nt_d(g, r[1], hi), _tn_d(r[0], g, hi)))
_nt_d.defvjp(lambda a, b, hi: (_dot(a, b, NT, hi), (a, b)),
             lambda hi, r, g: (_nn_d(g, r[1], hi), _tn_d(g, r[0], hi)))
_tn_d.defvjp(lambda a, b, hi: (_dot(a, b, TN, hi), (a, b)),
             lambda hi, r, g: (_nt_d(r[1], g, hi), _nn_d(r[0], g, hi)))


class _RawOps:
    nn = staticmethod(lambda a, b, hi=False: _dot(a, b, NN, hi))
    nt = staticmethod(lambda a, b, hi=False: _dot(a, b, NT, hi))
    tn = staticmethod(lambda a, b, hi=False: _dot(a, b, TN, hi))


class _DiffOps:
    nn = staticmethod(lambda a, b, hi=False: _nn_d(a, b, hi))
    nt = staticmethod(lambda a, b, hi=False: _nt_d(a, b, hi))
    tn = staticmethod(lambda a, b, hi=False: _tn_d(a, b, hi))


def _pick(dim, cands=(512, 384, 256, 128)):
    for c in cands:
        if dim % c == 0:
            return c
    return dim


def _pc(body, *, grid, in_specs, out_specs, out_shape, name, scratch=()):
    return pl.pallas_call(
        body, grid=grid, in_specs=in_specs, out_specs=out_specs, out_shape=out_shape,
        scratch_shapes=list(scratch), name=name,
        compiler_params=pltpu.CompilerParams(dimension_semantics=("arbitrary",) * len(grid),
                                             vmem_limit_bytes=VMEM_LIMIT))


def _rows(ts, w, cb=0):
    return pl.BlockSpec((ts, w), lambda i, *_: (i, cb))


def _whole(shape):
    return pl.BlockSpec(shape, lambda *_: (0,) * len(shape))


def _sds(shape, dtype=F32):
    return jax.ShapeDtypeStruct(shape, dtype)


def _mm(a, b, *, name, ta=False, tb=False, out_dtype=F32, a_fn=None, epi=None, e=None):
    m, k = (a.shape[1], a.shape[0]) if ta else a.shape
    n = b.shape[0] if tb else b.shape[1]
    assert k == (b.shape[1] if tb else b.shape[0]), (a.shape, b.shape, ta, tb)
    wide = (1024, 768, 512, 384, 256, 128)
    if k <= 2048 and not ta:
        tm, tn, tk = _pick(m, (512, 256, 128)), _pick(n, wide), k
    else:
        tm, tn, tk = _pick(m, wide), _pick(n, wide), _pick(k, wide)
    nk = k // tk
    dims = (((0 if ta else 1,), (1 if tb else 0,)), ((), ()))

    def body(*refs):
        if e is None:
            a_ref, b_ref, o_ref, acc = refs
        else:
            a_ref, b_ref, e_ref, o_ref, acc = refs
        kk = pl.program_id(2)

        @pl.when(kk == 0)
        def _():
            acc[...] = jnp.zeros_like(acc)

        av = a_ref[...]
        if a_fn is not None:
            av = a_fn(av.astype(F32))
        acc[...] += lax.dot_general(av.astype(MXU_DTYPE), b_ref[...].astype(MXU_DTYPE), dims,
                                    preferred_element_type=F32)

        @pl.when(kk == nk - 1)
        def _():
            r = acc[...]
            if epi is not None:
                r = epi(r, e_ref[...].astype(F32))
            o_ref[...] = r.astype(out_dtype)

    a_spec = pl.BlockSpec((tk, tm), lambda i, j, kk: (kk, i)) if ta else pl.BlockSpec((tm, tk), lambda i, j, kk: (i, kk))
    b_spec = pl.BlockSpec((tn, tk), lambda i, j, kk: (j, kk)) if tb else pl.BlockSpec((tk, tn), lambda i, j, kk: (kk, j))
    o_spec = pl.BlockSpec((tm, tn), lambda i, j, kk: (i, j))
    ins, specs = [a, b], [a_spec, b_spec]
    if e is not None:
        assert e.shape == (m, n)
        ins.append(e)
        specs.append(o_spec)
    return _pc(body, grid=(m // tm, n // tn, nk), in_specs=specs, out_specs=o_spec,
               out_shape=_sds((m, n), out_dtype), name=name, scratch=[pltpu.VMEM((tm, tn), F32)])(*ins)


def _relu2(v):
    r = jnp.maximum(v, 0.0)
    return r * r


def _relu2_bwd(acc, h1):
    return acc * (2.0 * jnp.maximum(h1, 0.0))


def _add_alpha(acc, dr):
    return acc + ALPHA * dr


def _ln(r, g, b):
    mu = jnp.mean(r, -1, keepdims=True)
    var = jnp.mean(jnp.square(r - mu), -1, keepdims=True)
    return (r - mu) * lax.rsqrt(var + LN_EPS) * g + b


def _ln_fwd(x, y, g, b, name):
    s, d = x.shape
    ts = _pick(s, (256, 128))

    def body(x_ref, y_ref, g_ref, b_ref, o_ref):
        o_ref[...] = _ln(ALPHA * x_ref[...] + y_ref[...], g_ref[...], b_ref[...])

    return _pc(body, grid=(s // ts,), in_specs=[_rows(ts, d), _rows(ts, d), _whole((1, d)), _whole((1, d))],
               out_specs=_rows(ts, d), out_shape=_sds((s, d)), name=name)(x, y, g, b)


def _ln_bwd(x, y, g, b, dout, name):
    s, d = x.shape
    ts = _pick(s, (256, 128))

    def body(x_ref, y_ref, g_ref, b_ref, do_ref, dr_ref, dg_ref, db_ref):
        @pl.when(pl.program_id(0) == 0)
        def _():
            dg_ref[...] = jnp.zeros_like(dg_ref)
            db_ref[...] = jnp.zeros_like(db_ref)

        r = ALPHA * x_ref[...] + y_ref[...]
        _, vjp = jax.vjp(_ln, r, g_ref[...], b_ref[...])
        dr, dg, db = vjp(do_ref[...])
        dr_ref[...] = dr
        dg_ref[...] += dg
        db_ref[...] += db

    return _pc(body, grid=(s // ts,),
               in_specs=[_rows(ts, d), _rows(ts, d), _whole((1, d)), _whole((1, d)), _rows(ts, d)],
               out_specs=[_rows(ts, d), _whole((1, d)), _whole((1, d))],
               out_shape=[_sds((s, d)), _sds((1, d)), _sds((1, d))], name=name)(x, y, g, b, dout)


def _loss_and_grad(y, target, name):
    s, d = y.shape
    ts = _pick(s, (256, 128))

    def body(y_ref, t_ref, l_ref, dy_ref):
        @pl.when(pl.program_id(0) == 0)
        def _():
            l_ref[...] = jnp.zeros_like(l_ref)

        diff = y_ref[...] - t_ref[...]
        per_tok = jnp.mean(jnp.square(diff), -1, keepdims=True)
        l_ref[...] += 0.5 * jnp.sum(per_tok, 0, keepdims=True) * jnp.ones((1, LANES), F32)
        dy_ref[...] = diff * (1.0 / d)

    return _pc(body, grid=(s // ts,), in_specs=[_rows(ts, d), _rows(ts, d)],
               out_specs=[_whole((1, LANES)), _rows(ts, d)],
               out_shape=[_sds((1, LANES)), _sds((s, d))], name=name)(y, target)


def _rope(blk, cs, sn):
    return blk * cs + pltpu.roll(blk, 64, 1) * sn


def _rope_t(dblk, cs, sn):
    return dblk * cs + pltpu.roll(dblk * sn, 64, 1)


def _rms(v, g):
    return v * lax.rsqrt(jnp.mean(v * v, -1, keepdims=True) + RMS_EPS) * g


def _mla_prep_fwd(h, qn, kvn, cs, sn, name):
    s = h.shape[0]
    ts = _pick(s, (512, 256, 128))

    def body(cq_ref, ckv_ref, kr_ref, qn_ref, kvn_ref, cs_ref, sn_ref, cqn_ref, ckvn_ref, krr_ref):
        cqn_ref[...] = _rms(cq_ref[...], qn_ref[...]).astype(cqn_ref.dtype)
        ckvn_ref[...] = _rms(ckv_ref[...], kvn_ref[...]).astype(ckvn_ref.dtype)
        krr_ref[...] = _rope(kr_ref[...], cs_ref[...], sn_ref[...])

    return _pc(body, grid=(s // ts,),
               in_specs=[_rows(ts, 512, 0), _rows(ts, 512, 1), _rows(ts, LANES, 12), _whole((1, 512)), _whole((1, 512)),
                         _rows(ts, LANES), _rows(ts, LANES)],
               out_specs=[_rows(ts, 512), _rows(ts, 512), _rows(ts, LANES)],
               out_shape=[_sds((s, 512), MXU_DTYPE), _sds((s, 512), MXU_DTYPE), _sds((s, LANES))],
               name=name)(h, h, h, qn, kvn, cs, sn)


def _mla_prep_bwd(h, qn, kvn, cs, sn, dcqn, dckvn, dkrr_heads, name):
    s = h.shape[0]
    ts = _pick(s, (512, 256, 128))

    def body(cq_ref, ckv_ref, qn_ref, kvn_ref, cs_ref, sn_ref, dcqn_ref, dckvn_ref, dkrr_ref,
             dcq_ref, dckv_ref, dkr_ref, dqn_ref, dkvn_ref):
        @pl.when(pl.program_id(0) == 0)
        def _():
            dqn_ref[...] = jnp.zeros_like(dqn_ref)
            dkvn_ref[...] = jnp.zeros_like(dkvn_ref)

        _, vjp = jax.vjp(_rms, cq_ref[...], qn_ref[...])
        dcq, dqn = vjp(dcqn_ref[...])
        dcq_ref[...] = dcq
        dqn_ref[...] += dqn
        _, vjp = jax.vjp(_rms, ckv_ref[...], kvn_ref[...])
        dckv, dkvn = vjp(dckvn_ref[...])
        dckv_ref[...] = dckv
        dkvn_ref[...] += dkvn
        dkrr = dkrr_ref[0]
        for hh in range(1, N_MIX_HEADS):
            dkrr = dkrr + dkrr_ref[hh]
        dkr_ref[...] = _rope_t(dkrr, cs_ref[...], sn_ref[...])

    heads3 = pl.BlockSpec((N_MIX_HEADS, ts, LANES), lambda i: (0, i, 0))
    return _pc(body, grid=(s // ts,),
               in_specs=[_rows(ts, 512, 0), _rows(ts, 512, 1), _whole((1, 512)), _whole((1, 512)),
                         _rows(ts, LANES), _rows(ts, LANES), _rows(ts, 512), _rows(ts, 512), heads3],
               out_specs=[_rows(ts, 512), _rows(ts, 512), _rows(ts, LANES), _whole((1, 512)), _whole((1, 512))],
               out_shape=[_sds((s, 512)), _sds((s, 512)), _sds((s, LANES)), _sds((1, 512)), _sds((1, 512))],
               name=name)(h, h, qn, kvn, cs, sn, dcqn, dckvn, dkrr_heads)


def _att_tiles(s):
    t = _pick(s, (512, 256, 128))
    return t, s // t


ATT_HEADS_PER_STEP = 6


def _tri_pairs(nb, k_major):
    pairs = [(i, j) for j in range(nb) for i in range(j, nb)] if k_major else [(i, j) for i in range(nb) for j in range(i + 1)]
    return jnp.asarray([p[0] for p in pairs], jnp.int32), jnp.asarray([p[1] for p in pairs], jnp.int32)


ATT_STRIP = 64


def _scaled_q(q_ref, c0, cs_ref, sn_ref, dst, hh):
    dst[hh, :, :LANES] = (q_ref[:, c0:c0 + LANES] * ATT_SCALE).astype(MXU_DTYPE)
    dst[hh, :, LANES:] = (_rope(q_ref[:, c0 + LANES:c0 + Q_HEAD_P], cs_ref[...], sn_ref[...]) * ATT_SCALE).astype(MXU_DTYPE)


def _cat_k(kv_ref, c0, kr, dst, hh):
    dst[hh, :, :LANES] = kv_ref[:, c0:c0 + LANES].astype(MXU_DTYPE)
    dst[hh, :, LANES:] = kr


def _tril_rows(r, n, t):
    return lax.broadcasted_iota(jnp.int32, (n, t), 1) <= r + lax.broadcasted_iota(jnp.int32, (n, t), 0)


def _pc_pairs(body, pairs, *, n_groups, in_specs, out_specs, out_shape, name, scratch, args):
    return pl.pallas_call(
        body, out_shape=out_shape, name=name,
        grid_spec=pltpu.PrefetchScalarGridSpec(num_scalar_prefetch=2, grid=(n_groups, pairs[0].shape[0]), in_specs=in_specs,
                                               out_specs=out_specs, scratch_shapes=list(scratch)),
        compiler_params=pltpu.CompilerParams(dimension_semantics=("arbitrary", "arbitrary"), vmem_limit_bytes=VMEM_LIMIT),
    )(*pairs, *args)


def _att_specs(t):
    qrow = lambda w: pl.BlockSpec((t, w), lambda h, p, qi, kj: (qi[p], h))
    krow = lambda w: pl.BlockSpec((t, w), lambda h, p, qi, kj: (kj[p], h))
    qtab = pl.BlockSpec((t, LANES), lambda h, p, qi, kj: (qi[p], 0))
    ktab = pl.BlockSpec((t, LANES), lambda h, p, qi, kj: (kj[p], 0))
    return qrow, krow, qtab, ktab


def _flash_fwd(q, kv, krr, cs, sn, name):
    s = q.shape[0]
    t, nb = _att_tiles(s)
    ah = ATT_HEADS_PER_STEP

    def body(qi_ref, kj_ref, q_ref, cs_ref, sn_ref, kv_ref, kr_ref, o_ref, lse_ref, q_s, m_s, l_s, acc_s, s_scr, p_scr):
        i, j = qi_ref[pl.program_id(1)], kj_ref[pl.program_id(1)]

        @pl.when(j == 0)
        def _():
            for hh in range(ah):
                _scaled_q(q_ref, hh * Q_HEAD_P, cs_ref, sn_ref, q_s, hh)
            m_s[...] = jnp.full_like(m_s, -jnp.inf)
            l_s[...] = jnp.zeros_like(l_s)
            acc_s[...] = jnp.zeros_like(acc_s)

        def update(masked):
            kr = kr_ref[...].astype(MXU_DTYPE)
            for hh in range(ah):
                c0 = hh * Q_HEAD_P
                s_scr[hh] = _dot(q_s[hh, :, :LANES], kv_ref[:, c0:c0 + LANES], NT) + _dot(q_s[hh, :, LANES:], kr, NT)
                for r in range(0, t, ATT_STRIP):
                    rows = slice(r, r + ATT_STRIP)
                    sc = s_scr[hh, rows, :]
                    if masked:
                        sc = jnp.where(_tril_rows(r, ATT_STRIP, t), sc, -jnp.inf)
                    m_old = m_s[hh, rows, :]
                    m_new = jnp.maximum(m_old, jnp.max(sc, -1, keepdims=True))
                    p = jnp.exp(sc - m_new[:, :1])
                    corr = jnp.exp(m_old - m_new)
                    l_s[hh, rows, :] = corr * l_s[hh, rows, :] + jnp.sum(p, -1, keepdims=True)
                    acc_s[hh, rows, :] = corr * acc_s[hh, rows, :]
                    m_s[hh, rows, :] = m_new
                    p_scr[hh, rows, :] = p.astype(MXU_DTYPE)
                acc_s[hh] += _dot(p_scr[hh], kv_ref[:, c0 + LANES:c0 + Q_HEAD_P], NN)

        @pl.when(j < i)
        def _():
            update(False)

        @pl.when(j == i)
        def _():
            update(True)
            for hh in range(ah):
                sl = slice(hh * LANES, (hh + 1) * LANES)
                o_ref[:, sl] = acc_s[hh] / l_s[hh]
                lse_ref[:, sl] = m_s[hh] + jnp.log(l_s[hh])

    qrow, krow, qtab, ktab = _att_specs(t)
    return _pc_pairs(body, _tri_pairs(nb, False), n_groups=N_MIX_HEADS // ah,
                     in_specs=[qrow(ah * Q_HEAD_P), qtab, qtab, krow(ah * Q_HEAD_P), ktab],
                     out_specs=[qrow(ah * LANES), qrow(ah * LANES)],
                     out_shape=[_sds((s, MIX_WIDTH)), _sds((s, MIX_WIDTH))], name=name,
                     scratch=[pltpu.VMEM((ah, t, Q_HEAD_P), MXU_DTYPE),
                              pltpu.VMEM((ah, t, LANES), F32), pltpu.VMEM((ah, t, LANES), F32), pltpu.VMEM((ah, t, LANES), F32),
                              pltpu.VMEM((ah, t, t), F32), pltpu.VMEM((ah, t, t), MXU_DTYPE)],
                     args=(q, cs, sn, kv, krr))


def _flash_bwd_q(q, kv, krr, cs, sn, o, lse, dcat, name):
    s = q.shape[0]
    t, nb = _att_tiles(s)
    ah = ATT_HEADS_PER_STEP

    def body(qi_ref, kj_ref, q_ref, cs_ref, sn_ref, o_ref, lse_ref, do_ref, kv_ref, kr_ref, dq_ref,
             q_s, k_s, dl_s, aq_s, s_scr, dp_scr, ds_scr):
        i, j = qi_ref[pl.program_id(1)], kj_ref[pl.program_id(1)]

        @pl.when(j == 0)
        def _():
            for hh in range(ah):
                sl = slice(hh * LANES, (hh + 1) * LANES)
                _scaled_q(q_ref, hh * Q_HEAD_P, cs_ref, sn_ref, q_s, hh)
                dl_s[hh] = jnp.sum(o_ref[:, sl] * do_ref[:, sl], -1, keepdims=True) * jnp.ones((1, LANES), F32)
            aq_s[...] = jnp.zeros_like(aq_s)

        def update(masked):
            kr = kr_ref[...].astype(MXU_DTYPE)
            for hh in range(ah):
                c0 = hh * Q_HEAD_P
                sl = slice(hh * LANES, (hh + 1) * LANES)
                _cat_k(kv_ref, c0, kr, k_s, hh)
                s_scr[hh] = _dot(q_s[hh], k_s[hh], NT)
                dp_scr[hh] = _dot(do_ref[:, sl], kv_ref[:, c0 + LANES:c0 + Q_HEAD_P], NT)
                for r in range(0, t, ATT_STRIP):
                    rows = slice(r, r + ATT_STRIP)
                    p = jnp.exp(s_scr[hh, rows, :] - lse_ref[rows, hh * LANES:hh * LANES + 1])
                    if masked:
                        p = jnp.where(_tril_rows(r, ATT_STRIP, t), p, 0.0)
                    ds_scr[hh, rows, :] = (p * (dp_scr[hh, rows, :] - dl_s[hh, rows, :1])).astype(MXU_DTYPE)
                aq_s[hh] += _dot(ds_scr[hh], k_s[hh], NN)

        @pl.when(j < i)
        def _():
            update(False)

        @pl.when(j == i)
        def _():
            update(True)
            for hh in range(ah):
                c0 = hh * Q_HEAD_P
                dq_ref[:, c0:c0 + LANES] = (aq_s[hh, :, :LANES] * ATT_SCALE).astype(dq_ref.dtype)
                dq_ref[:, c0 + LANES:c0 + Q_HEAD_P] = _rope_t(aq_s[hh, :, LANES:] * ATT_SCALE, cs_ref[...],
                                                              sn_ref[...]).astype(dq_ref.dtype)

    qrow, krow, qtab, ktab = _att_specs(t)
    return _pc_pairs(body, _tri_pairs(nb, False), n_groups=N_MIX_HEADS // ah,
                     in_specs=[qrow(ah * Q_HEAD_P), qtab, qtab, qrow(ah * LANES), qrow(ah * LANES), qrow(ah * LANES),
                               krow(ah * Q_HEAD_P), ktab],
                     out_specs=qrow(ah * Q_HEAD_P), out_shape=_sds((s, N_MIX_HEADS * Q_HEAD_P), MXU_DTYPE), name=name,
                     scratch=[pltpu.VMEM((ah, t, Q_HEAD_P), MXU_DTYPE), pltpu.VMEM((ah, t, Q_HEAD_P), MXU_DTYPE),
                              pltpu.VMEM((ah, t, LANES), F32), pltpu.VMEM((ah, t, Q_HEAD_P), F32),
                              pltpu.VMEM((ah, t, t), F32), pltpu.VMEM((ah, t, t), F32), pltpu.VMEM((ah, t, t), MXU_DTYPE)],
                     args=(q, cs, sn, o, lse, dcat, kv, krr))


def _flash_bwd_kv(q, kv, krr, cs, sn, o, lse, dcat, name):
    s = q.shape[0]
    t, nb = _att_tiles(s)
    ah = ATT_HEADS_PER_STEP

    def body(qi_ref, kj_ref, kv_ref, kr_ref, q_ref, cs_ref, sn_ref, o_ref, lse_ref, do_ref, dkv_ref, dkr_ref,
             ak_s, av_s, q_s, k_s, s_scr, dp_scr, p_scr, ds_scr):
        i, j = qi_ref[pl.program_id(1)], kj_ref[pl.program_id(1)]

        @pl.when(i == j)
        def _():
            ak_s[...] = jnp.zeros_like(ak_s)
            av_s[...] = jnp.zeros_like(av_s)

        def update(masked):
            kr = kr_ref[...].astype(MXU_DTYPE)
            for hh in range(ah):
                c0 = hh * Q_HEAD_P
                sl = slice(hh * LANES, (hh + 1) * LANES)
                _scaled_q(q_ref, c0, cs_ref, sn_ref, q_s, hh)
                _cat_k(kv_ref, c0, kr, k_s, hh)
                do = do_ref[:, sl].astype(MXU_DTYPE)
                s_scr[hh] = _dot(q_s[hh], k_s[hh], NT)
                dp_scr[hh] = _dot(do, kv_ref[:, c0 + LANES:c0 + Q_HEAD_P], NT)
                for r in range(0, t, ATT_STRIP):
                    rows = slice(r, r + ATT_STRIP)
                    p = jnp.exp(s_scr[hh, rows, :] - lse_ref[rows, hh * LANES:hh * LANES + 1])
                    if masked:
                        p = jnp.where(_tril_rows(r, ATT_STRIP, t), p, 0.0)
                    dl = jnp.sum(o_ref[rows, sl] * do_ref[rows, sl], -1, keepdims=True)
                    p_scr[hh, rows, :] = p.astype(MXU_DTYPE)
                    ds_scr[hh, rows, :] = (p * (dp_scr[hh, rows, :] - dl)).astype(MXU_DTYPE)
                av_s[hh] += _dot(p_scr[hh], do, TN)
                ak_s[hh] += _dot(ds_scr[hh], q_s[hh], TN)

        @pl.when(i > j)
        def _():
            update(False)

        @pl.when(i == j)
        def _():
            update(True)

        @pl.when(i == nb - 1)
        def _():
            for hh in range(ah):
                c0 = hh * Q_HEAD_P
                dkv_ref[:, c0:c0 + LANES] = ak_s[hh, :, :LANES].astype(dkv_ref.dtype)
                dkv_ref[:, c0 + LANES:c0 + Q_HEAD_P] = av_s[hh].astype(dkv_ref.dtype)
                dkr_ref[hh] = ak_s[hh, :, LANES:]

    qrow, krow, qtab, ktab = _att_specs(t)
    return _pc_pairs(body, _tri_pairs(nb, True), n_groups=N_MIX_HEADS // ah,
                     in_specs=[krow(ah * Q_HEAD_P), ktab, qrow(ah * Q_HEAD_P), qtab, qtab,
                               qrow(ah * LANES), qrow(ah * LANES), qrow(ah * LANES)],
                     out_specs=[krow(ah * Q_HEAD_P), pl.BlockSpec((ah, t, LANES), lambda h, p, qi, kj: (h, kj[p], 0))],
                     out_shape=[_sds((s, N_MIX_HEADS * Q_HEAD_P), MXU_DTYPE), _sds((N_MIX_HEADS, s, LANES))], name=name,
                     scratch=[pltpu.VMEM((ah, t, Q_HEAD_P), F32), pltpu.VMEM((ah, t, LANES), F32),
                              pltpu.VMEM((ah, t, Q_HEAD_P), MXU_DTYPE), pltpu.VMEM((ah, t, Q_HEAD_P), MXU_DTYPE),
                              pltpu.VMEM((ah, t, t), F32), pltpu.VMEM((ah, t, t), F32),
                              pltpu.VMEM((ah, t, t), MXU_DTYPE), pltpu.VMEM((ah, t, t), MXU_DTYPE)],
                     args=(kv, krr, q, cs, sn, o, lse, dcat))


def _mem_head(ops, qh, kh, vh):
    sc = ops.nt(qh, kh) * HEAD_DIM ** -0.5
    e = jnp.exp(sc - lax.stop_gradient(jnp.max(sc, -1, keepdims=True)))
    p = e / jnp.sum(e, -1, keepdims=True)
    return ops.nn(p, vh)


def _mem_fwd(h, qcol, mem_kv, name):
    s = h.shape[0]
    m = mem_kv.shape[0]
    ts = _pick(s, (512, 256, 128))

    def body(q_ref, kv_ref, o_ref):
        for hh in range(N_MEM_HEADS):
            sl = slice(hh * HEAD_DIM, (hh + 1) * HEAD_DIM)
            vsl = slice(MEM_WIDTH + hh * HEAD_DIM, MEM_WIDTH + (hh + 1) * HEAD_DIM)
            o_ref[:, sl] = _mem_head(_RawOps, q_ref[:, sl], kv_ref[:, sl], kv_ref[:, vsl])

    return _pc(body, grid=(s // ts,), in_specs=[_rows(ts, MEM_WIDTH, qcol), _whole((m, 2 * MEM_WIDTH))],
               out_specs=_rows(ts, MEM_WIDTH), out_shape=_sds((s, MEM_WIDTH)), name=name)(h, mem_kv)


def _mem_bwd(h, qcol, mem_kv, dcat, name):
    s = h.shape[0]
    m = mem_kv.shape[0]
    ts = _pick(s, (512, 256, 128))

    def body(q_ref, kv_ref, do_ref, dq_ref, dkv_ref):
        @pl.when(pl.program_id(0) == 0)
        def _():
            dkv_ref[...] = jnp.zeros_like(dkv_ref)

        for hh in range(N_MEM_HEADS):
            sl = slice(hh * HEAD_DIM, (hh + 1) * HEAD_DIM)
            vsl = slice(MEM_WIDTH + hh * HEAD_DIM, MEM_WIDTH + (hh + 1) * HEAD_DIM)
            _, vjp = jax.vjp(functools.partial(_mem_head, _DiffOps), q_ref[:, sl], kv_ref[:, sl], kv_ref[:, vsl])
            dq, dk, dv = vjp(do_ref[:, sl])
            dq_ref[:, sl] = dq
            dkv_ref[:, sl] += dk
            dkv_ref[:, vsl] += dv

    return _pc(body, grid=(s // ts,),
               in_specs=[_rows(ts, MEM_WIDTH, qcol), _whole((m, 2 * MEM_WIDTH)), _rows(ts, MEM_WIDTH, 3)],
               out_specs=[_rows(ts, MEM_WIDTH), _whole((m, 2 * MEM_WIDTH))],
               out_shape=[_sds((s, MEM_WIDTH)), _sds((m, 2 * MEM_WIDTH))], name=name)(h, mem_kv, dcat)


CONV_COLS = 3 * MIX_WIDTH
HALO = 8


def _conv_fwd(h, w, name):
    s = h.shape[0]
    ts = _pick(s, (512, 256, 128))
    wc = 512

    def body(x_ref, halo_ref, w_ref, o_ref, ext):
        i = pl.program_id(0)
        ext[pl.ds(0, HALO), :] = jnp.where(i > 0, halo_ref[...], 0.0)
        ext[pl.ds(HALO, ts), :] = x_ref[...]
        acc = w_ref[0:1, :] * ext[pl.ds(HALO - 3, ts), :]
        for j in range(1, CONV_WIDTH):
            acc = acc + w_ref[j:j + 1, :] * ext[pl.ds(HALO - 3 + j, ts), :]
        o_ref[...] = acc

    halo = pl.BlockSpec((HALO, wc), lambda i, c: (jnp.maximum(i * (ts // HALO) - 1, 0), c))
    blk = pl.BlockSpec((ts, wc), lambda i, c: (i, c))
    return _pc(body, grid=(s // ts, CONV_COLS // wc),
               in_specs=[blk, halo, pl.BlockSpec((CONV_WIDTH, wc), lambda i, c: (0, c))],
               out_specs=blk, out_shape=_sds((s, CONV_COLS)), name=name,
               scratch=[pltpu.VMEM((HALO + ts, wc), F32)])(h, h, w)


def _conv_bwd(h, w, dout, name):
    s = h.shape[0]
    ts = _pick(s, (512, 256, 128))
    nt = s // ts
    wc = 512

    def body(x_ref, xhalo_ref, d_ref, dhalo_ref, w_ref, dx_ref, dw_ref, xext, dext):
        i = pl.program_id(1)

        @pl.when(i == 0)
        def _():
            dw_ref[...] = jnp.zeros_like(dw_ref)

        xext[pl.ds(0, HALO), :] = jnp.where(i > 0, xhalo_ref[...], 0.0)
        xext[pl.ds(HALO, ts), :] = x_ref[...]
        dext[pl.ds(0, ts), :] = d_ref[...]
        dext[pl.ds(ts, HALO), :] = jnp.where(i < nt - 1, dhalo_ref[...], 0.0)
        d = d_ref[...]
        acc = w_ref[CONV_WIDTH - 1:CONV_WIDTH, :] * d
        for j in range(CONV_WIDTH - 1):
            acc = acc + w_ref[j:j + 1, :] * dext[pl.ds(3 - j, ts), :]
        dx_ref[...] = acc
        for j in range(CONV_WIDTH):
            dw_ref[j:j + 1, :] += jnp.sum(d * xext[pl.ds(HALO - 3 + j, ts), :], 0, keepdims=True)

    blk = pl.BlockSpec((ts, wc), lambda c, i: (i, c))
    halo_prev = pl.BlockSpec((HALO, wc), lambda c, i: (jnp.maximum(i * (ts // HALO) - 1, 0), c))
    halo_next = pl.BlockSpec((HALO, wc), lambda c, i: (jnp.minimum((i + 1) * (ts // HALO), s // HALO - 1), c))
    wspec = pl.BlockSpec((CONV_WIDTH, wc), lambda c, i: (0, c))
    return _pc(body, grid=(CONV_COLS // wc, nt), in_specs=[blk, halo_prev, blk, halo_next, wspec],
               out_specs=[blk, wspec], out_shape=[_sds((s, CONV_COLS)), _sds((CONV_WIDTH, CONV_COLS))], name=name,
               scratch=[pltpu.VMEM((HALO + ts, wc), F32), pltpu.VMEM((ts + HALO, wc), F32)])(h, h, dout, dout, w)


def _silu(v):
    return v * jax.nn.sigmoid(v)


def _softplus(v):
    return jnp.maximum(v, 0.0) + jnp.log1p(jnp.exp(-jnp.abs(v)))


def _gdn_prep_head(cq, ck, cv, a, b, alog, dtb):
    q = _silu(cq)
    q = q * lax.rsqrt(jnp.sum(q * q, -1, keepdims=True) + 1e-6) * HEAD_DIM ** -0.5
    k = _silu(ck)
    k = k * lax.rsqrt(jnp.sum(k * k, -1, keepdims=True) + 1e-6)
    v = _silu(cv)
    g = -jnp.exp(alog) * _softplus(a + dtb)
    beta = jax.nn.sigmoid(b)
    ones = jnp.ones((1, HEAD_DIM), F32)
    return q, k, v, g * ones, beta * ones


def _onehot_lane(idx):
    return (lax.broadcasted_iota(jnp.int32, (1, LANES), 1) == idx).astype(F32)


def _lane_pick(row, idx):
    return jnp.sum(row * _onehot_lane(idx), -1, keepdims=True)


def _gdn_prep_fwd(conv, h, alog, dtb, name):
    s = h.shape[0]
    ts = _pick(s, (128,))
    w = MIX_WIDTH

    def body(c_ref, ab_ref, alog_ref, dtb_ref, q_ref, k_ref, v_ref, g_ref, b_ref):
        ab = ab_ref[...]
        for hh in range(N_MIX_HEADS):
            sl = slice(hh * HEAD_DIM, (hh + 1) * HEAD_DIM)
            cols = [slice(p * w + hh * HEAD_DIM, p * w + (hh + 1) * HEAD_DIM) for p in range(3)]
            outs = _gdn_prep_head(c_ref[:, cols[0]], c_ref[:, cols[1]], c_ref[:, cols[2]], _lane_pick(ab, hh),
                                  _lane_pick(ab, N_MIX_HEADS + hh), _lane_pick(alog_ref[...], hh),
                                  _lane_pick(dtb_ref[...], hh))
            for ref, val in zip((q_ref, k_ref, v_ref, g_ref, b_ref), outs):
                ref[:, sl] = val

    return _pc(body, grid=(s // ts,),
               in_specs=[_rows(ts, 3 * w), _rows(ts, LANES, (4 * w + MEM_WIDTH) // LANES), _whole((1, LANES)), _whole((1, LANES))],
               out_specs=[_rows(ts, w)] * 5, out_shape=[_sds((s, w))] * 5, name=name)(conv, h, alog, dtb)


def _gdn_prep_bwd(conv, h, alog, dtb, dq, dk, dv, dg, db, name):
    s = h.shape[0]
    ts = _pick(s, (128,))
    w = MIX_WIDTH

    def body(c_ref, ab_ref, alog_ref, dtb_ref, dq_ref, dk_ref, dv_ref, dg_ref, db_ref,
             dc_ref, dab_ref, dalog_ref, ddtb_ref):
        @pl.when(pl.program_id(0) == 0)
        def _():
            dalog_ref[...] = jnp.zeros_like(dalog_ref)
            ddtb_ref[...] = jnp.zeros_like(ddtb_ref)

        ab = ab_ref[...]
        dab = jnp.zeros_like(ab)
        dalog = jnp.zeros((1, LANES), F32)
        ddtb = jnp.zeros((1, LANES), F32)
        for hh in range(N_MIX_HEADS):
            sl = slice(hh * HEAD_DIM, (hh + 1) * HEAD_DIM)
            cols = [slice(p * w + hh * HEAD_DIM, p * w + (hh + 1) * HEAD_DIM) for p in range(3)]
            _, vjp = jax.vjp(_gdn_prep_head, c_ref[:, cols[0]], c_ref[:, cols[1]], c_ref[:, cols[2]], _lane_pick(ab, hh),
                             _lane_pick(ab, N_MIX_HEADS + hh), _lane_pick(alog_ref[...], hh), _lane_pick(dtb_ref[...], hh))
            dcq, dck, dcv, da, dbb, dal, ddt = vjp((dq_ref[:, sl], dk_ref[:, sl], dv_ref[:, sl], dg_ref[:, sl], db_ref[:, sl]))
            dc_ref[:, cols[0]] = dcq
            dc_ref[:, cols[1]] = dck
            dc_ref[:, cols[2]] = dcv
            dab = dab + da * _onehot_lane(hh) + dbb * _onehot_lane(N_MIX_HEADS + hh)
            dalog = dalog + dal * _onehot_lane(hh)
            ddtb = ddtb + ddt * _onehot_lane(hh)
        dab_ref[...] = dab
        dalog_ref[...] += dalog
        ddtb_ref[...] += ddtb

    return _pc(body, grid=(s // ts,),
               in_specs=[_rows(ts, 3 * w), _rows(ts, LANES, (4 * w + MEM_WIDTH) // LANES),
                         _whole((1, LANES)), _whole((1, LANES))] + [_rows(ts, w)] * 5,
               out_specs=[_rows(ts, 3 * w), _rows(ts, LANES), _whole((1, LANES)), _whole((1, LANES))],
               out_shape=[_sds((s, 3 * w)), _sds((s, LANES)), _sds((1, LANES)), _sds((1, LANES))],
               name=name)(conv, h, alog, dtb, dq, dk, dv, dg, db)


NNB = (((2,), (1,)), ((0,), (0,)))
NTB = (((2,), (2,)), ((0,), (0,)))
TNB = (((1,), (1,)), ((0,), (0,)))


def _dg(a, b, dims):
    return lax.dot_general(a, b, dims, preferred_element_type=F32)


def _dotb(a, b, dims, mode):
    a1 = a.astype(MXU_DTYPE)
    b1 = b.astype(MXU_DTYPE)
    if mode == 1:
        return _dg(a1, b1, dims)
    rb = b - b1.astype(F32)
    b2 = rb.astype(MXU_DTYPE)
    if mode == 3:
        a2 = (a - a1.astype(F32)).astype(MXU_DTYPE)
        return _dg(a1, b1, dims) + (_dg(a1, b2, dims) + _dg(a2, b1, dims))
    b3 = (rb - b2.astype(F32)).astype(MXU_DTYPE)
    return _dg(a1, b1, dims) + (_dg(a1, b2, dims) + _dg(a1, b3, dims))


@functools.partial(jax.custom_vjp, nondiff_argnums=(2,))
def _nnb(a, b, mode):
    return _dotb(a, b, NNB, mode)


@functools.partial(jax.custom_vjp, nondiff_argnums=(2,))
def _ntb(a, b, mode):
    return _dotb(a, b, NTB, mode)


@functools.partial(jax.custom_vjp, nondiff_argnums=(2,))
def _tnb(a, b, mode):
    return _dotb(a, b, TNB, mode)


_nnb.defvjp(lambda a, b, mode: (_dotb(a, b, NNB, mode), (a, b)),
            lambda mode, r, g: (_ntb(g, r[1], mode), _tnb(r[0], g, mode)))
_ntb.defvjp(lambda a, b, mode: (_dotb(a, b, NTB, mode), (a, b)),
            lambda mode, r, g: (_nnb(g, r[1], mode), _tnb(g, r[0], mode)))
_tnb.defvjp(lambda a, b, mode: (_dotb(a, b, TNB, mode), (a, b)),
            lambda mode, r, g: (_ntb(r[1], g, mode), _nnb(r[0], g, mode)))


def _trilb(h):
    t = lax.broadcasted_iota(jnp.int32, (CHUNK, CHUNK), 0) >= lax.broadcasted_iota(jnp.int32, (CHUNK, CHUNK), 1)
    return jnp.broadcast_to(t.astype(F32)[None], (h, CHUNK, CHUNK))


def _cumsum_rows_raw(gb):
    return _dotb(_trilb(gb.shape[0]), gb, NNB, "lhs")


_cumsum_rows = jax.custom_vjp(_cumsum_rows_raw)
_cumsum_rows.defvjp(lambda gb: (_cumsum_rows_raw(gb), None),
                    lambda _, g: (_dotb(_trilb(g.shape[0]), g, TNB, "lhs"),))


def _row_col_raw(gc):
    return gc[:, :, :CHUNK], jnp.swapaxes(gc, 1, 2)[:, :CHUNK, :]


def _row_col_bwd(_, g):
    part = g[0] + jnp.swapaxes(g[1], 1, 2)
    return (jnp.concatenate([part, jnp.zeros_like(part)], -1),)


_row_col = jax.custom_vjp(_row_col_raw)
_row_col.defvjp(lambda gc: (_row_col_raw(gc), None), _row_col_bwd)


def _gdn_chunk(diff, state, q, k, v, gb, bb):
    if diff:
        nn, nt, tn, cumsum, row_col = _nnb, _ntb, _tnb, _cumsum_rows, _row_col
    else:
        nn = lambda a, b, m: _dotb(a, b, NNB, m)
        nt = lambda a, b, m: _dotb(a, b, NTB, m)
        tn = lambda a, b, m: _dotb(a, b, TNB, m)
        cumsum, row_col = _cumsum_rows_raw, _row_col_raw
    c = CHUNK
    row = lax.broadcasted_iota(jnp.int32, (1, c, c), 1)
    col = lax.broadcasted_iota(jnp.int32, (1, c, c), 2)
    tril = row >= col
    strict = row > col
    eye = (row == col).astype(F32)
    gc = cumsum(gb)
    gci, gcj = row_col(gc)
    decay = jnp.where(tril, jnp.exp(jnp.where(tril, gci - gcj, 0.0)), 0.0)
    kb = k * bb
    low = jnp.where(strict, nt(kb, k, 1) * decay, 0.0)
    inv = eye - low
    pw = low
    for _ in range(5):
        pw = nn(pw, pw, 3)
        inv = inv + nn(inv, pw, 3)
    ge = jnp.exp(gc)
    u = nn(inv, v * bb, 3)
    w = nn(inv, kb * ge, 3)
    a_qk = jnp.where(tril, nt(q, k, 1) * decay, 0.0)
    g_tot = jnp.sum(gb, 1, keepdims=True)
    k_dec = k * jnp.exp(g_tot - gc)
    v_new = u - nn(w, state, 1)
    o = nn(q * ge, state, 1) + nn(a_qk, v_new, 1)
    new_state = state * jnp.exp(g_tot) + tn(k_dec, v_new, 1)
    return new_state, o


def _stack_heads(ref, hp):
    return jnp.stack([ref[:, hh * HEAD_DIM:(hh + 1) * HEAD_DIM] for hh in range(hp)])


GDN_HEADS_FWD = 12
GDN_HEADS_BWD = 12


def _gdn_core_fwd(q, k, v, gb, bb, name):
    s = q.shape[0]
    nc = s // CHUNK
    hp = GDN_HEADS_FWD
    w = hp * HEAD_DIM

    def body(q_ref, k_ref, v_ref, g_ref, b_ref, o_ref, st_ref, state):
        @pl.when(pl.program_id(1) == 0)
        def _():
            state[...] = jnp.zeros_like(state)

        st = state[...]
        st_ref[0] = st
        new_state, o = _gdn_chunk(False, st, *(_stack_heads(r, hp) for r in (q_ref, k_ref, v_ref, g_ref, b_ref)))
        state[...] = new_state
        for hh in range(hp):
            o_ref[:, hh * HEAD_DIM:(hh + 1) * HEAD_DIM] = o[hh]

    blk = pl.BlockSpec((CHUNK, w), lambda hg, c: (c, hg))
    return _pc(body, grid=(N_MIX_HEADS // hp, nc), in_specs=[blk] * 5,
               out_specs=[blk, pl.BlockSpec((1, hp, HEAD_DIM, HEAD_DIM), lambda hg, c: (c, hg, 0, 0))],
               out_shape=[_sds((s, MIX_WIDTH)), _sds((nc, N_MIX_HEADS, HEAD_DIM, HEAD_DIM))], name=name,
               scratch=[pltpu.VMEM((hp, HEAD_DIM, HEAD_DIM), F32)])(q, k, v, gb, bb)


def _gdn_core_bwd(q, k, v, gb, bb, states, do, name):
    s = q.shape[0]
    nc = s // CHUNK
    hp = GDN_HEADS_BWD
    w = hp * HEAD_DIM

    def body(q_ref, k_ref, v_ref, g_ref, b_ref, st_ref, do_ref, dq_ref, dk_ref, dv_ref, dg_ref, db_ref, dstate):
        @pl.when(pl.program_id(1) == 0)
        def _():
            dstate[...] = jnp.zeros_like(dstate)

        _, vjp = jax.vjp(functools.partial(_gdn_chunk, True), st_ref[0],
                         *(_stack_heads(r, hp) for r in (q_ref, k_ref, v_ref, g_ref, b_ref)))
        grads = vjp((dstate[...], _stack_heads(do_ref, hp)))
        dstate[...] = grads[0]
        for ref, val in zip((dq_ref, dk_ref, dv_ref, dg_ref, db_ref), grads[1:]):
            for hh in range(hp):
                ref[:, hh * HEAD_DIM:(hh + 1) * HEAD_DIM] = val[hh]

    blk = pl.BlockSpec((CHUNK, w), lambda hg, c: (nc - 1 - c, hg))
    return _pc(body, grid=(N_MIX_HEADS // hp, nc),
               in_specs=[blk] * 5 + [pl.BlockSpec((1, hp, HEAD_DIM, HEAD_DIM), lambda hg, c: (nc - 1 - c, hg, 0, 0)), blk],
               out_specs=[blk] * 5, out_shape=[_sds((s, MIX_WIDTH))] * 5, name=name,
               scratch=[pltpu.VMEM((hp, HEAD_DIM, HEAD_DIM), F32)])(q, k, v, gb, bb, states, do)


def _gdn_out_head(o, z, g):
    return _rms(o, g) * _silu(z)


def _gdn_out_fwd(o, h, onorm, name):
    s = o.shape[0]
    ts = _pick(s, (256, 128))

    def body(o_ref, z_ref, g_ref, y_ref):
        for hh in range(N_MIX_HEADS):
            sl = slice(hh * HEAD_DIM, (hh + 1) * HEAD_DIM)
            y_ref[:, sl] = _gdn_out_head(o_ref[:, sl], z_ref[:, sl], g_ref[...])

    return _pc(body, grid=(s // ts,), in_specs=[_rows(ts, MIX_WIDTH), _rows(ts, MIX_WIDTH, 3), _whole((1, HEAD_DIM))],
               out_specs=_rows(ts, MIX_WIDTH), out_shape=_sds((s, MIX_WIDTH)), name=name)(o, h, onorm)


def _gdn_out_bwd(o, h, onorm, dcat, name):
    s = o.shape[0]
    ts = _pick(s, (256, 128))

    def body(o_ref, z_ref, g_ref, dy_ref, do_ref, dz_ref, dg_ref):
        @pl.when(pl.program_id(0) == 0)
        def _():
            dg_ref[...] = jnp.zeros_like(dg_ref)

        dgs = jnp.zeros((1, HEAD_DIM), F32)
        for hh in range(N_MIX_HEADS):
            sl = slice(hh * HEAD_DIM, (hh + 1) * HEAD_DIM)
            _, vjp = jax.vjp(_gdn_out_head, o_ref[:, sl], z_ref[:, sl], g_ref[...])
            do, dz, dg = vjp(dy_ref[:, sl])
            do_ref[:, sl] = do
            dz_ref[:, sl] = dz
            dgs = dgs + dg
        dg_ref[...] += dgs

    return _pc(body, grid=(s // ts,),
               in_specs=[_rows(ts, MIX_WIDTH), _rows(ts, MIX_WIDTH, 3), _whole((1, HEAD_DIM)), _rows(ts, MIX_WIDTH, 0)],
               out_specs=[_rows(ts, MIX_WIDTH), _rows(ts, MIX_WIDTH), _whole((1, HEAD_DIM))],
               out_shape=[_sds((s, MIX_WIDTH)), _sds((s, MIX_WIDTH)), _sds((1, HEAD_DIM))], name=name)(o, h, onorm, dcat)


def _row(v):
    return v.reshape(1, -1)


def _lane_row(v):
    return jnp.pad(v, (0, LANES - v.shape[0])).reshape(1, LANES)


def _rope_tables(positions):
    inv_freq = 1.0 / (ROPE_THETA ** (jnp.arange(0, QK_ROPE, 2, dtype=F32) / QK_ROPE))
    ang = positions.astype(F32)[:, None] * inv_freq
    cos, sin = jnp.cos(ang), jnp.sin(ang)
    z = jnp.zeros_like(cos)
    return jnp.concatenate([cos, z, cos, z], 1), jnp.concatenate([-sin, z, sin, z], 1)


def _local_step(x, mem, positions, loss_target, wts, small):
    cs, sn = _rope_tables(positions)
    saved = []
    for i in range(DEPTH):
        j = i // 2
        sv = {"x": x}
        sv["mem_kv"] = _mm(mem, wts["mem_w_kv"][i], name=f"l{i}_memkv")
        if i % 2 == 0:
            h = _mm(x, wts["mla_w_in"][j], name=f"l{i}_in")
            cqn, ckvn, krr = _mla_prep_fwd(h, _row(small["mla_q_norm"][j]), _row(small["mla_kv_norm"][j]), cs, sn, f"l{i}_mlaprep")
            q = _mm(cqn, wts["mla_w_uq"][j], name=f"l{i}_uq")
            kv = _mm(ckvn, wts["mla_w_ukv"][j], out_dtype=MXU_DTYPE, name=f"l{i}_ukv")
            mix, lse = _flash_fwd(q, kv, krr, cs, sn, f"l{i}_flash")
            sv.update(cqn=cqn, ckvn=ckvn, krr=krr, q=q, kv=kv, o=mix, lse=lse)
            qcol = 2
        else:
            h = _mm(x, wts["gdn_w_in"][j], name=f"l{i}_in")
            conv = _conv_fwd(h, small["gdn_conv"][j], f"l{i}_conv")
            qn, kn, vv, gb, bb = _gdn_prep_fwd(conv, h, _lane_row(small["gdn_a_log"][j]), _lane_row(small["gdn_dt_bias"][j]), f"l{i}_gdnprep")
            o, states = _gdn_core_fwd(qn, kn, vv, gb, bb, f"l{i}_gdncore")
            mix = _gdn_out_fwd(o, h, _row(small["gdn_o_norm"][j]), f"l{i}_gdnout")
            sv.update(conv=conv, qn=qn, kn=kn, vv=vv, gb=gb, bb=bb, o=o, states=states)
            qcol = 4 * MIX_WIDTH // MEM_WIDTH
        mem_o = _mem_fwd(h, qcol, sv["mem_kv"], f"l{i}_mem")
        cat = jnp.concatenate([mix, mem_o], 1).astype(MXU_DTYPE)
        y = _mm(cat, wts["w_out"][i], name=f"l{i}_out")
        x1 = _ln_fwd(x, y, _row(small["ln1_g"][i]), _row(small["ln1_b"][i]), f"l{i}_ln1")
        h1 = _mm(x1, wts["mlp_w1"][i], name=f"l{i}_w1")
        ff = _mm(h1, wts["mlp_w2"][i], a_fn=_relu2, name=f"l{i}_w2")
        x2 = _ln_fwd(x1, ff, _row(small["ln2_g"][i]), _row(small["ln2_b"][i]), f"l{i}_ln2")
        sv.update(h=h, qcol=qcol, cat=cat, y=y, x1=x1, h1=h1, ff=ff)
        saved.append(sv)
        x = x2

    loss_row, dx = _loss_and_grad(x, loss_target, "loss")

    gw = {k: [None] * len(v) for k, v in wts.items()}
    gs = {k: [None] * v.shape[0] for k, v in small.items()}
    gdt = COMM_DTYPE
    for i in reversed(range(DEPTH)):
        j = i // 2
        sv = saved[i]
        dr2, dg, db = _ln_bwd(sv["x1"], sv["ff"], _row(small["ln2_g"][i]), _row(small["ln2_b"][i]), dx, f"l{i}_ln2b")
        gs["ln2_g"][i], gs["ln2_b"][i] = dg[0], db[0]
        dh1 = _mm(dr2, wts["mlp_w2"][i], tb=True, epi=_relu2_bwd, e=sv["h1"], out_dtype=MXU_DTYPE, name=f"l{i}_dh1")
        gw["mlp_w2"][i] = _mm(sv["h1"], dr2, ta=True, a_fn=_relu2, out_dtype=gdt, name=f"l{i}_dw2")
        gw["mlp_w1"][i] = _mm(sv["x1"], dh1, ta=True, out_dtype=gdt, name=f"l{i}_dw1")
        dx1 = _mm(dh1, wts["mlp_w1"][i], tb=True, epi=_add_alpha, e=dr2, name=f"l{i}_dx1")
        dr1, dg, db = _ln_bwd(sv["x"], sv["y"], _row(small["ln1_g"][i]), _row(small["ln1_b"][i]), dx1, f"l{i}_ln1b")
        gs["ln1_g"][i], gs["ln1_b"][i] = dg[0], db[0]
        gw["w_out"][i] = _mm(sv["cat"], dr1, ta=True, out_dtype=gdt, name=f"l{i}_dwout")
        dcat = _mm(dr1, wts["w_out"][i], tb=True, name=f"l{i}_dcat")
        h = sv["h"]
        dqmem, dmem_kv = _mem_bwd(h, sv["qcol"], sv["mem_kv"], dcat, f"l{i}_memb")
        gw["mem_w_kv"][i] = _mm(mem, dmem_kv, ta=True, out_dtype=gdt, name=f"l{i}_dwmem")
        zpad = jnp.zeros((h.shape[0], LANES), F32)
        if i % 2 == 0:
            qnw, kvnw = _row(small["mla_q_norm"][j]), _row(small["mla_kv_norm"][j])
            dq = _flash_bwd_q(sv["q"], sv["kv"], sv["krr"], cs, sn, sv["o"], sv["lse"], dcat, f"l{i}_flashbq")
            dkv, dkrr = _flash_bwd_kv(sv["q"], sv["kv"], sv["krr"], cs, sn, sv["o"], sv["lse"], dcat, f"l{i}_flashbkv")
            gw["mla_w_ukv"][j] = _mm(sv["ckvn"], dkv, ta=True, out_dtype=gdt, name=f"l{i}_dwukv")
            dckvn = _mm(dkv, wts["mla_w_ukv"][j], tb=True, name=f"l{i}_dckvn")
            gw["mla_w_uq"][j] = _mm(sv["cqn"], dq, ta=True, out_dtype=gdt, name=f"l{i}_dwuq")
            dcqn = _mm(dq, wts["mla_w_uq"][j], tb=True, name=f"l{i}_dcqn")
            dcq, dckv, dkr, dqn, dkvn = _mla_prep_bwd(h, qnw, kvnw, cs, sn, dcqn, dckvn, dkrr, f"l{i}_mlaprepb")
            gs["mla_q_norm"][j], gs["mla_kv_norm"][j] = dqn[0], dkvn[0]
            dh = jnp.concatenate([dcq, dckv, dqmem, dkr, zpad], 1).astype(MXU_DTYPE)
            w_in = wts["mla_w_in"][j]
            key = "mla_w_in"
        else:
            alog, dtb = _lane_row(small["gdn_a_log"][j]), _lane_row(small["gdn_dt_bias"][j])
            do, dz, dgn = _gdn_out_bwd(sv["o"], h, _row(small["gdn_o_norm"][j]), dcat, f"l{i}_gdnoutb")
            gs["gdn_o_norm"][j] = dgn[0]
            dqn, dkn, dvv, dgb, dbb = _gdn_core_bwd(sv["qn"], sv["kn"], sv["vv"], sv["gb"], sv["bb"], sv["states"], do, f"l{i}_gdncoreb")
            dconv, dab, dalog, ddtb = _gdn_prep_bwd(sv["conv"], h, alog, dtb, dqn, dkn, dvv, dgb, dbb, f"l{i}_gdnprepb")
            gs["gdn_a_log"][j], gs["gdn_dt_bias"][j] = dalog[0, :N_MIX_HEADS], ddtb[0, :N_MIX_HEADS]
            dhqkv, dconvw = _conv_bwd(h, small["gdn_conv"][j], dconv, f"l{i}_convb")
            gs["gdn_conv"][j] = dconvw
            dh = jnp.concatenate([dhqkv, dz, dqmem, dab, zpad], 1).astype(MXU_DTYPE)
            w_in = wts["gdn_w_in"][j]
            key = "gdn_w_in"
        gw[key][j] = _mm(sv["x"], dh, ta=True, out_dtype=gdt, name=f"l{i}_dwin")
        dx = _mm(dh, w_in, tb=True, epi=_add_alpha, e=dr1, name=f"l{i}_dx")
    gs = {k: jnp.stack(v) for k, v in gs.items()}
    return loss_row, dx, gw, gs


def _mla_in_to_kernel(w):
    z32 = jnp.zeros((w.shape[0], 32), w.dtype)
    z128 = jnp.zeros((w.shape[0], LANES), w.dtype)
    return jnp.concatenate([w[:, :1024], w[:, 1088:1600], w[:, 1024:1056], z32, w[:, 1056:1088], z32, z128], 1)


def _mla_in_from_kernel(g):
    return jnp.concatenate([g[:, :1024], g[:, 1536:1568], g[:, 1600:1632], g[:, 1024:1536]], 1)


def _uq_to_kernel(w):
    w3 = w.reshape(Q_LORA, N_MIX_HEADS, QK_NOPE + QK_ROPE)
    z = jnp.zeros((Q_LORA, N_MIX_HEADS, 32), w.dtype)
    return jnp.concatenate([w3[:, :, :128], w3[:, :, 128:160], z, w3[:, :, 160:192], z], 2).reshape(Q_LORA, N_MIX_HEADS * Q_HEAD_P)


def _uq_from_kernel(g):
    g3 = g.reshape(Q_LORA, N_MIX_HEADS, Q_HEAD_P)
    return jnp.concatenate([g3[:, :, :128], g3[:, :, 128:160], g3[:, :, 192:224]], 2).reshape(Q_LORA, -1)


def _gdn_in_to_kernel(w):
    z = jnp.zeros((w.shape[0], LANES - 2 * N_MIX_HEADS + LANES), w.dtype)
    return jnp.concatenate([w[:, :6144], w[:, 6168:6680], w[:, 6144:6168], z], 1)


def _gdn_in_from_kernel(g):
    return jnp.concatenate([g[:, :6144], g[:, 6656:6680], g[:, 6144:6656]], 1)


def _exchange(srcs, name, gather):
    n = len(srcs)

    def body(*refs):
        src_refs, out_refs = refs[:n], refs[n:2 * n]
        send_sems, recv_sems, local_sems = refs[2 * n:]
        x, y, c = lax.axis_index("x"), lax.axis_index("y"), lax.axis_index("c")
        me = 4 * x + 2 * y + c
        local, sends, recvs = [], [], []
        for a in range(n):
            src_ref, out_ref = src_refs[a], out_refs[a]
            local.append(pltpu.make_async_copy(src_ref if gather else src_ref.at[me], out_ref.at[me], local_sems.at[a]))
            for kk in range(1, N_DEV):
                px, py, pc = x ^ ((kk >> 2) & 1), y ^ ((kk >> 1) & 1), c ^ (kk & 1)
                peer = 4 * px + 2 * py + pc
                piece = src_ref if gather else src_ref.at[peer]
                sends.append(pltpu.make_async_remote_copy(
                    src_ref=piece, dst_ref=out_ref.at[me], send_sem=send_sems.at[a, kk - 1], recv_sem=recv_sems.at[a, kk - 1],
                    device_id=(px, py, pc), device_id_type=pl.DeviceIdType.MESH))
                recvs.append(pltpu.make_async_remote_copy(
                    src_ref=piece, dst_ref=out_ref.at[peer], send_sem=send_sems.at[a, kk - 1], recv_sem=recv_sems.at[a, kk - 1],
                    device_id=(px, py, pc), device_id_type=pl.DeviceIdType.MESH))
        for cp in local + sends:
            cp.start()
        for cp in recvs:
            cp.wait_recv()
        for cp in sends:
            cp.wait_send()
        for cp in local:
            cp.wait()

    hbm = pl.BlockSpec(memory_space=pltpu.HBM)
    shapes = [_sds((N_DEV,) + tuple(s.shape if gather else s.shape[1:]), s.dtype) for s in srcs]
    return pl.pallas_call(
        body, in_specs=[hbm] * n, out_specs=[hbm] * n, out_shape=shapes, name=name,
        scratch_shapes=[pltpu.SemaphoreType.DMA((n, N_DEV - 1)), pltpu.SemaphoreType.DMA((n, N_DEV - 1)),
                        pltpu.SemaphoreType.DMA((n,))],
    )(*srcs)


def _adamw(parts, w, m, v, name):
    r, c = w.shape
    n_parts = parts.shape[0]
    tr = _pick(r, tuple(t for t in (256, 128, 64, 32, 16, 8) if t * c <= 256 * 1024))

    def body(p_ref, w_ref, m_ref, v_ref, g_ref, d_ref, nm_ref, nv_ref):
        g = p_ref[0].astype(F32)
        for dd in range(1, n_parts):
            g = g + p_ref[dd].astype(F32)
        nm = ADAM_B1 * m_ref[...] + (1.0 - ADAM_B1) * g
        nv = ADAM_B2 * v_ref[...] + (1.0 - ADAM_B2) * jnp.square(g)
        m_hat = nm / (1.0 - ADAM_B1 ** ADAM_STEP)
        v_hat = nv / (1.0 - ADAM_B2 ** ADAM_STEP)
        g_ref[...] = g
        d_ref[...] = -ADAM_LR * (m_hat / (jnp.sqrt(v_hat) + ADAM_EPS) + ADAM_WD * w_ref[...])
        nm_ref[...] = nm
        nv_ref[...] = nv

    blk = pl.BlockSpec((tr, c), lambda i: (i, 0))
    return _pc(body, grid=(r // tr,), in_specs=[pl.BlockSpec((n_parts, tr, c), lambda i: (0, i, 0)), blk, blk, blk],
               out_specs=[blk] * 4, out_shape=[_sds((r, c))] * 4, name=name)(parts, w, m, v)


N_CHIPS = 4
MESH_IDS = pl.DeviceIdType.MESH


def _place():
    x, y, c = lax.axis_index("x"), lax.axis_index("y"), lax.axis_index("c")
    return x, y, c, [(1 - x, y), (x, 1 - y), (1 - x, 1 - y)]


def _gather_two_level(srcs, name):
    n = len(srcs)

    def body(*refs):
        src_refs, out_refs = refs[:n], refs[n:2 * n]
        send_sems, recv_sems, local_sems = refs[2 * n:]
        x, y, c, chips = _place()
        sib = (x, y, 1 - c)

        def copy(a, k, block, to, src=None):
            rows = out_refs[a].at[4 * block[0] + 2 * block[1] + block[2]]
            return pltpu.make_async_remote_copy(src_ref=rows if src is None else src, dst_ref=rows,
                                                send_sem=send_sems.at[a, k], recv_sem=recv_sems.at[a, k],
                                                device_id=to, device_id_type=MESH_IDS)

        local = [pltpu.make_async_copy(src_refs[a], out_refs[a].at[4 * x + 2 * y + c], local_sems.at[a]) for a in range(n)]
        first = []
        for a in range(n):
            first.append(copy(a, 0, (x, y, c), sib, src_refs[a]))
            first += [copy(a, 1 + j, (x, y, c), (*chip, c), src_refs[a]) for j, chip in enumerate(chips)]
        for cp in local + first:
            cp.start()
        passed = []
        for a in range(n):
            for j, chip in enumerate(chips):
                copy(a, 1 + j, (*chip, c), (x, y, c)).wait_recv()
                passed.append(copy(a, 4 + j, (*chip, c), sib))
                passed[-1].start()
        for a in range(n):
            copy(a, 0, (x, y, 1 - c), (x, y, c)).wait_recv()
            for j, chip in enumerate(chips):
                copy(a, 4 + j, (*chip, 1 - c), (x, y, c)).wait_recv()
        for cp in first + passed:
            cp.wait_send()
        for cp in local:
            cp.wait()

    hbm = pl.BlockSpec(memory_space=pltpu.HBM)
    return pl.pallas_call(
        body, in_specs=[hbm] * n, out_specs=[hbm] * n, name=name,
        out_shape=[_sds((N_DEV,) + tuple(s.shape), s.dtype) for s in srcs],
        scratch_shapes=[pltpu.SemaphoreType.DMA((n, 7)), pltpu.SemaphoreType.DMA((n, 7)), pltpu.SemaphoreType.DMA((n,))],
    )(*srcs)


def _pair_exchange(srcs, name):
    n = len(srcs)

    def body(*refs):
        src_refs, out_refs = refs[:n], refs[n:2 * n]
        send_sems, recv_sems = refs[2 * n:]
        x, y, c, _ = _place()
        sends = []
        for a in range(n):
            for ch in range(N_CHIPS):
                sends.append(pltpu.make_async_remote_copy(
                    src_ref=src_refs[a].at[2 * ch + (1 - c)], dst_ref=out_refs[a].at[ch],
                    send_sem=send_sems.at[a, ch], recv_sem=recv_sems.at[a, ch], device_id=(x, y, 1 - c),
                    device_id_type=MESH_IDS))
        for cp in sends:
            cp.start()
        for cp in sends:
            cp.wait_recv()
        for cp in sends:
            cp.wait_send()

    hbm = pl.BlockSpec(memory_space=pltpu.HBM)
    return pl.pallas_call(
        body, in_specs=[hbm] * n, out_specs=[hbm] * n, name=name,
        out_shape=[_sds((N_CHIPS,) + tuple(s.shape[1:]), s.dtype) for s in srcs],
        scratch_shapes=[pltpu.SemaphoreType.DMA((n, N_CHIPS)), pltpu.SemaphoreType.DMA((n, N_CHIPS))],
    )(*srcs)


def _pair_sum(mine, theirs, my_c, name):
    _, r, c = mine.shape
    tr = _pick(r, tuple(t for t in (512, 256, 128, 64, 32, 16, 8) if t * c <= 512 * 1024))

    def body(c_ref, a_ref, b_ref, o_ref):
        o_ref[...] = (a_ref[...].astype(F32) + b_ref[...].astype(F32)).astype(o_ref.dtype)

    return pl.pallas_call(
        body, out_shape=_sds((N_CHIPS, r, c), mine.dtype), name=name,
        grid_spec=pltpu.PrefetchScalarGridSpec(
            num_scalar_prefetch=1, grid=(N_CHIPS, r // tr),
            in_specs=[pl.BlockSpec((1, tr, c), lambda ch, i, cc: (2 * ch + cc[0], i, 0)),
                      pl.BlockSpec((1, tr, c), lambda ch, i, cc: (ch, i, 0))],
            out_specs=pl.BlockSpec((1, tr, c), lambda ch, i, cc: (ch, i, 0))),
        compiler_params=pltpu.CompilerParams(dimension_semantics=("arbitrary", "arbitrary"), vmem_limit_bytes=VMEM_LIMIT),
    )(my_c, mine, theirs)


def _chip_exchange(srcs, name):
    n = len(srcs)

    def body(*refs):
        src_refs, out_refs = refs[:n], refs[n:2 * n]
        send_sems, recv_sems, local_sems = refs[2 * n:]
        x, y, c, chips = _place()
        my_chip = 2 * x + y
        local = [pltpu.make_async_copy(src_refs[a].at[my_chip], out_refs[a].at[my_chip], local_sems.at[a]) for a in range(n)]
        sends, recvs = [], []
        for a in range(n):
            for j, chip in enumerate(chips):
                ch = 2 * chip[0] + chip[1]
                sends.append(pltpu.make_async_remote_copy(
                    src_ref=src_refs[a].at[ch], dst_ref=out_refs[a].at[my_chip], send_sem=send_sems.at[a, j],
                    recv_sem=recv_sems.at[a, j], device_id=(*chip, c), device_id_type=MESH_IDS))
                recvs.append(pltpu.make_async_remote_copy(
                    src_ref=src_refs[a].at[ch], dst_ref=out_refs[a].at[ch], send_sem=send_sems.at[a, j],
                    recv_sem=recv_sems.at[a, j], device_id=(*chip, c), device_id_type=MESH_IDS))
        for cp in local + sends:
            cp.start()
        for cp in recvs:
            cp.wait_recv()
        for cp in sends:
            cp.wait_send()
        for cp in local:
            cp.wait()

    hbm = pl.BlockSpec(memory_space=pltpu.HBM)
    return pl.pallas_call(
        body, in_specs=[hbm] * n, out_specs=[hbm] * n, name=name,
        out_shape=[_sds(tuple(s.shape), s.dtype) for s in srcs],
        scratch_shapes=[pltpu.SemaphoreType.DMA((n, 3)), pltpu.SemaphoreType.DMA((n, 3)), pltpu.SemaphoreType.DMA((n,))],
    )(*srcs)


BIG = (("mla_w_in", 1), ("mla_w_uq", 2), ("mla_w_ukv", 2), ("gdn_w_in", 2), ("mem_w_kv", 1), ("w_out", 1),
       ("mlp_w1", 2), ("mlp_w2", 1), ("gdn_conv", 2))
SMALL = ("mla_q_norm", "mla_kv_norm", "gdn_a_log", "gdn_dt_bias", "gdn_o_norm", "ln1_g", "ln1_b", "ln2_g", "ln2_b")
PACK_COLS = 1024


def _unshard(pieces, axis):
    t = jnp.moveaxis(pieces, 0, axis)
    return t.reshape(t.shape[:axis] + (t.shape[axis] * t.shape[axis + 1],) + t.shape[axis + 2:])


def _shard(full, axis):
    t = full.reshape(full.shape[:axis] + (N_DEV, full.shape[axis] // N_DEV) + full.shape[axis + 1:])
    return jnp.moveaxis(t, axis, 0)


def _pack(arrs, lead, cols, mult):
    lead_shape = arrs[0].shape[:lead]
    flat = jnp.concatenate([a.reshape(lead_shape + (-1,)) for a in arrs], -1)
    n = flat.shape[-1]
    r = -(-n // (cols * mult)) * mult
    flat = jnp.pad(flat, [(0, 0)] * lead + [(0, r * cols - n)])
    return flat.reshape(lead_shape + (r, cols))


def _unpack(buf, lead, shapes):
    lead_shape = buf.shape[:lead]
    flat = buf.reshape(lead_shape + (-1,))
    out, off = [], 0
    for shp in shapes:
        n = 1
        for d in shp:
            n *= d
        out.append(flat[..., off:off + n].reshape(lead_shape + tuple(shp)))
        off += n
    return out


def kernel(x, mem, positions, mla_w_in, mla_q_norm, mla_w_uq, mla_kv_norm, mla_w_ukv, gdn_w_in, gdn_conv, gdn_a_log, gdn_dt_bias, gdn_o_norm, mem_w_kv, w_out, ln1_g, ln1_b, mlp_w1, mlp_w2, ln2_g, ln2_b, loss_target, m_mla_w_in, m_mla_q_norm, m_mla_w_uq, m_mla_kv_norm, m_mla_w_ukv, m_gdn_w_in, m_gdn_conv, m_gdn_a_log, m_gdn_dt_bias, m_gdn_o_norm, m_mem_w_kv, m_w_out, m_ln1_g, m_ln1_b, m_mlp_w1, m_mlp_w2, m_ln2_g, m_ln2_b, v_mla_w_in, v_mla_q_norm, v_mla_w_uq, v_mla_kv_norm, v_mla_w_ukv, v_gdn_w_in, v_gdn_conv, v_gdn_a_log, v_gdn_dt_bias, v_gdn_o_norm, v_mem_w_kv, v_w_out, v_ln1_g, v_ln1_b, v_mlp_w1, v_mlp_w2, v_ln2_g, v_ln2_b):
    args = dict(locals())
    w_loc = {n: args[n] for n, _ in BIG}
    m_loc = {n: args["m_" + n] for n, _ in BIG}
    v_loc = {n: args["v_" + n] for n, _ in BIG}
    small = {n: args[n] for n in SMALL}
    axis_of = dict(BIG)
    mm_names = [n for n, _ in BIG if n != "gdn_conv"]

    names = [n for n, _ in BIG]
    got = _gather_two_level([w_loc[n].astype(COMM_DTYPE) for n in mm_names] + [w_loc["gdn_conv"]], "gather_weights")
    full = {n: _unshard(p, axis_of[n]) for n, p in zip(names, got)}
    conv_full = full.pop("gdn_conv")

    wts = {n: [full[n][l] for l in range(full[n].shape[0])] for n in mm_names}
    wts["mla_w_in"] = [_mla_in_to_kernel(w) for w in wts["mla_w_in"]]
    wts["mla_w_uq"] = [_uq_to_kernel(w) for w in wts["mla_w_uq"]]
    wts["gdn_w_in"] = [_gdn_in_to_kernel(w) for w in wts["gdn_w_in"]]
    small_in = dict(small, gdn_conv=conv_full)

    loss_row, grad_x, gw, gs = _local_step(x[0], mem[0], positions[0], loss_target[0], wts, small_in)
    loss = lax.psum(loss_row[0, 0], ("x", "y", "c"))

    gw["mla_w_in"] = [_mla_in_from_kernel(g) for g in gw["mla_w_in"]]
    gw["mla_w_uq"] = [_uq_from_kernel(g) for g in gw["mla_w_uq"]]
    gw["gdn_w_in"] = [_gdn_in_from_kernel(g) for g in gw["gdn_w_in"]]
    gfull = {n: jnp.stack(gw[n]) for n in mm_names}
    gfull["gdn_conv"] = gs.pop("gdn_conv").astype(COMM_DTYPE)
    dims = {n: (w_loc[n].shape[0] * w_loc[n].shape[1], w_loc[n].shape[2]) for n in names}
    g_sent = [_shard(gfull[n], axis_of[n]).reshape((N_DEV,) + dims[n]) for n in names]
    g_sibling = _pair_exchange(g_sent, "grads_pair")
    my_c = lax.axis_index("c").astype(jnp.int32).reshape(1)
    g_pairs = [_pair_sum(a, b, my_c, f"pair_sum_{n}") for n, a, b in zip(names, g_sent, g_sibling)]
    g_got = _chip_exchange(g_pairs, "grads_chips")
    big_out = [{}, {}, {}, {}]
    for n, parts in zip(names, g_got):
        shp = w_loc[n].shape
        rows, cols = dims[n]
        res = _adamw(parts, w_loc[n].reshape(rows, cols), m_loc[n].reshape(rows, cols),
                     v_loc[n].reshape(rows, cols), f"adamw_{n}")
        for kind in range(4):
            big_out[kind][n] = res[kind].reshape(shp)

    s_sent = _pack([gs[n] for n in SMALL], 0, LANES, 8)
    s_got = _exchange([s_sent], "gather_small_grads", gather=True)[0]
    small_out = [dict(zip(SMALL, _unpack(o, 0, [small[n].shape for n in SMALL])))
                 for o in _adamw(s_got, _pack([small[n] for n in SMALL], 0, LANES, 8),
                                 _pack([args["m_" + n] for n in SMALL], 0, LANES, 8),
                                 _pack([args["v_" + n] for n in SMALL], 0, LANES, 8), "adamw_small")]

    order = ["mla_w_in", "mla_q_norm", "mla_w_uq", "mla_kv_norm", "mla_w_ukv", "gdn_w_in", "gdn_conv", "gdn_a_log",
             "gdn_dt_bias", "gdn_o_norm", "mem_w_kv", "w_out", "ln1_g", "ln1_b", "mlp_w1", "mlp_w2", "ln2_g", "ln2_b"]
    outs = [loss, grad_x[None]]
    for kind in range(4):
        for n in order:
            outs.append(big_out[kind][n] if n in axis_of else small_out[kind][n])
    return tuple(outs)
```

```python
import functools

import jax
import jax.numpy as jnp
from jax import lax
from jax.experimental import pallas as pl
from jax.experimental.pallas import tpu as pltpu

F32 = jnp.float32
MXU_DTYPE = jnp.bfloat16
COMM_DTYPE = jnp.bfloat16

N_DEV = 8
D_MODEL = 2048
DEPTH = 4
HEAD_DIM = 128
N_MIX_HEADS = 12
N_MEM_HEADS = 4
MIX_WIDTH = N_MIX_HEADS * HEAD_DIM
MEM_WIDTH = N_MEM_HEADS * HEAD_DIM
Q_LORA = 512
KV_LORA = 512
QK_NOPE = 128
QK_ROPE = 64
ROPE_THETA = 10000.0
CONV_WIDTH = 4
CHUNK = 64
D_FF = 4 * D_MODEL
ALPHA = (2 * DEPTH) ** 0.25
LN_EPS = 1e-5
RMS_EPS = 1e-6
MLA_IN = Q_LORA + KV_LORA + QK_ROPE + MEM_WIDTH
GDN_IN = 4 * MIX_WIDTH + 2 * N_MIX_HEADS + MEM_WIDTH
MLA_IN_P = 1792
GDN_IN_P = 6912
Q_HEAD_P = 256
ATT_SCALE = (QK_NOPE + QK_ROPE) ** -0.5
ADAM_LR, ADAM_B1, ADAM_B2, ADAM_EPS, ADAM_WD, ADAM_STEP = 0.001, 0.9, 0.999, 1e-08, 0.01, 10
LANES = 128
VMEM_LIMIT = 56 * 1024 * 1024

NN = (((1,), (0,)), ((), ()))
NT = (((1,), (1,)), ((), ()))
TN = (((0,), (0,)), ((), ()))
HI = lax.Precision.HIGHEST


def _dot(a, b, dims, hi=False):
    if hi:
        return lax.dot_general(a, b, dims, precision=HI, preferred_element_type=F32)
    return lax.dot_general(a.astype(MXU_DTYPE), b.astype(MXU_DTYPE), dims, preferred_element_type=F32)


@functools.partial(jax.custom_vjp, nondiff_argnums=(2,))
def _nn_d(a, b, hi):
    return _dot(a, b, NN, hi)


@functools.partial(jax.custom_vjp, nondiff_argnums=(2,))
def _nt_d(a, b, hi):
    return _dot(a, b, NT, hi)


@functools.partial(jax.custom_vjp, nondiff_argnums=(2,))
def _tn_d(a, b, hi):
    return _dot(a, b, TN, hi)


_nn_d.defvjp(lambda a, b, hi: (_dot(a, b, NN, hi), (a, b)),
             lambda hi, r, g: (_nt_d(g, r[1], hi), _tn_d(r[0], g, hi)))
_nt_d.defvjp(lambda a, b, hi: (_dot(a, b, NT, hi), (a, b)),
             lambda hi, r, g: (_nn_d(g, r[1], hi), _tn_d(g, r[0], hi)))
_tn_d.defvjp(lambda a, b, hi: (_dot(a, b, TN, hi), (a, b)),
             lambda hi, r, g: (_nt_d(r[1], g, hi), _nn_d(r[0], g, hi)))


class _RawOps:
    nn = staticmethod(lambda a, b, hi=False: _dot(a, b, NN, hi))
    nt = staticmethod(lambda a, b, hi=False: _dot(a, b, NT, hi))
    tn = staticmethod(lambda a, b, hi=False: _dot(a, b, TN, hi))


class _DiffOps:
    nn = staticmethod(lambda a, b, hi=False: _nn_d(a, b, hi))
    nt = staticmethod(lambda a, b, hi=False: _nt_d(a, b, hi))
    tn = staticmethod(lambda a, b, hi=False: _tn_d(a, b, hi))


def _pick(dim, cands=(512, 384, 256, 128)):
    for c in cands:
        if dim % c == 0:
            return c
    return dim


def _pc(body, *, grid, in_specs, out_specs, out_shape, name, scratch=()):
    return pl.pallas_call(
        body, grid=grid, in_specs=in_specs, out_specs=out_specs, out_shape=out_shape,
        scratch_shapes=list(scratch), name=name,
        compiler_params=pltpu.CompilerParams(dimension_semantics=("arbitrary",) * len(grid),
                                             vmem_limit_bytes=VMEM_LIMIT))


def _rows(ts, w, cb=0):
    return pl.BlockSpec((ts, w), lambda i, *_: (i, cb))


def _whole(shape):
    return pl.BlockSpec(shape, lambda *_: (0,) * len(shape))


def _sds(shape, dtype=F32):
    return jax.ShapeDtypeStruct(shape, dtype)


def _mm(a, b, *, name, ta=False, tb=False, out_dtype=F32, a_fn=None, epi=None, e=None):
    m, k = (a.shape[1], a.shape[0]) if ta else a.shape
    n = b.shape[0] if tb else b.shape[1]
    assert k == (b.shape[1] if tb else b.shape[0]), (a.shape, b.shape, ta, tb)
    wide = (1024, 768, 512, 384, 256, 128)
    if k <= 2048 and not ta:
        tm, tn, tk = _pick(m, (512, 256, 128)), _pick(n, wide), k
    else:
        tm, tn, tk = _pick(m, wide), _pick(n, wide), _pick(k, wide)
    nk = k // tk
    dims = (((0 if ta else 1,), (1 if tb else 0,)), ((), ()))

    def body(*refs):
        if e is None:
            a_ref, b_ref, o_ref, acc = refs
        else:
            a_ref, b_ref, e_ref, o_ref, acc = refs
        kk = pl.program_id(2)

        @pl.when(kk == 0)
        def _():
            acc[...] = jnp.zeros_like(acc)

        av = a_ref[...]
        if a_fn is not None:
            av = a_fn(av.astype(F32))
        acc[...] += lax.dot_general(av.astype(MXU_DTYPE), b_ref[...].astype(MXU_DTYPE), dims,
                                    preferred_element_type=F32)

        @pl.when(kk == nk - 1)
        def _():
            r = acc[...]
            if epi is not None:
                r = epi(r, e_ref[...].astype(F32))
            o_ref[...] = r.astype(out_dtype)

    a_spec = pl.BlockSpec((tk, tm), lambda i, j, kk: (kk, i)) if ta else pl.BlockSpec((tm, tk), lambda i, j, kk: (i, kk))
    b_spec = pl.BlockSpec((tn, tk), lambda i, j, kk: (j, kk)) if tb else pl.BlockSpec((tk, tn), lambda i, j, kk: (kk, j))
    o_spec = pl.BlockSpec((tm, tn), lambda i, j, kk: (i, j))
    ins, specs = [a, b], [a_spec, b_spec]
    if e is not None:
        assert e.shape == (m, n)
        ins.append(e)
        specs.append(o_spec)
    return _pc(body, grid=(m // tm, n // tn, nk), in_specs=specs, out_specs=o_spec,
               out_shape=_sds((m, n), out_dtype), name=name, scratch=[pltpu.VMEM((tm, tn), F32)])(*ins)


def _relu2(v):
    r = jnp.maximum(v, 0.0)
    return r * r


def _relu2_bwd(acc, h1):
    return acc * (2.0 * jnp.maximum(h1, 0.0))


def _add_alpha(acc, dr):
    return acc + ALPHA * dr


def _ln(r, g, b):
    mu = jnp.mean(r, -1, keepdims=True)
    var = jnp.mean(jnp.square(r - mu), -1, keepdims=True)
    return (r - mu) * lax.rsqrt(var + LN_EPS) * g + b


def _ln_fwd(x, y, g, b, name):
    s, d = x.shape
    ts = _pick(s, (256, 128))

    def body(x_ref, y_ref, g_ref, b_ref, o_ref, ob_ref):
        o = _ln(ALPHA * x_ref[...] + y_ref[...], g_ref[...], b_ref[...])
        o_ref[...] = o
        ob_ref[...] = o.astype(ob_ref.dtype)

    return _pc(body, grid=(s // ts,), in_specs=[_rows(ts, d), _rows(ts, d), _whole((1, d)), _whole((1, d))],
               out_specs=[_rows(ts, d), _rows(ts, d)], out_shape=[_sds((s, d)), _sds((s, d), MXU_DTYPE)],
               name=name)(x, y, g, b)


def _ln_bwd(x, y, g, b, dout, name):
    s, d = x.shape
    ts = _pick(s, (256, 128))

    def body(x_ref, y_ref, g_ref, b_ref, do_ref, dr_ref, drb_ref, dg_ref, db_ref):
        @pl.when(pl.program_id(0) == 0)
        def _():
            dg_ref[...] = jnp.zeros_like(dg_ref)
            db_ref[...] = jnp.zeros_like(db_ref)

        r = ALPHA * x_ref[...] + y_ref[...]
        _, vjp = jax.vjp(_ln, r, g_ref[...], b_ref[...])
        dr, dg, db = vjp(do_ref[...])
        dr_ref[...] = dr
        drb_ref[...] = dr.astype(drb_ref.dtype)
        dg_ref[...] += dg
        db_ref[...] += db

    return _pc(body, grid=(s // ts,),
               in_specs=[_rows(ts, d), _rows(ts, d), _whole((1, d)), _whole((1, d)), _rows(ts, d)],
               out_specs=[_rows(ts, d), _rows(ts, d), _whole((1, d)), _whole((1, d))],
               out_shape=[_sds((s, d)), _sds((s, d), MXU_DTYPE), _sds((1, d)), _sds((1, d))], name=name)(x, y, g, b, dout)


def _loss_and_grad(y, target, name):
    s, d = y.shape
    ts = _pick(s, (256, 128))

    def body(y_ref, t_ref, l_ref, dy_ref):
        @pl.when(pl.program_id(0) == 0)
        def _():
            l_ref[...] = jnp.zeros_like(l_ref)

        diff = y_ref[...] - t_ref[...]
        per_tok = jnp.mean(jnp.square(diff), -1, keepdims=True)
        l_ref[...] += 0.5 * jnp.sum(per_tok, 0, keepdims=True) * jnp.ones((1, LANES), F32)
        dy_ref[...] = diff * (1.0 / d)

    return _pc(body, grid=(s // ts,), in_specs=[_rows(ts, d), _rows(ts, d)],
               out_specs=[_whole((1, LANES)), _rows(ts, d)],
               out_shape=[_sds((1, LANES)), _sds((s, d))], name=name)(y, target)


def _rope(blk, cs, sn):
    return blk * cs + pltpu.roll(blk, 64, 1) * sn


def _rope_t(dblk, cs, sn):
    return dblk * cs + pltpu.roll(dblk * sn, 64, 1)


def _rms(v, g):
    return v * lax.rsqrt(jnp.mean(v * v, -1, keepdims=True) + RMS_EPS) * g


def _mla_prep_fwd(h, qn, kvn, cs, sn, name):
    s = h.shape[0]
    ts = _pick(s, (512, 256, 128))

    def body(cq_ref, ckv_ref, kr_ref, qn_ref, kvn_ref, cs_ref, sn_ref, cqn_ref, ckvn_ref, krr_ref):
        cqn_ref[...] = _rms(cq_ref[...], qn_ref[...]).astype(cqn_ref.dtype)
        ckvn_ref[...] = _rms(ckv_ref[...], kvn_ref[...]).astype(ckvn_ref.dtype)
        krr_ref[...] = _rope(kr_ref[...], cs_ref[...], sn_ref[...])

    return _pc(body, grid=(s // ts,),
               in_specs=[_rows(ts, 512, 0), _rows(ts, 512, 1), _rows(ts, LANES, 12), _whole((1, 512)), _whole((1, 512)),
                         _rows(ts, LANES), _rows(ts, LANES)],
               out_specs=[_rows(ts, 512), _rows(ts, 512), _rows(ts, LANES)],
               out_shape=[_sds((s, 512), MXU_DTYPE), _sds((s, 512), MXU_DTYPE), _sds((s, LANES))],
               name=name)(h, h, h, qn, kvn, cs, sn)


def _mla_prep_bwd(h, qn, kvn, cs, sn, dcqn, dckvn, dkrr_heads, name):
    s = h.shape[0]
    ts = _pick(s, (512, 256, 128))

    def body(cq_ref, ckv_ref, qn_ref, kvn_ref, cs_ref, sn_ref, dcqn_ref, dckvn_ref, dkrr_ref,
             dcq_ref, dckv_ref, dkr_ref, dqn_ref, dkvn_ref):
        @pl.when(pl.program_id(0) == 0)
        def _():
            dqn_ref[...] = jnp.zeros_like(dqn_ref)
            dkvn_ref[...] = jnp.zeros_like(dkvn_ref)

        _, vjp = jax.vjp(_rms, cq_ref[...], qn_ref[...])
        dcq, dqn = vjp(dcqn_ref[...])
        dcq_ref[...] = dcq
        dqn_ref[...] += dqn
        _, vjp = jax.vjp(_rms, ckv_ref[...], kvn_ref[...])
        dckv, dkvn = vjp(dckvn_ref[...])
        dckv_ref[...] = dckv
        dkvn_ref[...] += dkvn
        dkrr = dkrr_ref[0]
        for hh in range(1, N_MIX_HEADS):
            dkrr = dkrr + dkrr_ref[hh]
        dkr_ref[...] = _rope_t(dkrr, cs_ref[...], sn_ref[...])

    heads3 = pl.BlockSpec((N_MIX_HEADS, ts, LANES), lambda i: (0, i, 0))
    return _pc(body, grid=(s // ts,),
               in_specs=[_rows(ts, 512, 0), _rows(ts, 512, 1), _whole((1, 512)), _whole((1, 512)),
                         _rows(ts, LANES), _rows(ts, LANES), _rows(ts, 512), _rows(ts, 512), heads3],
               out_specs=[_rows(ts, 512), _rows(ts, 512), _rows(ts, LANES), _whole((1, 512)), _whole((1, 512))],
               out_shape=[_sds((s, 512)), _sds((s, 512)), _sds((s, LANES)), _sds((1, 512)), _sds((1, 512))],
               name=name)(h, h, qn, kvn, cs, sn, dcqn, dckvn, dkrr_heads)


def _att_tiles(s):
    t = _pick(s, (512, 256, 128))
    return t, s // t


ATT_HEADS_PER_STEP = 6


def _tri_pairs(nb, k_major):
    pairs = [(i, j) for j in range(nb) for i in range(j, nb)] if k_major else [(i, j) for i in range(nb) for j in range(i + 1)]
    return jnp.asarray([p[0] for p in pairs], jnp.int32), jnp.asarray([p[1] for p in pairs], jnp.int32)


ATT_STRIP = 64


def _scaled_q(q_ref, c0, cs_ref, sn_ref, dst, hh):
    dst[hh, :, :LANES] = (q_ref[:, c0:c0 + LANES] * ATT_SCALE).astype(MXU_DTYPE)
    dst[hh, :, LANES:] = (_rope(q_ref[:, c0 + LANES:c0 + Q_HEAD_P], cs_ref[...], sn_ref[...]) * ATT_SCALE).astype(MXU_DTYPE)


def _cat_k(kv_ref, c0, kr, dst, hh):
    dst[hh, :, :LANES] = kv_ref[:, c0:c0 + LANES].astype(MXU_DTYPE)
    dst[hh, :, LANES:] = kr


def _tril_rows(r, n, t):
    return lax.broadcasted_iota(jnp.int32, (n, t), 1) <= r + lax.broadcasted_iota(jnp.int32, (n, t), 0)


def _pc_pairs(body, pairs, *, n_groups, in_specs, out_specs, out_shape, name, scratch, args):
    return pl.pallas_call(
        body, out_shape=out_shape, name=name,
        grid_spec=pltpu.PrefetchScalarGridSpec(num_scalar_prefetch=2, grid=(n_groups, pairs[0].shape[0]), in_specs=in_specs,
                                               out_specs=out_specs, scratch_shapes=list(scratch)),
        compiler_params=pltpu.CompilerParams(dimension_semantics=("arbitrary", "arbitrary"), vmem_limit_bytes=VMEM_LIMIT),
    )(*pairs, *args)


def _att_specs(t):
    qrow = lambda w: pl.BlockSpec((t, w), lambda h, p, qi, kj: (qi[p], h))
    krow = lambda w: pl.BlockSpec((t, w), lambda h, p, qi, kj: (kj[p], h))
    qtab = pl.BlockSpec((t, LANES), lambda h, p, qi, kj: (qi[p], 0))
    ktab = pl.BlockSpec((t, LANES), lambda h, p, qi, kj: (kj[p], 0))
    return qrow, krow, qtab, ktab


def _flash_fwd(q, kv, krr, cs, sn, name):
    s = q.shape[0]
    t, nb = _att_tiles(s)
    ah = ATT_HEADS_PER_STEP

    def body(qi_ref, kj_ref, q_ref, cs_ref, sn_ref, kv_ref, kr_ref, o_ref, lse_ref, q_s, m_s, l_s, acc_s, s_scr, p_scr):
        i, j = qi_ref[pl.program_id(1)], kj_ref[pl.program_id(1)]

        @pl.when(j == 0)
        def _():
            for hh in range(ah):
                _scaled_q(q_ref, hh * Q_HEAD_P, cs_ref, sn_ref, q_s, hh)
            m_s[...] = jnp.full_like(m_s, -jnp.inf)
            l_s[...] = jnp.zeros_like(l_s)
            acc_s[...] = jnp.zeros_like(acc_s)

        def update(masked):
            kr = kr_ref[...].astype(MXU_DTYPE)
            for hh in range(ah):
                c0 = hh * Q_HEAD_P
                s_scr[hh] = _dot(q_s[hh, :, :LANES], kv_ref[:, c0:c0 + LANES], NT) + _dot(q_s[hh, :, LANES:], kr, NT)
                for r in range(0, t, ATT_STRIP):
                    rows = slice(r, r + ATT_STRIP)
                    sc = s_scr[hh, rows, :]
                    if masked:
                        sc = jnp.where(_tril_rows(r, ATT_STRIP, t), sc, -jnp.inf)
                    m_old = m_s[hh, rows, :]
                    m_new = jnp.maximum(m_old, jnp.max(sc, -1, keepdims=True))
                    p = jnp.exp(sc - m_new[:, :1])
                    corr = jnp.exp(m_old - m_new)
                    l_s[hh, rows, :] = corr * l_s[hh, rows, :] + jnp.sum(p, -1, keepdims=True)
                    acc_s[hh, rows, :] = corr * acc_s[hh, rows, :]
                    m_s[hh, rows, :] = m_new
                    p_scr[hh, rows, :] = p.astype(MXU_DTYPE)
                acc_s[hh] += _dot(p_scr[hh], kv_ref[:, c0 + LANES:c0 + Q_HEAD_P], NN)

        @pl.when(j < i)
        def _():
            update(False)

        @pl.when(j == i)
        def _():
            update(True)
            for hh in range(ah):
                sl = slice(hh * LANES, (hh + 1) * LANES)
                o_ref[:, sl] = acc_s[hh] / l_s[hh]
                lse_ref[:, sl] = m_s[hh] + jnp.log(l_s[hh])

    qrow, krow, qtab, ktab = _att_specs(t)
    return _pc_pairs(body, _tri_pairs(nb, False), n_groups=N_MIX_HEADS // ah,
                     in_specs=[qrow(ah * Q_HEAD_P), qtab, qtab, krow(ah * Q_HEAD_P), ktab],
                     out_specs=[qrow(ah * LANES), qrow(ah * LANES)],
                     out_shape=[_sds((s, MIX_WIDTH)), _sds((s, MIX_WIDTH))], name=name,
                     scratch=[pltpu.VMEM((ah, t, Q_HEAD_P), MXU_DTYPE),
                              pltpu.VMEM((ah, t, LANES), F32), pltpu.VMEM((ah, t, LANES), F32), pltpu.VMEM((ah, t, LANES), F32),
                              pltpu.VMEM((ah, t, t), F32), pltpu.VMEM((ah, t, t), MXU_DTYPE)],
                     args=(q, cs, sn, kv, krr))


def _flash_bwd_q(q, kv, krr, cs, sn, o, lse, dcat, name):
    s = q.shape[0]
    t, nb = _att_tiles(s)
    ah = ATT_HEADS_PER_STEP

    def body(qi_ref, kj_ref, q_ref, cs_ref, sn_ref, o_ref, lse_ref, do_ref, kv_ref, kr_ref, dq_ref,
             q_s, k_s, dl_s, aq_s, s_scr, dp_scr, ds_scr):
        i, j = qi_ref[pl.program_id(1)], kj_ref[pl.program_id(1)]

        @pl.when(j == 0)
        def _():
            for hh in range(ah):
                sl = slice(hh * LANES, (hh + 1) * LANES)
                _scaled_q(q_ref, hh * Q_HEAD_P, cs_ref, sn_ref, q_s, hh)
                dl_s[hh] = jnp.sum(o_ref[:, sl] * do_ref[:, sl], -1, keepdims=True) * jnp.ones((1, LANES), F32)
            aq_s[...] = jnp.zeros_like(aq_s)

        def update(masked):
            kr = kr_ref[...].astype(MXU_DTYPE)
            for hh in range(ah):
                c0 = hh * Q_HEAD_P
                sl = slice(hh * LANES, (hh + 1) * LANES)
                _cat_k(kv_ref, c0, kr, k_s, hh)
                s_scr[hh] = _dot(q_s[hh], k_s[hh], NT)
                dp_scr[hh] = _dot(do_ref[:, sl], kv_ref[:, c0 + LANES:c0 + Q_HEAD_P], NT)
                for r in range(0, t, ATT_STRIP):
                    rows = slice(r, r + ATT_STRIP)
                    p = jnp.exp(s_scr[hh, rows, :] - lse_ref[rows, hh * LANES:hh * LANES + 1])
                    if masked:
                        p = jnp.where(_tril_rows(r, ATT_STRIP, t), p, 0.0)
                    ds_scr[hh, rows, :] = (p * (dp_scr[hh, rows, :] - dl_s[hh, rows, :1])).astype(MXU_DTYPE)
                aq_s[hh] += _dot(ds_scr[hh], k_s[hh], NN)

        @pl.when(j < i)
        def _():
            update(False)

        @pl.when(j == i)
        def _():
            update(True)
            for hh in range(ah):
                c0 = hh * Q_HEAD_P
                dq_ref[:, c0:c0 + LANES] = (aq_s[hh, :, :LANES] * ATT_SCALE).astype(dq_ref.dtype)
                dq_ref[:, c0 + LANES:c0 + Q_HEAD_P] = _rope_t(aq_s[hh, :, LANES:] * ATT_SCALE, cs_ref[...],
                                                              sn_ref[...]).astype(dq_ref.dtype)

    qrow, krow, qtab, ktab = _att_specs(t)
    return _pc_pairs(body, _tri_pairs(nb, False), n_groups=N_MIX_HEADS // ah,
                     in_specs=[qrow(ah * Q_HEAD_P), qtab, qtab, qrow(ah * LANES), qrow(ah * LANES), qrow(ah * LANES),
                               krow(ah * Q_HEAD_P), ktab],
                     out_specs=qrow(ah * Q_HEAD_P), out_shape=_sds((s, N_MIX_HEADS * Q_HEAD_P), MXU_DTYPE), name=name,
                     scratch=[pltpu.VMEM((ah, t, Q_HEAD_P), MXU_DTYPE), pltpu.VMEM((ah, t, Q_HEAD_P), MXU_DTYPE),
                              pltpu.VMEM((ah, t, LANES), F32), pltpu.VMEM((ah, t, Q_HEAD_P), F32),
                              pltpu.VMEM((ah, t, t), F32), pltpu.VMEM((ah, t, t), F32), pltpu.VMEM((ah, t, t), MXU_DTYPE)],
                     args=(q, cs, sn, o, lse, dcat, kv, krr))


def _flash_bwd_kv(q, kv, krr, cs, sn, o, lse, dcat, name):
    s = q.shape[0]
    t, nb = _att_tiles(s)
    ah = ATT_HEADS_PER_STEP

    def body(qi_ref, kj_ref, kv_ref, kr_ref, q_ref, cs_ref, sn_ref, o_ref, lse_ref, do_ref, dkv_ref, dkr_ref,
             ak_s, av_s, q_s, k_s, s_scr, dp_scr, p_scr, ds_scr):
        i, j = qi_ref[pl.program_id(1)], kj_ref[pl.program_id(1)]

        @pl.when(i == j)
        def _():
            ak_s[...] = jnp.zeros_like(ak_s)
            av_s[...] = jnp.zeros_like(av_s)

        def update(masked):
            kr = kr_ref[...].astype(MXU_DTYPE)
            for hh in range(ah):
                c0 = hh * Q_HEAD_P
                sl = slice(hh * LANES, (hh + 1) * LANES)
                _scaled_q(q_ref, c0, cs_ref, sn_ref, q_s, hh)
                _cat_k(kv_ref, c0, kr, k_s, hh)
                do = do_ref[:, sl].astype(MXU_DTYPE)
                s_scr[hh] = _dot(q_s[hh], k_s[hh], NT)
                dp_scr[hh] = _dot(do, kv_ref[:, c0 + LANES:c0 + Q_HEAD_P], NT)
                for r in range(0, t, ATT_STRIP):
                    rows = slice(r, r + ATT_STRIP)
                    p = jnp.exp(s_scr[hh, rows, :] - lse_ref[rows, hh * LANES:hh * LANES + 1])
                    if masked:
                        p = jnp.where(_tril_rows(r, ATT_STRIP, t), p, 0.0)
                    dl = jnp.sum(o_ref[rows, sl] * do_ref[rows, sl], -1, keepdims=True)
                    p_scr[hh, rows, :] = p.astype(MXU_DTYPE)
                    ds_scr[hh, rows, :] = (p * (dp_scr[hh, rows, :] - dl)).astype(MXU_DTYPE)
                av_s[hh] += _dot(p_scr[hh], do, TN)
                ak_s[hh] += _dot(ds_scr[hh], q_s[hh], TN)

        @pl.when(i > j)
        def _():
            update(False)

        @pl.when(i == j)
        def _():
            update(True)

        @pl.when(i == nb - 1)
        def _():
            for hh in range(ah):
                c0 = hh * Q_HEAD_P
                dkv_ref[:, c0:c0 + LANES] = ak_s[hh, :, :LANES].astype(dkv_ref.dtype)
                dkv_ref[:, c0 + LANES:c0 + Q_HEAD_P] = av_s[hh].astype(dkv_ref.dtype)
                dkr_ref[hh] = ak_s[hh, :, LANES:]

    qrow, krow, qtab, ktab = _att_specs(t)
    return _pc_pairs(body, _tri_pairs(nb, True), n_groups=N_MIX_HEADS // ah,
                     in_specs=[krow(ah * Q_HEAD_P), ktab, qrow(ah * Q_HEAD_P), qtab, qtab,
                               qrow(ah * LANES), qrow(ah * LANES), qrow(ah * LANES)],
                     out_specs=[krow(ah * Q_HEAD_P), pl.BlockSpec((ah, t, LANES), lambda h, p, qi, kj: (h, kj[p], 0))],
                     out_shape=[_sds((s, N_MIX_HEADS * Q_HEAD_P), MXU_DTYPE), _sds((N_MIX_HEADS, s, LANES))], name=name,
                     scratch=[pltpu.VMEM((ah, t, Q_HEAD_P), F32), pltpu.VMEM((ah, t, LANES), F32),
                              pltpu.VMEM((ah, t, Q_HEAD_P), MXU_DTYPE), pltpu.VMEM((ah, t, Q_HEAD_P), MXU_DTYPE),
                              pltpu.VMEM((ah, t, t), F32), pltpu.VMEM((ah, t, t), F32),
                              pltpu.VMEM((ah, t, t), MXU_DTYPE), pltpu.VMEM((ah, t, t), MXU_DTYPE)],
                     args=(kv, krr, q, cs, sn, o, lse, dcat))


def _mem_head(ops, qh, kh, vh):
    sc = ops.nt(qh, kh) * HEAD_DIM ** -0.5
    e = jnp.exp(sc - lax.stop_gradient(jnp.max(sc, -1, keepdims=True)))
    p = e / jnp.sum(e, -1, keepdims=True)
    return ops.nn(p, vh)


def _mem_fwd(h, qcol, mem_kv, name):
    s = h.shape[0]
    m = mem_kv.shape[0]
    ts = _pick(s, (512, 256, 128))

    def body(q_ref, kv_ref, o_ref):
        for hh in range(N_MEM_HEADS):
            sl = slice(hh * HEAD_DIM, (hh + 1) * HEAD_DIM)
            vsl = slice(MEM_WIDTH + hh * HEAD_DIM, MEM_WIDTH + (hh + 1) * HEAD_DIM)
            o_ref[:, sl] = _mem_head(_RawOps, q_ref[:, sl], kv_ref[:, sl], kv_ref[:, vsl])

    return _pc(body, grid=(s // ts,), in_specs=[_rows(ts, MEM_WIDTH, qcol), _whole((m, 2 * MEM_WIDTH))],
               out_specs=_rows(ts, MEM_WIDTH), out_shape=_sds((s, MEM_WIDTH)), name=name)(h, mem_kv)


def _mem_bwd(h, qcol, mem_kv, dcat, name):
    s = h.shape[0]
    m = mem_kv.shape[0]
    ts = _pick(s, (512, 256, 128))

    def body(q_ref, kv_ref, do_ref, dq_ref, dkv_ref):
        @pl.when(pl.program_id(0) == 0)
        def _():
            dkv_ref[...] = jnp.zeros_like(dkv_ref)

        for hh in range(N_MEM_HEADS):
            sl = slice(hh * HEAD_DIM, (hh + 1) * HEAD_DIM)
            vsl = slice(MEM_WIDTH + hh * HEAD_DIM, MEM_WIDTH + (hh + 1) * HEAD_DIM)
            _, vjp = jax.vjp(functools.partial(_mem_head, _DiffOps), q_ref[:, sl], kv_ref[:, sl], kv_ref[:, vsl])
            dq, dk, dv = vjp(do_ref[:, sl])
            dq_ref[:, sl] = dq
            dkv_ref[:, sl] += dk
            dkv_ref[:, vsl] += dv

    return _pc(body, grid=(s // ts,),
               in_specs=[_rows(ts, MEM_WIDTH, qcol), _whole((m, 2 * MEM_WIDTH)), _rows(ts, MEM_WIDTH, 3)],
               out_specs=[_rows(ts, MEM_WIDTH), _whole((m, 2 * MEM_WIDTH))],
               out_shape=[_sds((s, MEM_WIDTH)), _sds((m, 2 * MEM_WIDTH))], name=name)(h, mem_kv, dcat)


CONV_COLS = 3 * MIX_WIDTH
HALO = 8


def _conv_fwd(h, w, name):
    s = h.shape[0]
    ts = _pick(s, (512, 256, 128))
    wc = 512

    def body(x_ref, halo_ref, w_ref, o_ref, ext):
        i = pl.program_id(0)
        ext[pl.ds(0, HALO), :] = jnp.where(i > 0, halo_ref[...], 0.0)
        ext[pl.ds(HALO, ts), :] = x_ref[...]
        acc = w_ref[0:1, :] * ext[pl.ds(HALO - 3, ts), :]
        for j in range(1, CONV_WIDTH):
            acc = acc + w_ref[j:j + 1, :] * ext[pl.ds(HALO - 3 + j, ts), :]
        o_ref[...] = acc

    halo = pl.BlockSpec((HALO, wc), lambda i, c: (jnp.maximum(i * (ts // HALO) - 1, 0), c))
    blk = pl.BlockSpec((ts, wc), lambda i, c: (i, c))
    return _pc(body, grid=(s // ts, CONV_COLS // wc),
               in_specs=[blk, halo, pl.BlockSpec((CONV_WIDTH, wc), lambda i, c: (0, c))],
               out_specs=blk, out_shape=_sds((s, CONV_COLS)), name=name,
               scratch=[pltpu.VMEM((HALO + ts, wc), F32)])(h, h, w)


def _conv_bwd(h, w, dout, name):
    s = h.shape[0]
    ts = _pick(s, (512, 256, 128))
    nt = s // ts
    wc = 512

    def body(x_ref, xhalo_ref, d_ref, dhalo_ref, w_ref, dx_ref, dw_ref, xext, dext):
        i = pl.program_id(1)

        @pl.when(i == 0)
        def _():
            dw_ref[...] = jnp.zeros_like(dw_ref)

        xext[pl.ds(0, HALO), :] = jnp.where(i > 0, xhalo_ref[...], 0.0)
        xext[pl.ds(HALO, ts), :] = x_ref[...]
        dext[pl.ds(0, ts), :] = d_ref[...]
        dext[pl.ds(ts, HALO), :] = jnp.where(i < nt - 1, dhalo_ref[...], 0.0)
        d = d_ref[...]
        acc = w_ref[CONV_WIDTH - 1:CONV_WIDTH, :] * d
        for j in range(CONV_WIDTH - 1):
            acc = acc + w_ref[j:j + 1, :] * dext[pl.ds(3 - j, ts), :]
        dx_ref[...] = acc
        for j in range(CONV_WIDTH):
            dw_ref[j:j + 1, :] += jnp.sum(d * xext[pl.ds(HALO - 3 + j, ts), :], 0, keepdims=True)

    blk = pl.BlockSpec((ts, wc), lambda c, i: (i, c))
    halo_prev = pl.BlockSpec((HALO, wc), lambda c, i: (jnp.maximum(i * (ts // HALO) - 1, 0), c))
    halo_next = pl.BlockSpec((HALO, wc), lambda c, i: (jnp.minimum((i + 1) * (ts // HALO), s // HALO - 1), c))
    wspec = pl.BlockSpec((CONV_WIDTH, wc), lambda c, i: (0, c))
    return _pc(body, grid=(CONV_COLS // wc, nt), in_specs=[blk, halo_prev, blk, halo_next, wspec],
               out_specs=[blk, wspec], out_shape=[_sds((s, CONV_COLS)), _sds((CONV_WIDTH, CONV_COLS))], name=name,
               scratch=[pltpu.VMEM((HALO + ts, wc), F32), pltpu.VMEM((ts + HALO, wc), F32)])(h, h, dout, dout, w)


def _silu(v):
    return v * jax.nn.sigmoid(v)


def _softplus(v):
    return jnp.maximum(v, 0.0) + jnp.log1p(jnp.exp(-jnp.abs(v)))


def _gdn_prep_head(cq, ck, cv, a, b, alog, dtb):
    q = _silu(cq)
    q = q * lax.rsqrt(jnp.sum(q * q, -1, keepdims=True) + 1e-6) * HEAD_DIM ** -0.5
    k = _silu(ck)
    k = k * lax.rsqrt(jnp.sum(k * k, -1, keepdims=True) + 1e-6)
    v = _silu(cv)
    g = -jnp.exp(alog) * _softplus(a + dtb)
    beta = jax.nn.sigmoid(b)
    ones = jnp.ones((1, HEAD_DIM), F32)
    return q, k, v, g * ones, beta * ones


def _onehot_lane(idx):
    return (lax.broadcasted_iota(jnp.int32, (1, LANES), 1) == idx).astype(F32)


def _lane_pick(row, idx):
    return jnp.sum(row * _onehot_lane(idx), -1, keepdims=True)


def _gdn_prep_fwd(conv, h, alog, dtb, name):
    s = h.shape[0]
    ts = _pick(s, (128,))
    w = MIX_WIDTH

    def body(c_ref, ab_ref, alog_ref, dtb_ref, q_ref, k_ref, v_ref, g_ref, b_ref):
        ab = ab_ref[...]
        for hh in range(N_MIX_HEADS):
            sl = slice(hh * HEAD_DIM, (hh + 1) * HEAD_DIM)
            cols = [slice(p * w + hh * HEAD_DIM, p * w + (hh + 1) * HEAD_DIM) for p in range(3)]
            outs = _gdn_prep_head(c_ref[:, cols[0]], c_ref[:, cols[1]], c_ref[:, cols[2]], _lane_pick(ab, hh),
                                  _lane_pick(ab, N_MIX_HEADS + hh), _lane_pick(alog_ref[...], hh),
                                  _lane_pick(dtb_ref[...], hh))
            for ref, val in zip((q_ref, k_ref, v_ref, g_ref, b_ref), outs):
                ref[:, sl] = val

    return _pc(body, grid=(s // ts,),
               in_specs=[_rows(ts, 3 * w), _rows(ts, LANES, (4 * w + MEM_WIDTH) // LANES), _whole((1, LANES)), _whole((1, LANES))],
               out_specs=[_rows(ts, w)] * 5, out_shape=[_sds((s, w))] * 5, name=name)(conv, h, alog, dtb)


def _gdn_prep_bwd(conv, h, alog, dtb, dq, dk, dv, dg, db, name):
    s = h.shape[0]
    ts = _pick(s, (128,))
    w = MIX_WIDTH

    def body(c_ref, ab_ref, alog_ref, dtb_ref, dq_ref, dk_ref, dv_ref, dg_ref, db_ref,
             dc_ref, dab_ref, dalog_ref, ddtb_ref):
        @pl.when(pl.program_id(0) == 0)
        def _():
            dalog_ref[...] = jnp.zeros_like(dalog_ref)
            ddtb_ref[...] = jnp.zeros_like(ddtb_ref)

        ab = ab_ref[...]
        dab = jnp.zeros_like(ab)
        dalog = jnp.zeros((1, LANES), F32)
        ddtb = jnp.zeros((1, LANES), F32)
        for hh in range(N_MIX_HEADS):
            sl = slice(hh * HEAD_DIM, (hh + 1) * HEAD_DIM)
            cols = [slice(p * w + hh * HEAD_DIM, p * w + (hh + 1) * HEAD_DIM) for p in range(3)]
            _, vjp = jax.vjp(_gdn_prep_head, c_ref[:, cols[0]], c_ref[:, cols[1]], c_ref[:, cols[2]], _lane_pick(ab, hh),
                             _lane_pick(ab, N_MIX_HEADS + hh), _lane_pick(alog_ref[...], hh), _lane_pick(dtb_ref[...], hh))
            dcq, dck, dcv, da, dbb, dal, ddt = vjp((dq_ref[:, sl], dk_ref[:, sl], dv_ref[:, sl], dg_ref[:, sl], db_ref[:, sl]))
            dc_ref[:, cols[0]] = dcq
            dc_ref[:, cols[1]] = dck
            dc_ref[:, cols[2]] = dcv
            dab = dab + da * _onehot_lane(hh) + dbb * _onehot_lane(N_MIX_HEADS + hh)
            dalog = dalog + dal * _onehot_lane(hh)
            ddtb = ddtb + ddt * _onehot_lane(hh)
        dab_ref[...] = dab
        dalog_ref[...] += dalog
        ddtb_ref[...] += ddtb

    return _pc(body, grid=(s // ts,),
               in_specs=[_rows(ts, 3 * w), _rows(ts, LANES, (4 * w + MEM_WIDTH) // LANES),
                         _whole((1, LANES)), _whole((1, LANES))] + [_rows(ts, w)] * 5,
               out_specs=[_rows(ts, 3 * w), _rows(ts, LANES), _whole((1, LANES)), _whole((1, LANES))],
               out_shape=[_sds((s, 3 * w)), _sds((s, LANES)), _sds((1, LANES)), _sds((1, LANES))],
               name=name)(conv, h, alog, dtb, dq, dk, dv, dg, db)


NNB = (((2,), (1,)), ((0,), (0,)))
NTB = (((2,), (2,)), ((0,), (0,)))
TNB = (((1,), (1,)), ((0,), (0,)))


def _dg(a, b, dims):
    return lax.dot_general(a, b, dims, preferred_element_type=F32)


def _dotb(a, b, dims, mode):
    a1 = a.astype(MXU_DTYPE)
    b1 = b.astype(MXU_DTYPE)
    if mode == 1:
        return _dg(a1, b1, dims)
    rb = b - b1.astype(F32)
    b2 = rb.astype(MXU_DTYPE)
    if mode == 3:
        a2 = (a - a1.astype(F32)).astype(MXU_DTYPE)
        return _dg(a1, b1, dims) + (_dg(a1, b2, dims) + _dg(a2, b1, dims))
    b3 = (rb - b2.astype(F32)).astype(MXU_DTYPE)
    return _dg(a1, b1, dims) + (_dg(a1, b2, dims) + _dg(a1, b3, dims))


@functools.partial(jax.custom_vjp, nondiff_argnums=(2,))
def _nnb(a, b, mode):
    return _dotb(a, b, NNB, mode)


@functools.partial(jax.custom_vjp, nondiff_argnums=(2,))
def _ntb(a, b, mode):
    return _dotb(a, b, NTB, mode)


@functools.partial(jax.custom_vjp, nondiff_argnums=(2,))
def _tnb(a, b, mode):
    return _dotb(a, b, TNB, mode)


_nnb.defvjp(lambda a, b, mode: (_dotb(a, b, NNB, mode), (a, b)),
            lambda mode, r, g: (_ntb(g, r[1], mode), _tnb(r[0], g, mode)))
_ntb.defvjp(lambda a, b, mode: (_dotb(a, b, NTB, mode), (a, b)),
            lambda mode, r, g: (_nnb(g, r[1], mode), _tnb(g, r[0], mode)))
_tnb.defvjp(lambda a, b, mode: (_dotb(a, b, TNB, mode), (a, b)),
            lambda mode, r, g: (_ntb(r[1], g, mode), _nnb(r[0], g, mode)))


def _trilb(h):
    t = lax.broadcasted_iota(jnp.int32, (CHUNK, CHUNK), 0) >= lax.broadcasted_iota(jnp.int32, (CHUNK, CHUNK), 1)
    return jnp.broadcast_to(t.astype(F32)[None], (h, CHUNK, CHUNK))


def _cumsum_rows_raw(gb):
    return _dotb(_trilb(gb.shape[0]), gb, NNB, "lhs")


_cumsum_rows = jax.custom_vjp(_cumsum_rows_raw)
_cumsum_rows.defvjp(lambda gb: (_cumsum_rows_raw(gb), None),
                    lambda _, g: (_dotb(_trilb(g.shape[0]), g, TNB, "lhs"),))


def _row_col_raw(gc):
    return gc[:, :, :CHUNK], jnp.swapaxes(gc, 1, 2)[:, :CHUNK, :]


def _row_col_bwd(_, g):
    part = g[0] + jnp.swapaxes(g[1], 1, 2)
    return (jnp.concatenate([part, jnp.zeros_like(part)], -1),)


_row_col = jax.custom_vjp(_row_col_raw)
_row_col.defvjp(lambda gc: (_row_col_raw(gc), None), _row_col_bwd)


def _gdn_chunk(diff, state, q, k, v, gb, bb):
    if diff:
        nn, nt, tn, cumsum, row_col = _nnb, _ntb, _tnb, _cumsum_rows, _row_col
    else:
        nn = lambda a, b, m: _dotb(a, b, NNB, m)
        nt = lambda a, b, m: _dotb(a, b, NTB, m)
        tn = lambda a, b, m: _dotb(a, b, TNB, m)
        cumsum, row_col = _cumsum_rows_raw, _row_col_raw
    c = CHUNK
    row = lax.broadcasted_iota(jnp.int32, (1, c, c), 1)
    col = lax.broadcasted_iota(jnp.int32, (1, c, c), 2)
    tril = row >= col
    strict = row > col
    eye = (row == col).astype(F32)
    gc = cumsum(gb)
    gci, gcj = row_col(gc)
    decay = jnp.where(tril, jnp.exp(jnp.where(tril, gci - gcj, 0.0)), 0.0)
    kb = k * bb
    low = jnp.where(strict, nt(kb, k, 1) * decay, 0.0)
    inv = eye - low
    pw = low
    for _ in range(5):
        pw = nn(pw, pw, 3)
        inv = inv + nn(inv, pw, 3)
    ge = jnp.exp(gc)
    u = nn(inv, v * bb, 3)
    w = nn(inv, kb * ge, 3)
    a_qk = jnp.where(tril, nt(q, k, 1) * decay, 0.0)
    g_tot = jnp.sum(gb, 1, keepdims=True)
    k_dec = k * jnp.exp(g_tot - gc)
    v_new = u - nn(w, state, 1)
    o = nn(q * ge, state, 1) + nn(a_qk, v_new, 1)
    new_state = state * jnp.exp(g_tot) + tn(k_dec, v_new, 1)
    return new_state, o


def _stack_heads(ref, hp):
    return jnp.stack([ref[:, hh * HEAD_DIM:(hh + 1) * HEAD_DIM] for hh in range(hp)])


GDN_HEADS_FWD = 12
GDN_HEADS_BWD = 12


def _gdn_core_fwd(q, k, v, gb, bb, name):
    s = q.shape[0]
    nc = s // CHUNK
    hp = GDN_HEADS_FWD
    w = hp * HEAD_DIM

    def body(q_ref, k_ref, v_ref, g_ref, b_ref, o_ref, st_ref, state):
        @pl.when(pl.program_id(1) == 0)
        def _():
            state[...] = jnp.zeros_like(state)

        st = state[...]
        st_ref[0] = st
        new_state, o = _gdn_chunk(False, st, *(_stack_heads(r, hp) for r in (q_ref, k_ref, v_ref, g_ref, b_ref)))
        state[...] = new_state
        for hh in range(hp):
            o_ref[:, hh * HEAD_DIM:(hh + 1) * HEAD_DIM] = o[hh]

    blk = pl.BlockSpec((CHUNK, w), lambda hg, c: (c, hg))
    return _pc(body, grid=(N_MIX_HEADS // hp, nc), in_specs=[blk] * 5,
               out_specs=[blk, pl.BlockSpec((1, hp, HEAD_DIM, HEAD_DIM), lambda hg, c: (c, hg, 0, 0))],
               out_shape=[_sds((s, MIX_WIDTH)), _sds((nc, N_MIX_HEADS, HEAD_DIM, HEAD_DIM))], name=name,
               scratch=[pltpu.VMEM((hp, HEAD_DIM, HEAD_DIM), F32)])(q, k, v, gb, bb)


def _gdn_core_bwd(q, k, v, gb, bb, states, do, name):
    s = q.shape[0]
    nc = s // CHUNK
    hp = GDN_HEADS_BWD
    w = hp * HEAD_DIM

    def body(q_ref, k_ref, v_ref, g_ref, b_ref, st_ref, do_ref, dq_ref, dk_ref, dv_ref, dg_ref, db_ref, dstate):
        @pl.when(pl.program_id(1) == 0)
        def _():
            dstate[...] = jnp.zeros_like(dstate)

        _, vjp = jax.vjp(functools.partial(_gdn_chunk, True), st_ref[0],
                         *(_stack_heads(r, hp) for r in (q_ref, k_ref, v_ref, g_ref, b_ref)))
        grads = vjp((dstate[...], _stack_heads(do_ref, hp)))
        dstate[...] = grads[0]
        for ref, val in zip((dq_ref, dk_ref, dv_ref, dg_ref, db_ref), grads[1:]):
            for hh in range(hp):
                ref[:, hh * HEAD_DIM:(hh + 1) * HEAD_DIM] = val[hh]

    blk = pl.BlockSpec((CHUNK, w), lambda hg, c: (nc - 1 - c, hg))
    return _pc(body, grid=(N_MIX_HEADS // hp, nc),
               in_specs=[blk] * 5 + [pl.BlockSpec((1, hp, HEAD_DIM, HEAD_DIM), lambda hg, c: (nc - 1 - c, hg, 0, 0)), blk],
               out_specs=[blk] * 5, out_shape=[_sds((s, MIX_WIDTH))] * 5, name=name,
               scratch=[pltpu.VMEM((hp, HEAD_DIM, HEAD_DIM), F32)])(q, k, v, gb, bb, states, do)


def _gdn_out_head(o, z, g):
    return _rms(o, g) * _silu(z)


def _gdn_out_fwd(o, h, onorm, name):
    s = o.shape[0]
    ts = _pick(s, (256, 128))

    def body(o_ref, z_ref, g_ref, y_ref):
        for hh in range(N_MIX_HEADS):
            sl = slice(hh * HEAD_DIM, (hh + 1) * HEAD_DIM)
            y_ref[:, sl] = _gdn_out_head(o_ref[:, sl], z_ref[:, sl], g_ref[...])

    return _pc(body, grid=(s // ts,), in_specs=[_rows(ts, MIX_WIDTH), _rows(ts, MIX_WIDTH, 3), _whole((1, HEAD_DIM))],
               out_specs=_rows(ts, MIX_WIDTH), out_shape=_sds((s, MIX_WIDTH)), name=name)(o, h, onorm)


def _gdn_out_bwd(o, h, onorm, dcat, name):
    s = o.shape[0]
    ts = _pick(s, (256, 128))

    def body(o_ref, z_ref, g_ref, dy_ref, do_ref, dz_ref, dg_ref):
        @pl.when(pl.program_id(0) == 0)
        def _():
            dg_ref[...] = jnp.zeros_like(dg_ref)

        dgs = jnp.zeros((1, HEAD_DIM), F32)
        for hh in range(N_MIX_HEADS):
            sl = slice(hh * HEAD_DIM, (hh + 1) * HEAD_DIM)
            _, vjp = jax.vjp(_gdn_out_head, o_ref[:, sl], z_ref[:, sl], g_ref[...])
            do, dz, dg = vjp(dy_ref[:, sl])
            do_ref[:, sl] = do
            dz_ref[:, sl] = dz
            dgs = dgs + dg
        dg_ref[...] += dgs

    return _pc(body, grid=(s // ts,),
               in_specs=[_rows(ts, MIX_WIDTH), _rows(ts, MIX_WIDTH, 3), _whole((1, HEAD_DIM)), _rows(ts, MIX_WIDTH, 0)],
               out_specs=[_rows(ts, MIX_WIDTH), _rows(ts, MIX_WIDTH), _whole((1, HEAD_DIM))],
               out_shape=[_sds((s, MIX_WIDTH)), _sds((s, MIX_WIDTH)), _sds((1, HEAD_DIM))], name=name)(o, h, onorm, dcat)


def _row(v):
    return v.reshape(1, -1)


def _lane_row(v):
    return jnp.pad(v, (0, LANES - v.shape[0])).reshape(1, LANES)


def _rope_tables(positions):
    inv_freq = 1.0 / (ROPE_THETA ** (jnp.arange(0, QK_ROPE, 2, dtype=F32) / QK_ROPE))
    ang = positions.astype(F32)[:, None] * inv_freq
    cos, sin = jnp.cos(ang), jnp.sin(ang)
    z = jnp.zeros_like(cos)
    return jnp.concatenate([cos, z, cos, z], 1), jnp.concatenate([-sin, z, sin, z], 1)


def _local_step(x, mem, positions, loss_target, wts, small):
    cs, sn = _rope_tables(positions)
    saved = []
    xb = x.astype(MXU_DTYPE)
    for i in range(DEPTH):
        j = i // 2
        sv = {"x": x, "xb": xb}
        sv["mem_kv"] = _mm(mem, wts["mem_w_kv"][i], name=f"l{i}_memkv")
        if i % 2 == 0:
            h = _mm(xb, wts["mla_w_in"][j], name=f"l{i}_in")
            cqn, ckvn, krr = _mla_prep_fwd(h, _row(small["mla_q_norm"][j]), _row(small["mla_kv_norm"][j]), cs, sn, f"l{i}_mlaprep")
            q = _mm(cqn, wts["mla_w_uq"][j], name=f"l{i}_uq")
            kv = _mm(ckvn, wts["mla_w_ukv"][j], out_dtype=MXU_DTYPE, name=f"l{i}_ukv")
            mix, lse = _flash_fwd(q, kv, krr, cs, sn, f"l{i}_flash")
            sv.update(cqn=cqn, ckvn=ckvn, krr=krr, q=q, kv=kv, o=mix, lse=lse)
            qcol = 2
        else:
            h = _mm(xb, wts["gdn_w_in"][j], name=f"l{i}_in")
            conv = _conv_fwd(h, small["gdn_conv"][j], f"l{i}_conv")
            qn, kn, vv, gb, bb = _gdn_prep_fwd(conv, h, _lane_row(small["gdn_a_log"][j]), _lane_row(small["gdn_dt_bias"][j]), f"l{i}_gdnprep")
            o, states = _gdn_core_fwd(qn, kn, vv, gb, bb, f"l{i}_gdncore")
            mix = _gdn_out_fwd(o, h, _row(small["gdn_o_norm"][j]), f"l{i}_gdnout")
            sv.update(conv=conv, qn=qn, kn=kn, vv=vv, gb=gb, bb=bb, o=o, states=states)
            qcol = 4 * MIX_WIDTH // MEM_WIDTH
        mem_o = _mem_fwd(h, qcol, sv["mem_kv"], f"l{i}_mem")
        cat = jnp.concatenate([mix, mem_o], 1).astype(MXU_DTYPE)
        y = _mm(cat, wts["w_out"][i], name=f"l{i}_out")
        x1, x1b = _ln_fwd(x, y, _row(small["ln1_g"][i]), _row(small["ln1_b"][i]), f"l{i}_ln1")
        h1 = _mm(x1b, wts["mlp_w1"][i], name=f"l{i}_w1")
        ff = _mm(h1, wts["mlp_w2"][i], a_fn=_relu2, name=f"l{i}_w2")
        x2, x2b = _ln_fwd(x1, ff, _row(small["ln2_g"][i]), _row(small["ln2_b"][i]), f"l{i}_ln2")
        sv.update(h=h, qcol=qcol, cat=cat, y=y, x1=x1, x1b=x1b, h1=h1, ff=ff)
        saved.append(sv)
        x, xb = x2, x2b

    loss_row, dx = _loss_and_grad(x, loss_target, "loss")

    gw = {k: [None] * len(v) for k, v in wts.items()}
    gs = {k: [None] * v.shape[0] for k, v in small.items()}
    gdt = COMM_DTYPE
    for i in reversed(range(DEPTH)):
        j = i // 2
        sv = saved[i]
        dr2, dr2b, dg, db = _ln_bwd(sv["x1"], sv["ff"], _row(small["ln2_g"][i]), _row(small["ln2_b"][i]), dx, f"l{i}_ln2b")
        gs["ln2_g"][i], gs["ln2_b"][i] = dg[0], db[0]
        dh1 = _mm(dr2b, wts["mlp_w2"][i], tb=True, epi=_relu2_bwd, e=sv["h1"], out_dtype=MXU_DTYPE, name=f"l{i}_dh1")
        gw["mlp_w2"][i] = _mm(sv["h1"], dr2b, ta=True, a_fn=_relu2, out_dtype=gdt, name=f"l{i}_dw2")
        gw["mlp_w1"][i] = _mm(sv["x1b"], dh1, ta=True, out_dtype=gdt, name=f"l{i}_dw1")
        dx1 = _mm(dh1, wts["mlp_w1"][i], tb=True, epi=_add_alpha, e=dr2, name=f"l{i}_dx1")
        dr1, dr1b, dg, db = _ln_bwd(sv["x"], sv["y"], _row(small["ln1_g"][i]), _row(small["ln1_b"][i]), dx1, f"l{i}_ln1b")
        gs["ln1_g"][i], gs["ln1_b"][i] = dg[0], db[0]
        gw["w_out"][i] = _mm(sv["cat"], dr1b, ta=True, out_dtype=gdt, name=f"l{i}_dwout")
        dcat = _mm(dr1b, wts["w_out"][i], tb=True, name=f"l{i}_dcat")
        h = sv["h"]
        dqmem, dmem_kv = _mem_bwd(h, sv["qcol"], sv["mem_kv"], dcat, f"l{i}_memb")
        gw["mem_w_kv"][i] = _mm(mem, dmem_kv, ta=True, out_dtype=gdt, name=f"l{i}_dwmem")
        zpad = jnp.zeros((h.shape[0], LANES), F32)
        if i % 2 == 0:
            qnw, kvnw = _row(small["mla_q_norm"][j]), _row(small["mla_kv_norm"][j])
            dq = _flash_bwd_q(sv["q"], sv["kv"], sv["krr"], cs, sn, sv["o"], sv["lse"], dcat, f"l{i}_flashbq")
            dkv, dkrr = _flash_bwd_kv(sv["q"], sv["kv"], sv["krr"], cs, sn, sv["o"], sv["lse"], dcat, f"l{i}_flashbkv")
            gw["mla_w_ukv"][j] = _mm(sv["ckvn"], dkv, ta=True, out_dtype=gdt, name=f"l{i}_dwukv")
            dckvn = _mm(dkv, wts["mla_w_ukv"][j], tb=True, name=f"l{i}_dckvn")
            gw["mla_w_uq"][j] = _mm(sv["cqn"], dq, ta=True, out_dtype=gdt, name=f"l{i}_dwuq")
            dcqn = _mm(dq, wts["mla_w_uq"][j], tb=True, name=f"l{i}_dcqn")
            dcq, dckv, dkr, dqn, dkvn = _mla_prep_bwd(h, qnw, kvnw, cs, sn, dcqn, dckvn, dkrr, f"l{i}_mlaprepb")
            gs["mla_q_norm"][j], gs["mla_kv_norm"][j] = dqn[0], dkvn[0]
            dh = jnp.concatenate([dcq, dckv, dqmem, dkr, zpad], 1).astype(MXU_DTYPE)
            w_in = wts["mla_w_in"][j]
            key = "mla_w_in"
        else:
            alog, dtb = _lane_row(small["gdn_a_log"][j]), _lane_row(small["gdn_dt_bias"][j])
            do, dz, dgn = _gdn_out_bwd(sv["o"], h, _row(small["gdn_o_norm"][j]), dcat, f"l{i}_gdnoutb")
            gs["gdn_o_norm"][j] = dgn[0]
            dqn, dkn, dvv, dgb, dbb = _gdn_core_bwd(sv["qn"], sv["kn"], sv["vv"], sv["gb"], sv["bb"], sv["states"], do, f"l{i}_gdncoreb")
            dconv, dab, dalog, ddtb = _gdn_prep_bwd(sv["conv"], h, alog, dtb, dqn, dkn, dvv, dgb, dbb, f"l{i}_gdnprepb")
            gs["gdn_a_log"][j], gs["gdn_dt_bias"][j] = dalog[0, :N_MIX_HEADS], ddtb[0, :N_MIX_HEADS]
            dhqkv, dconvw = _conv_bwd(h, small["gdn_conv"][j], dconv, f"l{i}_convb")
            gs["gdn_conv"][j] = dconvw
            dh = jnp.concatenate([dhqkv, dz, dqmem, dab, zpad], 1).astype(MXU_DTYPE)
            w_in = wts["gdn_w_in"][j]
            key = "gdn_w_in"
        gw[key][j] = _mm(sv["xb"], dh, ta=True, out_dtype=gdt, name=f"l{i}_dwin")
        dx = _mm(dh, w_in, tb=True, epi=_add_alpha, e=dr1, name=f"l{i}_dx")
    gs = {k: jnp.stack(v) for k, v in gs.items()}
    return loss_row, dx, gw, gs


def _mla_in_to_kernel(w):
    z32 = jnp.zeros((w.shape[0], 32), w.dtype)
    z128 = jnp.zeros((w.shape[0], LANES), w.dtype)
    return jnp.concatenate([w[:, :1024], w[:, 1088:1600], w[:, 1024:1056], z32, w[:, 1056:1088], z32, z128], 1)


def _mla_in_from_kernel(g):
    return jnp.concatenate([g[:, :1024], g[:, 1536:1568], g[:, 1600:1632], g[:, 1024:1536]], 1)


def _uq_to_kernel(w):
    w3 = w.reshape(Q_LORA, N_MIX_HEADS, QK_NOPE + QK_ROPE)
    z = jnp.zeros((Q_LORA, N_MIX_HEADS, 32), w.dtype)
    return jnp.concatenate([w3[:, :, :128], w3[:, :, 128:160], z, w3[:, :, 160:192], z], 2).reshape(Q_LORA, N_MIX_HEADS * Q_HEAD_P)


def _uq_from_kernel(g):
    g3 = g.reshape(Q_LORA, N_MIX_HEADS, Q_HEAD_P)
    return jnp.concatenate([g3[:, :, :128], g3[:, :, 128:160], g3[:, :, 192:224]], 2).reshape(Q_LORA, -1)


def _gdn_in_to_kernel(w):
    z = jnp.zeros((w.shape[0], LANES - 2 * N_MIX_HEADS + LANES), w.dtype)
    return jnp.concatenate([w[:, :6144], w[:, 6168:6680], w[:, 6144:6168], z], 1)


def _gdn_in_from_kernel(g):
    return jnp.concatenate([g[:, :6144], g[:, 6656:6680], g[:, 6144:6656]], 1)


def _exchange(srcs, name, gather):
    n = len(srcs)

    def body(*refs):
        src_refs, out_refs = refs[:n], refs[n:2 * n]
        send_sems, recv_sems, local_sems = refs[2 * n:]
        x, y, c = lax.axis_index("x"), lax.axis_index("y"), lax.axis_index("c")
        me = 4 * x + 2 * y + c
        local, sends, recvs = [], [], []
        for a in range(n):
            src_ref, out_ref = src_refs[a], out_refs[a]
            local.append(pltpu.make_async_copy(src_ref if gather else src_ref.at[me], out_ref.at[me], local_sems.at[a]))
            for kk in range(1, N_DEV):
                px, py, pc = x ^ ((kk >> 2) & 1), y ^ ((kk >> 1) & 1), c ^ (kk & 1)
                peer = 4 * px + 2 * py + pc
                piece = src_ref if gather else src_ref.at[peer]
                sends.append(pltpu.make_async_remote_copy(
                    src_ref=piece, dst_ref=out_ref.at[me], send_sem=send_sems.at[a, kk - 1], recv_sem=recv_sems.at[a, kk - 1],
                    device_id=(px, py, pc), device_id_type=pl.DeviceIdType.MESH))
                recvs.append(pltpu.make_async_remote_copy(
                    src_ref=piece, dst_ref=out_ref.at[peer], send_sem=send_sems.at[a, kk - 1], recv_sem=recv_sems.at[a, kk - 1],
                    device_id=(px, py, pc), device_id_type=pl.DeviceIdType.MESH))
        for cp in local + sends:
            cp.start()
        for cp in recvs:
            cp.wait_recv()
        for cp in sends:
            cp.wait_send()
        for cp in local:
            cp.wait()

    hbm = pl.BlockSpec(memory_space=pltpu.HBM)
    shapes = [_sds((N_DEV,) + tuple(s.shape if gather else s.shape[1:]), s.dtype) for s in srcs]
    return pl.pallas_call(
        body, in_specs=[hbm] * n, out_specs=[hbm] * n, out_shape=shapes, name=name,
        scratch_shapes=[pltpu.SemaphoreType.DMA((n, N_DEV - 1)), pltpu.SemaphoreType.DMA((n, N_DEV - 1)),
                        pltpu.SemaphoreType.DMA((n,))],
    )(*srcs)


def _adamw(parts, w, m, v, name):
    r, c = w.shape
    n_parts = parts.shape[0]
    tr = _pick(r, tuple(t for t in (256, 128, 64, 32, 16, 8) if t * c <= 256 * 1024))

    def body(p_ref, w_ref, m_ref, v_ref, g_ref, d_ref, nm_ref, nv_ref):
        g = p_ref[0].astype(F32)
        for dd in range(1, n_parts):
            g = g + p_ref[dd].astype(F32)
        nm = ADAM_B1 * m_ref[...] + (1.0 - ADAM_B1) * g
        nv = ADAM_B2 * v_ref[...] + (1.0 - ADAM_B2) * jnp.square(g)
        m_hat = nm / (1.0 - ADAM_B1 ** ADAM_STEP)
        v_hat = nv / (1.0 - ADAM_B2 ** ADAM_STEP)
        g_ref[...] = g
        d_ref[...] = -ADAM_LR * (m_hat / (jnp.sqrt(v_hat) + ADAM_EPS) + ADAM_WD * w_ref[...])
        nm_ref[...] = nm
        nv_ref[...] = nv

    blk = pl.BlockSpec((tr, c), lambda i: (i, 0))
    return _pc(body, grid=(r // tr,), in_specs=[pl.BlockSpec((n_parts, tr, c), lambda i: (0, i, 0)), blk, blk, blk],
               out_specs=[blk] * 4, out_shape=[_sds((r, c))] * 4, name=name)(parts, w, m, v)


N_CHIPS = 4
MESH_IDS = pl.DeviceIdType.MESH


def _place():
    x, y, c = lax.axis_index("x"), lax.axis_index("y"), lax.axis_index("c")
    return x, y, c, [(1 - x, y), (x, 1 - y), (1 - x, 1 - y)]


def _gather_two_level(srcs, name):
    n = len(srcs)

    def body(*refs):
        src_refs, out_refs = refs[:n], refs[n:2 * n]
        send_sems, recv_sems, local_sems = refs[2 * n:]
        x, y, c, chips = _place()
        sib = (x, y, 1 - c)

        def copy(a, k, block, to, src=None):
            rows = out_refs[a].at[4 * block[0] + 2 * block[1] + block[2]]
            return pltpu.make_async_remote_copy(src_ref=rows if src is None else src, dst_ref=rows,
                                                send_sem=send_sems.at[a, k], recv_sem=recv_sems.at[a, k],
                                                device_id=to, device_id_type=MESH_IDS)

        local = [pltpu.make_async_copy(src_refs[a], out_refs[a].at[4 * x + 2 * y + c], local_sems.at[a]) for a in range(n)]
        first = []
        for a in range(n):
            first.append(copy(a, 0, (x, y, c), sib, src_refs[a]))
            first += [copy(a, 1 + j, (x, y, c), (*chip, c), src_refs[a]) for j, chip in enumerate(chips)]
        for cp in local + first:
            cp.start()
        passed = []
        for a in range(n):
            for j, chip in enumerate(chips):
                copy(a, 1 + j, (*chip, c), (x, y, c)).wait_recv()
                passed.append(copy(a, 4 + j, (*chip, c), sib))
                passed[-1].start()
        for a in range(n):
            copy(a, 0, (x, y, 1 - c), (x, y, c)).wait_recv()
            for j, chip in enumerate(chips):
                copy(a, 4 + j, (*chip, 1 - c), (x, y, c)).wait_recv()
        for cp in first + passed:
            cp.wait_send()
        for cp in local:
            cp.wait()

    hbm = pl.BlockSpec(memory_space=pltpu.HBM)
    return pl.pallas_call(
        body, in_specs=[hbm] * n, out_specs=[hbm] * n, name=name,
        out_shape=[_sds((N_DEV,) + tuple(s.shape), s.dtype) for s in srcs],
        scratch_shapes=[pltpu.SemaphoreType.DMA((n, 7)), pltpu.SemaphoreType.DMA((n, 7)), pltpu.SemaphoreType.DMA((n,))],
    )(*srcs)


def _pair_exchange(srcs, name):
    n = len(srcs)

    def body(*refs):
        src_refs, out_refs = refs[:n], refs[n:2 * n]
        send_sems, recv_sems = refs[2 * n:]
        x, y, c, _ = _place()
        sends = []
        for a in range(n):
            for ch in range(N_CHIPS):
                sends.append(pltpu.make_async_remote_copy(
                    src_ref=src_refs[a].at[2 * ch + (1 - c)], dst_ref=out_refs[a].at[ch],
                    send_sem=send_sems.at[a, ch], recv_sem=recv_sems.at[a, ch], device_id=(x, y, 1 - c),
                    device_id_type=MESH_IDS))
        for cp in sends:
            cp.start()
        for cp in sends:
            cp.wait_recv()
        for cp in sends:
            cp.wait_send()

    hbm = pl.BlockSpec(memory_space=pltpu.HBM)
    return pl.pallas_call(
        body, in_specs=[hbm] * n, out_specs=[hbm] * n, name=name,
        out_shape=[_sds((N_CHIPS,) + tuple(s.shape[1:]), s.dtype) for s in srcs],
        scratch_shapes=[pltpu.SemaphoreType.DMA((n, N_CHIPS)), pltpu.SemaphoreType.DMA((n, N_CHIPS))],
    )(*srcs)


def _pair_sum(mine, theirs, my_c, name):
    _, r, c = mine.shape
    tr = _pick(r, tuple(t for t in (512, 256, 128, 64, 32, 16, 8) if t * c <= 512 * 1024))

    def body(c_ref, a_ref, b_ref, o_ref):
        o_ref[...] = (a_ref[...].astype(F32) + b_ref[...].astype(F32)).astype(o_ref.dtype)

    return pl.pallas_call(
        body, out_shape=_sds((N_CHIPS, r, c), mine.dtype), name=name,
        grid_spec=pltpu.PrefetchScalarGridSpec(
            num_scalar_prefetch=1, grid=(N_CHIPS, r // tr),
            in_specs=[pl.BlockSpec((1, tr, c), lambda ch, i, cc: (2 * ch + cc[0], i, 0)),
                      pl.BlockSpec((1, tr, c), lambda ch, i, cc: (ch, i, 0))],
            out_specs=pl.BlockSpec((1, tr, c), lambda ch, i, cc: (ch, i, 0))),
        compiler_params=pltpu.CompilerParams(dimension_semantics=("arbitrary", "arbitrary"), vmem_limit_bytes=VMEM_LIMIT),
    )(my_c, mine, theirs)


def _chip_exchange(srcs, name):
    n = len(srcs)

    def body(*refs):
        src_refs, out_refs = refs[:n], refs[n:2 * n]
        send_sems, recv_sems, local_sems = refs[2 * n:]
        x, y, c, chips = _place()
        my_chip = 2 * x + y
        local = [pltpu.make_async_copy(src_refs[a].at[my_chip], out_refs[a].at[my_chip], local_sems.at[a]) for a in range(n)]
        sends, recvs = [], []
        for a in range(n):
            for j, chip in enumerate(chips):
                ch = 2 * chip[0] + chip[1]
                sends.append(pltpu.make_async_remote_copy(
                    src_ref=src_refs[a].at[ch], dst_ref=out_refs[a].at[my_chip], send_sem=send_sems.at[a, j],
                    recv_sem=recv_sems.at[a, j], device_id=(*chip, c), device_id_type=MESH_IDS))
                recvs.append(pltpu.make_async_remote_copy(
                    src_ref=src_refs[a].at[ch], dst_ref=out_refs[a].at[ch], send_sem=send_sems.at[a, j],
                    recv_sem=recv_sems.at[a, j], device_id=(*chip, c), device_id_type=MESH_IDS))
        for cp in local + sends:
            cp.start()
        for cp in recvs:
            cp.wait_recv()
        for cp in sends:
            cp.wait_send()
        for cp in local:
            cp.wait()

    hbm = pl.BlockSpec(memory_space=pltpu.HBM)
    return pl.pallas_call(
        body, in_specs=[hbm] * n, out_specs=[hbm] * n, name=name,
        out_shape=[_sds(tuple(s.shape), s.dtype) for s in srcs],
        scratch_shapes=[pltpu.SemaphoreType.DMA((n, 3)), pltpu.SemaphoreType.DMA((n, 3)), pltpu.SemaphoreType.DMA((n,))],
    )(*srcs)


BIG = (("mla_w_in", 1), ("mla_w_uq", 2), ("mla_w_ukv", 2), ("gdn_w_in", 2), ("mem_w_kv", 1), ("w_out", 1),
       ("mlp_w1", 2), ("mlp_w2", 1), ("gdn_conv", 2))
SMALL = ("mla_q_norm", "mla_kv_norm", "gdn_a_log", "gdn_dt_bias", "gdn_o_norm", "ln1_g", "ln1_b", "ln2_g", "ln2_b")
PACK_COLS = 1024


def _unshard(pieces, axis):
    t = jnp.moveaxis(pieces, 0, axis)
    return t.reshape(t.shape[:axis] + (t.shape[axis] * t.shape[axis + 1],) + t.shape[axis + 2:])


def _shard(full, axis):
    t = full.reshape(full.shape[:axis] + (N_DEV, full.shape[axis] // N_DEV) + full.shape[axis + 1:])
    return jnp.moveaxis(t, axis, 0)


def _pack(arrs, lead, cols, mult):
    lead_shape = arrs[0].shape[:lead]
    flat = jnp.concatenate([a.reshape(lead_shape + (-1,)) for a in arrs], -1)
    n = flat.shape[-1]
    r = -(-n // (cols * mult)) * mult
    flat = jnp.pad(flat, [(0, 0)] * lead + [(0, r * cols - n)])
    return flat.reshape(lead_shape + (r, cols))


def _unpack(buf, lead, shapes):
    lead_shape = buf.shape[:lead]
    flat = buf.reshape(lead_shape + (-1,))
    out, off = [], 0
    for shp in shapes:
        n = 1
        for d in shp:
            n *= d
        out.append(flat[..., off:off + n].reshape(lead_shape + tuple(shp)))
        off += n
    return out


def kernel(x, mem, positions, mla_w_in, mla_q_norm, mla_w_uq, mla_kv_norm, mla_w_ukv, gdn_w_in, gdn_conv, gdn_a_log, gdn_dt_bias, gdn_o_norm, mem_w_kv, w_out, ln1_g, ln1_b, mlp_w1, mlp_w2, ln2_g, ln2_b, loss_target, m_mla_w_in, m_mla_q_norm, m_mla_w_uq, m_mla_kv_norm, m_mla_w_ukv, m_gdn_w_in, m_gdn_conv, m_gdn_a_log, m_gdn_dt_bias, m_gdn_o_norm, m_mem_w_kv, m_w_out, m_ln1_g, m_ln1_b, m_mlp_w1, m_mlp_w2, m_ln2_g, m_ln2_b, v_mla_w_in, v_mla_q_norm, v_mla_w_uq, v_mla_kv_norm, v_mla_w_ukv, v_gdn_w_in, v_gdn_conv, v_gdn_a_log, v_gdn_dt_bias, v_gdn_o_norm, v_mem_w_kv, v_w_out, v_ln1_g, v_ln1_b, v_mlp_w1, v_mlp_w2, v_ln2_g, v_ln2_b):
    args = dict(locals())
    w_loc = {n: args[n] for n, _ in BIG}
    m_loc = {n: args["m_" + n] for n, _ in BIG}
    v_loc = {n: args["v_" + n] for n, _ in BIG}
    small = {n: args[n] for n in SMALL}
    axis_of = dict(BIG)
    mm_names = [n for n, _ in BIG if n != "gdn_conv"]

    names = [n for n, _ in BIG]
    got = _gather_two_level([w_loc[n].astype(COMM_DTYPE) for n in mm_names] + [w_loc["gdn_conv"]], "gather_weights")
    full = {n: _unshard(p, axis_of[n]) for n, p in zip(names, got)}
    conv_full = full.pop("gdn_conv")

    wts = {n: [full[n][l] for l in range(full[n].shape[0])] for n in mm_names}
    wts["mla_w_in"] = [_mla_in_to_kernel(w) for w in wts["mla_w_in"]]
    wts["mla_w_uq"] = [_uq_to_kernel(w) for w in wts["mla_w_uq"]]
    wts["gdn_w_in"] = [_gdn_in_to_kernel(w) for w in wts["gdn_w_in"]]
    small_in = dict(small, gdn_conv=conv_full)

    loss_row, grad_x, gw, gs = _local_step(x[0], mem[0], positions[0], loss_target[0], wts, small_in)
    loss = lax.psum(loss_row[0, 0], ("x", "y", "c"))

    gw["mla_w_in"] = [_mla_in_from_kernel(g) for g in gw["mla_w_in"]]
    gw["mla_w_uq"] = [_uq_from_kernel(g) for g in gw["mla_w_uq"]]
    gw["gdn_w_in"] = [_gdn_in_from_kernel(g) for g in gw["gdn_w_in"]]
    gfull = {n: jnp.stack(gw[n]) for n in mm_names}
    gfull["gdn_conv"] = gs.pop("gdn_conv").astype(COMM_DTYPE)
    dims = {n: (w_loc[n].shape[0] * w_loc[n].shape[1], w_loc[n].shape[2]) for n in names}
    g_sent = [_shard(gfull[n], axis_of[n]).reshape((N_DEV,) + dims[n]) for n in names]
    g_sibling = _pair_exchange(g_sent, "grads_pair")
    my_c = lax.axis_index("c").astype(jnp.int32).reshape(1)
    g_pairs = [_pair_sum(a, b, my_c, f"pair_sum_{n}") for n, a, b in zip(names, g_sent, g_sibling)]
    g_got = _chip_exchange(g_pairs, "grads_chips")
    big_out = [{}, {}, {}, {}]
    for n, parts in zip(names, g_got):
        shp = w_loc[n].shape
        rows, cols = dims[n]
        res = _adamw(parts, w_loc[n].reshape(rows, cols), m_loc[n].reshape(rows, cols),
                     v_loc[n].reshape(rows, cols), f"adamw_{n}")
        for kind in range(4):
            big_out[kind][n] = res[kind].reshape(shp)

    s_sent = _pack([gs[n] for n in SMALL], 0, LANES, 8)
    s_got = _exchange([s_sent], "gather_small_grads", gather=True)[0]
    small_out = [dict(zip(SMALL, _unpack(o, 0, [small[n].shape for n in SMALL])))
                 for o in _adamw(s_got, _pack([small[n] for n in SMALL], 0, LANES, 8),
                                 _pack([args["m_" + n] for n in SMALL], 0, LANES, 8),
                                 _pack([args["v_" + n] for n in SMALL], 0, LANES, 8), "adamw_small")]

    order = ["mla_w_in", "mla_q_norm", "mla_w_uq", "mla_kv_norm", "mla_w_ukv", "gdn_w_in", "gdn_conv", "gdn_a_log",
             "gdn_dt_bias", "gdn_o_norm", "mem_w_kv", "w_out", "ln1_g", "ln1_b", "mlp_w1", "mlp_w2", "ln2_g", "ln2_b"]
    outs = [loss, grad_x[None]]
    for kind in range(4):
        for n in order:
            outs.append(big_out[kind][n] if n in axis_of else small_out[kind][n])
    return tuple(outs)
```
